```python
import jax, jax.numpy as jnp
from jax import lax
import numpy as np

D_MODEL = 1024
BATCH = 4
SEQ = 4096
DEPTH = 1

N_META = 16
LRU_WIDTH = 1024
LRU_HEADS = 8
LRU_BLOCK = LRU_WIDTH // LRU_HEADS
CONV_WIDTH = 4
LRU_C = 8.0
RET_HEADS = 8
RET_QK_DIM = 64
RET_V_DIM = 128
RET_QK_WIDTH = RET_HEADS * RET_QK_DIM
RET_WIDTH = RET_HEADS * RET_V_DIM
CHUNK = 128
ROPE_BASE = 10000.0
MIX_WIDTH = LRU_WIDTH + RET_WIDTH
SPLIT_SIZES = (LRU_WIDTH, LRU_WIDTH, RET_QK_WIDTH, RET_QK_WIDTH, RET_WIDTH, RET_WIDTH)
IN_WIDTH = sum(SPLIT_SIZES)
EPS = 1e-6

kernel_name = 'hymba_style_rglru_retention_hybrid'


def _rmsnorm(x, g):
    xf = x.astype(jnp.float32)
    y = xf * lax.rsqrt(jnp.mean(xf * xf, axis=-1, keepdims=True) + EPS)
    return (y * g.astype(jnp.float32)).astype(x.dtype)


def _causal_conv(x, w, b):
    T = x.shape[1]
    xp = jnp.pad(x, ((0, 0), (CONV_WIDTH - 1, 0), (0, 0)))
    y = b
    for k in range(CONV_WIDTH):
        y = y + xp[:, k:k + T] * w[k]
    return y


def _block_diag(x, w, b):
    B, T, _ = x.shape
    xh = x.reshape(B, T, LRU_HEADS, LRU_BLOCK)
    return jnp.einsum('bthi,hij->bthj', xh, w).reshape(B, T, LRU_WIDTH) + b


def _rg_lru(x, w_rg, b_rg, w_ig, b_ig, lam):
    r = jax.nn.sigmoid(_block_diag(x, w_rg, b_rg).astype(jnp.float32))
    i = jax.nn.sigmoid(_block_diag(x, w_ig, b_ig).astype(jnp.float32))
    log_a = -LRU_C * r * jax.nn.softplus(-lam.astype(jnp.float32))
    a = jnp.exp(log_a)
    beta = jnp.sqrt(-jnp.expm1(2.0 * log_a))
    u = beta * i * x.astype(jnp.float32)

    def combine(lhs, rhs):
        a1, b1 = lhs
        a2, b2 = rhs
        return a1 * a2, a2 * b1 + b2

    _, h = lax.associative_scan(combine, (a, u), axis=1)
    return h.astype(x.dtype)


def _rotary(t, pos):
    half = RET_QK_DIM // 2
    inv = ROPE_BASE ** (-jnp.arange(half, dtype=jnp.float32) / half)
    ang = pos.astype(jnp.float32)[:, None] * inv[None, :]
    cos = jnp.cos(ang)[None, :, None, :]
    sin = jnp.sin(ang)[None, :, None, :]
    t1, t2 = t[..., :half], t[..., half:]
    return jnp.concatenate([t1 * cos - t2 * sin, t1 * sin + t2 * cos], axis=-1)


def _retention(q, k, v):
    B, T, H, _ = q.shape
    pad = (-N_META) % CHUNK
    widths = ((0, 0), (pad, 0), (0, 0), (0, 0))
    q, k, v = jnp.pad(q, widths), jnp.pad(k, widths), jnp.pad(v, widths)
    n = (T + pad) // CHUNK
    q = q.reshape(B, n, CHUNK, H, RET_QK_DIM)
    k = k.reshape(B, n, CHUNK, H, RET_QK_DIM)
    v = v.reshape(B, n, CHUNK, H, RET_V_DIM)
    log_g = jnp.log1p(-jnp.exp2(-5.0 - jnp.arange(RET_HEADS, dtype=jnp.float32)))
    idx = jnp.arange(CHUNK, dtype=jnp.float32)
    diff = idx[:, None] - idx[None, :]
    dmask = jnp.where(diff[None] >= 0.0,
                      jnp.exp(jnp.maximum(diff, 0.0)[None] * log_g[:, None, None]), 0.0)
    s = jnp.einsum('bnchd,bnmhd->bnhcm', q, k) * dmask
    inner = jnp.einsum('bnhcm,bnmhe->bnche', s, v)
    k_dec = k * jnp.exp((CHUNK - 1.0 - idx)[:, None] * log_g[None, :])[:, :, None]
    kv = jnp.einsum('bnchd,bnche->bnhde', k_dec, v)
    g_chunk = jnp.exp(CHUNK * log_g)[None, :, None, None]

    def step(state, kv_n):
        return g_chunk * state + kv_n, state

    init = jnp.zeros((B, H, RET_QK_DIM, RET_V_DIM), jnp.float32)
    _, r_prev = lax.scan(step, init, jnp.moveaxis(kv, 1, 0))
    r_prev = jnp.moveaxis(r_prev, 0, 1)
    q_dec = q * jnp.exp((idx + 1.0)[:, None] * log_g[None, :])[:, :, None]
    cross = jnp.einsum('bnchd,bnhde->bnche', q_dec, r_prev)
    o = (inner + cross).reshape(B, n * CHUNK, H, RET_V_DIM)
    return o[:, pad:]


def _head_norm(o, g):
    mu = jnp.mean(o, axis=-1, keepdims=True)
    oc = o - mu
    var = jnp.mean(oc * oc, axis=-1, keepdims=True)
    return oc * lax.rsqrt(var + EPS) * g.astype(jnp.float32).reshape(RET_HEADS, RET_V_DIM)


def setup_inputs(seed: int = 0) -> dict:
    key = jax.random.key(seed)
    ks = jax.random.split(key, 16)
    f32 = jnp.float32
    x = jax.random.normal(ks[0], (BATCH, SEQ, D_MODEL), f32)
    meta_tokens = jax.random.normal(ks[1], (N_META, D_MODEL), f32)
    norm_gain = 1.0 + 0.01 * jax.random.normal(ks[2], (DEPTH, D_MODEL), f32)
    w_in = jax.random.normal(ks[3], (DEPTH, D_MODEL, IN_WIDTH), f32) * D_MODEL ** -0.5
    conv_w = jax.random.normal(ks[4], (DEPTH, CONV_WIDTH, LRU_WIDTH), f32) * CONV_WIDTH ** -0.5
    conv_b = 0.01 * jax.random.normal(ks[5], (DEPTH, LRU_WIDTH), f32)
    w_rg = jax.random.normal(ks[6], (DEPTH, LRU_HEADS, LRU_BLOCK, LRU_BLOCK), f32) * LRU_BLOCK ** -0.5
    b_rg = 0.01 * jax.random.normal(ks[7], (DEPTH, LRU_WIDTH), f32)
    w_ig = jax.random.normal(ks[8], (DEPTH, LRU_HEADS, LRU_BLOCK, LRU_BLOCK), f32) * LRU_BLOCK ** -0.5
    b_ig = 0.01 * jax.random.normal(ks[9], (DEPTH, LRU_WIDTH), f32)
    ac = jax.random.uniform(ks[10], (DEPTH, LRU_WIDTH), f32, minval=0.9, maxval=0.999)
    a = ac ** (1.0 / LRU_C)
    lru_lambda = jnp.log(a) - jnp.log1p(-a)
    ret_norm_gain = 1.0 + 0.01 * jax.random.normal(ks[11], (DEPTH, RET_WIDTH), f32)
    w_out = jax.random.normal(ks[12], (DEPTH, MIX_WIDTH, D_MODEL), f32) * MIX_WIDTH ** -0.5
    final_norm_gain = 1.0 + 0.01 * jax.random.normal(ks[13], (D_MODEL,), f32)
    return {'x': x, 'meta_tokens': meta_tokens, 'norm_gain': norm_gain, 'w_in': w_in,
            'conv_w': conv_w, 'conv_b': conv_b, 'w_rg': w_rg, 'b_rg': b_rg,
            'w_ig': w_ig, 'b_ig': b_ig, 'lru_lambda': lru_lambda,
            'ret_norm_gain': ret_norm_gain, 'w_out': w_out, 'final_norm_gain': final_norm_gain}


def reference(x, meta_tokens, norm_gain, w_in, conv_w, conv_b, w_rg, b_rg, w_ig, b_ig,
              lru_lambda, ret_norm_gain, w_out, final_norm_gain):
    B = x.shape[0]
    meta = jnp.broadcast_to(meta_tokens.astype(x.dtype)[None], (B, N_META, D_MODEL))
    h = jnp.concatenate([meta, x], axis=1)
    T = h.shape[1]
    pos = jnp.arange(T)
    split_idx = np.cumsum(SPLIT_SIZES)[:-1].tolist()
    for l in range(DEPTH):
        u = _rmsnorm(h, norm_gain[l])
        proj = jnp.einsum('btd,de->bte', u, w_in[l])
        lru_x, lru_gate, q, k, v, ret_gate = jnp.split(proj, split_idx, axis=-1)
        xc = _causal_conv(lru_x, conv_w[l], conv_b[l])
        y_lru = _rg_lru(xc, w_rg[l], b_rg[l], w_ig[l], b_ig[l], lru_lambda[l]) * jax.nn.silu(lru_gate)
        qh = _rotary(q.reshape(B, T, RET_HEADS, RET_QK_DIM).astype(jnp.float32), pos)
        kh = _rotary(k.reshape(B, T, RET_HEADS, RET_QK_DIM).astype(jnp.float32), pos) * RET_QK_DIM ** -0.5
        vh = v.reshape(B, T, RET_HEADS, RET_V_DIM).astype(jnp.float32)
        o = _head_norm(_retention(qh, kh, vh), ret_norm_gain[l])
        y_ret = o.reshape(B, T, RET_WIDTH).astype(x.dtype) * jax.nn.silu(ret_gate)
        y = jnp.concatenate([y_lru, y_ret], axis=-1)
        h = h + jnp.einsum('bte,ed->btd', y, w_out[l])
    return _rmsnorm(h, final_norm_gain)[:, N_META:]
```

```python
import functools

import numpy as np
import jax
import jax.numpy as jnp
from jax import lax
from jax.experimental import pallas as pl
from jax.experimental.pallas import tpu as pltpu

f32 = jnp.float32
bf16 = jnp.bfloat16

D_MODEL = 1024
N_META = 16
LRU_WIDTH = 1024
LRU_HEADS = 8
LRU_BLOCK = 128
CONV_WIDTH = 4
LRU_C = 8.0
RET_HEADS = 8
RET_QK_DIM = 64
RET_V_DIM = 128
RET_QK_WIDTH = 512
RET_WIDTH = 1024
CHUNK = 128
ROPE_BASE = 10000.0
MIX_WIDTH = 2048
EPS = 1e-6

OFF_LX, OFF_GATE, OFF_Q, OFF_K, OFF_V, OFF_RG = 0, 1024, 2048, 2560, 3072, 4096
IN_WIDTH = 5120

LANES = 128
SUBLANES = 8
N_PAIRS = RET_HEADS // 2
TILE_T = 256
SCAN_ROWS = 16
VMEM_LIMIT_BYTES = 56 * 1024 * 1024


def _qk_column_perm():
    perm = np.zeros((RET_QK_WIDTH,), np.int32)
    half = RET_QK_DIM // 2
    for p in range(N_PAIRS):
        for l in range(LANES):
            grp, f = divmod(l, half)
            head = 2 * p + (grp % 2)
            d = f + half * (grp // 2)
            perm[p * LANES + l] = head * RET_QK_DIM + d
    return perm


def _lane_head():
    half = RET_QK_DIM // 2
    l = np.arange(RET_QK_WIDTH)
    return 2 * (l // LANES) + ((l % LANES) // half) % 2


def _retention_tables():
    log_g = jnp.log1p(-jnp.exp2(-5.0 - jnp.arange(RET_HEADS, dtype=f32)))
    idx = jnp.arange(CHUNK, dtype=f32)
    diff = idx[:, None] - idx[None, :]
    dmask = jnp.where(diff[None] >= 0.0,
                      jnp.exp(jnp.maximum(diff, 0.0)[None] * log_g[:, None, None]), 0.0)
    lane_head = _lane_head()
    lg_lane = log_g[lane_head]
    k_dec = jnp.exp((CHUNK - 1.0 - idx)[:, None] * lg_lane[None, :])
    q_dec = jnp.exp((idx + 1.0)[:, None] * lg_lane[None, :])
    scale = RET_QK_DIM ** -0.5
    even = jnp.asarray((lane_head % 2 == 0), f32)[None, :]
    odd = 1.0 - even
    ones = jnp.ones((CHUNK, 1), f32)
    q_e = ones * even * scale
    q_o = ones * odd * scale
    qd_e = q_dec * even * scale
    qd_o = q_dec * odd * scale
    g_chunk = jnp.exp(CHUNK * log_g)
    g_cols = jnp.repeat(g_chunk, RET_V_DIM)
    g_state = jnp.broadcast_to(g_cols.reshape(N_PAIRS, 1, 2 * RET_V_DIM), (N_PAIRS, LANES, 2 * RET_V_DIM))
    return dmask, k_dec, q_e, q_o, qd_e, qd_o, g_state


def _rotary_tables(n_pos):
    half = RET_QK_DIM // 2
    inv = ROPE_BASE ** (-jnp.arange(half, dtype=f32) / half)
    ang = jnp.arange(n_pos).astype(f32)[:, None] * inv[None, :]
    cos, sin = jnp.cos(ang), jnp.sin(ang)
    cos_t = jnp.concatenate([cos, cos, cos, cos], axis=-1)
    sin_t = jnp.concatenate([-sin, -sin, sin, sin], axis=-1)
    return cos_t, sin_t


def _rmsnorm_rows(x, gain_row):
    ms = jnp.mean(x * x, axis=-1, keepdims=True)
    return x * lax.rsqrt(ms + EPS) * gain_row


def _conv_and_gates(n_rows, lx_scr, convw_ref, convb_ref, wg_ref, xc_scr, xcb_scr, pre_scr):
    base = SUBLANES
    xc = convb_ref[...] + convw_ref[3:4, :] * lx_scr[pl.ds(base, n_rows), :]
    xc = xc + convw_ref[2:3, :] * lx_scr[pl.ds(base - 1, n_rows), :]
    xc = xc + convw_ref[1:2, :] * lx_scr[pl.ds(base - 2, n_rows), :]
    xc = xc + convw_ref[0:1, :] * lx_scr[pl.ds(base - 3, n_rows), :]
    xc_scr[...] = xc
    xcb_scr[...] = xc.astype(bf16)
    for h in range(LRU_HEADS):
        pre = jnp.dot(xcb_scr[:, h * LRU_BLOCK:(h + 1) * LRU_BLOCK], wg_ref[h], preferred_element_type=f32)
        pre_scr[:, h * LRU_BLOCK:(h + 1) * LRU_BLOCK] = pre[:, :LRU_BLOCK]
        pre_scr[:, LRU_WIDTH + h * LRU_BLOCK:LRU_WIDTH + (h + 1) * LRU_BLOCK] = pre[:, LRU_BLOCK:]
    lx_scr[0:SUBLANES, :] = lx_scr[pl.ds(n_rows, SUBLANES), :]


def _lru_block(xc, pre_r, pre_i, brg, big, neg_c_sp, carry):
    rows = xc.shape[0]
    r = jax.nn.sigmoid(pre_r + brg)
    i = jax.nn.sigmoid(pre_i + big)
    log_a = r * neg_c_sp
    a = jnp.exp(log_a)
    beta = jnp.sqrt(-jnp.tanh(log_a) * (1.0 + a * a))
    b = beta * i * xc
    nv = rows // SUBLANES
    a3 = a.reshape(nv, SUBLANES, LRU_WIDTH)
    b3 = b.reshape(nv, SUBLANES, LRU_WIDTH)
    rowid = lax.broadcasted_iota(jnp.int32, (nv, SUBLANES, LRU_WIDTH), 1)
    for s in (1, 2, 4):
        keep = rowid >= s
        a_s = jnp.where(keep, pltpu.roll(a3, s, 1), 1.0)
        b_s = jnp.where(keep, pltpu.roll(b3, s, 1), 0.0)
        b3 = a3 * b_s + b3
        a3 = a3 * a_s
    hs = []
    for v in range(nv):
        h_v = a3[v] * carry + b3[v]
        carry = h_v[SUBLANES - 1:SUBLANES, :]
        hs.append(h_v)
    return jnp.concatenate(hs, axis=0), carry


def _rotary(t, cos, sin):
    return t * cos + pltpu.roll(t, LANES // 2, 1) * sin


def _dot_t0(a, b):
    return lax.dot_general(a, b, (((0,), (0,)), ((), ())), preferred_element_type=f32)


def _dot_nt(a, b):
    return lax.dot_general(a, b, (((1,), (1,)), ((), ())), preferred_element_type=f32)


def _meta_kernel(meta_ref, cos_ref, sin_ref, g1_ref, wlx_ref, wk_ref, wv_ref, convw_ref, convb_ref,
                 wg_ref, brg_ref, big_ref, lam_ref, kdec_ref,
                 lx_out, h_out, st_out,
                 lx_scr, xc_scr, xcb_scr, pre_scr):
    u = _rmsnorm_rows(meta_ref[...], g1_ref[...]).astype(bf16)
    lx_scr[0:SUBLANES, :] = jnp.zeros((SUBLANES, LRU_WIDTH), f32)
    lx_scr[SUBLANES:SUBLANES + N_META, :] = jnp.dot(u, wlx_ref[...], preferred_element_type=f32)
    k = jnp.dot(u, wk_ref[...], preferred_element_type=f32)
    v = jnp.dot(u, wv_ref[...], preferred_element_type=f32).astype(bf16)

    _conv_and_gates(N_META, lx_scr, convw_ref, convb_ref, wg_ref, xc_scr, xcb_scr, pre_scr)
    lx_out[...] = lx_scr[0:SUBLANES, :]
    neg_c_sp = -LRU_C * jax.nn.softplus(-lam_ref[...])
    _, carry = _lru_block(xc_scr[...], pre_scr[:, :LRU_WIDTH], pre_scr[:, LRU_WIDTH:],
                          brg_ref[...], big_ref[...], neg_c_sp, jnp.zeros((1, LRU_WIDTH), f32))
    h_out[...] = jnp.broadcast_to(carry, (SUBLANES, LRU_WIDTH))

    cos, sin = cos_ref[...], sin_ref[...]
    for p in range(N_PAIRS):
        sl = slice(p * LANES, (p + 1) * LANES)
        k_rot = _rotary(k[:, sl], cos, sin)
        kd = (k_rot * kdec_ref[CHUNK - N_META:CHUNK, sl]).astype(bf16)
        st_out[p] = _dot_t0(kd, v[:, p * 2 * RET_V_DIM:(p + 1) * 2 * RET_V_DIM])


def _main_kernel(x_ref, cos_ref, sin_ref, g1_ref, win_ref, convw_ref, convb_ref, wg_ref, brg_ref, big_ref,
                 lam_ref, rng_ref, wout_ref, fng_ref, dmask_ref, kdec_ref, qe_ref, qo_ref, qde_ref, qdo_ref,
                 gst_ref, lx0_ref, h0_ref, st0_ref,
                 o_ref,
                 u_scr, lx_scr, gate_scr, q_scr, k_scr, v_scr, rg_scr, xc_scr, xcb_scr, pre_scr, y_scr,
                 h_scr, st_scr):
    tt = TILE_T

    @pl.when(pl.program_id(1) == 0)
    def _():
        lx_scr[0:SUBLANES, :] = lx0_ref[...]
        h_scr[...] = h0_ref[...]
        st_scr[...] = st0_ref[...]

    u_scr[...] = _rmsnorm_rows(x_ref[...], g1_ref[...]).astype(bf16)
    lx_scr[SUBLANES:SUBLANES + tt, :] = jnp.dot(u_scr[...], win_ref[:, OFF_LX:OFF_LX + LRU_WIDTH],
                                               preferred_element_type=f32)
    gate_scr[...] = jnp.dot(u_scr[...], win_ref[:, OFF_GATE:OFF_GATE + LRU_WIDTH], preferred_element_type=f32)
    q_scr[...] = jnp.dot(u_scr[...], win_ref[:, OFF_Q:OFF_Q + RET_QK_WIDTH], preferred_element_type=f32)
    k_scr[...] = jnp.dot(u_scr[...], win_ref[:, OFF_K:OFF_K + RET_QK_WIDTH], preferred_element_type=f32)
    v_scr[...] = jnp.dot(u_scr[...], win_ref[:, OFF_V:OFF_V + RET_WIDTH], preferred_element_type=f32).astype(bf16)
    rg_scr[...] = jnp.dot(u_scr[...], win_ref[:, OFF_RG:OFF_RG + RET_WIDTH], preferred_element_type=f32)

    _conv_and_gates(tt, lx_scr, convw_ref, convb_ref, wg_ref, xc_scr, xcb_scr, pre_scr)
    neg_c_sp = -LRU_C * jax.nn.softplus(-lam_ref[...])
    brg, big = brg_ref[...], big_ref[...]

    def scan_body(it, carry):
        r0 = pl.multiple_of(it * SCAN_ROWS, SCAN_ROWS)
        rows = pl.ds(r0, SCAN_ROWS)
        h, carry = _lru_block(xc_scr[rows, :], pre_scr[rows, 0:LRU_WIDTH], pre_scr[rows, LRU_WIDTH:2 * LRU_WIDTH],
                              brg, big, neg_c_sp, carry)
        gate = gate_scr[rows, :]
        y_scr[rows, 0:LRU_WIDTH] = (h * (gate * jax.nn.sigmoid(gate))).astype(bf16)
        return carry

    carry = lax.fori_loop(0, tt // SCAN_ROWS, scan_body, h_scr[0:1, :])
    h_scr[...] = jnp.broadcast_to(carry, (SUBLANES, LRU_WIDTH))

    for c in range(tt // CHUNK):
        rows = slice(c * CHUNK, (c + 1) * CHUNK)
        cos, sin = cos_ref[rows, :], sin_ref[rows, :]
        for p in range(N_PAIRS):
            sl = slice(p * LANES, (p + 1) * LANES)
            q_rot = _rotary(q_scr[rows, sl], cos, sin)
            k_rot = _rotary(k_scr[rows, sl], cos, sin)
            k_b = k_rot.astype(bf16)
            k_d = (k_rot * kdec_ref[:, sl]).astype(bf16)
            st_pair = st_scr[p]
            for e, (qm_ref, qdm_ref) in enumerate(((qe_ref, qde_ref), (qo_ref, qdo_ref))):
                h = 2 * p + e
                hs = slice(h * RET_V_DIM, (h + 1) * RET_V_DIM)
                q_m = (q_rot * qm_ref[:, sl]).astype(bf16)
                q_dm = (q_rot * qdm_ref[:, sl]).astype(bf16)
                s = _dot_nt(q_m, k_b) * dmask_ref[h]
                st_h = st_pair[:, e * RET_V_DIM:(e + 1) * RET_V_DIM].astype(bf16)
                o = jnp.dot(s.astype(bf16), v_scr[rows, hs], preferred_element_type=f32)
                o = o + jnp.dot(q_dm, st_h, preferred_element_type=f32)
                mu = jnp.mean(o, axis=-1, keepdims=True)
                oc = o - mu
                var = jnp.mean(oc * oc, axis=-1, keepdims=True)
                on = oc * lax.rsqrt(var + EPS) * rng_ref[:, hs]
                rg = rg_scr[rows, hs]
                y_scr[rows, LRU_WIDTH + h * RET_V_DIM:LRU_WIDTH + (h + 1) * RET_V_DIM] = (
                    on * (rg * jax.nn.sigmoid(rg))).astype(bf16)
            kv = _dot_t0(k_d, v_scr[rows, p * 2 * RET_V_DIM:(p + 1) * 2 * RET_V_DIM])
            st_scr[p] = gst_ref[p] * st_pair + kv

    res = x_ref[...] + jnp.dot(y_scr[...], wout_ref[...], preferred_element_type=f32)
    o_ref[...] = _rmsnorm_rows(res, fng_ref[...])


def _const_spec(shape):
    nd = len(shape)
    return pl.BlockSpec(shape, lambda b, j, _nd=nd: (0,) * _nd, pipeline_mode=pl.Buffered(1))


def kernel(x, meta_tokens, norm_gain, w_in, conv_w, conv_b, w_rg, b_rg, w_ig, b_ig,
           lru_lambda, ret_norm_gain, w_out, final_norm_gain):
    B, S, D = x.shape
    assert D == D_MODEL and S % TILE_T == 0 and TILE_T % CHUNK == 0
    assert norm_gain.shape[0] == 1, "single-layer block"
    assert meta_tokens.shape == (N_META, D_MODEL)

    perm = _qk_column_perm()
    w = w_in[0]
    w_q = w[:, OFF_Q:OFF_K][:, perm]
    w_k = w[:, OFF_K:OFF_V][:, perm]
    win_b = jnp.concatenate([w[:, :OFF_Q], w_q, w_k, w[:, OFF_V:]], axis=1).astype(bf16)
    wg_b = jnp.concatenate([w_rg[0], w_ig[0]], axis=-1).astype(bf16)
    wout_b = w_out[0].astype(bf16)
    g1 = norm_gain[0].reshape(1, D_MODEL)
    fng = final_norm_gain.reshape(1, D_MODEL)
    convw = conv_w[0]
    convb = conv_b[0].reshape(1, LRU_WIDTH)
    brg = b_rg[0].reshape(1, LRU_WIDTH)
    big = b_ig[0].reshape(1, LRU_WIDTH)
    lam = lru_lambda[0].reshape(1, LRU_WIDTH)
    rng = ret_norm_gain[0].reshape(1, RET_WIDTH)

    dmask, k_dec, q_e, q_o, qd_e, qd_o, g_state = _retention_tables()
    cos_t, sin_t = _rotary_tables(N_META + S)

    n_lx = SUBLANES + N_META
    lx0, h0, st0 = pl.pallas_call(
        _meta_kernel,
        out_shape=(jax.ShapeDtypeStruct((SUBLANES, LRU_WIDTH), f32),
                   jax.ShapeDtypeStruct((SUBLANES, LRU_WIDTH), f32),
                   jax.ShapeDtypeStruct((N_PAIRS, LANES, 2 * RET_V_DIM), f32)),
        scratch_shapes=[pltpu.VMEM((n_lx, LRU_WIDTH), f32),
                        pltpu.VMEM((N_META, LRU_WIDTH), f32),
                        pltpu.VMEM((N_META, LRU_WIDTH), bf16),
                        pltpu.VMEM((N_META, 2 * LRU_WIDTH), f32)],
        compiler_params=pltpu.CompilerParams(vmem_limit_bytes=VMEM_LIMIT_BYTES),
        name="hybrid_meta_state",
    )(meta_tokens, cos_t[:N_META], sin_t[:N_META], g1, win_b[:, OFF_LX:OFF_LX + LRU_WIDTH],
      win_b[:, OFF_K:OFF_K + RET_QK_WIDTH], win_b[:, OFF_V:OFF_V + RET_WIDTH], convw, convb,
      wg_b, brg, big, lam, k_dec)

    tt = TILE_T
    n_tiles = S // tt
    row_spec = lambda width: pl.BlockSpec((tt, width), lambda b, j: (j, 0))
    in_specs = [
        pl.BlockSpec((None, tt, D_MODEL), lambda b, j: (b, j, 0)),
        row_spec(LANES), row_spec(LANES),
        _const_spec((1, D_MODEL)),
        _const_spec((D_MODEL, IN_WIDTH)),
        _const_spec((CONV_WIDTH, LRU_WIDTH)), _const_spec((1, LRU_WIDTH)),
        _const_spec((LRU_HEADS, LRU_BLOCK, 2 * LRU_BLOCK)),
        _const_spec((1, LRU_WIDTH)), _const_spec((1, LRU_WIDTH)), _const_spec((1, LRU_WIDTH)),
        _const_spec((1, RET_WIDTH)),
        _const_spec((MIX_WIDTH, D_MODEL)),
        _const_spec((1, D_MODEL)),
        _const_spec((RET_HEADS, CHUNK, CHUNK)),
        _const_spec((CHUNK, RET_QK_WIDTH)), _const_spec((CHUNK, RET_QK_WIDTH)), _const_spec((CHUNK, RET_QK_WIDTH)),
        _const_spec((CHUNK, RET_QK_WIDTH)), _const_spec((CHUNK, RET_QK_WIDTH)),
        _const_spec((N_PAIRS, LANES, 2 * RET_V_DIM)),
        _const_spec((SUBLANES, LRU_WIDTH)), _const_spec((SUBLANES, LRU_WIDTH)),
        _const_spec((N_PAIRS, LANES, 2 * RET_V_DIM)),
    ]
    scratch = [
        pltpu.VMEM((tt, D_MODEL), bf16),
        pltpu.VMEM((SUBLANES + tt, LRU_WIDTH), f32),
        pltpu.VMEM((tt, LRU_WIDTH), f32),
        pltpu.VMEM((tt, RET_QK_WIDTH), f32),
        pltpu.VMEM((tt, RET_QK_WIDTH), f32),
        pltpu.VMEM((tt, RET_WIDTH), bf16),
        pltpu.VMEM((tt, RET_WIDTH), f32),
        pltpu.VMEM((tt, LRU_WIDTH), f32),
        pltpu.VMEM((tt, LRU_WIDTH), bf16),
        pltpu.VMEM((tt, 2 * LRU_WIDTH), f32),
        pltpu.VMEM((tt, MIX_WIDTH), bf16),
        pltpu.VMEM((SUBLANES, LRU_WIDTH), f32),
        pltpu.VMEM((N_PAIRS, LANES, 2 * RET_V_DIM), f32),
    ]
    out = pl.pallas_call(
        _main_kernel,
        grid=(B, n_tiles),
        in_specs=in_specs,
        out_specs=pl.BlockSpec((None, tt, D_MODEL), lambda b, j: (b, j, 0)),
        out_shape=jax.ShapeDtypeStruct((B, S, D_MODEL), x.dtype),
        scratch_shapes=scratch,
        compiler_params=pltpu.CompilerParams(
            dimension_semantics=("arbitrary", "arbitrary"),
            vmem_limit_bytes=VMEM_LIMIT_BYTES),
        name="hybrid_main",
    )(x, cos_t[N_META:], sin_t[N_META:], g1, win_b, convw, convb, wg_b, brg, big, lam, rng, wout_b, fng,
      dmask, k_dec, q_e, q_o, qd_e, qd_o, g_state, lx0, h0, st0)
    return out
```

```python
import functools

import numpy as np
import jax
import jax.numpy as jnp
from jax import lax
from jax.experimental import pallas as pl
from jax.experimental.pallas import tpu as pltpu

f32 = jnp.float32
bf16 = jnp.bfloat16

D_MODEL = 1024
N_META = 16
LRU_WIDTH = 1024
LRU_HEADS = 8
LRU_BLOCK = 128
CONV_WIDTH = 4
LRU_C = 8.0
RET_HEADS = 8
RET_QK_DIM = 64
RET_V_DIM = 128
RET_QK_WIDTH = 512
RET_WIDTH = 1024
CHUNK = 128
ROPE_BASE = 10000.0
MIX_WIDTH = 2048
EPS = 1e-6

OFF_LX, OFF_GATE, OFF_Q, OFF_K, OFF_V, OFF_RG = 0, 1024, 2048, 2560, 3072, 4096
IN_WIDTH = 5120

LANES = 128
SUBLANES = 8
N_PAIRS = RET_HEADS // 2
TILE_T = 256
SCAN_ROWS = 16
VMEM_LIMIT_BYTES = 56 * 1024 * 1024


def _qk_column_perm():
    perm = np.zeros((RET_QK_WIDTH,), np.int32)
    half = RET_QK_DIM // 2
    for p in range(N_PAIRS):
        for l in range(LANES):
            grp, f = divmod(l, half)
            head = 2 * p + (grp % 2)
            d = f + half * (grp // 2)
            perm[p * LANES + l] = head * RET_QK_DIM + d
    return perm


def _lane_head():
    half = RET_QK_DIM // 2
    l = np.arange(RET_QK_WIDTH)
    return 2 * (l // LANES) + ((l % LANES) // half) % 2


def _retention_tables():
    log_g = jnp.log1p(-jnp.exp2(-5.0 - jnp.arange(RET_HEADS, dtype=f32)))
    idx = jnp.arange(CHUNK, dtype=f32)
    diff = idx[:, None] - idx[None, :]
    dmask = jnp.where(diff[None] >= 0.0,
                      jnp.exp(jnp.maximum(diff, 0.0)[None] * log_g[:, None, None]), 0.0)
    lane_head = _lane_head()
    lg_lane = log_g[lane_head]
    k_dec = jnp.exp((CHUNK - 1.0 - idx)[:, None] * lg_lane[None, :])
    q_dec = jnp.exp((idx + 1.0)[:, None] * lg_lane[None, :])
    scale = RET_QK_DIM ** -0.5
    even = jnp.asarray((lane_head % 2 == 0), f32)[None, :]
    odd = 1.0 - even
    ones = jnp.ones((CHUNK, 1), f32)
    q_e = ones * even * scale
    q_o = ones * odd * scale
    qd_e = q_dec * even * scale
    qd_o = q_dec * odd * scale
    g_chunk = jnp.exp(CHUNK * log_g)
    g_cols = jnp.repeat(g_chunk, RET_V_DIM)
    g_state = jnp.broadcast_to(g_cols.reshape(N_PAIRS, 1, 2 * RET_V_DIM), (N_PAIRS, LANES, 2 * RET_V_DIM))
    return dmask, k_dec, q_e, q_o, qd_e, qd_o, g_state


def _rotary_tables(n_pos):
    half = RET_QK_DIM // 2
    inv = ROPE_BASE ** (-jnp.arange(half, dtype=f32) / half)
    ang = jnp.arange(n_pos).astype(f32)[:, None] * inv[None, :]
    cos, sin = jnp.cos(ang), jnp.sin(ang)
    cos_t = jnp.concatenate([cos, cos, cos, cos], axis=-1)
    sin_t = jnp.concatenate([-sin, -sin, sin, sin], axis=-1)
    return cos_t, sin_t


def _rmsnorm_rows(x, gain_row):
    ms = jnp.mean(x * x, axis=-1, keepdims=True)
    return x * lax.rsqrt(ms + EPS) * gain_row


def _conv_and_gates(n_rows, lx_scr, convw_ref, convb_ref, wg_ref, xc_scr, xcb_scr, pre_scr):
    base = SUBLANES
    xc = convb_ref[...] + convw_ref[3:4, :] * lx_scr[pl.ds(base, n_rows), :]
    xc = xc + convw_ref[2:3, :] * lx_scr[pl.ds(base - 1, n_rows), :]
    xc = xc + convw_ref[1:2, :] * lx_scr[pl.ds(base - 2, n_rows), :]
    xc = xc + convw_ref[0:1, :] * lx_scr[pl.ds(base - 3, n_rows), :]
    xc_scr[...] = xc
    xcb_scr[...] = xc.astype(bf16)
    for h in range(LRU_HEADS):
        pre = jnp.dot(xcb_scr[:, h * LRU_BLOCK:(h + 1) * LRU_BLOCK], wg_ref[h], preferred_element_type=f32)
        pre_scr[:, h * LRU_BLOCK:(h + 1) * LRU_BLOCK] = pre[:, :LRU_BLOCK]
        pre_scr[:, LRU_WIDTH + h * LRU_BLOCK:LRU_WIDTH + (h + 1) * LRU_BLOCK] = pre[:, LRU_BLOCK:]
    lx_scr[0:SUBLANES, :] = lx_scr[pl.ds(n_rows, SUBLANES), :]


def _lru_block(xc, pre_r, pre_i, brg, big, neg_c_sp, carry):
    rows = xc.shape[0]
    r = jax.nn.sigmoid(pre_r + brg)
    i = jax.nn.sigmoid(pre_i + big)
    log_a = r * neg_c_sp
    a = jnp.exp(log_a)
    beta = jnp.sqrt(-jnp.tanh(log_a) * (1.0 + a * a))
    b = beta * i * xc
    nv = rows // SUBLANES
    a3 = a.reshape(nv, SUBLANES, LRU_WIDTH)
    b3 = b.reshape(nv, SUBLANES, LRU_WIDTH)
    rowid = lax.broadcasted_iota(jnp.int32, (nv, SUBLANES, LRU_WIDTH), 1)
    for s in (1, 2, 4):
        keep = rowid >= s
        a_s = jnp.where(keep, pltpu.roll(a3, s, 1), 1.0)
        b_s = jnp.where(keep, pltpu.roll(b3, s, 1), 0.0)
        b3 = a3 * b_s + b3
        a3 = a3 * a_s
    hs = []
    for v in range(nv):
        h_v = a3[v] * carry + b3[v]
        carry = h_v[SUBLANES - 1:SUBLANES, :]
        hs.append(h_v)
    return jnp.concatenate(hs, axis=0), carry


def _rotary(t, cos, sin):
    return t * cos + pltpu.roll(t, LANES // 2, 1) * sin


def _dot_t0(a, b):
    return lax.dot_general(a, b, (((0,), (0,)), ((), ())), preferred_element_type=f32)


def _dot_nt(a, b):
    return lax.dot_general(a, b, (((1,), (1,)), ((), ())), preferred_element_type=f32)


def _meta_kernel(meta_ref, cos_ref, sin_ref, g1_ref, wlx_ref, wk_ref, wv_ref, convw_ref, convb_ref,
                 wg_ref, brg_ref, big_ref, lam_ref, kdec_ref,
                 lx_out, h_out, st_out,
                 lx_scr, xc_scr, xcb_scr, pre_scr):
    u = _rmsnorm_rows(meta_ref[...], g1_ref[...]).astype(bf16)
    lx_scr[0:SUBLANES, :] = jnp.zeros((SUBLANES, LRU_WIDTH), f32)
    lx_scr[SUBLANES:SUBLANES + N_META, :] = jnp.dot(u, wlx_ref[...], preferred_element_type=f32)
    k = jnp.dot(u, wk_ref[...], preferred_element_type=f32)
    v = jnp.dot(u, wv_ref[...], preferred_element_type=f32).astype(bf16)

    _conv_and_gates(N_META, lx_scr, convw_ref, convb_ref, wg_ref, xc_scr, xcb_scr, pre_scr)
    lx_out[...] = lx_scr[0:SUBLANES, :]
    neg_c_sp = -LRU_C * jax.nn.softplus(-lam_ref[...])
    _, carry = _lru_block(xc_scr[...], pre_scr[:, :LRU_WIDTH], pre_scr[:, LRU_WIDTH:],
                          brg_ref[...], big_ref[...], neg_c_sp, jnp.zeros((1, LRU_WIDTH), f32))
    h_out[...] = jnp.broadcast_to(carry, (SUBLANES, LRU_WIDTH))

    cos, sin = cos_ref[...], sin_ref[...]
    for p in range(N_PAIRS):
        sl = slice(p * LANES, (p + 1) * LANES)
        k_rot = _rotary(k[:, sl], cos, sin)
        kd = (k_rot * kdec_ref[CHUNK - N_META:CHUNK, sl]).astype(bf16)
        st_out[p] = _dot_t0(kd, v[:, p * 2 * RET_V_DIM:(p + 1) * 2 * RET_V_DIM])


def _main_kernel(x_ref, cos_ref, sin_ref, g1_ref, win_ref, convw_ref, convb_ref, wg_ref, brg_ref, big_ref,
                 lam_ref, rng_ref, wout_ref, fng_ref, dmask_ref, kdec_ref, qe_ref, qo_ref, qde_ref, qdo_ref,
                 gst_ref, lx0_ref, h0_ref, st0_ref,
                 o_ref,
                 u_scr, lx_scr, gate_scr, q_scr, k_scr, v_scr, rg_scr, xc_scr, xcb_scr, pre_scr, y_scr,
                 h_scr, st_scr):
    tt = TILE_T

    @pl.when(pl.program_id(1) == 0)
    def _():
        lx_scr[0:SUBLANES, :] = lx0_ref[...]
        h_scr[...] = h0_ref[...]
        st_scr[...] = st0_ref[...]

    u_scr[...] = _rmsnorm_rows(x_ref[...], g1_ref[...]).astype(bf16)
    lx_scr[SUBLANES:SUBLANES + tt, :] = jnp.dot(u_scr[...], win_ref[:, OFF_LX:OFF_LX + LRU_WIDTH],
                                               preferred_element_type=f32)
    gate_scr[...] = jnp.dot(u_scr[...], win_ref[:, OFF_GATE:OFF_GATE + LRU_WIDTH], preferred_element_type=f32)
    q_scr[...] = jnp.dot(u_scr[...], win_ref[:, OFF_Q:OFF_Q + RET_QK_WIDTH], preferred_element_type=f32)
    k_scr[...] = jnp.dot(u_scr[...], win_ref[:, OFF_K:OFF_K + RET_QK_WIDTH], preferred_element_type=f32)
    v_scr[...] = jnp.dot(u_scr[...], win_ref[:, OFF_V:OFF_V + RET_WIDTH], preferred_element_type=f32).astype(bf16)
    rg_scr[...] = jnp.dot(u_scr[...], win_ref[:, OFF_RG:OFF_RG + RET_WIDTH], preferred_element_type=f32)

    _conv_and_gates(tt, lx_scr, convw_ref, convb_ref, wg_ref, xc_scr, xcb_scr, pre_scr)
    neg_c_sp = -LRU_C * jax.nn.softplus(-lam_ref[...])
    brg, big = brg_ref[...], big_ref[...]

    def scan_body(it, carry):
        r0 = pl.multiple_of(it * SCAN_ROWS, SCAN_ROWS)
        rows = pl.ds(r0, SCAN_ROWS)
        h, carry = _lru_block(xc_scr[rows, :], pre_scr[rows, 0:LRU_WIDTH], pre_scr[rows, LRU_WIDTH:2 * LRU_WIDTH],
                              brg, big, neg_c_sp, carry)
        gate = gate_scr[rows, :]
        y_scr[rows, 0:LRU_WIDTH] = (h * (gate * jax.nn.sigmoid(gate))).astype(bf16)
        return carry

    carry = h_scr[0:1, :]
    for it in range(tt // SCAN_ROWS):
        carry = scan_body(it, carry)
    h_scr[...] = jnp.broadcast_to(carry, (SUBLANES, LRU_WIDTH))

    for c in range(tt // CHUNK):
        rows = slice(c * CHUNK, (c + 1) * CHUNK)
        cos, sin = cos_ref[rows, :], sin_ref[rows, :]
        for p in range(N_PAIRS):
            sl = slice(p * LANES, (p + 1) * LANES)
            q_rot = _rotary(q_scr[rows, sl], cos, sin)
            k_rot = _rotary(k_scr[rows, sl], cos, sin)
            k_b = k_rot.astype(bf16)
            k_d = (k_rot * kdec_ref[:, sl]).astype(bf16)
            st_pair = st_scr[p]
            for e, (qm_ref, qdm_ref) in enumerate(((qe_ref, qde_ref), (qo_ref, qdo_ref))):
                h = 2 * p + e
                hs = slice(h * RET_V_DIM, (h + 1) * RET_V_DIM)
                q_m = (q_rot * qm_ref[:, sl]).astype(bf16)
                q_dm = (q_rot * qdm_ref[:, sl]).astype(bf16)
                s = _dot_nt(q_m, k_b) * dmask_ref[h]
                st_h = st_pair[:, e * RET_V_DIM:(e + 1) * RET_V_DIM].astype(bf16)
                o = jnp.dot(s.astype(bf16), v_scr[rows, hs], preferred_element_type=f32)
                o = o + jnp.dot(q_dm, st_h, preferred_element_type=f32)
                mu = jnp.mean(o, axis=-1, keepdims=True)
                oc = o - mu
                var = jnp.mean(oc * oc, axis=-1, keepdims=True)
                on = oc * lax.rsqrt(var + EPS) * rng_ref[:, hs]
                rg = rg_scr[rows, hs]
                y_scr[rows, LRU_WIDTH + h * RET_V_DIM:LRU_WIDTH + (h + 1) * RET_V_DIM] = (
                    on * (rg * jax.nn.sigmoid(rg))).astype(bf16)
            kv = _dot_t0(k_d, v_scr[rows, p * 2 * RET_V_DIM:(p + 1) * 2 * RET_V_DIM])
            st_scr[p] = gst_ref[p] * st_pair + kv

    res = x_ref[...] + jnp.dot(y_scr[...], wout_ref[...], preferred_element_type=f32)
    o_ref[...] = _rmsnorm_rows(res, fng_ref[...])


def _const_spec(shape):
    nd = len(shape)
    return pl.BlockSpec(shape, lambda b, j, _nd=nd: (0,) * _nd, pipeline_mode=pl.Buffered(1))


def kernel(x, meta_tokens, norm_gain, w_in, conv_w, conv_b, w_rg, b_rg, w_ig, b_ig,
           lru_lambda, ret_norm_gain, w_out, final_norm_gain):
    B, S, D = x.shape
    assert D == D_MODEL and S % TILE_T == 0 and TILE_T % CHUNK == 0
    assert norm_gain.shape[0] == 1, "single-layer block"
    assert meta_tokens.shape == (N_META, D_MODEL)

    perm = _qk_column_perm()
    w = w_in[0]
    w_q = w[:, OFF_Q:OFF_K][:, perm]
    w_k = w[:, OFF_K:OFF_V][:, perm]
    win_b = jnp.concatenate([w[:, :OFF_Q], w_q, w_k, w[:, OFF_V:]], axis=1).astype(bf16)
    wg_b = jnp.concatenate([w_rg[0], w_ig[0]], axis=-1).astype(bf16)
    wout_b = w_out[0].astype(bf16)
    g1 = norm_gain[0].reshape(1, D_MODEL)
    fng = final_norm_gain.reshape(1, D_MODEL)
    convw = conv_w[0]
    convb = conv_b[0].reshape(1, LRU_WIDTH)
    brg = b_rg[0].reshape(1, LRU_WIDTH)
    big = b_ig[0].reshape(1, LRU_WIDTH)
    lam = lru_lambda[0].reshape(1, LRU_WIDTH)
    rng = ret_norm_gain[0].reshape(1, RET_WIDTH)

    dmask, k_dec, q_e, q_o, qd_e, qd_o, g_state = _retention_tables()
    cos_t, sin_t = _rotary_tables(N_META + S)

    n_lx = SUBLANES + N_META
    lx0, h0, st0 = pl.pallas_call(
        _meta_kernel,
        out_shape=(jax.ShapeDtypeStruct((SUBLANES, LRU_WIDTH), f32),
                   jax.ShapeDtypeStruct((SUBLANES, LRU_WIDTH), f32),
                   jax.ShapeDtypeStruct((N_PAIRS, LANES, 2 * RET_V_DIM), f32)),
        scratch_shapes=[pltpu.VMEM((n_lx, LRU_WIDTH), f32),
                        pltpu.VMEM((N_META, LRU_WIDTH), f32),
                        pltpu.VMEM((N_META, LRU_WIDTH), bf16),
                        pltpu.VMEM((N_META, 2 * LRU_WIDTH), f32)],
        compiler_params=pltpu.CompilerParams(vmem_limit_bytes=VMEM_LIMIT_BYTES),
        name="hybrid_meta_state",
    )(meta_tokens, cos_t[:N_META], sin_t[:N_META], g1, win_b[:, OFF_LX:OFF_LX + LRU_WIDTH],
      win_b[:, OFF_K:OFF_K + RET_QK_WIDTH], win_b[:, OFF_V:OFF_V + RET_WIDTH], convw, convb,
      wg_b, brg, big, lam, k_dec)

    tt = TILE_T
    n_tiles = S // tt
    row_spec = lambda width: pl.BlockSpec((tt, width), lambda b, j: (j, 0))
    in_specs = [
        pl.BlockSpec((None, tt, D_MODEL), lambda b, j: (b, j, 0)),
        row_spec(LANES), row_spec(LANES),
        _const_spec((1, D_MODEL)),
        _const_spec((D_MODEL, IN_WIDTH)),
        _const_spec((CONV_WIDTH, LRU_WIDTH)), _const_spec((1, LRU_WIDTH)),
        _const_spec((LRU_HEADS, LRU_BLOCK, 2 * LRU_BLOCK)),
        _const_spec((1, LRU_WIDTH)), _const_spec((1, LRU_WIDTH)), _const_spec((1, LRU_WIDTH)),
        _const_spec((1, RET_WIDTH)),
        _const_spec((MIX_WIDTH, D_MODEL)),
        _const_spec((1, D_MODEL)),
        _const_spec((RET_HEADS, CHUNK, CHUNK)),
        _const_spec((CHUNK, RET_QK_WIDTH)), _const_spec((CHUNK, RET_QK_WIDTH)), _const_spec((CHUNK, RET_QK_WIDTH)),
        _const_spec((CHUNK, RET_QK_WIDTH)), _const_spec((CHUNK, RET_QK_WIDTH)),
        _const_spec((N_PAIRS, LANES, 2 * RET_V_DIM)),
        _const_spec((SUBLANES, LRU_WIDTH)), _const_spec((SUBLANES, LRU_WIDTH)),
        _const_spec((N_PAIRS, LANES, 2 * RET_V_DIM)),
    ]
    scratch = [
        pltpu.VMEM((tt, D_MODEL), bf16),
        pltpu.VMEM((SUBLANES + tt, LRU_WIDTH), f32),
        pltpu.VMEM((tt, LRU_WIDTH), f32),
        pltpu.VMEM((tt, RET_QK_WIDTH), f32),
        pltpu.VMEM((tt, RET_QK_WIDTH), f32),
        pltpu.VMEM((tt, RET_WIDTH), bf16),
        pltpu.VMEM((tt, RET_WIDTH), f32),
        pltpu.VMEM((tt, LRU_WIDTH), f32),
        pltpu.VMEM((tt, LRU_WIDTH), bf16),
        pltpu.VMEM((tt, 2 * LRU_WIDTH), f32),
        pltpu.VMEM((tt, MIX_WIDTH), bf16),
        pltpu.VMEM((SUBLANES, LRU_WIDTH), f32),
        pltpu.VMEM((N_PAIRS, LANES, 2 * RET_V_DIM), f32),
    ]
    out = pl.pallas_call(
        _main_kernel,
        grid=(B, n_tiles),
        in_specs=in_specs,
        out_specs=pl.BlockSpec((None, tt, D_MODEL), lambda b, j: (b, j, 0)),
        out_shape=jax.ShapeDtypeStruct((B, S, D_MODEL), x.dtype),
        scratch_shapes=scratch,
        compiler_params=pltpu.CompilerParams(
            dimension_semantics=("arbitrary", "arbitrary"),
            vmem_limit_bytes=VMEM_LIMIT_BYTES),
        name="hybrid_main",
    )(x, cos_t[N_META:], sin_t[N_META:], g1, win_b, convw, convb, wg_b, brg, big, lam, rng, wout_b, fng,
      dmask, k_dec, q_e, q_o, qd_e, qd_o, g_state, lx0, h0, st0)
    return out
```

```python
import functools

import numpy as np
import jax
import jax.numpy as jnp
from jax import lax
from jax.experimental import pallas as pl
from jax.experimental.pallas import tpu as pltpu

f32 = jnp.float32
bf16 = jnp.bfloat16

D_MODEL = 1024
N_META = 16
LRU_WIDTH = 1024
LRU_HEADS = 8
LRU_BLOCK = 128
CONV_WIDTH = 4
LRU_C = 8.0
RET_HEADS = 8
RET_QK_DIM = 64
RET_V_DIM = 128
RET_QK_WIDTH = 512
RET_WIDTH = 1024
CHUNK = 128
ROPE_BASE = 10000.0
MIX_WIDTH = 2048
EPS = 1e-6

OFF_LX, OFF_GATE, OFF_Q, OFF_K, OFF_V, OFF_RG = 0, 1024, 2048, 2560, 3072, 4096
IN_WIDTH = 5120

LANES = 128
SUBLANES = 8
N_PAIRS = RET_HEADS // 2
TILE_T = 256
SCAN_ROWS = 16
VMEM_LIMIT_BYTES = 56 * 1024 * 1024


def _qk_column_perm():
    perm = np.zeros((RET_QK_WIDTH,), np.int32)
    half = RET_QK_DIM // 2
    for p in range(N_PAIRS):
        for l in range(LANES):
            grp, f = divmod(l, half)
            head = 2 * p + (grp % 2)
            d = f + half * (grp // 2)
            perm[p * LANES + l] = head * RET_QK_DIM + d
    return perm


def _lane_head():
    half = RET_QK_DIM // 2
    l = np.arange(RET_QK_WIDTH)
    return 2 * (l // LANES) + ((l % LANES) // half) % 2


def _retention_tables():
    log_g = jnp.log1p(-jnp.exp2(-5.0 - jnp.arange(RET_HEADS, dtype=f32)))
    idx = jnp.arange(CHUNK, dtype=f32)
    diff = idx[:, None] - idx[None, :]
    dmask = jnp.where(diff[None] >= 0.0,
                      jnp.exp(jnp.maximum(diff, 0.0)[None] * log_g[:, None, None]), 0.0)
    lane_head = _lane_head()
    lg_lane = log_g[lane_head]
    k_dec = jnp.exp((CHUNK - 1.0 - idx)[:, None] * lg_lane[None, :])
    q_dec = jnp.exp((idx + 1.0)[:, None] * lg_lane[None, :])
    scale = RET_QK_DIM ** -0.5
    even = jnp.asarray((lane_head % 2 == 0), f32)[None, :]
    odd = 1.0 - even
    ones = jnp.ones((CHUNK, 1), f32)
    q_e = ones * even * scale
    q_o = ones * odd * scale
    qd_e = q_dec * even * scale
    qd_o = q_dec * odd * scale
    g_chunk = jnp.exp(CHUNK * log_g)
    g_cols = jnp.repeat(g_chunk, RET_V_DIM)
    g_state = jnp.broadcast_to(g_cols.reshape(N_PAIRS, 1, 2 * RET_V_DIM), (N_PAIRS, LANES, 2 * RET_V_DIM))
    return dmask, k_dec, q_e, q_o, qd_e, qd_o, g_state


def _rotary_tables(n_pos):
    half = RET_QK_DIM // 2
    inv = ROPE_BASE ** (-jnp.arange(half, dtype=f32) / half)
    ang = jnp.arange(n_pos).astype(f32)[:, None] * inv[None, :]
    cos, sin = jnp.cos(ang), jnp.sin(ang)
    cos_t = jnp.concatenate([cos, cos, cos, cos], axis=-1)
    sin_t = jnp.concatenate([-sin, -sin, sin, sin], axis=-1)
    return cos_t, sin_t


def _rmsnorm_rows(x, gain_row):
    ms = jnp.mean(x * x, axis=-1, keepdims=True)
    return x * lax.rsqrt(ms + EPS) * gain_row


def _sigmoid(x):
    return 0.5 * jnp.tanh(0.5 * x) + 0.5


def _silu(x):
    hx = 0.5 * x
    return hx * jnp.tanh(hx) + hx


def _conv_and_gates(n_rows, lx_scr, convw_ref, convb_ref, wg_ref, xc_scr, xcb_scr, pre_scr):
    base = SUBLANES
    xc = convb_ref[...] + convw_ref[3:4, :] * lx_scr[pl.ds(base, n_rows), :]
    xc = xc + convw_ref[2:3, :] * lx_scr[pl.ds(base - 1, n_rows), :]
    xc = xc + convw_ref[1:2, :] * lx_scr[pl.ds(base - 2, n_rows), :]
    xc = xc + convw_ref[0:1, :] * lx_scr[pl.ds(base - 3, n_rows), :]
    xc_scr[...] = xc
    xcb_scr[...] = xc.astype(bf16)
    for h in range(LRU_HEADS):
        pre = jnp.dot(xcb_scr[:, h * LRU_BLOCK:(h + 1) * LRU_BLOCK], wg_ref[h], preferred_element_type=f32)
        pre_scr[:, h * LRU_BLOCK:(h + 1) * LRU_BLOCK] = pre[:, :LRU_BLOCK]
        pre_scr[:, LRU_WIDTH + h * LRU_BLOCK:LRU_WIDTH + (h + 1) * LRU_BLOCK] = pre[:, LRU_BLOCK:]


def _lru_block(xc, pre_r, pre_i, brg, big, c_sp, carry):
    rows = xc.shape[0]
    r = _sigmoid(pre_r + brg)
    i = _sigmoid(pre_i + big)
    nl = r * c_sp
    a = jnp.exp(-nl)
    z = jnp.tanh(nl) * (1.0 + a * a)
    beta = jnp.where(z > 0.0, z * lax.rsqrt(z), 0.0)
    b = beta * i * xc
    nv = rows // SUBLANES
    a3 = a.reshape(nv, SUBLANES, LRU_WIDTH)
    b3 = b.reshape(nv, SUBLANES, LRU_WIDTH)
    rowid = lax.broadcasted_iota(jnp.int32, (nv, SUBLANES, LRU_WIDTH), 1)
    for s in (1, 2, 4):
        keep = rowid >= s
        a_s = jnp.where(keep, pltpu.roll(a3, s, 1), 1.0)
        b_s = jnp.where(keep, pltpu.roll(b3, s, 1), 0.0)
        b3 = a3 * b_s + b3
        a3 = a3 * a_s
    hs = []
    for v in range(nv):
        h_v = a3[v] * carry + b3[v]
        carry = h_v[SUBLANES - 1:SUBLANES, :]
        hs.append(h_v)
    return jnp.concatenate(hs, axis=0), carry


def _rotary(t, cos, sin):
    return t * cos + pltpu.roll(t, LANES // 2, 1) * sin


def _dot_t0(a, b):
    return lax.dot_general(a, b, (((0,), (0,)), ((), ())), preferred_element_type=f32)


def _dot_nt(a, b):
    return lax.dot_general(a, b, (((1,), (1,)), ((), ())), preferred_element_type=f32)


def _meta_kernel(meta_ref, cos_ref, sin_ref, g1_ref, wlx_ref, wk_ref, wv_ref, convw_ref, convb_ref,
                 wg_ref, brg_ref, big_ref, lam_ref, kdec_ref,
                 lx_out, h_out, st_out,
                 lx_scr, xc_scr, xcb_scr, pre_scr):
    u = _rmsnorm_rows(meta_ref[...], g1_ref[...]).astype(bf16)
    lx_scr[0:SUBLANES, :] = jnp.zeros((SUBLANES, LRU_WIDTH), f32)
    lx_scr[SUBLANES:SUBLANES + N_META, :] = jnp.dot(u, wlx_ref[...], preferred_element_type=f32)
    k = jnp.dot(u, wk_ref[...], preferred_element_type=f32)
    v = jnp.dot(u, wv_ref[...], preferred_element_type=f32).astype(bf16)

    _conv_and_gates(N_META, lx_scr, convw_ref, convb_ref, wg_ref, xc_scr, xcb_scr, pre_scr)
    lx_out[...] = lx_scr[pl.ds(N_META, SUBLANES), :]
    c_sp = LRU_C * jax.nn.softplus(-lam_ref[...])
    _, carry = _lru_block(xc_scr[...], pre_scr[:, :LRU_WIDTH], pre_scr[:, LRU_WIDTH:],
                          brg_ref[...], big_ref[...], c_sp, jnp.zeros((1, LRU_WIDTH), f32))
    h_out[...] = jnp.broadcast_to(carry, (SUBLANES, LRU_WIDTH))

    cos, sin = cos_ref[...], sin_ref[...]
    for p in range(N_PAIRS):
        sl = slice(p * LANES, (p + 1) * LANES)
        k_rot = _rotary(k[:, sl], cos, sin)
        kd = (k_rot * kdec_ref[CHUNK - N_META:CHUNK, sl]).astype(bf16)
        st_out[p] = _dot_t0(kd, v[:, p * 2 * RET_V_DIM:(p + 1) * 2 * RET_V_DIM])


def _project_tile(x_ref, g1_ref, win_ref, u_scr, slot):
    lx_scr, gate_scr, q_scr, k_scr, v_scr, rg_scr = slot
    tt = TILE_T
    u_scr[...] = _rmsnorm_rows(x_ref[...], g1_ref[...]).astype(bf16)
    lx_scr[SUBLANES:SUBLANES + tt, :] = jnp.dot(u_scr[...], win_ref[:, OFF_LX:OFF_LX + LRU_WIDTH],
                                               preferred_element_type=f32)
    gate_scr[...] = jnp.dot(u_scr[...], win_ref[:, OFF_GATE:OFF_GATE + LRU_WIDTH], preferred_element_type=f32)
    q_scr[...] = jnp.dot(u_scr[...], win_ref[:, OFF_Q:OFF_Q + RET_QK_WIDTH], preferred_element_type=f32)
    k_scr[...] = jnp.dot(u_scr[...], win_ref[:, OFF_K:OFF_K + RET_QK_WIDTH], preferred_element_type=f32)
    v_scr[...] = jnp.dot(u_scr[...], win_ref[:, OFF_V:OFF_V + RET_WIDTH], preferred_element_type=f32).astype(bf16)
    rg_scr[...] = jnp.dot(u_scr[...], win_ref[:, OFF_RG:OFF_RG + RET_WIDTH], preferred_element_type=f32)


def _finish_tile(x_ref, cos_ref, sin_ref, convw_ref, convb_ref, wg_ref, brg_ref, big_ref, lam_ref, rng_ref,
                 wout_ref, fng_ref, dmask_ref, kdec_ref, qe_ref, qo_ref, qde_ref, qdo_ref, gst_ref,
                 o_ref, slot, xc_scr, xcb_scr, pre_scr, y_scr, lxtail_scr, h_scr, st_scr):
    lx_scr, gate_scr, q_scr, k_scr, v_scr, rg_scr = slot
    tt = TILE_T

    lx_scr[0:SUBLANES, :] = lxtail_scr[...]
    _conv_and_gates(tt, lx_scr, convw_ref, convb_ref, wg_ref, xc_scr, xcb_scr, pre_scr)
    lxtail_scr[...] = lx_scr[pl.ds(tt, SUBLANES), :]
    c_sp = LRU_C * jax.nn.softplus(-lam_ref[...])
    brg, big = brg_ref[...], big_ref[...]
    carry = h_scr[0:1, :]
    for it in range(tt // SCAN_ROWS):
        rows = slice(it * SCAN_ROWS, (it + 1) * SCAN_ROWS)
        h, carry = _lru_block(xc_scr[rows, :], pre_scr[rows, 0:LRU_WIDTH], pre_scr[rows, LRU_WIDTH:2 * LRU_WIDTH],
                              brg, big, c_sp, carry)
        y_scr[rows, 0:LRU_WIDTH] = (h * _silu(gate_scr[rows, :])).astype(bf16)
    h_scr[...] = jnp.broadcast_to(carry, (SUBLANES, LRU_WIDTH))

    for c in range(tt // CHUNK):
        rows = slice(c * CHUNK, (c + 1) * CHUNK)
        cos, sin = cos_ref[rows, :], sin_ref[rows, :]
        for p in range(N_PAIRS):
            sl = slice(p * LANES, (p + 1) * LANES)
            q_rot = _rotary(q_scr[rows, sl], cos, sin)
            k_rot = _rotary(k_scr[rows, sl], cos, sin)
            k_b = k_rot.astype(bf16)
            k_d = (k_rot * kdec_ref[:, sl]).astype(bf16)
            st_pair = st_scr[p]
            for e, (qm_ref, qdm_ref) in enumerate(((qe_ref, qde_ref), (qo_ref, qdo_ref))):
                h = 2 * p + e
                hs = slice(h * RET_V_DIM, (h + 1) * RET_V_DIM)
                q_m = (q_rot * qm_ref[:, sl]).astype(bf16)
                q_dm = (q_rot * qdm_ref[:, sl]).astype(bf16)
                s = _dot_nt(q_m, k_b) * dmask_ref[h]
                st_h = st_pair[:, e * RET_V_DIM:(e + 1) * RET_V_DIM].astype(bf16)
                o = jnp.dot(s.astype(bf16), v_scr[rows, hs], preferred_element_type=f32)
                o = o + jnp.dot(q_dm, st_h, preferred_element_type=f32)
                mu = jnp.mean(o, axis=-1, keepdims=True)
                oc = o - mu
                var = jnp.mean(oc * oc, axis=-1, keepdims=True)
                on = oc * lax.rsqrt(var + EPS) * rng_ref[:, hs]
                y_scr[rows, LRU_WIDTH + h * RET_V_DIM:LRU_WIDTH + (h + 1) * RET_V_DIM] = (
                    on * _silu(rg_scr[rows, hs])).astype(bf16)
            kv = _dot_t0(k_d, v_scr[rows, p * 2 * RET_V_DIM:(p + 1) * 2 * RET_V_DIM])
            st_scr[p] = gst_ref[p] * st_pair + kv

    res = x_ref[...] + jnp.dot(y_scr[...], wout_ref[...], preferred_element_type=f32)
    o_ref[...] = _rmsnorm_rows(res, fng_ref[...])


def _main_kernel(n_tiles,
                 xa_ref, xb_ref, cos_ref, sin_ref, g1_ref, win_ref, convw_ref, convb_ref, wg_ref, brg_ref, big_ref,
                 lam_ref, rng_ref, wout_ref, fng_ref, dmask_ref, kdec_ref, qe_ref, qo_ref, qde_ref, qdo_ref,
                 gst_ref, lx0_ref, h0_ref, st0_ref,
                 o_ref,
                 u_scr,
                 lx_a, gate_a, q_a, k_a, v_a, rg_a,
                 lx_b, gate_b, q_b, k_b, v_b, rg_b,
                 xc_scr, xcb_scr, pre_scr, y_scr, lxtail_scr, h_scr, st_scr):
    g = pl.program_id(0)
    slots = ((lx_a, gate_a, q_a, k_a, v_a, rg_a), (lx_b, gate_b, q_b, k_b, v_b, rg_b))

    @pl.when(g == 0)
    def _():
        for ref in slots[1]:
            ref[...] = jnp.zeros(ref.shape, ref.dtype)
        lxtail_scr[...] = jnp.zeros(lxtail_scr.shape, f32)
        h_scr[...] = jnp.zeros(h_scr.shape, f32)
        st_scr[...] = jnp.zeros(st_scr.shape, f32)

    @pl.when(lax.rem(g + n_tiles - 1, n_tiles) == 0)
    def _():
        lxtail_scr[...] = lx0_ref[...]
        h_scr[...] = h0_ref[...]
        st_scr[...] = st0_ref[...]

    def step(slot_project, slot_finish):
        _project_tile(xa_ref, g1_ref, win_ref, u_scr, slot_project)
        _finish_tile(xb_ref, cos_ref, sin_ref, convw_ref, convb_ref, wg_ref, brg_ref, big_ref, lam_ref, rng_ref,
                     wout_ref, fng_ref, dmask_ref, kdec_ref, qe_ref, qo_ref, qde_ref, qdo_ref, gst_ref,
                     o_ref, slot_finish, xc_scr, xcb_scr, pre_scr, y_scr, lxtail_scr, h_scr, st_scr)

    @pl.when(lax.rem(g, 2) == 0)
    def _():
        step(slots[0], slots[1])

    @pl.when(lax.rem(g, 2) == 1)
    def _():
        step(slots[1], slots[0])


def _const_spec(shape):
    nd = len(shape)
    return pl.BlockSpec(shape, lambda g, _nd=nd: (0,) * _nd, pipeline_mode=pl.Buffered(1))


def kernel(x, meta_tokens, norm_gain, w_in, conv_w, conv_b, w_rg, b_rg, w_ig, b_ig,
           lru_lambda, ret_norm_gain, w_out, final_norm_gain):
    B, S, D = x.shape
    assert D == D_MODEL and S % TILE_T == 0 and TILE_T % CHUNK == 0
    assert norm_gain.shape[0] == 1, "single-layer block"
    assert meta_tokens.shape == (N_META, D_MODEL)

    perm = _qk_column_perm()
    w = w_in[0]
    w_q = w[:, OFF_Q:OFF_K][:, perm]
    w_k = w[:, OFF_K:OFF_V][:, perm]
    win_b = jnp.concatenate([w[:, :OFF_Q], w_q, w_k, w[:, OFF_V:]], axis=1).astype(bf16)
    wg_b = jnp.concatenate([w_rg[0], w_ig[0]], axis=-1).astype(bf16)
    wout_b = w_out[0].astype(bf16)
    g1 = norm_gain[0].reshape(1, D_MODEL)
    fng = final_norm_gain.reshape(1, D_MODEL)
    convw = conv_w[0]
    convb = conv_b[0].reshape(1, LRU_WIDTH)
    brg = b_rg[0].reshape(1, LRU_WIDTH)
    big = b_ig[0].reshape(1, LRU_WIDTH)
    lam = lru_lambda[0].reshape(1, LRU_WIDTH)
    rng = ret_norm_gain[0].reshape(1, RET_WIDTH)

    dmask, k_dec, q_e, q_o, qd_e, qd_o, g_state = _retention_tables()
    cos_t, sin_t = _rotary_tables(N_META + S)

    n_lx = SUBLANES + N_META
    lx0, h0, st0 = pl.pallas_call(
        _meta_kernel,
        out_shape=(jax.ShapeDtypeStruct((SUBLANES, LRU_WIDTH), f32),
                   jax.ShapeDtypeStruct((SUBLANES, LRU_WIDTH), f32),
                   jax.ShapeDtypeStruct((N_PAIRS, LANES, 2 * RET_V_DIM), f32)),
        scratch_shapes=[pltpu.VMEM((n_lx, LRU_WIDTH), f32),
                        pltpu.VMEM((N_META, LRU_WIDTH), f32),
                        pltpu.VMEM((N_META, LRU_WIDTH), bf16),
                        pltpu.VMEM((N_META, 2 * LRU_WIDTH), f32)],
        compiler_params=pltpu.CompilerParams(vmem_limit_bytes=VMEM_LIMIT_BYTES),
        name="hybrid_meta_state",
    )(meta_tokens, cos_t[:N_META], sin_t[:N_META], g1, win_b[:, OFF_LX:OFF_LX + LRU_WIDTH],
      win_b[:, OFF_K:OFF_K + RET_QK_WIDTH], win_b[:, OFF_V:OFF_V + RET_WIDTH], convw, convb,
      wg_b, brg, big, lam, k_dec)

    tt = TILE_T
    n_tiles = S // tt
    n_total = B * n_tiles

    def proj_idx(g):
        gp = jnp.minimum(g, n_total - 1)
        return gp // n_tiles, gp % n_tiles

    def fin_idx(g):
        gf = jnp.maximum(g - 1, 0)
        return gf // n_tiles, gf % n_tiles

    rot_spec = pl.BlockSpec((tt, LANES), lambda g: (fin_idx(g)[1], 0))
    in_specs = [
        pl.BlockSpec((None, tt, D_MODEL), lambda g: (*proj_idx(g), 0)),
        pl.BlockSpec((None, tt, D_MODEL), lambda g: (*fin_idx(g), 0)),
        rot_spec, rot_spec,
        _const_spec((1, D_MODEL)),
        _const_spec((D_MODEL, IN_WIDTH)),
        _const_spec((CONV_WIDTH, LRU_WIDTH)), _const_spec((1, LRU_WIDTH)),
        _const_spec((LRU_HEADS, LRU_BLOCK, 2 * LRU_BLOCK)),
        _const_spec((1, LRU_WIDTH)), _const_spec((1, LRU_WIDTH)), _const_spec((1, LRU_WIDTH)),
        _const_spec((1, RET_WIDTH)),
        _const_spec((MIX_WIDTH, D_MODEL)),
        _const_spec((1, D_MODEL)),
        _const_spec((RET_HEADS, CHUNK, CHUNK)),
        _const_spec((CHUNK, RET_QK_WIDTH)), _const_spec((CHUNK, RET_QK_WIDTH)), _const_spec((CHUNK, RET_QK_WIDTH)),
        _const_spec((CHUNK, RET_QK_WIDTH)), _const_spec((CHUNK, RET_QK_WIDTH)),
        _const_spec((N_PAIRS, LANES, 2 * RET_V_DIM)),
        _const_spec((SUBLANES, LRU_WIDTH)), _const_spec((SUBLANES, LRU_WIDTH)),
        _const_spec((N_PAIRS, LANES, 2 * RET_V_DIM)),
    ]
    slot_scratch = [
        pltpu.VMEM((SUBLANES + tt, LRU_WIDTH), f32),
        pltpu.VMEM((tt, LRU_WIDTH), f32),
        pltpu.VMEM((tt, RET_QK_WIDTH), f32),
        pltpu.VMEM((tt, RET_QK_WIDTH), f32),
        pltpu.VMEM((tt, RET_WIDTH), bf16),
        pltpu.VMEM((tt, RET_WIDTH), f32),
    ]
    scratch = [pltpu.VMEM((tt, D_MODEL), bf16)] + slot_scratch + slot_scratch + [
        pltpu.VMEM((tt, LRU_WIDTH), f32),
        pltpu.VMEM((tt, LRU_WIDTH), bf16),
        pltpu.VMEM((tt, 2 * LRU_WIDTH), f32),
        pltpu.VMEM((tt, MIX_WIDTH), bf16),
        pltpu.VMEM((SUBLANES, LRU_WIDTH), f32),
        pltpu.VMEM((SUBLANES, LRU_WIDTH), f32),
        pltpu.VMEM((N_PAIRS, LANES, 2 * RET_V_DIM), f32),
    ]
    out = pl.pallas_call(
        functools.partial(_main_kernel, n_tiles),
        grid=(n_total + 1,),
        in_specs=in_specs,
        out_specs=pl.BlockSpec((None, tt, D_MODEL), lambda g: (*fin_idx(g), 0)),
        out_shape=jax.ShapeDtypeStruct((B, S, D_MODEL), x.dtype),
        scratch_shapes=scratch,
        compiler_params=pltpu.CompilerParams(
            dimension_semantics=("arbitrary",),
            vmem_limit_bytes=VMEM_LIMIT_BYTES),
        name="hybrid_main",
    )(x, x, cos_t[N_META:], sin_t[N_META:], g1, win_b, convw, convb, wg_b, brg, big, lam, rng, wout_b, fng,
      dmask, k_dec, q_e, q_o, qd_e, qd_o, g_state, lx0, h0, st0)
    return out
```

```python
import functools

import numpy as np
import jax
import jax.numpy as jnp
from jax import lax
from jax.experimental import pallas as pl
from jax.experimental.pallas import tpu as pltpu

f32 = jnp.float32
bf16 = jnp.bfloat16

D_MODEL = 1024
N_META = 16
LRU_WIDTH = 1024
LRU_HEADS = 8
LRU_BLOCK = 128
CONV_WIDTH = 4
LRU_C = 8.0
RET_HEADS = 8
RET_QK_DIM = 64
RET_V_DIM = 128
RET_QK_WIDTH = 512
RET_WIDTH = 1024
CHUNK = 128
ROPE_BASE = 10000.0
MIX_WIDTH = 2048
EPS = 1e-6
QK_SCALE = RET_QK_DIM ** -0.5

OFF_LX, OFF_GATE, OFF_Q, OFF_K, OFF_V, OFF_RG = 0, 1024, 2048, 2560, 3072, 4096
IN_WIDTH = 5120

LANES = 128
SUBLANES = 8
N_PAIRS = RET_HEADS // 2
PAIR_V = 2 * RET_V_DIM
TILE_T = 256
N_CHUNKS = TILE_T // CHUNK
SCAN_ROWS = 16
VMEM_LIMIT_BYTES = 56 * 1024 * 1024


def _qk_column_perm():
    perm = np.zeros((RET_QK_WIDTH,), np.int32)
    half = RET_QK_DIM // 2
    for p in range(N_PAIRS):
        for l in range(LANES):
            grp, f = divmod(l, half)
            head = 2 * p + (grp % 2)
            d = f + half * (grp // 2)
            perm[p * LANES + l] = head * RET_QK_DIM + d
    return perm


def _lane_head():
    half = RET_QK_DIM // 2
    l = np.arange(RET_QK_WIDTH)
    return 2 * (l // LANES) + ((l % LANES) // half) % 2


def _retention_tables():
    log_g = jnp.log1p(-jnp.exp2(-5.0 - jnp.arange(RET_HEADS, dtype=f32)))
    idx = jnp.arange(CHUNK, dtype=f32)
    diff = idx[:, None] - idx[None, :]
    dmask = jnp.where(diff[None] >= 0.0,
                      jnp.exp(jnp.maximum(diff, 0.0)[None] * log_g[:, None, None]), 0.0)
    dmask_pair = jnp.concatenate([dmask[0::2], dmask[1::2]], axis=-1)
    lane_head = _lane_head()
    lg_lane = log_g[lane_head]
    k_dec = jnp.exp((CHUNK - 1.0 - idx)[:, None] * lg_lane[None, :])
    q_dec = jnp.exp((idx + 1.0)[:, None] * lg_lane[None, :]) * QK_SCALE
    even = jnp.asarray((lane_head % 2 == 0), f32)[None, :]
    odd = 1.0 - even
    g_chunk = jnp.exp(CHUNK * log_g)
    g_cols = jnp.repeat(g_chunk, RET_V_DIM)
    g_state = jnp.broadcast_to(g_cols.reshape(N_PAIRS, 1, PAIR_V), (N_PAIRS, LANES, PAIR_V))
    row_par = (np.arange(LANES) // (RET_QK_DIM // 2)) % 2
    col_par = np.arange(PAIR_V) // RET_V_DIM
    bd_mask = jnp.asarray(row_par[:, None] == col_par[None, :], f32)
    return dmask_pair, k_dec, q_dec, even, odd, g_state, bd_mask


def _rotary_tables(n_pos):
    half = RET_QK_DIM // 2
    inv = ROPE_BASE ** (-jnp.arange(half, dtype=f32) / half)
    ang = jnp.arange(n_pos).astype(f32)[:, None] * inv[None, :]
    cos, sin = jnp.cos(ang), jnp.sin(ang)
    cos_t = jnp.concatenate([cos, cos, cos, cos], axis=-1)
    sin_t = jnp.concatenate([-sin, -sin, sin, sin], axis=-1)
    return cos_t, sin_t


def _rmsnorm_rows(x, gain_row):
    ms = jnp.mean(x * x, axis=-1, keepdims=True)
    return x * lax.rsqrt(ms + EPS) * gain_row


def _sigmoid(x):
    return 0.5 * jnp.tanh(0.5 * x) + 0.5


def _silu(x):
    hx = 0.5 * x
    return hx * jnp.tanh(hx) + hx


def _conv(n_rows, lx_scr, convw_ref, convb_ref, xc_scr, xcb_scr):
    base = SUBLANES
    xc = convb_ref[...] + convw_ref[3:4, :] * lx_scr[pl.ds(base, n_rows), :]
    xc = xc + convw_ref[2:3, :] * lx_scr[pl.ds(base - 1, n_rows), :]
    xc = xc + convw_ref[1:2, :] * lx_scr[pl.ds(base - 2, n_rows), :]
    xc = xc + convw_ref[0:1, :] * lx_scr[pl.ds(base - 3, n_rows), :]
    xc_scr[...] = xc
    xcb_scr[...] = xc.astype(bf16)


def _gates(xcb_scr, wg_ref, pre_scr):
    for h in range(LRU_HEADS):
        pre = jnp.dot(xcb_scr[:, h * LRU_BLOCK:(h + 1) * LRU_BLOCK], wg_ref[h], preferred_element_type=f32)
        pre_scr[:, h * LRU_BLOCK:(h + 1) * LRU_BLOCK] = pre[:, :LRU_BLOCK]
        pre_scr[:, LRU_WIDTH + h * LRU_BLOCK:LRU_WIDTH + (h + 1) * LRU_BLOCK] = pre[:, LRU_BLOCK:]


def _lru_block(xc, pre_r, pre_i, brg, big, c_sp, carry):
    rows, width = xc.shape
    r = _sigmoid(pre_r + brg)
    i = _sigmoid(pre_i + big)
    nl = r * c_sp
    a = jnp.exp(-nl)
    z = jnp.tanh(nl) * (1.0 + a * a)
    beta = jnp.where(z > 0.0, z * lax.rsqrt(z), 0.0)
    b = beta * i * xc
    nv = rows // SUBLANES
    a3 = a.reshape(nv, SUBLANES, width)
    b3 = b.reshape(nv, SUBLANES, width)
    rowid = lax.broadcasted_iota(jnp.int32, (nv, SUBLANES, width), 1)
    for s in (1, 2, 4):
        keep = rowid >= s
        a_s = jnp.where(keep, pltpu.roll(a3, s, 1), 1.0)
        b_s = jnp.where(keep, pltpu.roll(b3, s, 1), 0.0)
        b3 = a3 * b_s + b3
        a3 = a3 * a_s
    hs = []
    for v in range(nv):
        h_v = a3[v] * carry + b3[v]
        carry = h_v[SUBLANES - 1:SUBLANES, :]
        hs.append(h_v)
    return jnp.concatenate(hs, axis=0), carry


def _rotary(t, cos, sin):
    return t * cos + pltpu.roll(t, LANES // 2, 1) * sin


def _dot_t0(a, b):
    return lax.dot_general(a, b, (((0,), (0,)), ((), ())), preferred_element_type=f32)


def _dot_nt(a, b):
    return lax.dot_general(a, b, (((1,), (1,)), ((), ())), preferred_element_type=f32)


def _meta_kernel(meta_ref, cos_ref, sin_ref, g1_ref, wlx_ref, wk_ref, wv_ref, convw_ref, convb_ref,
                 wg_ref, brg_ref, big_ref, lam_ref, kdec_ref, bdm_ref,
                 lx_out, h_out, st_out,
                 lx_scr, xc_scr, xcb_scr, pre_scr):
    u = _rmsnorm_rows(meta_ref[...], g1_ref[...]).astype(bf16)
    lx_scr[0:SUBLANES, :] = jnp.zeros((SUBLANES, LRU_WIDTH), f32)
    lx_scr[SUBLANES:SUBLANES + N_META, :] = jnp.dot(u, wlx_ref[...], preferred_element_type=f32)
    k = jnp.dot(u, wk_ref[...], preferred_element_type=f32)
    v = jnp.dot(u, wv_ref[...], preferred_element_type=f32).astype(bf16)

    _conv(N_META, lx_scr, convw_ref, convb_ref, xc_scr, xcb_scr)
    _gates(xcb_scr, wg_ref, pre_scr)
    lx_out[...] = lx_scr[pl.ds(N_META, SUBLANES), :]
    c_sp = LRU_C * jax.nn.softplus(-lam_ref[...])
    _, carry = _lru_block(xc_scr[...], pre_scr[:, :LRU_WIDTH], pre_scr[:, LRU_WIDTH:],
                          brg_ref[...], big_ref[...], c_sp, jnp.zeros((1, LRU_WIDTH), f32))
    h_out[...] = jnp.broadcast_to(carry, (SUBLANES, LRU_WIDTH))

    cos, sin = cos_ref[...], sin_ref[...]
    for p in range(N_PAIRS):
        sl = slice(p * LANES, (p + 1) * LANES)
        k_rot = _rotary(k[:, sl], cos, sin)
        kd = (k_rot * kdec_ref[CHUNK - N_META:CHUNK, sl]).astype(bf16)
        st_out[p] = _dot_t0(kd, v[:, p * PAIR_V:(p + 1) * PAIR_V]) * bdm_ref[...]


def _norm_tile(x_ref, g1_ref, u_scr):
    u_scr[...] = _rmsnorm_rows(x_ref[...], g1_ref[...]).astype(bf16)


def _project_cols(u_scr, win_ref, proj_scr, cols):
    proj_scr[SUBLANES:SUBLANES + TILE_T, cols] = jnp.dot(u_scr[...], win_ref[:, cols], preferred_element_type=f32)


def _prepare_tile(proj_scr, cos_ref, sin_ref, convw_ref, convb_ref, kdec_ref, qdec_ref, even_ref, odd_ref,
                  lxtail_scr, mix):
    xc_scr, xcb_scr, qb_scr, qd_scr, kk_scr, kd_scr, vbd_scr, vb_scr = mix
    lx_view = proj_scr.at[:, OFF_LX:OFF_LX + LRU_WIDTH]
    lx_view[0:SUBLANES, :] = lxtail_scr[...]
    _conv(TILE_T, lx_view, convw_ref, convb_ref, xc_scr, xcb_scr)
    lxtail_scr[...] = lx_view[pl.ds(TILE_T, SUBLANES), :]

    body = slice(SUBLANES, SUBLANES + TILE_T)
    vb_scr[...] = proj_scr[body, OFF_V:OFF_V + RET_WIDTH].astype(bf16)
    for c in range(N_CHUNKS):
        rows = slice(c * CHUNK, (c + 1) * CHUNK)
        prow = slice(SUBLANES + c * CHUNK, SUBLANES + (c + 1) * CHUNK)
        cos, sin = cos_ref[rows, :], sin_ref[rows, :]
        for p in range(N_PAIRS):
            sl = slice(p * LANES, (p + 1) * LANES)
            idx = c * N_PAIRS + p
            q_rot = _rotary(proj_scr[prow, OFF_Q + p * LANES:OFF_Q + (p + 1) * LANES], cos, sin)
            k_rot = _rotary(proj_scr[prow, OFF_K + p * LANES:OFF_K + (p + 1) * LANES], cos, sin)
            qb_scr[rows, sl] = (q_rot * QK_SCALE).astype(bf16)
            qd_scr[rows, sl] = (q_rot * qdec_ref[:, sl]).astype(bf16)
            kd_scr[rows, sl] = (k_rot * kdec_ref[:, sl]).astype(bf16)
            kk_scr[idx, 0:CHUNK, :] = (k_rot * even_ref[:, sl]).astype(bf16)
            kk_scr[idx, CHUNK:2 * CHUNK, :] = (k_rot * odd_ref[:, sl]).astype(bf16)
            vbd_scr[idx, 0:CHUNK, 0:RET_V_DIM] = vb_scr[rows, p * PAIR_V:p * PAIR_V + RET_V_DIM]
            vbd_scr[idx, CHUNK:2 * CHUNK, RET_V_DIM:PAIR_V] = vb_scr[rows, p * PAIR_V + RET_V_DIM:(p + 1) * PAIR_V]


def _gates_head(h, xcb_scr, wg_ref, pre_scr):
    lanes = slice(h * LRU_BLOCK, (h + 1) * LRU_BLOCK)
    pre = jnp.dot(xcb_scr[:, lanes], wg_ref[h], preferred_element_type=f32)
    pre_scr[:, lanes] = pre[:, :LRU_BLOCK]
    pre_scr[:, LRU_WIDTH + h * LRU_BLOCK:LRU_WIDTH + (h + 1) * LRU_BLOCK] = pre[:, LRU_BLOCK:]


def _scan_head(h, brg_ref, big_ref, lam_ref, proj_scr, xc_scr, pre_scr, y_scr, h_scr):
    lanes = slice(h * LRU_BLOCK, (h + 1) * LRU_BLOCK)
    ilanes = slice(LRU_WIDTH + h * LRU_BLOCK, LRU_WIDTH + (h + 1) * LRU_BLOCK)
    glanes = slice(OFF_GATE + h * LRU_BLOCK, OFF_GATE + (h + 1) * LRU_BLOCK)
    c_sp = LRU_C * jax.nn.softplus(-lam_ref[:, lanes])
    brg, big = brg_ref[:, lanes], big_ref[:, lanes]
    carry = h_scr[0:1, lanes]
    for blk in range(TILE_T // SCAN_ROWS):
        rows = slice(blk * SCAN_ROWS, (blk + 1) * SCAN_ROWS)
        prow = slice(SUBLANES + blk * SCAN_ROWS, SUBLANES + (blk + 1) * SCAN_ROWS)
        hb, carry = _lru_block(xc_scr[rows, lanes], pre_scr[rows, lanes], pre_scr[rows, ilanes], brg, big, c_sp, carry)
        y_scr[rows, lanes] = (hb * _silu(proj_scr[prow, glanes])).astype(bf16)
    h_scr[:, lanes] = jnp.broadcast_to(carry, (SUBLANES, LRU_BLOCK))


def _retention_tile(rng_ref, dmaskp_ref, gst_ref, bdm_ref, proj_scr, mix, y_scr, st_scr):
    _, _, qb_scr, qd_scr, kk_scr, kd_scr, vbd_scr, vb_scr = mix
    for c in range(N_CHUNKS):
        rows = slice(c * CHUNK, (c + 1) * CHUNK)
        prow = slice(SUBLANES + c * CHUNK, SUBLANES + (c + 1) * CHUNK)
        for p in range(N_PAIRS):
            sl = slice(p * LANES, (p + 1) * LANES)
            idx = c * N_PAIRS + p
            st_pair = st_scr[p]
            s = _dot_nt(qb_scr[rows, sl], kk_scr[idx]) * dmaskp_ref[p]
            o = jnp.dot(s.astype(bf16), vbd_scr[idx], preferred_element_type=f32)
            o = o + jnp.dot(qd_scr[rows, sl], st_pair.astype(bf16), preferred_element_type=f32)
            for e in range(2):
                h = 2 * p + e
                hs = slice(h * RET_V_DIM, (h + 1) * RET_V_DIM)
                o_h = o[:, e * RET_V_DIM:(e + 1) * RET_V_DIM]
                mu = jnp.mean(o_h, axis=-1, keepdims=True)
                oc = o_h - mu
                var = jnp.mean(oc * oc, axis=-1, keepdims=True)
                on = oc * lax.rsqrt(var + EPS) * rng_ref[:, hs]
                rg = proj_scr[prow, OFF_RG + h * RET_V_DIM:OFF_RG + (h + 1) * RET_V_DIM]
                y_scr[rows, LRU_WIDTH + h * RET_V_DIM:LRU_WIDTH + (h + 1) * RET_V_DIM] = (on * _silu(rg)).astype(bf16)
            kv = _dot_t0(kd_scr[rows, sl], vb_scr[rows, p * PAIR_V:(p + 1) * PAIR_V])
            st_scr[p] = gst_ref[p] * st_pair + bdm_ref[...] * kv


def _output_cols(y_scr, wout_ref, res_scr, cols):
    res_scr[:, cols] = jnp.dot(y_scr[...], wout_ref[:, cols], preferred_element_type=f32)


N_MIX = 8
MXU_COLS = 256
HEAD_TILES = ((3, 0), (3, 0), (3, 0), (3, 0), (2, 1), (2, 1), (2, 1), (2, 1))
assert sum(t[0] for t in HEAD_TILES) * MXU_COLS == IN_WIDTH and sum(t[1] for t in HEAD_TILES) * MXU_COLS == D_MODEL


def _main_kernel(n_tiles,
                 xa_ref, xb_ref, cos_ref, sin_ref, g1_ref, win_ref, convw_ref, convb_ref, wg_ref, brg_ref, big_ref,
                 lam_ref, rng_ref, wout_ref, fng_ref, dmaskp_ref, kdec_ref, qdec_ref, even_ref, odd_ref,
                 gst_ref, bdm_ref, lx0_ref, h0_ref, st0_ref,
                 o_ref, *scratch):
    u_slots = scratch[0:2]
    proj_slots = scratch[2:4]
    mix_slots = (scratch[4:4 + N_MIX], scratch[4 + N_MIX:4 + 2 * N_MIX])
    base = 4 + 2 * N_MIX
    y_slots = scratch[base:base + 2]
    pre_scr, res_scr, lxtail_scr, h_scr, st_scr = scratch[base + 2:]
    g = pl.program_id(0)

    @pl.when(g == 0)
    def _():
        for ref in (u_slots[1], proj_slots[0]) + mix_slots[0] + (mix_slots[1][6], y_slots[1]):
            ref[...] = jnp.zeros(ref.shape, ref.dtype)
        lxtail_scr[...] = jnp.zeros(lxtail_scr.shape, f32)
        h_scr[...] = jnp.zeros(h_scr.shape, f32)
        st_scr[...] = jnp.zeros(st_scr.shape, f32)

    @pl.when(lax.rem(g + n_tiles - 1, n_tiles) == 0)
    def _():
        lxtail_scr[...] = lx0_ref[...]

    @pl.when(lax.rem(g + 2 * n_tiles - 2, n_tiles) == 0)
    def _():
        h_scr[...] = h0_ref[...]
        st_scr[...] = st0_ref[...]

    def step(cur, prev):
        u_new, u_old = u_slots[cur], u_slots[prev]
        proj_new, proj_old = proj_slots[prev], proj_slots[cur]
        mix_new, mix_old = mix_slots[prev], mix_slots[cur]
        y_new, y_old = y_slots[cur], y_slots[prev]
        xc_scr = mix_old[0]

        _norm_tile(xa_ref, g1_ref, u_new)
        in_tile = out_tile = 0
        for h, (n_in, n_out) in enumerate(HEAD_TILES):
            _gates_head(h, mix_old[1], wg_ref, pre_scr)
            for _ in range(n_in):
                _project_cols(u_old, win_ref, proj_new, slice(in_tile * MXU_COLS, (in_tile + 1) * MXU_COLS))
                in_tile += 1
            for _ in range(n_out):
                _output_cols(y_old, wout_ref, res_scr, slice(out_tile * MXU_COLS, (out_tile + 1) * MXU_COLS))
                out_tile += 1
            _scan_head(h, brg_ref, big_ref, lam_ref, proj_old, xc_scr, pre_scr, y_new, h_scr)

        _prepare_tile(proj_new, cos_ref, sin_ref, convw_ref, convb_ref, kdec_ref, qdec_ref, even_ref, odd_ref,
                      lxtail_scr, mix_new)
        _retention_tile(rng_ref, dmaskp_ref, gst_ref, bdm_ref, proj_old, mix_old, y_new, st_scr)
        o_ref[...] = _rmsnorm_rows(xb_ref[...] + res_scr[...], fng_ref[...])

    @pl.when(lax.rem(g, 2) == 0)
    def _():
        step(0, 1)

    @pl.when(lax.rem(g, 2) == 1)
    def _():
        step(1, 0)


def _const_spec(shape):
    nd = len(shape)
    return pl.BlockSpec(shape, lambda g, _nd=nd: (0,) * _nd, pipeline_mode=pl.Buffered(1))


def kernel(x, meta_tokens, norm_gain, w_in, conv_w, conv_b, w_rg, b_rg, w_ig, b_ig,
           lru_lambda, ret_norm_gain, w_out, final_norm_gain):
    B, S, D = x.shape
    assert D == D_MODEL and S % TILE_T == 0 and TILE_T % CHUNK == 0
    assert norm_gain.shape[0] == 1, "single-layer block"
    assert meta_tokens.shape == (N_META, D_MODEL)

    perm = _qk_column_perm()
    w = w_in[0]
    w_q = w[:, OFF_Q:OFF_K][:, perm]
    w_k = w[:, OFF_K:OFF_V][:, perm]
    win_b = jnp.concatenate([w[:, :OFF_Q], w_q, w_k, w[:, OFF_V:]], axis=1).astype(bf16)
    wg_b = jnp.concatenate([w_rg[0], w_ig[0]], axis=-1).astype(bf16)
    wout_b = w_out[0].astype(bf16)
    g1 = norm_gain[0].reshape(1, D_MODEL)
    fng = final_norm_gain.reshape(1, D_MODEL)
    convw = conv_w[0]
    convb = conv_b[0].reshape(1, LRU_WIDTH)
    brg = b_rg[0].reshape(1, LRU_WIDTH)
    big = b_ig[0].reshape(1, LRU_WIDTH)
    lam = lru_lambda[0].reshape(1, LRU_WIDTH)
    rng = ret_norm_gain[0].reshape(1, RET_WIDTH)

    dmask_pair, k_dec, q_dec, even, odd, g_state, bd_mask = _retention_tables()
    cos_t, sin_t = _rotary_tables(N_META + S)

    n_lx = SUBLANES + N_META
    lx0, h0, st0 = pl.pallas_call(
        _meta_kernel,
        out_shape=(jax.ShapeDtypeStruct((SUBLANES, LRU_WIDTH), f32),
                   jax.ShapeDtypeStruct((SUBLANES, LRU_WIDTH), f32),
                   jax.ShapeDtypeStruct((N_PAIRS, LANES, PAIR_V), f32)),
        scratch_shapes=[pltpu.VMEM((n_lx, LRU_WIDTH), f32),
                        pltpu.VMEM((N_META, LRU_WIDTH), f32),
                        pltpu.VMEM((N_META, LRU_WIDTH), bf16),
                        pltpu.VMEM((N_META, 2 * LRU_WIDTH), f32)],
        compiler_params=pltpu.CompilerParams(vmem_limit_bytes=VMEM_LIMIT_BYTES),
        name="hybrid_meta_state",
    )(meta_tokens, cos_t[:N_META], sin_t[:N_META], g1, win_b[:, OFF_LX:OFF_LX + LRU_WIDTH],
      win_b[:, OFF_K:OFF_K + RET_QK_WIDTH], win_b[:, OFF_V:OFF_V + RET_WIDTH], convw, convb,
      wg_b, brg, big, lam, k_dec, bd_mask)

    tt = TILE_T
    n_tiles = S // tt
    n_total = B * n_tiles

    def tile_idx(g, lag):
        t = jnp.clip(g - lag, 0, n_total - 1)
        return t // n_tiles, t % n_tiles

    rot_spec = pl.BlockSpec((tt, LANES), lambda g: (tile_idx(g, 1)[1], 0))
    in_specs = [
        pl.BlockSpec((None, tt, D_MODEL), lambda g: (*tile_idx(g, 0), 0)),
        pl.BlockSpec((None, tt, D_MODEL), lambda g: (*tile_idx(g, 3), 0)),
        rot_spec, rot_spec,
        _const_spec((1, D_MODEL)),
        _const_spec((D_MODEL, IN_WIDTH)),
        _const_spec((CONV_WIDTH, LRU_WIDTH)), _const_spec((1, LRU_WIDTH)),
        _const_spec((LRU_HEADS, LRU_BLOCK, 2 * LRU_BLOCK)),
        _const_spec((1, LRU_WIDTH)), _const_spec((1, LRU_WIDTH)), _const_spec((1, LRU_WIDTH)),
        _const_spec((1, RET_WIDTH)),
        _const_spec((MIX_WIDTH, D_MODEL)),
        _const_spec((1, D_MODEL)),
        _const_spec((N_PAIRS, CHUNK, 2 * CHUNK)),
        _const_spec((CHUNK, RET_QK_WIDTH)), _const_spec((CHUNK, RET_QK_WIDTH)),
        _const_spec((1, RET_QK_WIDTH)), _const_spec((1, RET_QK_WIDTH)),
        _const_spec((N_PAIRS, LANES, PAIR_V)),
        _const_spec((LANES, PAIR_V)),
        _const_spec((SUBLANES, LRU_WIDTH)), _const_spec((SUBLANES, LRU_WIDTH)),
        _const_spec((N_PAIRS, LANES, PAIR_V)),
    ]
    mix_scratch = [
        pltpu.VMEM((tt, LRU_WIDTH), f32),
        pltpu.VMEM((tt, LRU_WIDTH), bf16),
        pltpu.VMEM((tt, RET_QK_WIDTH), bf16),
        pltpu.VMEM((tt, RET_QK_WIDTH), bf16),
        pltpu.VMEM((N_CHUNKS * N_PAIRS, 2 * CHUNK, LANES), bf16),
        pltpu.VMEM((tt, RET_QK_WIDTH), bf16),
        pltpu.VMEM((N_CHUNKS * N_PAIRS, 2 * CHUNK, PAIR_V), bf16),
        pltpu.VMEM((tt, RET_WIDTH), bf16),
    ]
    assert len(mix_scratch) == N_MIX
    scratch = (
        [pltpu.VMEM((tt, D_MODEL), bf16)] * 2
        + [pltpu.VMEM((SUBLANES + tt, IN_WIDTH), f32)] * 2
        + mix_scratch * 2
        + [pltpu.VMEM((tt, MIX_WIDTH), bf16)] * 2
        + [pltpu.VMEM((tt, 2 * LRU_WIDTH), f32),
           pltpu.VMEM((tt, D_MODEL), f32),
           pltpu.VMEM((SUBLANES, LRU_WIDTH), f32),
           pltpu.VMEM((SUBLANES, LRU_WIDTH), f32),
           pltpu.VMEM((N_PAIRS, LANES, PAIR_V), f32)])
    out = pl.pallas_call(
        functools.partial(_main_kernel, n_tiles),
        grid=(n_total + 3,),
        in_specs=in_specs,
        out_specs=pl.BlockSpec((None, tt, D_MODEL), lambda g: (*tile_idx(g, 3), 0)),
        out_shape=jax.ShapeDtypeStruct((B, S, D_MODEL), x.dtype),
        scratch_shapes=scratch,
        compiler_params=pltpu.CompilerParams(
            dimension_semantics=("arbitrary",),
            vmem_limit_bytes=VMEM_LIMIT_BYTES),
        name="hybrid_main",
    )(x, x, cos_t[N_META:], sin_t[N_META:], g1, win_b, convw, convb, wg_b, brg, big, lam, rng, wout_b, fng,
      dmask_pair, k_dec, q_dec, even, odd, g_state, bd_mask, lx0, h0, st0)
    return out
```

```python
import functools

import numpy as np
import jax
import jax.numpy as jnp
from jax import lax
from jax.experimental import pallas as pl
from jax.experimental.pallas import tpu as pltpu

f32 = jnp.float32
bf16 = jnp.bfloat16

D_MODEL = 1024
N_META = 16
LRU_WIDTH = 1024
LRU_HEADS = 8
LRU_BLOCK = 128
CONV_WIDTH = 4
LRU_C = 8.0
RET_HEADS = 8
RET_QK_DIM = 64
RET_V_DIM = 128
RET_QK_WIDTH = 512
RET_WIDTH = 1024
CHUNK = 128
ROPE_BASE = 10000.0
MIX_WIDTH = 2048
EPS = 1e-6
QK_SCALE = RET_QK_DIM ** -0.5

OFF_LX, OFF_GATE, OFF_Q, OFF_K, OFF_V, OFF_RG = 0, 1024, 2048, 2560, 3072, 4096
IN_WIDTH = 5120

LANES = 128
SUBLANES = 8
N_PAIRS = RET_HEADS // 2
PAIR_V = 2 * RET_V_DIM
TILE_T = 256
SCAN_ROWS = 16
VMEM_LIMIT_BYTES = 56 * 1024 * 1024


def _permute_qk_columns(w):
    half = RET_QK_DIM // 2
    return w.reshape(w.shape[0], N_PAIRS, 2, 2, half).transpose(0, 1, 3, 2, 4).reshape(w.shape[0], RET_QK_WIDTH)


def _lane_head():
    half = RET_QK_DIM // 2
    l = np.arange(RET_QK_WIDTH)
    return 2 * (l // LANES) + ((l % LANES) // half) % 2


def _retention_tables():
    log_g = np.log1p(-np.exp2(-5.0 - np.arange(RET_HEADS, dtype=np.float32))).astype(np.float32)
    idx = np.arange(CHUNK, dtype=np.float32)
    diff = idx[:, None] - idx[None, :]
    dmask = np.where(diff[None] >= 0.0, np.exp(np.maximum(diff, 0.0)[None] * log_g[:, None, None]), 0.0)
    dmask_pair = np.concatenate([dmask[0::2], dmask[1::2]], axis=-1)
    lg_lane = log_g[_lane_head()]
    k_dec = np.exp((CHUNK - 1.0 - idx)[:, None] * lg_lane[None, :])
    q_dec = np.exp((idx + 1.0)[:, None] * lg_lane[None, :]) * QK_SCALE
    even = (_lane_head() % 2 == 0)[None, :]
    g_chunk = np.exp(CHUNK * log_g)
    g_state = np.broadcast_to(np.repeat(g_chunk, RET_V_DIM).reshape(N_PAIRS, 1, PAIR_V), (N_PAIRS, LANES, PAIR_V))
    row_par = (np.arange(LANES) // (RET_QK_DIM // 2)) % 2
    col_par = np.arange(PAIR_V) // RET_V_DIM
    bd_mask = row_par[:, None] == col_par[None, :]
    as_f32 = lambda a: jnp.asarray(np.asarray(a, np.float32))
    return tuple(as_f32(t) for t in (dmask_pair, k_dec, q_dec, even, ~even, g_state, bd_mask))


def _rotary_tables(n_pos):
    half = RET_QK_DIM // 2
    inv = (np.float32(ROPE_BASE) ** (-np.arange(half, dtype=np.float32) / half)).astype(np.float32)
    ang = np.arange(n_pos).astype(np.float32)[:, None] * inv[None, :]
    cos, sin = np.cos(ang), np.sin(ang)
    cos_t = np.concatenate([cos, cos, cos, cos], axis=-1).astype(np.float32)
    sin_t = np.concatenate([-sin, -sin, sin, sin], axis=-1).astype(np.float32)
    return jnp.asarray(cos_t), jnp.asarray(sin_t)


def _rmsnorm_rows(x, gain_row):
    ms = jnp.mean(x * x, axis=-1, keepdims=True)
    return x * lax.rsqrt(ms + EPS) * gain_row


def _sigmoid(x):
    return 0.5 * jnp.tanh(0.5 * x) + 0.5


def _silu(x):
    hx = 0.5 * x
    return hx * jnp.tanh(hx) + hx


def _conv_and_gates(n_rows, lx_scr, convw_ref, convb_ref, wg_ref, xc_scr, xcb_scr, pre_scr):
    base = SUBLANES
    xc = convb_ref[...] + convw_ref[3:4, :] * lx_scr[pl.ds(base, n_rows), :]
    xc = xc + convw_ref[2:3, :] * lx_scr[pl.ds(base - 1, n_rows), :]
    xc = xc + convw_ref[1:2, :] * lx_scr[pl.ds(base - 2, n_rows), :]
    xc = xc + convw_ref[0:1, :] * lx_scr[pl.ds(base - 3, n_rows), :]
    xc_scr[...] = xc
    xcb_scr[...] = xc.astype(bf16)
    for h in range(LRU_HEADS):
        pre = jnp.dot(xcb_scr[:, h * LRU_BLOCK:(h + 1) * LRU_BLOCK], wg_ref[h], preferred_element_type=f32)
        pre_scr[:, h * LRU_BLOCK:(h + 1) * LRU_BLOCK] = pre[:, :LRU_BLOCK]
        pre_scr[:, LRU_WIDTH + h * LRU_BLOCK:LRU_WIDTH + (h + 1) * LRU_BLOCK] = pre[:, LRU_BLOCK:]


def _lru_block(xc, pre_r, pre_i, brg, big, c_sp, carry):
    rows = xc.shape[0]
    r = _sigmoid(pre_r + brg)
    i = _sigmoid(pre_i + big)
    nl = r * c_sp
    a = jnp.exp(-nl)
    z = jnp.tanh(nl) * (1.0 + a * a)
    beta = jnp.where(z > 0.0, z * lax.rsqrt(z), 0.0)
    b = beta * i * xc
    nv = rows // SUBLANES
    a3 = a.reshape(nv, SUBLANES, LRU_WIDTH)
    b3 = b.reshape(nv, SUBLANES, LRU_WIDTH)
    rowid = lax.broadcasted_iota(jnp.int32, (nv, SUBLANES, LRU_WIDTH), 1)
    for s in (1, 2, 4):
        keep = rowid >= s
        a_s = jnp.where(keep, pltpu.roll(a3, s, 1), 1.0)
        b_s = jnp.where(keep, pltpu.roll(b3, s, 1), 0.0)
        b3 = a3 * b_s + b3
        a3 = a3 * a_s
    hs = []
    for v in range(nv):
        h_v = a3[v] * carry + b3[v]
        carry = h_v[SUBLANES - 1:SUBLANES, :]
        hs.append(h_v)
    return jnp.concatenate(hs, axis=0), carry


def _rotary(t, cos, sin):
    return t * cos + pltpu.roll(t, LANES // 2, 1) * sin


def _dot_t0(a, b):
    return lax.dot_general(a, b, (((0,), (0,)), ((), ())), preferred_element_type=f32)


def _dot_nt(a, b):
    return lax.dot_general(a, b, (((1,), (1,)), ((), ())), preferred_element_type=f32)


def _meta_kernel(meta_ref, cos_ref, sin_ref, g1_ref, wlx_ref, wk_ref, wv_ref, convw_ref, convb_ref,
                 wg_ref, brg_ref, big_ref, lam_ref, kdec_ref, bdm_ref,
                 lx_out, h_out, st_out,
                 lx_scr, xc_scr, xcb_scr, pre_scr):
    u = _rmsnorm_rows(meta_ref[...], g1_ref[...]).astype(bf16)
    lx_scr[0:SUBLANES, :] = jnp.zeros((SUBLANES, LRU_WIDTH), f32)
    lx_scr[SUBLANES:SUBLANES + N_META, :] = jnp.dot(u, wlx_ref[...], preferred_element_type=f32)
    k = jnp.dot(u, wk_ref[...], preferred_element_type=f32)
    v = jnp.dot(u, wv_ref[...], preferred_element_type=f32).astype(bf16)

    _conv_and_gates(N_META, lx_scr, convw_ref, convb_ref, wg_ref, xc_scr, xcb_scr, pre_scr)
    lx_out[...] = lx_scr[pl.ds(N_META, SUBLANES), :]
    c_sp = LRU_C * jax.nn.softplus(-lam_ref[...])
    _, carry = _lru_block(xc_scr[...], pre_scr[:, :LRU_WIDTH], pre_scr[:, LRU_WIDTH:],
                          brg_ref[...], big_ref[...], c_sp, jnp.zeros((1, LRU_WIDTH), f32))
    h_out[...] = jnp.broadcast_to(carry, (SUBLANES, LRU_WIDTH))

    cos, sin = cos_ref[...], sin_ref[...]
    for p in range(N_PAIRS):
        sl = slice(p * LANES, (p + 1) * LANES)
        k_rot = _rotary(k[:, sl], cos, sin)
        kd = (k_rot * kdec_ref[CHUNK - N_META:CHUNK, sl]).astype(bf16)
        st_out[p] = _dot_t0(kd, v[:, p * PAIR_V:(p + 1) * PAIR_V]) * bdm_ref[...]


def _project_tile(x_ref, g1_ref, win_ref, u_scr, slot):
    lx_scr, gate_scr, q_scr, k_scr, v_scr, rg_scr = slot
    tt = TILE_T
    u_scr[...] = _rmsnorm_rows(x_ref[...], g1_ref[...]).astype(bf16)
    lx_scr[SUBLANES:SUBLANES + tt, :] = jnp.dot(u_scr[...], win_ref[:, OFF_LX:OFF_LX + LRU_WIDTH],
                                               preferred_element_type=f32)
    gate_scr[...] = jnp.dot(u_scr[...], win_ref[:, OFF_GATE:OFF_GATE + LRU_WIDTH], preferred_element_type=f32)
    q_scr[...] = jnp.dot(u_scr[...], win_ref[:, OFF_Q:OFF_Q + RET_QK_WIDTH], preferred_element_type=f32)
    k_scr[...] = jnp.dot(u_scr[...], win_ref[:, OFF_K:OFF_K + RET_QK_WIDTH], preferred_element_type=f32)
    v_scr[...] = jnp.dot(u_scr[...], win_ref[:, OFF_V:OFF_V + RET_WIDTH], preferred_element_type=f32).astype(bf16)
    rg_scr[...] = jnp.dot(u_scr[...], win_ref[:, OFF_RG:OFF_RG + RET_WIDTH], preferred_element_type=f32)


def _finish_tile(x_ref, cos_ref, sin_ref, convw_ref, convb_ref, wg_ref, brg_ref, big_ref, lam_ref, rng_ref,
                 wout_ref, fng_ref, dmaskp_ref, kdec_ref, qdec_ref, even_ref, odd_ref, gst_ref, bdm_ref,
                 o_ref, slot, xc_scr, xcb_scr, pre_scr, y_scr, lxtail_scr, h_scr, st_scr):
    lx_scr, gate_scr, q_scr, k_scr, v_scr, rg_scr = slot
    tt = TILE_T

    lx_scr[0:SUBLANES, :] = lxtail_scr[...]
    _conv_and_gates(tt, lx_scr, convw_ref, convb_ref, wg_ref, xc_scr, xcb_scr, pre_scr)
    lxtail_scr[...] = lx_scr[pl.ds(tt, SUBLANES), :]
    c_sp = LRU_C * jax.nn.softplus(-lam_ref[...])
    brg, big = brg_ref[...], big_ref[...]
    carry = h_scr[0:1, :]
    for it in range(tt // SCAN_ROWS):
        rows = slice(it * SCAN_ROWS, (it + 1) * SCAN_ROWS)
        h, carry = _lru_block(xc_scr[rows, :], pre_scr[rows, 0:LRU_WIDTH], pre_scr[rows, LRU_WIDTH:2 * LRU_WIDTH],
                              brg, big, c_sp, carry)
        y_scr[rows, 0:LRU_WIDTH] = (h * _silu(gate_scr[rows, :])).astype(bf16)
    h_scr[...] = jnp.broadcast_to(carry, (SUBLANES, LRU_WIDTH))

    first_head_cols = lax.broadcasted_iota(jnp.int32, (CHUNK, PAIR_V), 1) < RET_V_DIM
    for c in range(tt // CHUNK):
        rows = slice(c * CHUNK, (c + 1) * CHUNK)
        cos, sin = cos_ref[rows, :], sin_ref[rows, :]
        for p in range(N_PAIRS):
            sl = slice(p * LANES, (p + 1) * LANES)
            q_rot = _rotary(q_scr[rows, sl], cos, sin)
            k_rot = _rotary(k_scr[rows, sl], cos, sin)
            q_b = (q_rot * QK_SCALE).astype(bf16)
            q_d = (q_rot * qdec_ref[:, sl]).astype(bf16)
            k_d = (k_rot * kdec_ref[:, sl]).astype(bf16)
            kk = jnp.concatenate([(k_rot * even_ref[:, sl]).astype(bf16), (k_rot * odd_ref[:, sl]).astype(bf16)], axis=0)
            v_pair = v_scr[rows, p * PAIR_V:(p + 1) * PAIR_V]
            zeros = jnp.zeros_like(v_pair)
            v_bd = jnp.concatenate([jnp.where(first_head_cols, v_pair, zeros),
                                    jnp.where(first_head_cols, zeros, v_pair)], axis=0)
            st_pair = st_scr[p]
            s = _dot_nt(q_b, kk) * dmaskp_ref[p]
            o = jnp.dot(s.astype(bf16), v_bd, preferred_element_type=f32)
            o = o + jnp.dot(q_d, st_pair.astype(bf16), preferred_element_type=f32)
            for e in range(2):
                h = 2 * p + e
                hs = slice(h * RET_V_DIM, (h + 1) * RET_V_DIM)
                o_h = o[:, e * RET_V_DIM:(e + 1) * RET_V_DIM]
                mu = jnp.mean(o_h, axis=-1, keepdims=True)
                oc = o_h - mu
                var = jnp.mean(oc * oc, axis=-1, keepdims=True)
                on = oc * lax.rsqrt(var + EPS) * rng_ref[:, hs]
                y_scr[rows, LRU_WIDTH + h * RET_V_DIM:LRU_WIDTH + (h + 1) * RET_V_DIM] = (
                    on * _silu(rg_scr[rows, hs])).astype(bf16)
            kv = _dot_t0(k_d, v_pair)
            st_scr[p] = gst_ref[p] * st_pair + bdm_ref[...] * kv

    res = x_ref[...] + jnp.dot(y_scr[...], wout_ref[...], preferred_element_type=f32)
    o_ref[...] = _rmsnorm_rows(res, fng_ref[...])


def _main_kernel(n_tiles,
                 xa_ref, xb_ref, cos_ref, sin_ref, g1_ref, win_ref, convw_ref, convb_ref, wg_ref, brg_ref, big_ref,
                 lam_ref, rng_ref, wout_ref, fng_ref, dmaskp_ref, kdec_ref, qdec_ref, even_ref, odd_ref,
                 gst_ref, bdm_ref, lx0_ref, h0_ref, st0_ref,
                 o_ref,
                 u_scr,
                 lx_a, gate_a, q_a, k_a, v_a, rg_a,
                 lx_b, gate_b, q_b, k_b, v_b, rg_b,
                 xc_scr, xcb_scr, pre_scr, y_scr, lxtail_scr, h_scr, st_scr):
    g = pl.program_id(0)
    slots = ((lx_a, gate_a, q_a, k_a, v_a, rg_a), (lx_b, gate_b, q_b, k_b, v_b, rg_b))

    @pl.when(g == 0)
    def _():
        for ref in slots[1]:
            ref[...] = jnp.zeros(ref.shape, ref.dtype)
        lxtail_scr[...] = jnp.zeros(lxtail_scr.shape, f32)
        h_scr[...] = jnp.zeros(h_scr.shape, f32)
        st_scr[...] = jnp.zeros(st_scr.shape, f32)

    @pl.when(lax.rem(g + n_tiles - 1, n_tiles) == 0)
    def _():
        lxtail_scr[...] = lx0_ref[...]
        h_scr[...] = h0_ref[...]
        st_scr[...] = st0_ref[...]

    def step(slot_project, slot_finish):
        _project_tile(xa_ref, g1_ref, win_ref, u_scr, slot_project)
        _finish_tile(xb_ref, cos_ref, sin_ref, convw_ref, convb_ref, wg_ref, brg_ref, big_ref, lam_ref, rng_ref,
                     wout_ref, fng_ref, dmaskp_ref, kdec_ref, qdec_ref, even_ref, odd_ref, gst_ref, bdm_ref,
                     o_ref, slot_finish, xc_scr, xcb_scr, pre_scr, y_scr, lxtail_scr, h_scr, st_scr)

    @pl.when(lax.rem(g, 2) == 0)
    def _():
        step(slots[0], slots[1])

    @pl.when(lax.rem(g, 2) == 1)
    def _():
        step(slots[1], slots[0])


def _const_spec(shape):
    nd = len(shape)
    return pl.BlockSpec(shape, lambda g, _nd=nd: (0,) * _nd, pipeline_mode=pl.Buffered(1))


def kernel(x, meta_tokens, norm_gain, w_in, conv_w, conv_b, w_rg, b_rg, w_ig, b_ig,
           lru_lambda, ret_norm_gain, w_out, final_norm_gain):
    B, S, D = x.shape
    assert D == D_MODEL and S % TILE_T == 0 and TILE_T % CHUNK == 0
    assert norm_gain.shape[0] == 1, "single-layer block"
    assert meta_tokens.shape == (N_META, D_MODEL)

    w = w_in[0].astype(bf16)
    win_b = jnp.concatenate([w[:, :OFF_Q], _permute_qk_columns(w[:, OFF_Q:OFF_K]),
                             _permute_qk_columns(w[:, OFF_K:OFF_V]), w[:, OFF_V:]], axis=1)
    wg_b = jnp.concatenate([w_rg[0], w_ig[0]], axis=-1).astype(bf16)
    wout_b = w_out[0].astype(bf16)
    g1 = norm_gain[0].reshape(1, D_MODEL)
    fng = final_norm_gain.reshape(1, D_MODEL)
    convw = conv_w[0]
    convb = conv_b[0].reshape(1, LRU_WIDTH)
    brg = b_rg[0].reshape(1, LRU_WIDTH)
    big = b_ig[0].reshape(1, LRU_WIDTH)
    lam = lru_lambda[0].reshape(1, LRU_WIDTH)
    rng = ret_norm_gain[0].reshape(1, RET_WIDTH)

    dmask_pair, k_dec, q_dec, even, odd, g_state, bd_mask = _retention_tables()
    cos_t, sin_t = _rotary_tables(N_META + S)

    n_lx = SUBLANES + N_META
    lx0, h0, st0 = pl.pallas_call(
        _meta_kernel,
        out_shape=(jax.ShapeDtypeStruct((SUBLANES, LRU_WIDTH), f32),
                   jax.ShapeDtypeStruct((SUBLANES, LRU_WIDTH), f32),
                   jax.ShapeDtypeStruct((N_PAIRS, LANES, PAIR_V), f32)),
        scratch_shapes=[pltpu.VMEM((n_lx, LRU_WIDTH), f32),
                        pltpu.VMEM((N_META, LRU_WIDTH), f32),
                        pltpu.VMEM((N_META, LRU_WIDTH), bf16),
                        pltpu.VMEM((N_META, 2 * LRU_WIDTH), f32)],
        compiler_params=pltpu.CompilerParams(vmem_limit_bytes=VMEM_LIMIT_BYTES),
        name="hybrid_meta_state",
    )(meta_tokens, cos_t[:N_META], sin_t[:N_META], g1, win_b[:, OFF_LX:OFF_LX + LRU_WIDTH],
      win_b[:, OFF_K:OFF_K + RET_QK_WIDTH], win_b[:, OFF_V:OFF_V + RET_WIDTH], convw, convb,
      wg_b, brg, big, lam, k_dec, bd_mask)

    tt = TILE_T
    n_tiles = S // tt
    n_total = B * n_tiles

    def proj_idx(g):
        gp = jnp.minimum(g, n_total - 1)
        return gp // n_tiles, gp % n_tiles

    def fin_idx(g):
        gf = jnp.maximum(g - 1, 0)
        return gf // n_tiles, gf % n_tiles

    rot_spec = pl.BlockSpec((tt, LANES), lambda g: (fin_idx(g)[1], 0))
    in_specs = [
        pl.BlockSpec((None, tt, D_MODEL), lambda g: (*proj_idx(g), 0)),
        pl.BlockSpec((None, tt, D_MODEL), lambda g: (*fin_idx(g), 0)),
        rot_spec, rot_spec,
        _const_spec((1, D_MODEL)),
        _const_spec((D_MODEL, IN_WIDTH)),
        _const_spec((CONV_WIDTH, LRU_WIDTH)), _const_spec((1, LRU_WIDTH)),
        _const_spec((LRU_HEADS, LRU_BLOCK, 2 * LRU_BLOCK)),
        _const_spec((1, LRU_WIDTH)), _const_spec((1, LRU_WIDTH)), _const_spec((1, LRU_WIDTH)),
        _const_spec((1, RET_WIDTH)),
        _const_spec((MIX_WIDTH, D_MODEL)),
        _const_spec((1, D_MODEL)),
        _const_spec((N_PAIRS, CHUNK, 2 * CHUNK)),
        _const_spec((CHUNK, RET_QK_WIDTH)), _const_spec((CHUNK, RET_QK_WIDTH)),
        _const_spec((1, RET_QK_WIDTH)), _const_spec((1, RET_QK_WIDTH)),
        _const_spec((N_PAIRS, LANES, PAIR_V)),
        _const_spec((LANES, PAIR_V)),
        _const_spec((SUBLANES, LRU_WIDTH)), _const_spec((SUBLANES, LRU_WIDTH)),
        _const_spec((N_PAIRS, LANES, PAIR_V)),
    ]
    slot_scratch = [
        pltpu.VMEM((SUBLANES + tt, LRU_WIDTH), f32),
        pltpu.VMEM((tt, LRU_WIDTH), f32),
        pltpu.VMEM((tt, RET_QK_WIDTH), f32),
        pltpu.VMEM((tt, RET_QK_WIDTH), f32),
        pltpu.VMEM((tt, RET_WIDTH), bf16),
        pltpu.VMEM((tt, RET_WIDTH), f32),
    ]
    scratch = [pltpu.VMEM((tt, D_MODEL), bf16)] + slot_scratch + slot_scratch + [
        pltpu.VMEM((tt, LRU_WIDTH), f32),
        pltpu.VMEM((tt, LRU_WIDTH), bf16),
        pltpu.VMEM((tt, 2 * LRU_WIDTH), f32),
        pltpu.VMEM((tt, MIX_WIDTH), bf16),
        pltpu.VMEM((SUBLANES, LRU_WIDTH), f32),
        pltpu.VMEM((SUBLANES, LRU_WIDTH), f32),
        pltpu.VMEM((N_PAIRS, LANES, PAIR_V), f32),
    ]
    out = pl.pallas_call(
        functools.partial(_main_kernel, n_tiles),
        grid=(n_total + 1,),
        in_specs=in_specs,
        out_specs=pl.BlockSpec((None, tt, D_MODEL), lambda g: (*fin_idx(g), 0)),
        out_shape=jax.ShapeDtypeStruct((B, S, D_MODEL), x.dtype),
        scratch_shapes=scratch,
        compiler_params=pltpu.CompilerParams(
            dimension_semantics=("arbitrary",),
            vmem_limit_bytes=VMEM_LIMIT_BYTES),
        name="hybrid_main",
    )(x, x, cos_t[N_META:], sin_t[N_META:], g1, win_b, convw, convb, wg_b, brg, big, lam, rng, wout_b, fng,
      dmask_pair, k_dec, q_dec, even, odd, g_state, bd_mask, lx0, h0, st0)
    return out
```

```python
import functools

import numpy as np
import jax
import jax.numpy as jnp
from jax import lax
from jax.experimental import pallas as pl
from jax.experimental.pallas import tpu as pltpu

f32 = jnp.float32
bf16 = jnp.bfloat16

D_MODEL = 1024
N_META = 16
LRU_WIDTH = 1024
LRU_HEADS = 8
LRU_BLOCK = 128
CONV_WIDTH = 4
LRU_C = 8.0
RET_HEADS = 8
RET_QK_DIM = 64
RET_V_DIM = 128
RET_QK_WIDTH = 512
RET_WIDTH = 1024
CHUNK = 128
ROPE_BASE = 10000.0
MIX_WIDTH = 2048
EPS = 1e-6
QK_SCALE = RET_QK_DIM ** -0.5

OFF_LX, OFF_GATE, OFF_Q, OFF_K, OFF_V, OFF_RG = 0, 1024, 2048, 2560, 3072, 4096
IN_WIDTH = 5120

LANES = 128
SUBLANES = 8
N_PAIRS = RET_HEADS // 2
PAIR_V = 2 * RET_V_DIM
TILE_T = 256
SCAN_ROWS = 16
VMEM_LIMIT_BYTES = 56 * 1024 * 1024


def _lane_head():
    return np.arange(RET_QK_WIDTH) // RET_QK_DIM


def _retention_tables():
    log_g = np.log1p(-np.exp2(-5.0 - np.arange(RET_HEADS, dtype=np.float32))).astype(np.float32)
    idx = np.arange(CHUNK, dtype=np.float32)
    diff = idx[:, None] - idx[None, :]
    dmask = np.where(diff[None] >= 0.0, np.exp(np.maximum(diff, 0.0)[None] * log_g[:, None, None]), 0.0)
    dmask_pair = np.concatenate([dmask[0::2], dmask[1::2]], axis=-1)
    lg_lane = log_g[_lane_head()]
    k_dec = np.exp((CHUNK - 1.0 - idx)[:, None] * lg_lane[None, :])
    q_dec = np.exp((idx + 1.0)[:, None] * lg_lane[None, :]) * QK_SCALE
    even = (_lane_head() % 2 == 0)[None, :]
    g_chunk = np.exp(CHUNK * log_g)
    g_state = np.broadcast_to(np.repeat(g_chunk, RET_V_DIM).reshape(N_PAIRS, 1, PAIR_V), (N_PAIRS, LANES, PAIR_V))
    row_par = np.arange(LANES) // RET_QK_DIM
    col_par = np.arange(PAIR_V) // RET_V_DIM
    bd_mask = row_par[:, None] == col_par[None, :]
    as_f32 = lambda a: jnp.asarray(np.asarray(a, np.float32))
    return tuple(as_f32(t) for t in (dmask_pair, k_dec, q_dec, even, ~even, g_state, bd_mask))


def _rotary_tables(n_pos):
    half = RET_QK_DIM // 2
    inv = (np.float32(ROPE_BASE) ** (-np.arange(half, dtype=np.float32) / half)).astype(np.float32)
    ang = np.arange(n_pos).astype(np.float32)[:, None] * inv[None, :]
    cos, sin = np.cos(ang), np.sin(ang)
    cos_t = np.concatenate([cos, cos, cos, cos], axis=-1).astype(np.float32)
    sin_t = np.concatenate([-sin, sin, -sin, sin], axis=-1).astype(np.float32)
    return jnp.asarray(cos_t), jnp.asarray(sin_t)


def _rmsnorm_rows(x, gain_row):
    ms = jnp.mean(x * x, axis=-1, keepdims=True)
    return x * lax.rsqrt(ms + EPS) * gain_row


def _sigmoid(x):
    return 0.5 * jnp.tanh(0.5 * x) + 0.5


def _silu(x):
    hx = 0.5 * x
    return hx * jnp.tanh(hx) + hx


def _conv_and_gates(n_rows, lx_scr, convw_ref, convb_ref, wrg_ref, wig_ref, xc_scr, xcb_scr, pre_scr):
    base = SUBLANES
    xc = convb_ref[...] + convw_ref[3:4, :] * lx_scr[pl.ds(base, n_rows), :]
    xc = xc + convw_ref[2:3, :] * lx_scr[pl.ds(base - 1, n_rows), :]
    xc = xc + convw_ref[1:2, :] * lx_scr[pl.ds(base - 2, n_rows), :]
    xc = xc + convw_ref[0:1, :] * lx_scr[pl.ds(base - 3, n_rows), :]
    xc_scr[...] = xc
    xcb_scr[...] = xc.astype(bf16)
    for h in range(LRU_HEADS):
        wg = jnp.concatenate([wrg_ref[h], wig_ref[h]], axis=-1).astype(bf16)
        pre = jnp.dot(xcb_scr[:, h * LRU_BLOCK:(h + 1) * LRU_BLOCK], wg, preferred_element_type=f32)
        pre_scr[:, h * LRU_BLOCK:(h + 1) * LRU_BLOCK] = pre[:, :LRU_BLOCK]
        pre_scr[:, LRU_WIDTH + h * LRU_BLOCK:LRU_WIDTH + (h + 1) * LRU_BLOCK] = pre[:, LRU_BLOCK:]


def _lru_block(xc, pre_r, pre_i, brg, big, c_sp, carry):
    rows = xc.shape[0]
    r = _sigmoid(pre_r + brg)
    i = _sigmoid(pre_i + big)
    nl = r * c_sp
    a = jnp.exp(-nl)
    z = jnp.tanh(nl) * (1.0 + a * a)
    beta = jnp.where(z > 0.0, z * lax.rsqrt(z), 0.0)
    b = beta * i * xc
    nv = rows // SUBLANES
    a3 = a.reshape(nv, SUBLANES, LRU_WIDTH)
    b3 = b.reshape(nv, SUBLANES, LRU_WIDTH)
    rowid = lax.broadcasted_iota(jnp.int32, (nv, SUBLANES, LRU_WIDTH), 1)
    for s in (1, 2, 4):
        keep = rowid >= s
        a_s = jnp.where(keep, pltpu.roll(a3, s, 1), 1.0)
        b_s = jnp.where(keep, pltpu.roll(b3, s, 1), 0.0)
        b3 = a3 * b_s + b3
        a3 = a3 * a_s
    hs = []
    for v in range(nv):
        h_v = a3[v] * carry + b3[v]
        carry = h_v[SUBLANES - 1:SUBLANES, :]
        hs.append(h_v)
    return jnp.concatenate(hs, axis=0), carry


def _rotary(t, cos, sin):
    half = RET_QK_DIM // 2
    in_first_half = (lax.broadcasted_iota(jnp.int32, t.shape, 1) // half) % 2 == 0
    partner = jnp.where(in_first_half, pltpu.roll(t, LANES - half, 1), pltpu.roll(t, half, 1))
    return t * cos + partner * sin


def _dot_t0(a, b):
    return lax.dot_general(a, b, (((0,), (0,)), ((), ())), preferred_element_type=f32)


def _dot_nt(a, b):
    return lax.dot_general(a, b, (((1,), (1,)), ((), ())), preferred_element_type=f32)


def _meta_kernel(meta_ref, cos_ref, sin_ref, g1_ref, wlx_ref, wk_ref, wv_ref, convw_ref, convb_ref,
                 wrg_ref, wig_ref, brg_ref, big_ref, lam_ref, kdec_ref, bdm_ref,
                 lx_out, h_out, st_out,
                 lx_scr, xc_scr, xcb_scr, pre_scr):
    u = _rmsnorm_rows(meta_ref[...], g1_ref[...]).astype(bf16)
    lx_scr[0:SUBLANES, :] = jnp.zeros((SUBLANES, LRU_WIDTH), f32)
    lx_scr[SUBLANES:SUBLANES + N_META, :] = jnp.dot(u, wlx_ref[...].astype(bf16), preferred_element_type=f32)
    k = jnp.dot(u, wk_ref[...].astype(bf16), preferred_element_type=f32)
    v = jnp.dot(u, wv_ref[...].astype(bf16), preferred_element_type=f32).astype(bf16)

    _conv_and_gates(N_META, lx_scr, convw_ref, convb_ref, wrg_ref, wig_ref, xc_scr, xcb_scr, pre_scr)
    lx_out[...] = lx_scr[pl.ds(N_META, SUBLANES), :]
    c_sp = LRU_C * jax.nn.softplus(-lam_ref[...])
    _, carry = _lru_block(xc_scr[...], pre_scr[:, :LRU_WIDTH], pre_scr[:, LRU_WIDTH:],
                          brg_ref[...], big_ref[...], c_sp, jnp.zeros((1, LRU_WIDTH), f32))
    h_out[...] = jnp.broadcast_to(carry, (SUBLANES, LRU_WIDTH))

    cos, sin = cos_ref[...], sin_ref[...]
    for p in range(N_PAIRS):
        sl = slice(p * LANES, (p + 1) * LANES)
        k_rot = _rotary(k[:, sl], cos, sin)
        kd = (k_rot * kdec_ref[CHUNK - N_META:CHUNK, sl]).astype(bf16)
        st_out[p] = _dot_t0(kd, v[:, p * PAIR_V:(p + 1) * PAIR_V]) * bdm_ref[...]


def _project_tile(x_ref, g1_ref, win_ref, u_scr, slot):
    lx_scr, gate_scr, q_scr, k_scr, v_scr, rg_scr = slot
    tt = TILE_T
    u_scr[...] = _rmsnorm_rows(x_ref[...], g1_ref[...]).astype(bf16)

    def proj_cols(off, width):
        return jnp.dot(u_scr[...], win_ref[:, off:off + width].astype(bf16), preferred_element_type=f32)

    lx_scr[SUBLANES:SUBLANES + tt, :] = proj_cols(OFF_LX, LRU_WIDTH)
    gate_scr[...] = proj_cols(OFF_GATE, LRU_WIDTH)
    q_scr[...] = proj_cols(OFF_Q, RET_QK_WIDTH)
    k_scr[...] = proj_cols(OFF_K, RET_QK_WIDTH)
    v_scr[...] = proj_cols(OFF_V, RET_WIDTH).astype(bf16)
    rg_scr[...] = proj_cols(OFF_RG, RET_WIDTH)


def _finish_tile(x_ref, cos_ref, sin_ref, convw_ref, convb_ref, wrg_ref, wig_ref, brg_ref, big_ref, lam_ref, rng_ref,
                 wout_ref, fng_ref, dmaskp_ref, kdec_ref, qdec_ref, even_ref, odd_ref, gst_ref, bdm_ref,
                 o_ref, slot, xc_scr, xcb_scr, pre_scr, y_scr, lxtail_scr, h_scr, st_scr):
    lx_scr, gate_scr, q_scr, k_scr, v_scr, rg_scr = slot
    tt = TILE_T

    lx_scr[0:SUBLANES, :] = lxtail_scr[...]
    _conv_and_gates(tt, lx_scr, convw_ref, convb_ref, wrg_ref, wig_ref, xc_scr, xcb_scr, pre_scr)
    lxtail_scr[...] = lx_scr[pl.ds(tt, SUBLANES), :]
    c_sp = LRU_C * jax.nn.softplus(-lam_ref[...])
    brg, big = brg_ref[...], big_ref[...]
    carry = h_scr[0:1, :]
    for it in range(tt // SCAN_ROWS):
        rows = slice(it * SCAN_ROWS, (it + 1) * SCAN_ROWS)
        h, carry = _lru_block(xc_scr[rows, :], pre_scr[rows, 0:LRU_WIDTH], pre_scr[rows, LRU_WIDTH:2 * LRU_WIDTH],
                              brg, big, c_sp, carry)
        y_scr[rows, 0:LRU_WIDTH] = (h * _silu(gate_scr[rows, :])).astype(bf16)
    h_scr[...] = jnp.broadcast_to(carry, (SUBLANES, LRU_WIDTH))

    first_head_cols = lax.broadcasted_iota(jnp.int32, (CHUNK, PAIR_V), 1) < RET_V_DIM
    for c in range(tt // CHUNK):
        rows = slice(c * CHUNK, (c + 1) * CHUNK)
        cos, sin = cos_ref[rows, :], sin_ref[rows, :]
        for p in range(N_PAIRS):
            sl = slice(p * LANES, (p + 1) * LANES)
            q_rot = _rotary(q_scr[rows, sl], cos, sin)
            k_rot = _rotary(k_scr[rows, sl], cos, sin)
            q_b = (q_rot * QK_SCALE).astype(bf16)
            q_d = (q_rot * qdec_ref[:, sl]).astype(bf16)
            k_d = (k_rot * kdec_ref[:, sl]).astype(bf16)
            kk = jnp.concatenate([(k_rot * even_ref[:, sl]).astype(bf16), (k_rot * odd_ref[:, sl]).astype(bf16)], axis=0)
            v_pair = v_scr[rows, p * PAIR_V:(p + 1) * PAIR_V]
            zeros = jnp.zeros_like(v_pair)
            v_bd = jnp.concatenate([jnp.where(first_head_cols, v_pair, zeros),
                                    jnp.where(first_head_cols, zeros, v_pair)], axis=0)
            st_pair = st_scr[p]
            s = _dot_nt(q_b, kk) * dmaskp_ref[p]
            o = jnp.dot(s.astype(bf16), v_bd, preferred_element_type=f32)
            o = o + jnp.dot(q_d, st_pair.astype(bf16), preferred_element_type=f32)
            for e in range(2):
                h = 2 * p + e
                hs = slice(h * RET_V_DIM, (h + 1) * RET_V_DIM)
                o_h = o[:, e * RET_V_DIM:(e + 1) * RET_V_DIM]
                mu = jnp.mean(o_h, axis=-1, keepdims=True)
                oc = o_h - mu
                var = jnp.mean(oc * oc, axis=-1, keepdims=True)
                on = oc * lax.rsqrt(var + EPS) * rng_ref[:, hs]
                y_scr[rows, LRU_WIDTH + h * RET_V_DIM:LRU_WIDTH + (h + 1) * RET_V_DIM] = (
                    on * _silu(rg_scr[rows, hs])).astype(bf16)
            kv = _dot_t0(k_d, v_pair)
            st_scr[p] = gst_ref[p] * st_pair + bdm_ref[...] * kv

    res = x_ref[...] + jnp.dot(y_scr[...], wout_ref[...].astype(bf16), preferred_element_type=f32)
    o_ref[...] = _rmsnorm_rows(res, fng_ref[...])


def _main_kernel(n_tiles,
                 xa_ref, xb_ref, cos_ref, sin_ref, g1_ref, win_ref, convw_ref, convb_ref, wrg_ref, wig_ref,
                 brg_ref, big_ref, lam_ref, rng_ref, wout_ref, fng_ref, dmaskp_ref, kdec_ref, qdec_ref, even_ref, odd_ref,
                 gst_ref, bdm_ref, lx0_ref, h0_ref, st0_ref,
                 o_ref,
                 u_scr,
                 lx_a, gate_a, q_a, k_a, v_a, rg_a,
                 lx_b, gate_b, q_b, k_b, v_b, rg_b,
                 xc_scr, xcb_scr, pre_scr, y_scr, lxtail_scr, h_scr, st_scr):
    g = pl.program_id(0)
    slots = ((lx_a, gate_a, q_a, k_a, v_a, rg_a), (lx_b, gate_b, q_b, k_b, v_b, rg_b))

    @pl.when(g == 0)
    def _():
        for ref in slots[1]:
            ref[...] = jnp.zeros(ref.shape, ref.dtype)
        lxtail_scr[...] = jnp.zeros(lxtail_scr.shape, f32)
        h_scr[...] = jnp.zeros(h_scr.shape, f32)
        st_scr[...] = jnp.zeros(st_scr.shape, f32)

    @pl.when(lax.rem(g + n_tiles - 1, n_tiles) == 0)
    def _():
        lxtail_scr[...] = lx0_ref[...]
        h_scr[...] = h0_ref[...]
        st_scr[...] = st0_ref[...]

    def step(slot_project, slot_finish):
        _project_tile(xa_ref, g1_ref, win_ref, u_scr, slot_project)
        _finish_tile(xb_ref, cos_ref, sin_ref, convw_ref, convb_ref, wrg_ref, wig_ref, brg_ref, big_ref, lam_ref, rng_ref,
                     wout_ref, fng_ref, dmaskp_ref, kdec_ref, qdec_ref, even_ref, odd_ref, gst_ref, bdm_ref,
                     o_ref, slot_finish, xc_scr, xcb_scr, pre_scr, y_scr, lxtail_scr, h_scr, st_scr)

    @pl.when(lax.rem(g, 2) == 0)
    def _():
        step(slots[0], slots[1])

    @pl.when(lax.rem(g, 2) == 1)
    def _():
        step(slots[1], slots[0])


def _const_spec(shape):
    nd = len(shape)
    return pl.BlockSpec(shape, lambda g, _nd=nd: (0,) * _nd, pipeline_mode=pl.Buffered(1))


def kernel(x, meta_tokens, norm_gain, w_in, conv_w, conv_b, w_rg, b_rg, w_ig, b_ig,
           lru_lambda, ret_norm_gain, w_out, final_norm_gain):
    B, S, D = x.shape
    assert D == D_MODEL and S % TILE_T == 0 and TILE_T % CHUNK == 0
    assert norm_gain.shape[0] == 1, "single-layer block"
    assert meta_tokens.shape == (N_META, D_MODEL)

    assert w_in.shape == (1, D_MODEL, IN_WIDTH) and w_out.shape == (1, MIX_WIDTH, D_MODEL)
    assert w_rg.shape == w_ig.shape == (1, LRU_HEADS, LRU_BLOCK, LRU_BLOCK)

    g1 = norm_gain[0].reshape(1, D_MODEL)
    fng = final_norm_gain.reshape(1, D_MODEL)
    convw = conv_w[0]
    convb = conv_b[0].reshape(1, LRU_WIDTH)
    brg = b_rg[0].reshape(1, LRU_WIDTH)
    big = b_ig[0].reshape(1, LRU_WIDTH)
    lam = lru_lambda[0].reshape(1, LRU_WIDTH)
    rng = ret_norm_gain[0].reshape(1, RET_WIDTH)

    dmask_pair, k_dec, q_dec, even, odd, g_state, bd_mask = _retention_tables()
    cos_t, sin_t = _rotary_tables(N_META + S)

    n_lx = SUBLANES + N_META
    whole = lambda a: pl.BlockSpec(a.shape, lambda i, _nd=a.ndim: (0,) * _nd)
    w_in_cols = lambda off, width: pl.BlockSpec((None, D_MODEL, width), lambda i, _b=off // width: (0, 0, _b))
    gate_w_spec = pl.BlockSpec((None, LRU_HEADS, LRU_BLOCK, LRU_BLOCK), lambda i: (0, 0, 0, 0))
    cos_m, sin_m = cos_t[:N_META], sin_t[:N_META]
    meta_out_shapes = ((SUBLANES, LRU_WIDTH), (SUBLANES, LRU_WIDTH), (N_PAIRS, LANES, PAIR_V))
    lx0, h0, st0 = pl.pallas_call(
        _meta_kernel,
        grid=(1,),
        out_specs=tuple(pl.BlockSpec(s, lambda i, _nd=len(s): (0,) * _nd) for s in meta_out_shapes),
        in_specs=[whole(meta_tokens), whole(cos_m), whole(sin_m), whole(g1),
                  w_in_cols(OFF_LX, LRU_WIDTH), w_in_cols(OFF_K, RET_QK_WIDTH), w_in_cols(OFF_V, RET_WIDTH),
                  whole(convw), whole(convb), gate_w_spec, gate_w_spec,
                  whole(brg), whole(big), whole(lam), whole(k_dec), whole(bd_mask)],
        out_shape=(jax.ShapeDtypeStruct((SUBLANES, LRU_WIDTH), f32),
                   jax.ShapeDtypeStruct((SUBLANES, LRU_WIDTH), f32),
                   jax.ShapeDtypeStruct((N_PAIRS, LANES, PAIR_V), f32)),
        scratch_shapes=[pltpu.VMEM((n_lx, LRU_WIDTH), f32),
                        pltpu.VMEM((N_META, LRU_WIDTH), f32),
                        pltpu.VMEM((N_META, LRU_WIDTH), bf16),
                        pltpu.VMEM((N_META, 2 * LRU_WIDTH), f32)],
        compiler_params=pltpu.CompilerParams(vmem_limit_bytes=VMEM_LIMIT_BYTES),
        name="hybrid_meta_state",
    )(meta_tokens, cos_m, sin_m, g1, w_in, w_in, w_in, convw, convb, w_rg, w_ig, brg, big, lam, k_dec, bd_mask)

    tt = TILE_T
    n_tiles = S // tt
    n_total = B * n_tiles

    def proj_idx(g):
        gp = jnp.minimum(g, n_total - 1)
        return gp // n_tiles, gp % n_tiles

    def fin_idx(g):
        gf = jnp.maximum(g - 1, 0)
        return gf // n_tiles, gf % n_tiles

    rot_spec = pl.BlockSpec((tt, LANES), lambda g: (fin_idx(g)[1], 0))
    in_specs = [
        pl.BlockSpec((None, tt, D_MODEL), lambda g: (*proj_idx(g), 0)),
        pl.BlockSpec((None, tt, D_MODEL), lambda g: (*fin_idx(g), 0)),
        rot_spec, rot_spec,
        _const_spec((1, D_MODEL)),
        _const_spec((None, D_MODEL, IN_WIDTH)),
        _const_spec((CONV_WIDTH, LRU_WIDTH)), _const_spec((1, LRU_WIDTH)),
        _const_spec((None, LRU_HEADS, LRU_BLOCK, LRU_BLOCK)),
        _const_spec((None, LRU_HEADS, LRU_BLOCK, LRU_BLOCK)),
        _const_spec((1, LRU_WIDTH)), _const_spec((1, LRU_WIDTH)), _const_spec((1, LRU_WIDTH)),
        _const_spec((1, RET_WIDTH)),
        _const_spec((None, MIX_WIDTH, D_MODEL)),
        _const_spec((1, D_MODEL)),
        _const_spec((N_PAIRS, CHUNK, 2 * CHUNK)),
        _const_spec((CHUNK, RET_QK_WIDTH)), _const_spec((CHUNK, RET_QK_WIDTH)),
        _const_spec((1, RET_QK_WIDTH)), _const_spec((1, RET_QK_WIDTH)),
        _const_spec((N_PAIRS, LANES, PAIR_V)),
        _const_spec((LANES, PAIR_V)),
        _const_spec((SUBLANES, LRU_WIDTH)), _const_spec((SUBLANES, LRU_WIDTH)),
        _const_spec((N_PAIRS, LANES, PAIR_V)),
    ]
    slot_scratch = [
        pltpu.VMEM((SUBLANES + tt, LRU_WIDTH), f32),
        pltpu.VMEM((tt, LRU_WIDTH), f32),
        pltpu.VMEM((tt, RET_QK_WIDTH), f32),
        pltpu.VMEM((tt, RET_QK_WIDTH), f32),
        pltpu.VMEM((tt, RET_WIDTH), bf16),
        pltpu.VMEM((tt, RET_WIDTH), f32),
    ]
    scratch = [pltpu.VMEM((tt, D_MODEL), bf16)] + slot_scratch + slot_scratch + [
        pltpu.VMEM((tt, LRU_WIDTH), f32),
        pltpu.VMEM((tt, LRU_WIDTH), bf16),
        pltpu.VMEM((tt, 2 * LRU_WIDTH), f32),
        pltpu.VMEM((tt, MIX_WIDTH), bf16),
        pltpu.VMEM((SUBLANES, LRU_WIDTH), f32),
        pltpu.VMEM((SUBLANES, LRU_WIDTH), f32),
        pltpu.VMEM((N_PAIRS, LANES, PAIR_V), f32),
    ]
    out = pl.pallas_call(
        functools.partial(_main_kernel, n_tiles),
        grid=(n_total + 1,),
        in_specs=in_specs,
        out_specs=pl.BlockSpec((None, tt, D_MODEL), lambda g: (*fin_idx(g), 0)),
        out_shape=jax.ShapeDtypeStruct((B, S, D_MODEL), x.dtype),
        scratch_shapes=scratch,
        compiler_params=pltpu.CompilerParams(
            dimension_semantics=("arbitrary",),
            vmem_limit_bytes=VMEM_LIMIT_BYTES),
        name="hybrid_main",
    )(x, x, cos_t[N_META:], sin_t[N_META:], g1, w_in, convw, convb, w_rg, w_ig, brg, big, lam, rng, w_out, fng,
      dmask_pair, k_dec, q_dec, even, odd, g_state, bd_mask, lx0, h0, st0)
    return out
```

```python
import functools

import numpy as np
import jax
import jax.numpy as jnp
from jax import lax
from jax.experimental import pallas as pl
from jax.experimental.pallas import tpu as pltpu

f32 = jnp.float32
bf16 = jnp.bfloat16

D_MODEL = 1024
N_META = 16
LRU_WIDTH = 1024
LRU_HEADS = 8
LRU_BLOCK = 128
CONV_WIDTH = 4
LRU_C = 8.0
RET_HEADS = 8
RET_QK_DIM = 64
RET_V_DIM = 128
RET_QK_WIDTH = 512
RET_WIDTH = 1024
CHUNK = 128
ROPE_BASE = 10000.0
MIX_WIDTH = 2048
EPS = 1e-6
QK_SCALE = RET_QK_DIM ** -0.5

OFF_LX, OFF_GATE, OFF_Q, OFF_K, OFF_V, OFF_RG = 0, 1024, 2048, 2560, 3072, 4096
IN_WIDTH = 5120

LANES = 128
SUBLANES = 8
N_PAIRS = RET_HEADS // 2
PAIR_V = 2 * RET_V_DIM
TILE_T = 256
SCAN_ROWS = 16
VMEM_LIMIT_BYTES = 60000 * 1024


def _lane_head():
    return np.arange(RET_QK_WIDTH) // RET_QK_DIM


def _retention_tables():
    log_g = np.log1p(-np.exp2(-5.0 - np.arange(RET_HEADS, dtype=np.float32))).astype(np.float32)
    idx = np.arange(CHUNK, dtype=np.float32)
    diff = idx[:, None] - idx[None, :]
    dmask = np.where(diff[None] >= 0.0, np.exp(np.maximum(diff, 0.0)[None] * log_g[:, None, None]), 0.0)
    dmask_pair = np.concatenate([dmask[0::2], dmask[1::2]], axis=-1)
    lg_lane = log_g[_lane_head()]
    k_dec = np.exp((CHUNK - 1.0 - idx)[:, None] * lg_lane[None, :])
    q_dec = np.exp((idx + 1.0)[:, None] * lg_lane[None, :]) * QK_SCALE
    even = (_lane_head() % 2 == 0)[None, :]
    g_chunk = np.exp(CHUNK * log_g)
    g_state = np.broadcast_to(np.repeat(g_chunk, RET_V_DIM).reshape(N_PAIRS, 1, PAIR_V), (N_PAIRS, LANES, PAIR_V))
    row_par = np.arange(LANES) // RET_QK_DIM
    col_par = np.arange(PAIR_V) // RET_V_DIM
    bd_mask = row_par[:, None] == col_par[None, :]
    as_f32 = lambda a: jnp.asarray(np.asarray(a, np.float32))
    return tuple(as_f32(t) for t in (dmask_pair, k_dec, q_dec, even, ~even, g_state, bd_mask))


def _rotary_tables(n_pos):
    half = RET_QK_DIM // 2
    inv = (np.float32(ROPE_BASE) ** (-np.arange(half, dtype=np.float32) / half)).astype(np.float32)
    ang = np.arange(n_pos).astype(np.float32)[:, None] * inv[None, :]
    cos, sin = np.cos(ang), np.sin(ang)
    cos_t = np.concatenate([cos, cos, cos, cos], axis=-1).astype(np.float32)
    sin_t = np.concatenate([-sin, sin, -sin, sin], axis=-1).astype(np.float32)
    return jnp.asarray(cos_t), jnp.asarray(sin_t)


def _rmsnorm_rows(x, gain_row):
    ms = jnp.mean(x * x, axis=-1, keepdims=True)
    return x * lax.rsqrt(ms + EPS) * gain_row


def _sigmoid(x):
    return 0.5 * jnp.tanh(0.5 * x) + 0.5


def _silu(x):
    hx = 0.5 * x
    return hx * jnp.tanh(hx) + hx


def _conv(n_rows, lx_scr, convw_ref, convb_ref, xc_scr, xcb_scr):
    base = SUBLANES
    xc = convb_ref[...] + convw_ref[3:4, :] * lx_scr[pl.ds(base, n_rows), :]
    xc = xc + convw_ref[2:3, :] * lx_scr[pl.ds(base - 1, n_rows), :]
    xc = xc + convw_ref[1:2, :] * lx_scr[pl.ds(base - 2, n_rows), :]
    xc = xc + convw_ref[0:1, :] * lx_scr[pl.ds(base - 3, n_rows), :]
    xc_scr[...] = xc
    xcb_scr[...] = xc.astype(bf16)


def _gates_head(h, xcb_scr, wrg_ref, wig_ref, pre_scr):
    lanes = slice(h * LRU_BLOCK, (h + 1) * LRU_BLOCK)
    wg = jnp.concatenate([wrg_ref[h], wig_ref[h]], axis=-1).astype(bf16)
    pre = jnp.dot(xcb_scr[:, lanes], wg, preferred_element_type=f32)
    pre_scr[:, lanes] = pre[:, :LRU_BLOCK]
    pre_scr[:, LRU_WIDTH + h * LRU_BLOCK:LRU_WIDTH + (h + 1) * LRU_BLOCK] = pre[:, LRU_BLOCK:]


def _lru_block(xc, pre_r, pre_i, brg, big, c_sp, carry):
    rows, width = xc.shape
    r = _sigmoid(pre_r + brg)
    i = _sigmoid(pre_i + big)
    nl = r * c_sp
    a = jnp.exp(-nl)
    z = jnp.tanh(nl) * (1.0 + a * a)
    beta = jnp.where(z > 0.0, z * lax.rsqrt(z), 0.0)
    b = beta * i * xc
    nv = rows // SUBLANES
    a3 = a.reshape(nv, SUBLANES, width)
    b3 = b.reshape(nv, SUBLANES, width)
    rowid = lax.broadcasted_iota(jnp.int32, (nv, SUBLANES, width), 1)
    for s in (1, 2, 4):
        keep = rowid >= s
        a_s = jnp.where(keep, pltpu.roll(a3, s, 1), 1.0)
        b_s = jnp.where(keep, pltpu.roll(b3, s, 1), 0.0)
        b3 = a3 * b_s + b3
        a3 = a3 * a_s
    hs = []
    for v in range(nv):
        h_v = a3[v] * carry + b3[v]
        carry = h_v[SUBLANES - 1:SUBLANES, :]
        hs.append(h_v)
    return jnp.concatenate(hs, axis=0), carry


def _rotary(t, cos, sin):
    half = RET_QK_DIM // 2
    in_first_half = (lax.broadcasted_iota(jnp.int32, t.shape, 1) // half) % 2 == 0
    partner = jnp.where(in_first_half, pltpu.roll(t, LANES - half, 1), pltpu.roll(t, half, 1))
    return t * cos + partner * sin


def _dot_t0(a, b):
    return lax.dot_general(a, b, (((0,), (0,)), ((), ())), preferred_element_type=f32)


def _dot_nt(a, b):
    return lax.dot_general(a, b, (((1,), (1,)), ((), ())), preferred_element_type=f32)


def _meta_kernel(meta_ref, cos_ref, sin_ref, g1_ref, wlx_ref, wk_ref, wv_ref, convw_ref, convb_ref,
                 wrg_ref, wig_ref, brg_ref, big_ref, lam_ref, kdec_ref, bdm_ref,
                 lx_out, h_out, st_out,
                 lx_scr, xc_scr, xcb_scr, pre_scr):
    u = _rmsnorm_rows(meta_ref[...], g1_ref[...]).astype(bf16)
    lx_scr[0:SUBLANES, :] = jnp.zeros((SUBLANES, LRU_WIDTH), f32)
    lx_scr[SUBLANES:SUBLANES + N_META, :] = jnp.dot(u, wlx_ref[...].astype(bf16), preferred_element_type=f32)
    k = jnp.dot(u, wk_ref[...].astype(bf16), preferred_element_type=f32)
    v = jnp.dot(u, wv_ref[...].astype(bf16), preferred_element_type=f32).astype(bf16)

    _conv(N_META, lx_scr, convw_ref, convb_ref, xc_scr, xcb_scr)
    for h in range(LRU_HEADS):
        _gates_head(h, xcb_scr, wrg_ref, wig_ref, pre_scr)
    lx_out[...] = lx_scr[pl.ds(N_META, SUBLANES), :]
    c_sp = LRU_C * jax.nn.softplus(-lam_ref[...])
    _, carry = _lru_block(xc_scr[...], pre_scr[:, :LRU_WIDTH], pre_scr[:, LRU_WIDTH:],
                          brg_ref[...], big_ref[...], c_sp, jnp.zeros((1, LRU_WIDTH), f32))
    h_out[...] = jnp.broadcast_to(carry, (SUBLANES, LRU_WIDTH))

    cos, sin = cos_ref[...], sin_ref[...]
    for p in range(N_PAIRS):
        sl = slice(p * LANES, (p + 1) * LANES)
        k_rot = _rotary(k[:, sl], cos, sin)
        kd = (k_rot * kdec_ref[CHUNK - N_META:CHUNK, sl]).astype(bf16)
        st_out[p] = _dot_t0(kd, v[:, p * PAIR_V:(p + 1) * PAIR_V]) * bdm_ref[...]


def _norm_tile(x_ref, g1_ref, u_scr):
    u_scr[...] = _rmsnorm_rows(x_ref[...], g1_ref[...]).astype(bf16)


MXU_COLS = 256
N_IN_TILES = IN_WIDTH // MXU_COLS


def _project_col_tile(t, u_scr, win_ref, slot):
    lx_scr, gate_scr, q_scr, k_scr, v_scr, rg_scr = slot
    off = t * MXU_COLS
    res = jnp.dot(u_scr[...], win_ref[:, off:off + MXU_COLS].astype(bf16), preferred_element_type=f32)
    for dst, start, rows, dt in ((lx_scr, OFF_LX, slice(SUBLANES, SUBLANES + TILE_T), f32), (gate_scr, OFF_GATE, slice(None), f32),
                                 (q_scr, OFF_Q, slice(None), f32), (k_scr, OFF_K, slice(None), f32),
                                 (v_scr, OFF_V, slice(None), bf16), (rg_scr, OFF_RG, slice(None), f32)):
        width = dst.shape[1]
        if start <= off < start + width:
            dst[rows, off - start:off - start + MXU_COLS] = res.astype(dt)
            return
    raise AssertionError("column tile outside the projection")


def _finish_tile(x_ref, cos_ref, sin_ref, convw_ref, convb_ref, wrg_ref, wig_ref, brg_ref, big_ref, lam_ref, rng_ref,
                 wout_ref, fng_ref, dmaskp_ref, kdec_ref, qdec_ref, even_ref, odd_ref, gst_ref, bdm_ref,
                 o_ref, slot, xc_scr, xcb_scr, pre_scr, y_scr, lxtail_scr, h_scr, st_scr, after_gates):
    lx_scr, gate_scr, q_scr, k_scr, v_scr, rg_scr = slot
    tt = TILE_T

    lx_scr[0:SUBLANES, :] = lxtail_scr[...]
    _conv(tt, lx_scr, convw_ref, convb_ref, xc_scr, xcb_scr)
    lxtail_scr[...] = lx_scr[pl.ds(tt, SUBLANES), :]
    for h in range(LRU_HEADS):
        lanes = slice(h * LRU_BLOCK, (h + 1) * LRU_BLOCK)
        ilanes = slice(LRU_WIDTH + h * LRU_BLOCK, LRU_WIDTH + (h + 1) * LRU_BLOCK)
        _gates_head(h, xcb_scr, wrg_ref, wig_ref, pre_scr)
        after_gates(h)
        c_sp = LRU_C * jax.nn.softplus(-lam_ref[:, lanes])
        brg, big = brg_ref[:, lanes], big_ref[:, lanes]
        carry = h_scr[0:1, lanes]
        for it in range(tt // SCAN_ROWS):
            rows = slice(it * SCAN_ROWS, (it + 1) * SCAN_ROWS)
            hb, carry = _lru_block(xc_scr[rows, lanes], pre_scr[rows, lanes], pre_scr[rows, ilanes], brg, big, c_sp, carry)
            y_scr[rows, lanes] = (hb * _silu(gate_scr[rows, lanes])).astype(bf16)
        h_scr[:, lanes] = jnp.broadcast_to(carry, (SUBLANES, LRU_BLOCK))

    first_head_cols = lax.broadcasted_iota(jnp.int32, (CHUNK, PAIR_V), 1) < RET_V_DIM
    for c in range(tt // CHUNK):
        rows = slice(c * CHUNK, (c + 1) * CHUNK)
        cos, sin = cos_ref[rows, :], sin_ref[rows, :]
        for p in range(N_PAIRS):
            sl = slice(p * LANES, (p + 1) * LANES)
            q_rot = _rotary(q_scr[rows, sl], cos, sin)
            k_rot = _rotary(k_scr[rows, sl], cos, sin)
            q_b = (q_rot * QK_SCALE).astype(bf16)
            q_d = (q_rot * qdec_ref[:, sl]).astype(bf16)
            k_d = (k_rot * kdec_ref[:, sl]).astype(bf16)
            kk = jnp.concatenate([(k_rot * even_ref[:, sl]).astype(bf16), (k_rot * odd_ref[:, sl]).astype(bf16)], axis=0)
            v_pair = v_scr[rows, p * PAIR_V:(p + 1) * PAIR_V]
            zeros = jnp.zeros_like(v_pair)
            v_bd = jnp.concatenate([jnp.where(first_head_cols, v_pair, zeros),
                                    jnp.where(first_head_cols, zeros, v_pair)], axis=0)
            st_pair = st_scr[p]
            s = _dot_nt(q_b, kk) * dmaskp_ref[p]
            o = jnp.dot(s.astype(bf16), v_bd, preferred_element_type=f32)
            o = o + jnp.dot(q_d, st_pair.astype(bf16), preferred_element_type=f32)
            for e in range(2):
                h = 2 * p + e
                hs = slice(h * RET_V_DIM, (h + 1) * RET_V_DIM)
                o_h = o[:, e * RET_V_DIM:(e + 1) * RET_V_DIM]
                mu = jnp.mean(o_h, axis=-1, keepdims=True)
                oc = o_h - mu
                var = jnp.mean(oc * oc, axis=-1, keepdims=True)
                on = oc * lax.rsqrt(var + EPS) * rng_ref[:, hs]
                y_scr[rows, LRU_WIDTH + h * RET_V_DIM:LRU_WIDTH + (h + 1) * RET_V_DIM] = (
                    on * _silu(rg_scr[rows, hs])).astype(bf16)
            kv = _dot_t0(k_d, v_pair)
            st_scr[p] = gst_ref[p] * st_pair + bdm_ref[...] * kv

    res = x_ref[...] + jnp.dot(y_scr[...], wout_ref[...].astype(bf16), preferred_element_type=f32)
    o_ref[...] = _rmsnorm_rows(res, fng_ref[...])


def _main_kernel(n_tiles,
                 xa_ref, xb_ref, cos_ref, sin_ref, g1_ref, win_ref, convw_ref, convb_ref, wrg_ref, wig_ref,
                 brg_ref, big_ref, lam_ref, rng_ref, wout_ref, fng_ref, dmaskp_ref, kdec_ref, qdec_ref, even_ref, odd_ref,
                 gst_ref, bdm_ref, lx0_ref, h0_ref, st0_ref,
                 o_ref,
                 u_scr,
                 lx_a, gate_a, q_a, k_a, v_a, rg_a,
                 lx_b, gate_b, q_b, k_b, v_b, rg_b,
                 xc_scr, xcb_scr, pre_scr, y_scr, lxtail_scr, h_scr, st_scr):
    g = pl.program_id(0)
    slots = ((lx_a, gate_a, q_a, k_a, v_a, rg_a), (lx_b, gate_b, q_b, k_b, v_b, rg_b))

    @pl.when(g == 0)
    def _():
        for ref in slots[1]:
            ref[...] = jnp.zeros(ref.shape, ref.dtype)
        lxtail_scr[...] = jnp.zeros(lxtail_scr.shape, f32)
        h_scr[...] = jnp.zeros(h_scr.shape, f32)
        st_scr[...] = jnp.zeros(st_scr.shape, f32)

    @pl.when(lax.rem(g + n_tiles - 1, n_tiles) == 0)
    def _():
        lxtail_scr[...] = lx0_ref[...]
        h_scr[...] = h0_ref[...]
        st_scr[...] = st0_ref[...]

    tiles_after_head = (3, 3, 3, 3, 2, 2, 2, 2)
    assert sum(tiles_after_head) == N_IN_TILES and len(tiles_after_head) == LRU_HEADS

    def step(slot_project, slot_finish):
        _norm_tile(xa_ref, g1_ref, u_scr)

        def after_gates(h):
            first = sum(tiles_after_head[:h])
            for t in range(first, first + tiles_after_head[h]):
                _project_col_tile(t, u_scr, win_ref, slot_project)

        _finish_tile(xb_ref, cos_ref, sin_ref, convw_ref, convb_ref, wrg_ref, wig_ref, brg_ref, big_ref, lam_ref, rng_ref,
                     wout_ref, fng_ref, dmaskp_ref, kdec_ref, qdec_ref, even_ref, odd_ref, gst_ref, bdm_ref,
                     o_ref, slot_finish, xc_scr, xcb_scr, pre_scr, y_scr, lxtail_scr, h_scr, st_scr, after_gates)

    @pl.when(lax.rem(g, 2) == 0)
    def _():
        step(slots[0], slots[1])

    @pl.when(lax.rem(g, 2) == 1)
    def _():
        step(slots[1], slots[0])


def _const_spec(shape):
    nd = len(shape)
    return pl.BlockSpec(shape, lambda g, _nd=nd: (0,) * _nd, pipeline_mode=pl.Buffered(1))


def kernel(x, meta_tokens, norm_gain, w_in, conv_w, conv_b, w_rg, b_rg, w_ig, b_ig,
           lru_lambda, ret_norm_gain, w_out, final_norm_gain):
    B, S, D = x.shape
    assert D == D_MODEL and S % TILE_T == 0 and TILE_T % CHUNK == 0
    assert norm_gain.shape[0] == 1, "single-layer block"
    assert meta_tokens.shape == (N_META, D_MODEL)

    assert w_in.shape == (1, D_MODEL, IN_WIDTH) and w_out.shape == (1, MIX_WIDTH, D_MODEL)
    assert w_rg.shape == w_ig.shape == (1, LRU_HEADS, LRU_BLOCK, LRU_BLOCK)

    g1 = norm_gain[0].reshape(1, D_MODEL)
    fng = final_norm_gain.reshape(1, D_MODEL)
    convw = conv_w[0]
    convb = conv_b[0].reshape(1, LRU_WIDTH)
    brg = b_rg[0].reshape(1, LRU_WIDTH)
    big = b_ig[0].reshape(1, LRU_WIDTH)
    lam = lru_lambda[0].reshape(1, LRU_WIDTH)
    rng = ret_norm_gain[0].reshape(1, RET_WIDTH)

    dmask_pair, k_dec, q_dec, even, odd, g_state, bd_mask = _retention_tables()
    cos_t, sin_t = _rotary_tables(N_META + S)

    n_lx = SUBLANES + N_META
    whole = lambda a: pl.BlockSpec(a.shape, lambda i, _nd=a.ndim: (0,) * _nd)
    w_in_cols = lambda off, width: pl.BlockSpec((None, D_MODEL, width), lambda i, _b=off // width: (0, 0, _b))
    gate_w_spec = pl.BlockSpec((None, LRU_HEADS, LRU_BLOCK, LRU_BLOCK), lambda i: (0, 0, 0, 0))
    cos_m, sin_m = cos_t[:N_META], sin_t[:N_META]
    meta_out_shapes = ((SUBLANES, LRU_WIDTH), (SUBLANES, LRU_WIDTH), (N_PAIRS, LANES, PAIR_V))
    lx0, h0, st0 = pl.pallas_call(
        _meta_kernel,
        grid=(1,),
        out_specs=tuple(pl.BlockSpec(s, lambda i, _nd=len(s): (0,) * _nd) for s in meta_out_shapes),
        in_specs=[whole(meta_tokens), whole(cos_m), whole(sin_m), whole(g1),
                  w_in_cols(OFF_LX, LRU_WIDTH), w_in_cols(OFF_K, RET_QK_WIDTH), w_in_cols(OFF_V, RET_WIDTH),
                  whole(convw), whole(convb), gate_w_spec, gate_w_spec,
                  whole(brg), whole(big), whole(lam), whole(k_dec), whole(bd_mask)],
        out_shape=(jax.ShapeDtypeStruct((SUBLANES, LRU_WIDTH), f32),
                   jax.ShapeDtypeStruct((SUBLANES, LRU_WIDTH), f32),
                   jax.ShapeDtypeStruct((N_PAIRS, LANES, PAIR_V), f32)),
        scratch_shapes=[pltpu.VMEM((n_lx, LRU_WIDTH), f32),
                        pltpu.VMEM((N_META, LRU_WIDTH), f32),
                        pltpu.VMEM((N_META, LRU_WIDTH), bf16),
                        pltpu.VMEM((N_META, 2 * LRU_WIDTH), f32)],
        compiler_params=pltpu.CompilerParams(vmem_limit_bytes=VMEM_LIMIT_BYTES),
        name="hybrid_meta_state",
    )(meta_tokens, cos_m, sin_m, g1, w_in, w_in, w_in, convw, convb, w_rg, w_ig, brg, big, lam, k_dec, bd_mask)

    tt = TILE_T
    n_tiles = S // tt
    n_total = B * n_tiles

    def proj_idx(g):
        gp = jnp.minimum(g, n_total - 1)
        return gp // n_tiles, gp % n_tiles

    def fin_idx(g):
        gf = jnp.maximum(g - 1, 0)
        return gf // n_tiles, gf % n_tiles

    rot_spec = pl.BlockSpec((tt, LANES), lambda g: (fin_idx(g)[1], 0))
    in_specs = [
        pl.BlockSpec((None, tt, D_MODEL), lambda g: (*proj_idx(g), 0)),
        pl.BlockSpec((None, tt, D_MODEL), lambda g: (*fin_idx(g), 0)),
        rot_spec, rot_spec,
        _const_spec((1, D_MODEL)),
        _const_spec((None, D_MODEL, IN_WIDTH)),
        _const_spec((CONV_WIDTH, LRU_WIDTH)), _const_spec((1, LRU_WIDTH)),
        _const_spec((None, LRU_HEADS, LRU_BLOCK, LRU_BLOCK)),
        _const_spec((None, LRU_HEADS, LRU_BLOCK, LRU_BLOCK)),
        _const_spec((1, LRU_WIDTH)), _const_spec((1, LRU_WIDTH)), _const_spec((1, LRU_WIDTH)),
        _const_spec((1, RET_WIDTH)),
        _const_spec((None, MIX_WIDTH, D_MODEL)),
        _const_spec((1, D_MODEL)),
        _const_spec((N_PAIRS, CHUNK, 2 * CHUNK)),
        _const_spec((CHUNK, RET_QK_WIDTH)), _const_spec((CHUNK, RET_QK_WIDTH)),
        _const_spec((1, RET_QK_WIDTH)), _const_spec((1, RET_QK_WIDTH)),
        _const_spec((N_PAIRS, LANES, PAIR_V)),
        _const_spec((LANES, PAIR_V)),
        _const_spec((SUBLANES, LRU_WIDTH)), _const_spec((SUBLANES, LRU_WIDTH)),
        _const_spec((N_PAIRS, LANES, PAIR_V)),
    ]
    slot_scratch = [
        pltpu.VMEM((SUBLANES + tt, LRU_WIDTH), f32),
        pltpu.VMEM((tt, LRU_WIDTH), f32),
        pltpu.VMEM((tt, RET_QK_WIDTH), f32),
        pltpu.VMEM((tt, RET_QK_WIDTH), f32),
        pltpu.VMEM((tt, RET_WIDTH), bf16),
        pltpu.VMEM((tt, RET_WIDTH), f32),
    ]
    scratch = [pltpu.VMEM((tt, D_MODEL), bf16)] + slot_scratch + slot_scratch + [
        pltpu.VMEM((tt, LRU_WIDTH), f32),
        pltpu.VMEM((tt, LRU_WIDTH), bf16),
        pltpu.VMEM((tt, 2 * LRU_WIDTH), f32),
        pltpu.VMEM((tt, MIX_WIDTH), bf16),
        pltpu.VMEM((SUBLANES, LRU_WIDTH), f32),
        pltpu.VMEM((SUBLANES, LRU_WIDTH), f32),
        pltpu.VMEM((N_PAIRS, LANES, PAIR_V), f32),
    ]
    out = pl.pallas_call(
        functools.partial(_main_kernel, n_tiles),
        grid=(n_total + 1,),
        in_specs=in_specs,
        out_specs=pl.BlockSpec((None, tt, D_MODEL), lambda g: (*fin_idx(g), 0)),
        out_shape=jax.ShapeDtypeStruct((B, S, D_MODEL), x.dtype),
        scratch_shapes=scratch,
        compiler_params=pltpu.CompilerParams(
            dimension_semantics=("arbitrary",),
            vmem_limit_bytes=VMEM_LIMIT_BYTES),
        name="hybrid_main",
    )(x, x, cos_t[N_META:], sin_t[N_META:], g1, w_in, convw, convb, w_rg, w_ig, brg, big, lam, rng, w_out, fng,
      dmask_pair, k_dec, q_dec, even, odd, g_state, bd_mask, lx0, h0, st0)
    return out
```

```python
import functools

import numpy as np
import jax
import jax.numpy as jnp
from jax import lax
from jax.experimental import pallas as pl
from jax.experimental.pallas import tpu as pltpu

f32 = jnp.float32
bf16 = jnp.bfloat16

D_MODEL = 1024
N_META = 16
LRU_WIDTH = 1024
LRU_HEADS = 8
LRU_BLOCK = 128
CONV_WIDTH = 4
LRU_C = 8.0
RET_HEADS = 8
RET_QK_DIM = 64
RET_V_DIM = 128
RET_QK_WIDTH = 512
RET_WIDTH = 1024
CHUNK = 128
ROPE_BASE = 10000.0
MIX_WIDTH = 2048
EPS = 1e-6
QK_SCALE = RET_QK_DIM ** -0.5

OFF_LX, OFF_GATE, OFF_Q, OFF_K, OFF_V, OFF_RG = 0, 1024, 2048, 2560, 3072, 4096
IN_WIDTH = 5120

LANES = 128
SUBLANES = 8
N_PAIRS = RET_HEADS // 2
PAIR_V = 2 * RET_V_DIM
TILE_T = 256
SCAN_ROWS = 32
VMEM_LIMIT_BYTES = 60000 * 1024


def _lane_head():
    return np.arange(RET_QK_WIDTH) // RET_QK_DIM


def _retention_tables():
    log_g = np.log1p(-np.exp2(-5.0 - np.arange(RET_HEADS, dtype=np.float32))).astype(np.float32)
    idx = np.arange(CHUNK, dtype=np.float32)
    diff = idx[:, None] - idx[None, :]
    dmask = np.where(diff[None] >= 0.0, np.exp(np.maximum(diff, 0.0)[None] * log_g[:, None, None]), 0.0)
    dmask_pair = np.concatenate([dmask[0::2], dmask[1::2]], axis=-1)
    lg_lane = log_g[_lane_head()]
    k_dec = np.exp((CHUNK - 1.0 - idx)[:, None] * lg_lane[None, :])
    q_dec = np.exp((idx + 1.0)[:, None] * lg_lane[None, :]) * QK_SCALE
    even = (_lane_head() % 2 == 0)[None, :]
    g_chunk = np.exp(CHUNK * log_g)
    g_state = np.broadcast_to(np.repeat(g_chunk, RET_V_DIM).reshape(N_PAIRS, 1, PAIR_V), (N_PAIRS, LANES, PAIR_V))
    row_par = np.arange(LANES) // RET_QK_DIM
    col_par = np.arange(PAIR_V) // RET_V_DIM
    bd_mask = row_par[:, None] == col_par[None, :]
    as_f32 = lambda a: jnp.asarray(np.asarray(a, np.float32))
    return tuple(as_f32(t) for t in (dmask_pair, k_dec, q_dec, even, ~even, g_state, bd_mask))


def _rotary_tables(n_pos):
    half = RET_QK_DIM // 2
    inv = (np.float32(ROPE_BASE) ** (-np.arange(half, dtype=np.float32) / half)).astype(np.float32)
    ang = np.arange(n_pos).astype(np.float32)[:, None] * inv[None, :]
    cos, sin = np.cos(ang), np.sin(ang)
    cos_t = np.concatenate([cos, cos, cos, cos], axis=-1).astype(np.float32)
    sin_t = np.concatenate([-sin, sin, -sin, sin], axis=-1).astype(np.float32)
    return jnp.asarray(cos_t), jnp.asarray(sin_t)


def _rmsnorm_rows(x, gain_row):
    ms = jnp.mean(x * x, axis=-1, keepdims=True)
    return x * lax.rsqrt(ms + EPS) * gain_row


def _sigmoid(x):
    return 0.5 * jnp.tanh(0.5 * x) + 0.5


def _silu(x):
    hx = 0.5 * x
    return hx * jnp.tanh(hx) + hx


def _conv(n_rows, lx_scr, convw_ref, convb_ref, xc_scr, xcb_scr):
    base = SUBLANES
    xc = convb_ref[...] + convw_ref[3:4, :] * lx_scr[pl.ds(base, n_rows), :]
    xc = xc + convw_ref[2:3, :] * lx_scr[pl.ds(base - 1, n_rows), :]
    xc = xc + convw_ref[1:2, :] * lx_scr[pl.ds(base - 2, n_rows), :]
    xc = xc + convw_ref[0:1, :] * lx_scr[pl.ds(base - 3, n_rows), :]
    xc_scr[...] = xc
    xcb_scr[...] = xc.astype(bf16)


def _gates_head(h, xcb_scr, wrg_ref, wig_ref, pre_scr):
    lanes = slice(h * LRU_BLOCK, (h + 1) * LRU_BLOCK)
    wg = jnp.concatenate([wrg_ref[h], wig_ref[h]], axis=-1).astype(bf16)
    pre = jnp.dot(xcb_scr[:, lanes], wg, preferred_element_type=f32)
    pre_scr[:, lanes] = pre[:, :LRU_BLOCK]
    pre_scr[:, LRU_WIDTH + h * LRU_BLOCK:LRU_WIDTH + (h + 1) * LRU_BLOCK] = pre[:, LRU_BLOCK:]


def _lru_block(xc, pre_r, pre_i, brg, big, c_sp, carry):
    rows, width = xc.shape
    r = _sigmoid(pre_r + brg)
    i = _sigmoid(pre_i + big)
    nl = r * c_sp
    a = jnp.exp(-nl)
    z = jnp.tanh(nl) * (1.0 + a * a)
    beta = jnp.where(z > 0.0, z * lax.rsqrt(z), 0.0)
    b = beta * i * xc
    nv = rows // SUBLANES
    a3 = a.reshape(nv, SUBLANES, width)
    b3 = b.reshape(nv, SUBLANES, width)
    rowid = lax.broadcasted_iota(jnp.int32, (nv, SUBLANES, width), 1)
    for s in (1, 2, 4):
        keep = rowid >= s
        a_s = jnp.where(keep, pltpu.roll(a3, s, 1), 1.0)
        b_s = jnp.where(keep, pltpu.roll(b3, s, 1), 0.0)
        b3 = a3 * b_s + b3
        a3 = a3 * a_s
    hs = []
    for v in range(nv):
        h_v = a3[v] * carry + b3[v]
        carry = h_v[SUBLANES - 1:SUBLANES, :]
        hs.append(h_v)
    return jnp.concatenate(hs, axis=0), carry


def _rotary(t, cos, sin):
    half = RET_QK_DIM // 2
    in_first_half = (lax.broadcasted_iota(jnp.int32, t.shape, 1) // half) % 2 == 0
    partner = jnp.where(in_first_half, pltpu.roll(t, LANES - half, 1), pltpu.roll(t, half, 1))
    return t * cos + partner * sin


def _dot_t0(a, b):
    return lax.dot_general(a, b, (((0,), (0,)), ((), ())), preferred_element_type=f32)


def _dot_nt(a, b):
    return lax.dot_general(a, b, (((1,), (1,)), ((), ())), preferred_element_type=f32)


def _meta_kernel(meta_ref, cos_ref, sin_ref, g1_ref, wlx_ref, wk_ref, wv_ref, convw_ref, convb_ref,
                 wrg_ref, wig_ref, brg_ref, big_ref, lam_ref, kdec_ref, bdm_ref,
                 lx_out, h_out, st_out,
                 lx_scr, xc_scr, xcb_scr, pre_scr):
    u = _rmsnorm_rows(meta_ref[...], g1_ref[...]).astype(bf16)
    lx_scr[0:SUBLANES, :] = jnp.zeros((SUBLANES, LRU_WIDTH), f32)
    lx_scr[SUBLANES:SUBLANES + N_META, :] = jnp.dot(u, wlx_ref[...].astype(bf16), preferred_element_type=f32)
    k = jnp.dot(u, wk_ref[...].astype(bf16), preferred_element_type=f32)
    v = jnp.dot(u, wv_ref[...].astype(bf16), preferred_element_type=f32).astype(bf16)

    _conv(N_META, lx_scr, convw_ref, convb_ref, xc_scr, xcb_scr)
    for h in range(LRU_HEADS):
        _gates_head(h, xcb_scr, wrg_ref, wig_ref, pre_scr)
    lx_out[...] = lx_scr[pl.ds(N_META, SUBLANES), :]
    c_sp = LRU_C * jax.nn.softplus(-lam_ref[...])
    _, carry = _lru_block(xc_scr[...], pre_scr[:, :LRU_WIDTH], pre_scr[:, LRU_WIDTH:],
                          brg_ref[...], big_ref[...], c_sp, jnp.zeros((1, LRU_WIDTH), f32))
    h_out[...] = jnp.broadcast_to(carry, (SUBLANES, LRU_WIDTH))

    cos, sin = cos_ref[...], sin_ref[...]
    for p in range(N_PAIRS):
        sl = slice(p * LANES, (p + 1) * LANES)
        k_rot = _rotary(k[:, sl], cos, sin)
        kd = (k_rot * kdec_ref[CHUNK - N_META:CHUNK, sl]).astype(bf16)
        st_out[p] = _dot_t0(kd, v[:, p * PAIR_V:(p + 1) * PAIR_V]) * bdm_ref[...]


def _norm_tile(x_ref, g1_ref, u_scr):
    u_scr[...] = _rmsnorm_rows(x_ref[...], g1_ref[...]).astype(bf16)


MXU_COLS = 256
N_IN_TILES = IN_WIDTH // MXU_COLS


def _project_col_tile(t, u_scr, win_ref, slot):
    lx_scr, gate_scr, q_scr, k_scr, v_scr, rg_scr = slot
    off = t * MXU_COLS
    res = jnp.dot(u_scr[...], win_ref[:, off:off + MXU_COLS].astype(bf16), preferred_element_type=f32)
    for dst, start, rows, dt in ((lx_scr, OFF_LX, slice(SUBLANES, SUBLANES + TILE_T), f32), (gate_scr, OFF_GATE, slice(None), f32),
                                 (q_scr, OFF_Q, slice(None), f32), (k_scr, OFF_K, slice(None), f32),
                                 (v_scr, OFF_V, slice(None), bf16), (rg_scr, OFF_RG, slice(None), f32)):
        width = dst.shape[1]
        if start <= off < start + width:
            dst[rows, off - start:off - start + MXU_COLS] = res.astype(dt)
            return
    raise AssertionError("column tile outside the projection")


def _finish_tile(x_ref, cos_ref, sin_ref, convw_ref, convb_ref, wrg_ref, wig_ref, brg_ref, big_ref, lam_ref, rng_ref,
                 wout_ref, fng_ref, dmaskp_ref, kdec_ref, qdec_ref, even_ref, odd_ref, gst_ref, bdm_ref,
                 o_ref, slot, xc_scr, xcb_scr, pre_scr, y_scr, lxtail_scr, h_scr, st_scr, after_gates):
    lx_scr, gate_scr, q_scr, k_scr, v_scr, rg_scr = slot
    tt = TILE_T

    lx_scr[0:SUBLANES, :] = lxtail_scr[...]
    _conv(tt, lx_scr, convw_ref, convb_ref, xc_scr, xcb_scr)
    lxtail_scr[...] = lx_scr[pl.ds(tt, SUBLANES), :]
    for h in range(LRU_HEADS):
        lanes = slice(h * LRU_BLOCK, (h + 1) * LRU_BLOCK)
        ilanes = slice(LRU_WIDTH + h * LRU_BLOCK, LRU_WIDTH + (h + 1) * LRU_BLOCK)
        _gates_head(h, xcb_scr, wrg_ref, wig_ref, pre_scr)
        after_gates(h)
        c_sp = LRU_C * jax.nn.softplus(-lam_ref[:, lanes])
        brg, big = brg_ref[:, lanes], big_ref[:, lanes]
        carry = h_scr[0:1, lanes]
        for it in range(tt // SCAN_ROWS):
            rows = slice(it * SCAN_ROWS, (it + 1) * SCAN_ROWS)
            hb, carry = _lru_block(xc_scr[rows, lanes], pre_scr[rows, lanes], pre_scr[rows, ilanes], brg, big, c_sp, carry)
            y_scr[rows, lanes] = (hb * _silu(gate_scr[rows, lanes])).astype(bf16)
        h_scr[:, lanes] = jnp.broadcast_to(carry, (SUBLANES, LRU_BLOCK))

    first_head_cols = lax.broadcasted_iota(jnp.int32, (CHUNK, PAIR_V), 1) < RET_V_DIM
    for c in range(tt // CHUNK):
        rows = slice(c * CHUNK, (c + 1) * CHUNK)
        cos, sin = cos_ref[rows, :], sin_ref[rows, :]
        for p in range(N_PAIRS):
            sl = slice(p * LANES, (p + 1) * LANES)
            q_rot = _rotary(q_scr[rows, sl], cos, sin)
            k_rot = _rotary(k_scr[rows, sl], cos, sin)
            q_b = (q_rot * QK_SCALE).astype(bf16)
            q_d = (q_rot * qdec_ref[:, sl]).astype(bf16)
            k_d = (k_rot * kdec_ref[:, sl]).astype(bf16)
            kk = jnp.concatenate([(k_rot * even_ref[:, sl]).astype(bf16), (k_rot * odd_ref[:, sl]).astype(bf16)], axis=0)
            v_pair = v_scr[rows, p * PAIR_V:(p + 1) * PAIR_V]
            zeros = jnp.zeros_like(v_pair)
            v_bd = jnp.concatenate([jnp.where(first_head_cols, v_pair, zeros),
                                    jnp.where(first_head_cols, zeros, v_pair)], axis=0)
            st_pair = st_scr[p]
            s = _dot_nt(q_b, kk) * dmaskp_ref[p]
            o = jnp.dot(s.astype(bf16), v_bd, preferred_element_type=f32)
            o = o + jnp.dot(q_d, st_pair.astype(bf16), preferred_element_type=f32)
            for e in range(2):
                h = 2 * p + e
                hs = slice(h * RET_V_DIM, (h + 1) * RET_V_DIM)
                o_h = o[:, e * RET_V_DIM:(e + 1) * RET_V_DIM]
                mu = jnp.mean(o_h, axis=-1, keepdims=True)
                oc = o_h - mu
                var = jnp.mean(oc * oc, axis=-1, keepdims=True)
                on = oc * lax.rsqrt(var + EPS) * rng_ref[:, hs]
                y_scr[rows, LRU_WIDTH + h * RET_V_DIM:LRU_WIDTH + (h + 1) * RET_V_DIM] = (
                    on * _silu(rg_scr[rows, hs])).astype(bf16)
            kv = _dot_t0(k_d, v_pair)
            st_scr[p] = gst_ref[p] * st_pair + bdm_ref[...] * kv

    res = x_ref[...] + jnp.dot(y_scr[...], wout_ref[...].astype(bf16), preferred_element_type=f32)
    o_ref[...] = _rmsnorm_rows(res, fng_ref[...])


def _main_kernel(n_tiles,
                 xa_ref, xb_ref, cos_ref, sin_ref, g1_ref, win_ref, convw_ref, convb_ref, wrg_ref, wig_ref,
                 brg_ref, big_ref, lam_ref, rng_ref, wout_ref, fng_ref, dmaskp_ref, kdec_ref, qdec_ref, even_ref, odd_ref,
                 gst_ref, bdm_ref, lx0_ref, h0_ref, st0_ref,
                 o_ref,
                 u_scr,
                 lx_a, gate_a, q_a, k_a, v_a, rg_a,
                 lx_b, gate_b, q_b, k_b, v_b, rg_b,
                 xc_scr, xcb_scr, pre_scr, y_scr, lxtail_scr, h_scr, st_scr):
    g = pl.program_id(0)
    slots = ((lx_a, gate_a, q_a, k_a, v_a, rg_a), (lx_b, gate_b, q_b, k_b, v_b, rg_b))

    @pl.when(g == 0)
    def _():
        for ref in slots[1]:
            ref[...] = jnp.zeros(ref.shape, ref.dtype)
        lxtail_scr[...] = jnp.zeros(lxtail_scr.shape, f32)
        h_scr[...] = jnp.zeros(h_scr.shape, f32)
        st_scr[...] = jnp.zeros(st_scr.shape, f32)

    @pl.when(lax.rem(g + n_tiles - 1, n_tiles) == 0)
    def _():
        lxtail_scr[...] = lx0_ref[...]
        h_scr[...] = h0_ref[...]
        st_scr[...] = st0_ref[...]

    tiles_after_head = (4, 4, 3, 3, 2, 2, 1, 1)
    assert sum(tiles_after_head) == N_IN_TILES and len(tiles_after_head) == LRU_HEADS

    def step(slot_project, slot_finish):
        _norm_tile(xa_ref, g1_ref, u_scr)

        def after_gates(h):
            first = sum(tiles_after_head[:h])
            for t in range(first, first + tiles_after_head[h]):
                _project_col_tile(t, u_scr, win_ref, slot_project)

        _finish_tile(xb_ref, cos_ref, sin_ref, convw_ref, convb_ref, wrg_ref, wig_ref, brg_ref, big_ref, lam_ref, rng_ref,
                     wout_ref, fng_ref, dmaskp_ref, kdec_ref, qdec_ref, even_ref, odd_ref, gst_ref, bdm_ref,
                     o_ref, slot_finish, xc_scr, xcb_scr, pre_scr, y_scr, lxtail_scr, h_scr, st_scr, after_gates)

    @pl.when(lax.rem(g, 2) == 0)
    def _():
        step(slots[0], slots[1])

    @pl.when(lax.rem(g, 2) == 1)
    def _():
        step(slots[1], slots[0])


def _const_spec(shape):
    nd = len(shape)
    return pl.BlockSpec(shape, lambda g, _nd=nd: (0,) * _nd, pipeline_mode=pl.Buffered(1))


def kernel(x, meta_tokens, norm_gain, w_in, conv_w, conv_b, w_rg, b_rg, w_ig, b_ig,
           lru_lambda, ret_norm_gain, w_out, final_norm_gain):
    B, S, D = x.shape
    assert D == D_MODEL and S % TILE_T == 0 and TILE_T % CHUNK == 0
    assert norm_gain.shape[0] == 1, "single-layer block"
    assert meta_tokens.shape == (N_META, D_MODEL)

    assert w_in.shape == (1, D_MODEL, IN_WIDTH) and w_out.shape == (1, MIX_WIDTH, D_MODEL)
    assert w_rg.shape == w_ig.shape == (1, LRU_HEADS, LRU_BLOCK, LRU_BLOCK)

    g1 = norm_gain[0].reshape(1, D_MODEL)
    fng = final_norm_gain.reshape(1, D_MODEL)
    convw = conv_w[0]
    convb = conv_b[0].reshape(1, LRU_WIDTH)
    brg = b_rg[0].reshape(1, LRU_WIDTH)
    big = b_ig[0].reshape(1, LRU_WIDTH)
    lam = lru_lambda[0].reshape(1, LRU_WIDTH)
    rng = ret_norm_gain[0].reshape(1, RET_WIDTH)

    dmask_pair, k_dec, q_dec, even, odd, g_state, bd_mask = _retention_tables()
    cos_t, sin_t = _rotary_tables(N_META + S)

    n_lx = SUBLANES + N_META
    whole = lambda a: pl.BlockSpec(a.shape, lambda i, _nd=a.ndim: (0,) * _nd)
    w_in_cols = lambda off, width: pl.BlockSpec((None, D_MODEL, width), lambda i, _b=off // width: (0, 0, _b))
    gate_w_spec = pl.BlockSpec((None, LRU_HEADS, LRU_BLOCK, LRU_BLOCK), lambda i: (0, 0, 0, 0))
    cos_m, sin_m = cos_t[:N_META], sin_t[:N_META]
    meta_out_shapes = ((SUBLANES, LRU_WIDTH), (SUBLANES, LRU_WIDTH), (N_PAIRS, LANES, PAIR_V))
    lx0, h0, st0 = pl.pallas_call(
        _meta_kernel,
        grid=(1,),
        out_specs=tuple(pl.BlockSpec(s, lambda i, _nd=len(s): (0,) * _nd) for s in meta_out_shapes),
        in_specs=[whole(meta_tokens), whole(cos_m), whole(sin_m), whole(g1),
                  w_in_cols(OFF_LX, LRU_WIDTH), w_in_cols(OFF_K, RET_QK_WIDTH), w_in_cols(OFF_V, RET_WIDTH),
                  whole(convw), whole(convb), gate_w_spec, gate_w_spec,
                  whole(brg), whole(big), whole(lam), whole(k_dec), whole(bd_mask)],
        out_shape=(jax.ShapeDtypeStruct((SUBLANES, LRU_WIDTH), f32),
                   jax.ShapeDtypeStruct((SUBLANES, LRU_WIDTH), f32),
                   jax.ShapeDtypeStruct((N_PAIRS, LANES, PAIR_V), f32)),
        scratch_shapes=[pltpu.VMEM((n_lx, LRU_WIDTH), f32),
                        pltpu.VMEM((N_META, LRU_WIDTH), f32),
                        pltpu.VMEM((N_META, LRU_WIDTH), bf16),
                        pltpu.VMEM((N_META, 2 * LRU_WIDTH), f32)],
        compiler_params=pltpu.CompilerParams(vmem_limit_bytes=VMEM_LIMIT_BYTES),
        name="hybrid_meta_state",
    )(meta_tokens, cos_m, sin_m, g1, w_in, w_in, w_in, convw, convb, w_rg, w_ig, brg, big, lam, k_dec, bd_mask)

    tt = TILE_T
    n_tiles = S // tt
    n_total = B * n_tiles

    def proj_idx(g):
        gp = jnp.minimum(g, n_total - 1)
        return gp // n_tiles, gp % n_tiles

    def fin_idx(g):
        gf = jnp.maximum(g - 1, 0)
        return gf // n_tiles, gf % n_tiles

    rot_spec = pl.BlockSpec((tt, LANES), lambda g: (fin_idx(g)[1], 0))
    in_specs = [
        pl.BlockSpec((None, tt, D_MODEL), lambda g: (*proj_idx(g), 0)),
        pl.BlockSpec((None, tt, D_MODEL), lambda g: (*fin_idx(g), 0)),
        rot_spec, rot_spec,
        _const_spec((1, D_MODEL)),
        _const_spec((None, D_MODEL, IN_WIDTH)),
        _const_spec((CONV_WIDTH, LRU_WIDTH)), _const_spec((1, LRU_WIDTH)),
        _const_spec((None, LRU_HEADS, LRU_BLOCK, LRU_BLOCK)),
        _const_spec((None, LRU_HEADS, LRU_BLOCK, LRU_BLOCK)),
        _const_spec((1, LRU_WIDTH)), _const_spec((1, LRU_WIDTH)), _const_spec((1, LRU_WIDTH)),
        _const_spec((1, RET_WIDTH)),
        _const_spec((None, MIX_WIDTH, D_MODEL)),
        _const_spec((1, D_MODEL)),
        _const_spec((N_PAIRS, CHUNK, 2 * CHUNK)),
        _const_spec((CHUNK, RET_QK_WIDTH)), _const_spec((CHUNK, RET_QK_WIDTH)),
        _const_spec((1, RET_QK_WIDTH)), _const_spec((1, RET_QK_WIDTH)),
        _const_spec((N_PAIRS, LANES, PAIR_V)),
        _const_spec((LANES, PAIR_V)),
        _const_spec((SUBLANES, LRU_WIDTH)), _const_spec((SUBLANES, LRU_WIDTH)),
        _const_spec((N_PAIRS, LANES, PAIR_V)),
    ]
    slot_scratch = [
        pltpu.VMEM((SUBLANES + tt, LRU_WIDTH), f32),
        pltpu.VMEM((tt, LRU_WIDTH), f32),
        pltpu.VMEM((tt, RET_QK_WIDTH), f32),
        pltpu.VMEM((tt, RET_QK_WIDTH), f32),
        pltpu.VMEM((tt, RET_WIDTH), bf16),
        pltpu.VMEM((tt, RET_WIDTH), f32),
    ]
    scratch = [pltpu.VMEM((tt, D_MODEL), bf16)] + slot_scratch + slot_scratch + [
        pltpu.VMEM((tt, LRU_WIDTH), f32),
        pltpu.VMEM((tt, LRU_WIDTH), bf16),
        pltpu.VMEM((tt, 2 * LRU_WIDTH), f32),
        pltpu.VMEM((tt, MIX_WIDTH), bf16),
        pltpu.VMEM((SUBLANES, LRU_WIDTH), f32),
        pltpu.VMEM((SUBLANES, LRU_WIDTH), f32),
        pltpu.VMEM((N_PAIRS, LANES, PAIR_V), f32),
    ]
    out = pl.pallas_call(
        functools.partial(_main_kernel, n_tiles),
        grid=(n_total + 1,),
        in_specs=in_specs,
        out_specs=pl.BlockSpec((None, tt, D_MODEL), lambda g: (*fin_idx(g), 0)),
        out_shape=jax.ShapeDtypeStruct((B, S, D_MODEL), x.dtype),
        scratch_shapes=scratch,
        compiler_params=pltpu.CompilerParams(
            dimension_semantics=("arbitrary",),
            vmem_limit_bytes=VMEM_LIMIT_BYTES),
        name="hybrid_main",
    )(x, x, cos_t[N_META:], sin_t[N_META:], g1, w_in, convw, convb, w_rg, w_ig, brg, big, lam, rng, w_out, fng,
      dmask_pair, k_dec, q_dec, even, odd, g_state, bd_mask, lx0, h0, st0)
    return out
```

```python
import functools

import numpy as np
import jax
import jax.numpy as jnp
from jax import lax
from jax.experimental import pallas as pl
from jax.experimental.pallas import tpu as pltpu

f32 = jnp.float32
bf16 = jnp.bfloat16

D_MODEL = 1024
N_META = 16
LRU_WIDTH = 1024
LRU_HEADS = 8
LRU_BLOCK = 128
CONV_WIDTH = 4
LRU_C = 8.0
RET_HEADS = 8
RET_QK_DIM = 64
RET_V_DIM = 128
RET_QK_WIDTH = 512
RET_WIDTH = 1024
CHUNK = 128
ROPE_BASE = 10000.0
MIX_WIDTH = 2048
EPS = 1e-6
QK_SCALE = RET_QK_DIM ** -0.5

OFF_LX, OFF_GATE, OFF_Q, OFF_K, OFF_V, OFF_RG = 0, 1024, 2048, 2560, 3072, 4096
IN_WIDTH = 5120

LANES = 128
SUBLANES = 8
N_PAIRS = RET_HEADS // 2
PAIR_V = 2 * RET_V_DIM
TILE_T = 256
SCAN_ROWS = 32
VMEM_LIMIT_BYTES = 60000 * 1024


def _lane_head():
    return np.arange(RET_QK_WIDTH) // RET_QK_DIM


def _retention_tables():
    log_g = np.log1p(-np.exp2(-5.0 - np.arange(RET_HEADS, dtype=np.float32))).astype(np.float32)
    idx = np.arange(CHUNK, dtype=np.float32)
    diff = idx[:, None] - idx[None, :]
    dmask = np.where(diff[None] >= 0.0, np.exp(np.maximum(diff, 0.0)[None] * log_g[:, None, None]), 0.0)
    dmask_pair = np.concatenate([dmask[0::2], dmask[1::2]], axis=-1)
    lg_lane = log_g[_lane_head()]
    k_dec = np.exp((CHUNK - 1.0 - idx)[:, None] * lg_lane[None, :])
    q_dec = np.exp((idx + 1.0)[:, None] * lg_lane[None, :]) * QK_SCALE
    even = (_lane_head() % 2 == 0)[None, :]
    g_chunk = np.exp(CHUNK * log_g)
    g_state = np.broadcast_to(np.repeat(g_chunk, RET_V_DIM).reshape(N_PAIRS, 1, PAIR_V), (N_PAIRS, LANES, PAIR_V))
    row_par = np.arange(LANES) // RET_QK_DIM
    col_par = np.arange(PAIR_V) // RET_V_DIM
    bd_mask = row_par[:, None] == col_par[None, :]
    as_f32 = lambda a: jnp.asarray(np.asarray(a, np.float32))
    return tuple(as_f32(t) for t in (dmask_pair, k_dec, q_dec, even, ~even, g_state, bd_mask))


def _rotary_tables(n_pos):
    half = RET_QK_DIM // 2
    inv = (np.float32(ROPE_BASE) ** (-np.arange(half, dtype=np.float32) / half)).astype(np.float32)
    ang = np.arange(n_pos).astype(np.float32)[:, None] * inv[None, :]
    cos, sin = np.cos(ang), np.sin(ang)
    cos_t = np.concatenate([cos, cos, cos, cos], axis=-1).astype(np.float32)
    sin_t = np.concatenate([-sin, sin, -sin, sin], axis=-1).astype(np.float32)
    return jnp.asarray(cos_t), jnp.asarray(sin_t)


def _rmsnorm_rows(x, gain_row):
    ms = jnp.mean(x * x, axis=-1, keepdims=True)
    return x * lax.rsqrt(ms + EPS) * gain_row


def _sigmoid(x):
    return 0.5 * jnp.tanh(0.5 * x) + 0.5


def _silu(x):
    hx = 0.5 * x
    return hx * jnp.tanh(hx) + hx


def _conv(n_rows, lx_scr, convw_ref, convb_ref, xc_scr, xcb_scr):
    base = SUBLANES
    xc = convb_ref[...] + convw_ref[3:4, :] * lx_scr[pl.ds(base, n_rows), :]
    xc = xc + convw_ref[2:3, :] * lx_scr[pl.ds(base - 1, n_rows), :]
    xc = xc + convw_ref[1:2, :] * lx_scr[pl.ds(base - 2, n_rows), :]
    xc = xc + convw_ref[0:1, :] * lx_scr[pl.ds(base - 3, n_rows), :]
    xc_scr[...] = xc
    xcb_scr[...] = xc.astype(bf16)


def _gates_head(h, xcb_scr, wrg_ref, wig_ref, pre_scr):
    lanes = slice(h * LRU_BLOCK, (h + 1) * LRU_BLOCK)
    wg = jnp.concatenate([wrg_ref[h], wig_ref[h]], axis=-1).astype(bf16)
    pre = jnp.dot(xcb_scr[:, lanes], wg, preferred_element_type=f32)
    pre_scr[:, lanes] = pre[:, :LRU_BLOCK]
    pre_scr[:, LRU_WIDTH + h * LRU_BLOCK:LRU_WIDTH + (h + 1) * LRU_BLOCK] = pre[:, LRU_BLOCK:]


def _lru_block(xc, pre_r, pre_i, brg, big, c_sp, carry):
    rows, width = xc.shape
    r = _sigmoid(pre_r + brg)
    i = _sigmoid(pre_i + big)
    nl = r * c_sp
    a = jnp.exp(-nl)
    z = jnp.tanh(nl) * (1.0 + a * a)
    beta = jnp.where(z > 0.0, z * lax.rsqrt(z), 0.0)
    b = beta * i * xc
    nv = rows // SUBLANES
    a3 = a.reshape(nv, SUBLANES, width)
    b3 = b.reshape(nv, SUBLANES, width)
    rowid = lax.broadcasted_iota(jnp.int32, (nv, SUBLANES, width), 1)
    for s in (1, 2, 4):
        keep = rowid >= s
        a_s = jnp.where(keep, pltpu.roll(a3, s, 1), 1.0)
        b_s = jnp.where(keep, pltpu.roll(b3, s, 1), 0.0)
        b3 = a3 * b_s + b3
        a3 = a3 * a_s
    hs = []
    for v in range(nv):
        h_v = a3[v] * carry + b3[v]
        carry = h_v[SUBLANES - 1:SUBLANES, :]
        hs.append(h_v)
    return jnp.concatenate(hs, axis=0), carry


def _rotary(t, cos, sin):
    half = RET_QK_DIM // 2
    in_first_half = (lax.broadcasted_iota(jnp.int32, t.shape, 1) // half) % 2 == 0
    partner = jnp.where(in_first_half, pltpu.roll(t, LANES - half, 1), pltpu.roll(t, half, 1))
    return t * cos + partner * sin


def _dot_t0(a, b):
    return lax.dot_general(a, b, (((0,), (0,)), ((), ())), preferred_element_type=f32)


def _dot_nt(a, b):
    return lax.dot_general(a, b, (((1,), (1,)), ((), ())), preferred_element_type=f32)


def _meta_kernel(meta_ref, cos_ref, sin_ref, g1_ref, wlx_ref, wk_ref, wv_ref, convw_ref, convb_ref,
                 wrg_ref, wig_ref, brg_ref, big_ref, lam_ref, kdec_ref, bdm_ref,
                 lx_out, h_out, st_out,
                 lx_scr, xc_scr, xcb_scr, pre_scr):
    u = _rmsnorm_rows(meta_ref[...], g1_ref[...]).astype(bf16)
    lx_scr[0:SUBLANES, :] = jnp.zeros((SUBLANES, LRU_WIDTH), f32)
    lx_scr[SUBLANES:SUBLANES + N_META, :] = jnp.dot(u, wlx_ref[...].astype(bf16), preferred_element_type=f32)
    k = jnp.dot(u, wk_ref[...].astype(bf16), preferred_element_type=f32)
    v = jnp.dot(u, wv_ref[...].astype(bf16), preferred_element_type=f32).astype(bf16)

    _conv(N_META, lx_scr, convw_ref, convb_ref, xc_scr, xcb_scr)
    for h in range(LRU_HEADS):
        _gates_head(h, xcb_scr, wrg_ref, wig_ref, pre_scr)
    lx_out[...] = lx_scr[pl.ds(N_META, SUBLANES), :]
    c_sp = LRU_C * jax.nn.softplus(-lam_ref[...])
    _, carry = _lru_block(xc_scr[...], pre_scr[:, :LRU_WIDTH], pre_scr[:, LRU_WIDTH:],
                          brg_ref[...], big_ref[...], c_sp, jnp.zeros((1, LRU_WIDTH), f32))
    h_out[...] = jnp.broadcast_to(carry, (SUBLANES, LRU_WIDTH))

    cos, sin = cos_ref[...], sin_ref[...]
    for p in range(N_PAIRS):
        sl = slice(p * LANES, (p + 1) * LANES)
        k_rot = _rotary(k[:, sl], cos, sin)
        kd = (k_rot * kdec_ref[CHUNK - N_META:CHUNK, sl]).astype(bf16)
        st_out[p] = _dot_t0(kd, v[:, p * PAIR_V:(p + 1) * PAIR_V]) * bdm_ref[...]


def _norm_tile(x_ref, g1_ref, u_scr):
    u_scr[...] = _rmsnorm_rows(x_ref[...], g1_ref[...]).astype(bf16)


MXU_COLS = 256
N_IN_TILES = IN_WIDTH // MXU_COLS


def _project_col_tile(t, u_scr, win_ref, slot):
    lx_scr, gate_scr, q_scr, k_scr, v_scr, rg_scr = slot
    off = t * MXU_COLS
    res = jnp.dot(u_scr[...], win_ref[:, off:off + MXU_COLS].astype(bf16), preferred_element_type=f32)
    for dst, start, rows, dt in ((lx_scr, OFF_LX, slice(SUBLANES, SUBLANES + TILE_T), f32), (gate_scr, OFF_GATE, slice(None), f32),
                                 (q_scr, OFF_Q, slice(None), f32), (k_scr, OFF_K, slice(None), f32),
                                 (v_scr, OFF_V, slice(None), bf16), (rg_scr, OFF_RG, slice(None), f32)):
        width = dst.shape[1]
        if start <= off < start + width:
            dst[rows, off - start:off - start + MXU_COLS] = res.astype(dt)
            return
    raise AssertionError("column tile outside the projection")


def _finish_tile(x_ref, cos_ref, sin_ref, convw_ref, convb_ref, wrg_ref, wig_ref, brg_ref, big_ref, lam_ref, rng_ref,
                 wout_ref, fng_ref, dmaskp_ref, kdec_ref, qdec_ref, even_ref, odd_ref, gst_ref, bdm_ref,
                 o_ref, slot, xc_scr, xcb_scr, pre_scr, y_scr, lxtail_scr, h_scr, st_scr, after_gates):
    lx_scr, gate_scr, q_scr, k_scr, v_scr, rg_scr = slot
    tt = TILE_T

    lx_scr[0:SUBLANES, :] = lxtail_scr[...]
    _conv(tt, lx_scr, convw_ref, convb_ref, xc_scr, xcb_scr)
    lxtail_scr[...] = lx_scr[pl.ds(tt, SUBLANES), :]
    for h in range(LRU_HEADS):
        lanes = slice(h * LRU_BLOCK, (h + 1) * LRU_BLOCK)
        ilanes = slice(LRU_WIDTH + h * LRU_BLOCK, LRU_WIDTH + (h + 1) * LRU_BLOCK)
        _gates_head(h, xcb_scr, wrg_ref, wig_ref, pre_scr)
        after_gates(h)
        full = (SCAN_ROWS, LRU_BLOCK)
        c_sp = jnp.broadcast_to(LRU_C * jax.nn.softplus(-lam_ref[:, lanes]), full)
        brg, big = jnp.broadcast_to(brg_ref[:, lanes], full), jnp.broadcast_to(big_ref[:, lanes], full)
        carry = h_scr[0:1, lanes]
        for it in range(tt // SCAN_ROWS):
            rows = slice(it * SCAN_ROWS, (it + 1) * SCAN_ROWS)
            hb, carry = _lru_block(xc_scr[rows, lanes], pre_scr[rows, lanes], pre_scr[rows, ilanes], brg, big, c_sp, carry)
            y_scr[rows, lanes] = (hb * _silu(gate_scr[rows, lanes])).astype(bf16)
        h_scr[:, lanes] = jnp.broadcast_to(carry, (SUBLANES, LRU_BLOCK))

    for c in range(tt // CHUNK):
        rows = slice(c * CHUNK, (c + 1) * CHUNK)
        cos, sin = cos_ref[rows, :], sin_ref[rows, :]
        for p in range(N_PAIRS):
            sl = slice(p * LANES, (p + 1) * LANES)
            q_rot = _rotary(q_scr[rows, sl], cos, sin)
            k_rot = _rotary(k_scr[rows, sl], cos, sin)
            q_b = (q_rot * QK_SCALE).astype(bf16)
            q_d = (q_rot * qdec_ref[:, sl]).astype(bf16)
            k_d = (k_rot * kdec_ref[:, sl]).astype(bf16)
            kk = jnp.concatenate([(k_rot * even_ref[:, sl]).astype(bf16), (k_rot * odd_ref[:, sl]).astype(bf16)], axis=0)
            v_pair = v_scr[rows, p * PAIR_V:(p + 1) * PAIR_V]
            zeros = jnp.zeros((CHUNK, RET_V_DIM), bf16)
            v_bd = jnp.concatenate([jnp.concatenate([v_pair[:, :RET_V_DIM], zeros], axis=1),
                                    jnp.concatenate([zeros, v_pair[:, RET_V_DIM:]], axis=1)], axis=0)
            st_pair = st_scr[p]
            s = _dot_nt(q_b, kk) * dmaskp_ref[p]
            o = jnp.dot(s.astype(bf16), v_bd, preferred_element_type=f32)
            o = o + jnp.dot(q_d, st_pair.astype(bf16), preferred_element_type=f32)
            for e in range(2):
                h = 2 * p + e
                hs = slice(h * RET_V_DIM, (h + 1) * RET_V_DIM)
                o_h = o[:, e * RET_V_DIM:(e + 1) * RET_V_DIM]
                mu = jnp.mean(o_h, axis=-1, keepdims=True)
                oc = o_h - mu
                var = jnp.mean(oc * oc, axis=-1, keepdims=True)
                on = oc * lax.rsqrt(var + EPS) * rng_ref[:, hs]
                y_scr[rows, LRU_WIDTH + h * RET_V_DIM:LRU_WIDTH + (h + 1) * RET_V_DIM] = (
                    on * _silu(rg_scr[rows, hs])).astype(bf16)
            kv = _dot_t0(k_d, v_pair)
            st_scr[p] = gst_ref[p] * st_pair + bdm_ref[...] * kv

    res = x_ref[...] + jnp.dot(y_scr[...], wout_ref[...].astype(bf16), preferred_element_type=f32)
    o_ref[...] = _rmsnorm_rows(res, fng_ref[...])


def _main_kernel(n_tiles,
                 xa_ref, xb_ref, cos_ref, sin_ref, g1_ref, win_ref, convw_ref, convb_ref, wrg_ref, wig_ref,
                 brg_ref, big_ref, lam_ref, rng_ref, wout_ref, fng_ref, dmaskp_ref, kdec_ref, qdec_ref, even_ref, odd_ref,
                 gst_ref, bdm_ref, lx0_ref, h0_ref, st0_ref,
                 o_ref,
                 u_scr,
                 lx_a, gate_a, q_a, k_a, v_a, rg_a,
                 lx_b, gate_b, q_b, k_b, v_b, rg_b,
                 xc_scr, xcb_scr, pre_scr, y_scr, lxtail_scr, h_scr, st_scr):
    g = pl.program_id(0)
    slots = ((lx_a, gate_a, q_a, k_a, v_a, rg_a), (lx_b, gate_b, q_b, k_b, v_b, rg_b))

    @pl.when(g == 0)
    def _():
        for ref in slots[1]:
            ref[...] = jnp.zeros(ref.shape, ref.dtype)
        lxtail_scr[...] = jnp.zeros(lxtail_scr.shape, f32)
        h_scr[...] = jnp.zeros(h_scr.shape, f32)
        st_scr[...] = jnp.zeros(st_scr.shape, f32)

    @pl.when(lax.rem(g + n_tiles - 1, n_tiles) == 0)
    def _():
        lxtail_scr[...] = lx0_ref[...]
        h_scr[...] = h0_ref[...]
        st_scr[...] = st0_ref[...]

    tiles_after_head = (4, 4, 3, 3, 2, 2, 1, 1)
    assert sum(tiles_after_head) == N_IN_TILES and len(tiles_after_head) == LRU_HEADS

    def step(slot_project, slot_finish):
        _norm_tile(xa_ref, g1_ref, u_scr)

        def after_gates(h):
            first = sum(tiles_after_head[:h])
            for t in range(first, first + tiles_after_head[h]):
                _project_col_tile(t, u_scr, win_ref, slot_project)

        _finish_tile(xb_ref, cos_ref, sin_ref, convw_ref, convb_ref, wrg_ref, wig_ref, brg_ref, big_ref, lam_ref, rng_ref,
                     wout_ref, fng_ref, dmaskp_ref, kdec_ref, qdec_ref, even_ref, odd_ref, gst_ref, bdm_ref,
                     o_ref, slot_finish, xc_scr, xcb_scr, pre_scr, y_scr, lxtail_scr, h_scr, st_scr, after_gates)

    @pl.when(lax.rem(g, 2) == 0)
    def _():
        step(slots[0], slots[1])

    @pl.when(lax.rem(g, 2) == 1)
    def _():
        step(slots[1], slots[0])


def _const_spec(shape):
    nd = len(shape)
    return pl.BlockSpec(shape, lambda g, _nd=nd: (0,) * _nd, pipeline_mode=pl.Buffered(1))


def kernel(x, meta_tokens, norm_gain, w_in, conv_w, conv_b, w_rg, b_rg, w_ig, b_ig,
           lru_lambda, ret_norm_gain, w_out, final_norm_gain):
    B, S, D = x.shape
    assert D == D_MODEL and S % TILE_T == 0 and TILE_T % CHUNK == 0
    assert norm_gain.shape[0] == 1, "single-layer block"
    assert meta_tokens.shape == (N_META, D_MODEL)

    assert w_in.shape == (1, D_MODEL, IN_WIDTH) and w_out.shape == (1, MIX_WIDTH, D_MODEL)
    assert w_rg.shape == w_ig.shape == (1, LRU_HEADS, LRU_BLOCK, LRU_BLOCK)

    g1 = norm_gain[0].reshape(1, D_MODEL)
    fng = final_norm_gain.reshape(1, D_MODEL)
    convw = conv_w[0]
    convb = conv_b[0].reshape(1, LRU_WIDTH)
    brg = b_rg[0].reshape(1, LRU_WIDTH)
    big = b_ig[0].reshape(1, LRU_WIDTH)
    lam = lru_lambda[0].reshape(1, LRU_WIDTH)
    rng = ret_norm_gain[0].reshape(1, RET_WIDTH)

    dmask_pair, k_dec, q_dec, even, odd, g_state, bd_mask = _retention_tables()
    cos_t, sin_t = _rotary_tables(N_META + S)

    n_lx = SUBLANES + N_META
    whole = lambda a: pl.BlockSpec(a.shape, lambda i, _nd=a.ndim: (0,) * _nd)
    w_in_cols = lambda off, width: pl.BlockSpec((None, D_MODEL, width), lambda i, _b=off // width: (0, 0, _b))
    gate_w_spec = pl.BlockSpec((None, LRU_HEADS, LRU_BLOCK, LRU_BLOCK), lambda i: (0, 0, 0, 0))
    cos_m, sin_m = cos_t[:N_META], sin_t[:N_META]
    meta_out_shapes = ((SUBLANES, LRU_WIDTH), (SUBLANES, LRU_WIDTH), (N_PAIRS, LANES, PAIR_V))
    lx0, h0, st0 = pl.pallas_call(
        _meta_kernel,
        grid=(1,),
        out_specs=tuple(pl.BlockSpec(s, lambda i, _nd=len(s): (0,) * _nd) for s in meta_out_shapes),
        in_specs=[whole(meta_tokens), whole(cos_m), whole(sin_m), whole(g1),
                  w_in_cols(OFF_LX, LRU_WIDTH), w_in_cols(OFF_K, RET_QK_WIDTH), w_in_cols(OFF_V, RET_WIDTH),
                  whole(convw), whole(convb), gate_w_spec, gate_w_spec,
                  whole(brg), whole(big), whole(lam), whole(k_dec), whole(bd_mask)],
        out_shape=(jax.ShapeDtypeStruct((SUBLANES, LRU_WIDTH), f32),
                   jax.ShapeDtypeStruct((SUBLANES, LRU_WIDTH), f32),
                   jax.ShapeDtypeStruct((N_PAIRS, LANES, PAIR_V), f32)),
        scratch_shapes=[pltpu.VMEM((n_lx, LRU_WIDTH), f32),
                        pltpu.VMEM((N_META, LRU_WIDTH), f32),
                        pltpu.VMEM((N_META, LRU_WIDTH), bf16),
                        pltpu.VMEM((N_META, 2 * LRU_WIDTH), f32)],
        compiler_params=pltpu.CompilerParams(vmem_limit_bytes=VMEM_LIMIT_BYTES),
        name="hybrid_meta_state",
    )(meta_tokens, cos_m, sin_m, g1, w_in, w_in, w_in, convw, convb, w_rg, w_ig, brg, big, lam, k_dec, bd_mask)

    tt = TILE_T
    n_tiles = S // tt
    n_total = B * n_tiles

    def proj_idx(g):
        gp = jnp.minimum(g, n_total - 1)
        return gp // n_tiles, gp % n_tiles

    def fin_idx(g):
        gf = jnp.maximum(g - 1, 0)
        return gf // n_tiles, gf % n_tiles

    rot_spec = pl.BlockSpec((tt, LANES), lambda g: (fin_idx(g)[1], 0))
    in_specs = [
        pl.BlockSpec((None, tt, D_MODEL), lambda g: (*proj_idx(g), 0)),
        pl.BlockSpec((None, tt, D_MODEL), lambda g: (*fin_idx(g), 0)),
        rot_spec, rot_spec,
        _const_spec((1, D_MODEL)),
        _const_spec((None, D_MODEL, IN_WIDTH)),
        _const_spec((CONV_WIDTH, LRU_WIDTH)), _const_spec((1, LRU_WIDTH)),
        _const_spec((None, LRU_HEADS, LRU_BLOCK, LRU_BLOCK)),
        _const_spec((None, LRU_HEADS, LRU_BLOCK, LRU_BLOCK)),
        _const_spec((1, LRU_WIDTH)), _const_spec((1, LRU_WIDTH)), _const_spec((1, LRU_WIDTH)),
        _const_spec((1, RET_WIDTH)),
        _const_spec((None, MIX_WIDTH, D_MODEL)),
        _const_spec((1, D_MODEL)),
        _const_spec((N_PAIRS, CHUNK, 2 * CHUNK)),
        _const_spec((CHUNK, RET_QK_WIDTH)), _const_spec((CHUNK, RET_QK_WIDTH)),
        _const_spec((1, RET_QK_WIDTH)), _const_spec((1, RET_QK_WIDTH)),
        _const_spec((N_PAIRS, LANES, PAIR_V)),
        _const_spec((LANES, PAIR_V)),
        _const_spec((SUBLANES, LRU_WIDTH)), _const_spec((SUBLANES, LRU_WIDTH)),
        _const_spec((N_PAIRS, LANES, PAIR_V)),
    ]
    slot_scratch = [
        pltpu.VMEM((SUBLANES + tt, LRU_WIDTH), f32),
        pltpu.VMEM((tt, LRU_WIDTH), f32),
        pltpu.VMEM((tt, RET_QK_WIDTH), f32),
        pltpu.VMEM((tt, RET_QK_WIDTH), f32),
        pltpu.VMEM((tt, RET_WIDTH), bf16),
        pltpu.VMEM((tt, RET_WIDTH), f32),
    ]
    scratch = [pltpu.VMEM((tt, D_MODEL), bf16)] + slot_scratch + slot_scratch + [
        pltpu.VMEM((tt, LRU_WIDTH), f32),
        pltpu.VMEM((tt, LRU_WIDTH), bf16),
        pltpu.VMEM((tt, 2 * LRU_WIDTH), f32),
        pltpu.VMEM((tt, MIX_WIDTH), bf16),
        pltpu.VMEM((SUBLANES, LRU_WIDTH), f32),
        pltpu.VMEM((SUBLANES, LRU_WIDTH), f32),
        pltpu.VMEM((N_PAIRS, LANES, PAIR_V), f32),
    ]
    out = pl.pallas_call(
        functools.partial(_main_kernel, n_tiles),
        grid=(n_total + 1,),
        in_specs=in_specs,
        out_specs=pl.BlockSpec((None, tt, D_MODEL), lambda g: (*fin_idx(g), 0)),
        out_shape=jax.ShapeDtypeStruct((B, S, D_MODEL), x.dtype),
        scratch_shapes=scratch,
        compiler_params=pltpu.CompilerParams(
            dimension_semantics=("arbitrary",),
            vmem_limit_bytes=VMEM_LIMIT_BYTES),
        name="hybrid_main",
    )(x, x, cos_t[N_META:], sin_t[N_META:], g1, w_in, convw, convb, w_rg, w_ig, brg, big, lam, rng, w_out, fng,
      dmask_pair, k_dec, q_dec, even, odd, g_state, bd_mask, lx0, h0, st0)
    return out
```

```python
import functools

import numpy as np
import jax
import jax.numpy as jnp
from jax import lax
from jax.experimental import pallas as pl
from jax.experimental.pallas import tpu as pltpu

f32 = jnp.float32
bf16 = jnp.bfloat16

D_MODEL = 1024
N_META = 16
LRU_WIDTH = 1024
LRU_HEADS = 8
LRU_BLOCK = 128
CONV_WIDTH = 4
LRU_C = 8.0
RET_HEADS = 8
RET_QK_DIM = 64
RET_V_DIM = 128
RET_QK_WIDTH = 512
RET_WIDTH = 1024
CHUNK = 128
ROPE_BASE = 10000.0
MIX_WIDTH = 2048
EPS = 1e-6
QK_SCALE = RET_QK_DIM ** -0.5

OFF_LX, OFF_GATE, OFF_Q, OFF_K, OFF_V, OFF_RG = 0, 1024, 2048, 2560, 3072, 4096
IN_WIDTH = 5120

LANES = 128
SUBLANES = 8
N_PAIRS = RET_HEADS // 2
PAIR_V = 2 * RET_V_DIM
TILE_T = 256
VMEM_LIMIT_BYTES = 60000 * 1024


def _lane_head():
    return np.arange(RET_QK_WIDTH) // RET_QK_DIM


BLOCK_STEPS = CHUNK // SUBLANES
CHUNK_ROW_TIME = np.arange(CHUNK).reshape(SUBLANES, BLOCK_STEPS).T.reshape(-1)


def _chunk_rows_blocked(table):
    n = table.shape[0]
    return table.reshape((n // CHUNK, CHUNK) + table.shape[1:])[:, CHUNK_ROW_TIME].reshape(table.shape)


def _retention_tables():
    log_g = np.log1p(-np.exp2(-5.0 - np.arange(RET_HEADS, dtype=np.float32))).astype(np.float32)
    idx = np.arange(CHUNK, dtype=np.float32)
    diff = idx[:, None] - idx[None, :]
    dmask = np.where(diff[None] >= 0.0, np.exp(np.maximum(diff, 0.0)[None] * log_g[:, None, None]), 0.0)
    dmask_pair = np.concatenate([dmask[0::2], dmask[1::2]], axis=-1)
    lg_lane = log_g[_lane_head()]
    k_dec = np.exp((CHUNK - 1.0 - idx)[:, None] * lg_lane[None, :])
    q_dec = np.exp((idx + 1.0)[:, None] * lg_lane[None, :]) * QK_SCALE
    even = (_lane_head() % 2 == 0)[None, :]
    g_chunk = np.exp(CHUNK * log_g)
    g_state = np.broadcast_to(np.repeat(g_chunk, RET_V_DIM).reshape(N_PAIRS, 1, PAIR_V), (N_PAIRS, LANES, PAIR_V))
    row_par = np.arange(LANES) // RET_QK_DIM
    col_par = np.arange(PAIR_V) // RET_V_DIM
    bd_mask = row_par[:, None] == col_par[None, :]
    key_order = np.concatenate([CHUNK_ROW_TIME, CHUNK + CHUNK_ROW_TIME])
    dmask_blocked = dmask_pair[:, CHUNK_ROW_TIME][:, :, key_order]
    as_f32 = lambda a: jnp.asarray(np.asarray(a, np.float32))
    return tuple(as_f32(t) for t in (k_dec, dmask_blocked, k_dec[CHUNK_ROW_TIME], q_dec[CHUNK_ROW_TIME],
                                     even, ~even, g_state, bd_mask))


def _rotary_tables(n_pos):
    half = RET_QK_DIM // 2
    inv = (np.float32(ROPE_BASE) ** (-np.arange(half, dtype=np.float32) / half)).astype(np.float32)
    ang = np.arange(n_pos).astype(np.float32)[:, None] * inv[None, :]
    cos, sin = np.cos(ang), np.sin(ang)
    cos_t = np.concatenate([cos, cos, cos, cos], axis=-1).astype(np.float32)
    sin_t = np.concatenate([-sin, sin, -sin, sin], axis=-1).astype(np.float32)
    return cos_t, sin_t


def _rmsnorm_rows(x, gain_row):
    ms = jnp.mean(x * x, axis=-1, keepdims=True)
    return x * lax.rsqrt(ms + EPS) * gain_row


def _sigmoid(x):
    return 0.5 * jnp.tanh(0.5 * x) + 0.5


def _silu(x):
    hx = 0.5 * x
    return hx * jnp.tanh(hx) + hx


def _conv(n_rows, lx_scr, convw_ref, convb_ref, xc_scr, xcb_scr):
    base = SUBLANES
    xc = convb_ref[...] + convw_ref[3:4, :] * lx_scr[pl.ds(base, n_rows), :]
    xc = xc + convw_ref[2:3, :] * lx_scr[pl.ds(base - 1, n_rows), :]
    xc = xc + convw_ref[1:2, :] * lx_scr[pl.ds(base - 2, n_rows), :]
    xc = xc + convw_ref[0:1, :] * lx_scr[pl.ds(base - 3, n_rows), :]
    xc_scr[...] = xc
    xcb_scr[...] = xc.astype(bf16)


CONV_HIST = (CONV_WIDTH - 1) * SUBLANES


def _conv_blocked(lx_scr, convw_ref, convb_ref, xc_scr, xcb_scr):
    last_sublane = lax.broadcasted_iota(jnp.int32, (SUBLANES, LRU_WIDTH), 0) == SUBLANES - 1
    for c in range(TILE_T // CHUNK):
        base = CONV_HIST + c * CHUNK
        rows = lambda group, n: lx_scr[base + group * SUBLANES:base + (group + n) * SUBLANES, :]
        wrapped = []
        for k in range(1, CONV_WIDTH):
            own, prev = rows(BLOCK_STEPS - k, 1), rows(-k, 1)
            wrapped.append(pltpu.roll(jnp.where(last_sublane, prev, own), 1, 0))
        xc = convb_ref[...] + convw_ref[3:4, :] * rows(0, BLOCK_STEPS)
        for k in range(1, CONV_WIDTH):
            shifted = jnp.concatenate(wrapped[:k][::-1] + [rows(0, BLOCK_STEPS - k)], axis=0)
            xc = xc + convw_ref[CONV_WIDTH - 1 - k:CONV_WIDTH - k, :] * shifted
        xc_scr[c * CHUNK:(c + 1) * CHUNK, :] = xc
        xcb_scr[c * CHUNK:(c + 1) * CHUNK, :] = xc.astype(bf16)


def _gates_head(h, xcb_scr, wrg_ref, wig_ref, pre_scr):
    lanes = slice(h * LRU_BLOCK, (h + 1) * LRU_BLOCK)
    wg = jnp.concatenate([wrg_ref[h], wig_ref[h]], axis=-1).astype(bf16)
    pre = jnp.dot(xcb_scr[:, lanes], wg, preferred_element_type=f32)
    pre_scr[:, lanes] = pre[:, :LRU_BLOCK]
    pre_scr[:, LRU_WIDTH + h * LRU_BLOCK:LRU_WIDTH + (h + 1) * LRU_BLOCK] = pre[:, LRU_BLOCK:]


def _lru_maps(xc, pre_r, pre_i, brg, big, c_sp):
    r = _sigmoid(pre_r + brg)
    i = _sigmoid(pre_i + big)
    nl = r * c_sp
    a = jnp.exp(-nl)
    z = jnp.tanh(nl) * (1.0 + a * a)
    beta = jnp.where(z > 0.0, z * lax.rsqrt(z), 0.0)
    return a, beta * i * xc


def _scan_sublanes(a, b):
    rowid = lax.broadcasted_iota(jnp.int32, a.shape, 0)
    for s in (1, 2, 4):
        keep = rowid >= s
        a_s = jnp.where(keep, pltpu.roll(a, s, 0), 1.0)
        b_s = jnp.where(keep, pltpu.roll(b, s, 0), 0.0)
        b = a * b_s + b
        a = a * a_s
    return a, b


def _lru_chunk_blocked(xc, pre_r, pre_i, brg, big, c_sp, carry):
    a, b = _lru_maps(xc, pre_r, pre_i, brg, big, c_sp)
    width = xc.shape[1]
    a3 = a.reshape(BLOCK_STEPS, SUBLANES, width)
    b3 = b.reshape(BLOCK_STEPS, SUBLANES, width)
    decay, local = [a3[0]], [b3[0]]
    for j in range(1, BLOCK_STEPS):
        local.append(a3[j] * local[-1] + b3[j])
        decay.append(a3[j] * decay[-1])
    a_blocks, b_blocks = _scan_sublanes(decay[-1], local[-1])
    after = a_blocks * carry + b_blocks
    rowid = lax.broadcasted_iota(jnp.int32, carry.shape, 0)
    before = jnp.where(rowid >= 1, pltpu.roll(after, 1, 0), carry)
    hs = [local[j] + decay[j] * before for j in range(BLOCK_STEPS)]
    new_carry = jnp.broadcast_to(after[SUBLANES - 1:SUBLANES, :], carry.shape)
    return jnp.concatenate(hs, axis=0), new_carry


def _lru_block(xc, pre_r, pre_i, brg, big, c_sp, carry):
    rows, width = xc.shape
    a, b = _lru_maps(xc, pre_r, pre_i, brg, big, c_sp)
    nv = rows // SUBLANES
    a3 = a.reshape(nv, SUBLANES, width)
    b3 = b.reshape(nv, SUBLANES, width)
    rowid = lax.broadcasted_iota(jnp.int32, (nv, SUBLANES, width), 1)
    for s in (1, 2, 4):
        keep = rowid >= s
        a_s = jnp.where(keep, pltpu.roll(a3, s, 1), 1.0)
        b_s = jnp.where(keep, pltpu.roll(b3, s, 1), 0.0)
        b3 = a3 * b_s + b3
        a3 = a3 * a_s
    hs = []
    for v in range(nv):
        h_v = a3[v] * carry + b3[v]
        carry = h_v[SUBLANES - 1:SUBLANES, :]
        hs.append(h_v)
    return jnp.concatenate(hs, axis=0), carry


def _rotary(t, cos, sin):
    half = RET_QK_DIM // 2
    in_first_half = (lax.broadcasted_iota(jnp.int32, t.shape, 1) // half) % 2 == 0
    partner = jnp.where(in_first_half, pltpu.roll(t, LANES - half, 1), pltpu.roll(t, half, 1))
    return t * cos + partner * sin


def _dot_t0(a, b):
    return lax.dot_general(a, b, (((0,), (0,)), ((), ())), preferred_element_type=f32)


def _dot_nt(a, b):
    return lax.dot_general(a, b, (((1,), (1,)), ((), ())), preferred_element_type=f32)


def _meta_kernel(meta_ref, cos_ref, sin_ref, g1_ref, wlx_ref, wk_ref, wv_ref, convw_ref, convb_ref,
                 wrg_ref, wig_ref, brg_ref, big_ref, lam_ref, kdec_ref, bdm_ref,
                 lx_out, h_out, st_out,
                 lx_scr, xc_scr, xcb_scr, pre_scr):
    u = _rmsnorm_rows(meta_ref[...], g1_ref[...]).astype(bf16)
    lx_scr[0:SUBLANES, :] = jnp.zeros((SUBLANES, LRU_WIDTH), f32)
    lx_scr[SUBLANES:SUBLANES + N_META, :] = jnp.dot(u, wlx_ref[...].astype(bf16), preferred_element_type=f32)
    k = jnp.dot(u, wk_ref[...].astype(bf16), preferred_element_type=f32)
    v = jnp.dot(u, wv_ref[...].astype(bf16), preferred_element_type=f32).astype(bf16)

    _conv(N_META, lx_scr, convw_ref, convb_ref, xc_scr, xcb_scr)
    for h in range(LRU_HEADS):
        _gates_head(h, xcb_scr, wrg_ref, wig_ref, pre_scr)
    for back in range(1, CONV_WIDTH):
        row = lx_scr[SUBLANES + N_META - back:SUBLANES + N_META - back + 1, :]
        group = CONV_WIDTH - 1 - back
        lx_out[group * SUBLANES:(group + 1) * SUBLANES, :] = jnp.broadcast_to(row, (SUBLANES, LRU_WIDTH))
    c_sp = LRU_C * jax.nn.softplus(-lam_ref[...])
    _, carry = _lru_block(xc_scr[...], pre_scr[:, :LRU_WIDTH], pre_scr[:, LRU_WIDTH:],
                          brg_ref[...], big_ref[...], c_sp, jnp.zeros((1, LRU_WIDTH), f32))
    h_out[...] = jnp.broadcast_to(carry, (SUBLANES, LRU_WIDTH))

    cos, sin = cos_ref[...], sin_ref[...]
    for p in range(N_PAIRS):
        sl = slice(p * LANES, (p + 1) * LANES)
        k_rot = _rotary(k[:, sl], cos, sin)
        kd = (k_rot * kdec_ref[CHUNK - N_META:CHUNK, sl]).astype(bf16)
        st_out[p] = _dot_t0(kd, v[:, p * PAIR_V:(p + 1) * PAIR_V]) * bdm_ref[...]


def _norm_tile(x_tile, g1_ref, u_scr):
    u_scr[...] = _rmsnorm_rows(x_tile, g1_ref[...]).astype(bf16)


MXU_COLS = 256
N_IN_TILES = IN_WIDTH // MXU_COLS


def _project_col_tile(t, u_scr, win_ref, slot):
    lx_scr, gate_scr, q_scr, k_scr, v_scr, rg_scr = slot
    off = t * MXU_COLS
    res = jnp.dot(u_scr[...], win_ref[:, off:off + MXU_COLS].astype(bf16), preferred_element_type=f32)
    for dst, start, rows, dt in ((lx_scr, OFF_LX, slice(CONV_HIST, CONV_HIST + TILE_T), f32), (gate_scr, OFF_GATE, slice(None), f32),
                                 (q_scr, OFF_Q, slice(None), f32), (k_scr, OFF_K, slice(None), f32),
                                 (v_scr, OFF_V, slice(None), bf16), (rg_scr, OFF_RG, slice(None), f32)):
        width = dst.shape[1]
        if start <= off < start + width:
            dst[rows, off - start:off - start + MXU_COLS] = res.astype(dt)
            return
    raise AssertionError("column tile outside the projection")


def _finish_tile(load_x, cos_ref, sin_ref, convw_ref, convb_ref, wrg_ref, wig_ref, brg_ref, big_ref, lam_ref, rng_ref,
                 wout_ref, fng_ref, dmaskp_ref, kdec_ref, qdec_ref, even_ref, odd_ref, gst_ref, bdm_ref,
                 slot, xc_scr, xcb_scr, pre_scr, y_scr, lxtail_scr, h_scr, st_scr, after_gates):
    lx_scr, gate_scr, q_scr, k_scr, v_scr, rg_scr = slot
    tt = TILE_T

    lx_scr[0:CONV_HIST, :] = lxtail_scr[...]
    _conv_blocked(lx_scr, convw_ref, convb_ref, xc_scr, xcb_scr)
    lxtail_scr[...] = lx_scr[pl.ds(tt, CONV_HIST), :]
    for h in range(LRU_HEADS):
        lanes = slice(h * LRU_BLOCK, (h + 1) * LRU_BLOCK)
        ilanes = slice(LRU_WIDTH + h * LRU_BLOCK, LRU_WIDTH + (h + 1) * LRU_BLOCK)
        _gates_head(h, xcb_scr, wrg_ref, wig_ref, pre_scr)
        after_gates(h)
        full = (CHUNK, LRU_BLOCK)
        c_sp = jnp.broadcast_to(LRU_C * jax.nn.softplus(-lam_ref[:, lanes]), full)
        brg, big = jnp.broadcast_to(brg_ref[:, lanes], full), jnp.broadcast_to(big_ref[:, lanes], full)
        carry = h_scr[:, lanes]
        for c in range(tt // CHUNK):
            rows = slice(c * CHUNK, (c + 1) * CHUNK)
            hb, carry = _lru_chunk_blocked(xc_scr[rows, lanes], pre_scr[rows, lanes], pre_scr[rows, ilanes],
                                           brg, big, c_sp, carry)
            y_scr[rows, lanes] = (hb * _silu(gate_scr[rows, lanes])).astype(bf16)
        h_scr[:, lanes] = carry

    for c in range(tt // CHUNK):
        rows = slice(c * CHUNK, (c + 1) * CHUNK)
        cos, sin = cos_ref[rows, :], sin_ref[rows, :]
        for p in range(N_PAIRS):
            sl = slice(p * LANES, (p + 1) * LANES)
            q_rot = _rotary(q_scr[rows, sl], cos, sin)
            k_rot = _rotary(k_scr[rows, sl], cos, sin)
            q_b = (q_rot * QK_SCALE).astype(bf16)
            q_d = (q_rot * qdec_ref[:, sl]).astype(bf16)
            k_d = (k_rot * kdec_ref[:, sl]).astype(bf16)
            kk = jnp.concatenate([(k_rot * even_ref[:, sl]).astype(bf16), (k_rot * odd_ref[:, sl]).astype(bf16)], axis=0)
            v_pair = v_scr[rows, p * PAIR_V:(p + 1) * PAIR_V]
            zeros = jnp.zeros((CHUNK, RET_V_DIM), bf16)
            v_bd = jnp.concatenate([jnp.concatenate([v_pair[:, :RET_V_DIM], zeros], axis=1),
                                    jnp.concatenate([zeros, v_pair[:, RET_V_DIM:]], axis=1)], axis=0)
            st_pair = st_scr[p]
            s = _dot_nt(q_b, kk) * dmaskp_ref[p]
            o = jnp.dot(s.astype(bf16), v_bd, preferred_element_type=f32)
            o = o + jnp.dot(q_d, st_pair.astype(bf16), preferred_element_type=f32)
            for e in range(2):
                h = 2 * p + e
                hs = slice(h * RET_V_DIM, (h + 1) * RET_V_DIM)
                o_h = o[:, e * RET_V_DIM:(e + 1) * RET_V_DIM]
                mu = jnp.mean(o_h, axis=-1, keepdims=True)
                oc = o_h - mu
                var = jnp.mean(oc * oc, axis=-1, keepdims=True)
                on = oc * lax.rsqrt(var + EPS) * rng_ref[:, hs]
                y_scr[rows, LRU_WIDTH + h * RET_V_DIM:LRU_WIDTH + (h + 1) * RET_V_DIM] = (
                    on * _silu(rg_scr[rows, hs])).astype(bf16)
            kv = _dot_t0(k_d, v_pair)
            st_scr[p] = gst_ref[p] * st_pair + bdm_ref[...] * kv

    res = load_x() + jnp.dot(y_scr[...], wout_ref[...].astype(bf16), preferred_element_type=f32)
    return _rmsnorm_rows(res, fng_ref[...])


X_SLOTS = 3
OUT_SLOTS = 2
ROW_GROUPS = TILE_T // SUBLANES


def _tile_block_copies(hbm_ref, vmem_ref, sem_ref, tile, n_slots, n_tiles, to_hbm):
    b = lax.div(tile, n_tiles)
    first_row = (tile - b * n_tiles) * TILE_T
    slot = lax.rem(tile, n_slots)
    copies = []
    for c in range(TILE_T // CHUNK):
        for s in range(SUBLANES):
            hbm = hbm_ref.at[b, pl.ds(first_row + c * CHUNK + s * BLOCK_STEPS, BLOCK_STEPS), :]
            vmem = vmem_ref.at[slot, pl.ds(c * BLOCK_STEPS, BLOCK_STEPS), s, :]
            src, dst = (vmem, hbm) if to_hbm else (hbm, vmem)
            copies.append(pltpu.make_async_copy(src, dst, sem_ref.at[slot]))
    return copies


def _main_kernel(n_tiles, n_total,
                 x_hbm, cos_ref, sin_ref, g1_ref, win_ref, convw_ref, convb_ref, wrg_ref, wig_ref,
                 brg_ref, big_ref, lam_ref, rng_ref, wout_ref, fng_ref, dmaskp_ref, kdec_ref, qdec_ref, even_ref, odd_ref,
                 gst_ref, bdm_ref, lx0_ref, h0_ref, st0_ref,
                 o_hbm,
                 u_scr,
                 lx_a, gate_a, q_a, k_a, v_a, rg_a,
                 lx_b, gate_b, q_b, k_b, v_b, rg_b,
                 xc_scr, xcb_scr, pre_scr, y_scr, lxtail_scr, h_scr, st_scr,
                 xin_scr, out_scr, xin_sem, out_sem):
    g = pl.program_id(0)
    slots = ((lx_a, gate_a, q_a, k_a, v_a, rg_a), (lx_b, gate_b, q_b, k_b, v_b, rg_b))
    fetch = lambda tile: _tile_block_copies(x_hbm, xin_scr, xin_sem, tile, X_SLOTS, n_tiles, to_hbm=False)
    write_back = lambda tile: _tile_block_copies(o_hbm, out_scr, out_sem, tile, OUT_SLOTS, n_tiles, to_hbm=True)

    @pl.when(g == 0)
    def _():
        for ref in slots[1]:
            ref[...] = jnp.zeros(ref.shape, ref.dtype)
        xin_scr[X_SLOTS - 1] = jnp.zeros(xin_scr.shape[1:], f32)
        lxtail_scr[...] = jnp.zeros(lxtail_scr.shape, f32)
        h_scr[...] = jnp.zeros(h_scr.shape, f32)
        st_scr[...] = jnp.zeros(st_scr.shape, f32)
        for copy in fetch(g):
            copy.start()

    @pl.when(g < n_total)
    def _():
        for copy in fetch(g):
            copy.wait()

    @pl.when(g + 1 < n_total)
    def _():
        for copy in fetch(g + 1):
            copy.start()

    @pl.when(g >= 1 + OUT_SLOTS)
    def _():
        for copy in write_back(g - 1 - OUT_SLOTS):
            copy.wait()

    @pl.when(lax.rem(g + n_tiles - 1, n_tiles) == 0)
    def _():
        lxtail_scr[...] = lx0_ref[...]
        h_scr[...] = h0_ref[...]
        st_scr[...] = st0_ref[...]

    tiles_after_head = (4, 4, 3, 3, 2, 2, 1, 1)
    assert sum(tiles_after_head) == N_IN_TILES and len(tiles_after_head) == LRU_HEADS

    x_tile = lambda tile: xin_scr[lax.rem(tile + X_SLOTS, X_SLOTS)].reshape(TILE_T, D_MODEL)

    def step(slot_project, slot_finish):
        _norm_tile(x_tile(g), g1_ref, u_scr)

        def after_gates(h):
            first = sum(tiles_after_head[:h])
            for t in range(first, first + tiles_after_head[h]):
                _project_col_tile(t, u_scr, win_ref, slot_project)

        out = _finish_tile(lambda: x_tile(g - 1), cos_ref, sin_ref, convw_ref, convb_ref, wrg_ref, wig_ref, brg_ref, big_ref,
                           lam_ref, rng_ref, wout_ref, fng_ref, dmaskp_ref, kdec_ref, qdec_ref, even_ref, odd_ref, gst_ref,
                           bdm_ref, slot_finish, xc_scr, xcb_scr, pre_scr, y_scr, lxtail_scr, h_scr, st_scr, after_gates)
        out_scr[lax.rem(g - 1 + OUT_SLOTS, OUT_SLOTS)] = out.reshape(ROW_GROUPS, SUBLANES, D_MODEL)

    @pl.when(lax.rem(g, 2) == 0)
    def _():
        step(slots[0], slots[1])

    @pl.when(lax.rem(g, 2) == 1)
    def _():
        step(slots[1], slots[0])

    @pl.when(g >= 1)
    def _():
        for copy in write_back(g - 1):
            copy.start()

    @pl.when(g == n_total)
    def _():
        for tile in (g - 2, g - 1):
            for copy in write_back(tile):
                copy.wait()


def _const_spec(shape):
    nd = len(shape)
    return pl.BlockSpec(shape, lambda g, _nd=nd: (0,) * _nd, pipeline_mode=pl.Buffered(1))


def kernel(x, meta_tokens, norm_gain, w_in, conv_w, conv_b, w_rg, b_rg, w_ig, b_ig,
           lru_lambda, ret_norm_gain, w_out, final_norm_gain):
    B, S, D = x.shape
    assert D == D_MODEL and S % TILE_T == 0 and TILE_T % CHUNK == 0
    assert norm_gain.shape[0] == 1, "single-layer block"
    assert meta_tokens.shape == (N_META, D_MODEL)

    assert w_in.shape == (1, D_MODEL, IN_WIDTH) and w_out.shape == (1, MIX_WIDTH, D_MODEL)
    assert w_rg.shape == w_ig.shape == (1, LRU_HEADS, LRU_BLOCK, LRU_BLOCK)

    g1 = norm_gain[0].reshape(1, D_MODEL)
    fng = final_norm_gain.reshape(1, D_MODEL)
    convw = conv_w[0]
    convb = conv_b[0].reshape(1, LRU_WIDTH)
    brg = b_rg[0].reshape(1, LRU_WIDTH)
    big = b_ig[0].reshape(1, LRU_WIDTH)
    lam = lru_lambda[0].reshape(1, LRU_WIDTH)
    rng = ret_norm_gain[0].reshape(1, RET_WIDTH)

    k_dec, dmask_blocked, k_dec_blocked, q_dec_blocked, even, odd, g_state, bd_mask = _retention_tables()
    cos_t, sin_t = _rotary_tables(N_META + S)
    cos_m, sin_m = jnp.asarray(cos_t[:N_META]), jnp.asarray(sin_t[:N_META])
    cos_blocked = jnp.asarray(_chunk_rows_blocked(cos_t[N_META:]))
    sin_blocked = jnp.asarray(_chunk_rows_blocked(sin_t[N_META:]))

    n_lx = SUBLANES + N_META
    whole = lambda a: pl.BlockSpec(a.shape, lambda i, _nd=a.ndim: (0,) * _nd)
    w_in_cols = lambda off, width: pl.BlockSpec((None, D_MODEL, width), lambda i, _b=off // width: (0, 0, _b))
    gate_w_spec = pl.BlockSpec((None, LRU_HEADS, LRU_BLOCK, LRU_BLOCK), lambda i: (0, 0, 0, 0))
    meta_out_shapes = ((CONV_HIST, LRU_WIDTH), (SUBLANES, LRU_WIDTH), (N_PAIRS, LANES, PAIR_V))
    lx0, h0, st0 = pl.pallas_call(
        _meta_kernel,
        grid=(1,),
        out_specs=tuple(pl.BlockSpec(s, lambda i, _nd=len(s): (0,) * _nd) for s in meta_out_shapes),
        in_specs=[whole(meta_tokens), whole(cos_m), whole(sin_m), whole(g1),
                  w_in_cols(OFF_LX, LRU_WIDTH), w_in_cols(OFF_K, RET_QK_WIDTH), w_in_cols(OFF_V, RET_WIDTH),
                  whole(convw), whole(convb), gate_w_spec, gate_w_spec,
                  whole(brg), whole(big), whole(lam), whole(k_dec), whole(bd_mask)],
        out_shape=tuple(jax.ShapeDtypeStruct(s, f32) for s in meta_out_shapes),
        scratch_shapes=[pltpu.VMEM((n_lx, LRU_WIDTH), f32),
                        pltpu.VMEM((N_META, LRU_WIDTH), f32),
                        pltpu.VMEM((N_META, LRU_WIDTH), bf16),
                        pltpu.VMEM((N_META, 2 * LRU_WIDTH), f32)],
        compiler_params=pltpu.CompilerParams(vmem_limit_bytes=VMEM_LIMIT_BYTES),
        name="hybrid_meta_state",
    )(meta_tokens, cos_m, sin_m, g1, w_in, w_in, w_in, convw, convb, w_rg, w_ig, brg, big, lam, k_dec, bd_mask)

    tt = TILE_T
    n_tiles = S // tt
    n_total = B * n_tiles

    rot_spec = pl.BlockSpec((tt, LANES), lambda g: (jnp.maximum(g - 1, 0) % n_tiles, 0))
    in_specs = [
        pl.BlockSpec(memory_space=pl.ANY),
        rot_spec, rot_spec,
        _const_spec((1, D_MODEL)),
        _const_spec((None, D_MODEL, IN_WIDTH)),
        _const_spec((CONV_WIDTH, LRU_WIDTH)), _const_spec((1, LRU_WIDTH)),
        _const_spec((None, LRU_HEADS, LRU_BLOCK, LRU_BLOCK)),
        _const_spec((None, LRU_HEADS, LRU_BLOCK, LRU_BLOCK)),
        _const_spec((1, LRU_WIDTH)), _const_spec((1, LRU_WIDTH)), _const_spec((1, LRU_WIDTH)),
        _const_spec((1, RET_WIDTH)),
        _const_spec((None, MIX_WIDTH, D_MODEL)),
        _const_spec((1, D_MODEL)),
        _const_spec((N_PAIRS, CHUNK, 2 * CHUNK)),
        _const_spec((CHUNK, RET_QK_WIDTH)), _const_spec((CHUNK, RET_QK_WIDTH)),
        _const_spec((1, RET_QK_WIDTH)), _const_spec((1, RET_QK_WIDTH)),
        _const_spec((N_PAIRS, LANES, PAIR_V)),
        _const_spec((LANES, PAIR_V)),
        _const_spec((CONV_HIST, LRU_WIDTH)), _const_spec((SUBLANES, LRU_WIDTH)),
        _const_spec((N_PAIRS, LANES, PAIR_V)),
    ]
    slot_scratch = [
        pltpu.VMEM((CONV_HIST + tt, LRU_WIDTH), f32),
        pltpu.VMEM((tt, LRU_WIDTH), f32),
        pltpu.VMEM((tt, RET_QK_WIDTH), f32),
        pltpu.VMEM((tt, RET_QK_WIDTH), f32),
        pltpu.VMEM((tt, RET_WIDTH), bf16),
        pltpu.VMEM((tt, RET_WIDTH), f32),
    ]
    scratch = [pltpu.VMEM((tt, D_MODEL), bf16)] + slot_scratch + slot_scratch + [
        pltpu.VMEM((tt, LRU_WIDTH), f32),
        pltpu.VMEM((tt, LRU_WIDTH), bf16),
        pltpu.VMEM((tt, 2 * LRU_WIDTH), f32),
        pltpu.VMEM((tt, MIX_WIDTH), bf16),
        pltpu.VMEM((CONV_HIST, LRU_WIDTH), f32),
        pltpu.VMEM((SUBLANES, LRU_WIDTH), f32),
        pltpu.VMEM((N_PAIRS, LANES, PAIR_V), f32),
        pltpu.VMEM((X_SLOTS, ROW_GROUPS, SUBLANES, D_MODEL), f32),
        pltpu.VMEM((OUT_SLOTS, ROW_GROUPS, SUBLANES, D_MODEL), f32),
        pltpu.SemaphoreType.DMA((X_SLOTS,)),
        pltpu.SemaphoreType.DMA((OUT_SLOTS,)),
    ]
    out = pl.pallas_call(
        functools.partial(_main_kernel, n_tiles, n_total),
        grid=(n_total + 1,),
        in_specs=in_specs,
        out_specs=pl.BlockSpec(memory_space=pl.ANY),
        out_shape=jax.ShapeDtypeStruct((B, S, D_MODEL), x.dtype),
        scratch_shapes=scratch,
        compiler_params=pltpu.CompilerParams(
            dimension_semantics=("arbitrary",),
            vmem_limit_bytes=VMEM_LIMIT_BYTES),
        name="hybrid_main",
    )(x, cos_blocked, sin_blocked, g1, w_in, convw, convb, w_rg, w_ig, brg, big, lam, rng, w_out, fng,
      dmask_blocked, k_dec_blocked, q_dec_blocked, even, odd, g_state, bd_mask, lx0, h0, st0)
    return out
```

```python
import functools

import numpy as np
import jax
import jax.numpy as jnp
from jax import lax
from jax.experimental import pallas as pl
from jax.experimental.pallas import tpu as pltpu

f32 = jnp.float32
bf16 = jnp.bfloat16

D_MODEL = 1024
N_META = 16
LRU_WIDTH = 1024
LRU_HEADS = 8
LRU_BLOCK = 128
CONV_WIDTH = 4
LRU_C = 8.0
RET_HEADS = 8
RET_QK_DIM = 64
RET_V_DIM = 128
RET_QK_WIDTH = 512
RET_WIDTH = 1024
CHUNK = 128
ROPE_BASE = 10000.0
MIX_WIDTH = 2048
EPS = 1e-6
QK_SCALE = RET_QK_DIM ** -0.5

OFF_LX, OFF_GATE, OFF_Q, OFF_K, OFF_V, OFF_RG = 0, 1024, 2048, 2560, 3072, 4096
IN_WIDTH = 5120

LANES = 128
SUBLANES = 8
N_PAIRS = RET_HEADS // 2
PAIR_V = 2 * RET_V_DIM
TILE_T = 256
VMEM_LIMIT_BYTES = 60000 * 1024


def _lane_head():
    return np.arange(RET_QK_WIDTH) // RET_QK_DIM


BLOCK_STEPS = CHUNK // SUBLANES
CHUNK_ROW_TIME = np.arange(CHUNK).reshape(SUBLANES, BLOCK_STEPS).T.reshape(-1)


def _chunk_rows_blocked(table):
    n = table.shape[0]
    return table.reshape((n // CHUNK, CHUNK) + table.shape[1:])[:, CHUNK_ROW_TIME].reshape(table.shape)


def _retention_tables():
    log_g = np.log1p(-np.exp2(-5.0 - np.arange(RET_HEADS, dtype=np.float32))).astype(np.float32)
    idx = np.arange(CHUNK, dtype=np.float32)
    diff = idx[:, None] - idx[None, :]
    dmask = np.where(diff[None] >= 0.0, np.exp(np.maximum(diff, 0.0)[None] * log_g[:, None, None]), 0.0)
    dmask_pair = np.concatenate([dmask[0::2], dmask[1::2]], axis=-1)
    lg_lane = log_g[_lane_head()]
    k_dec = np.exp((CHUNK - 1.0 - idx)[:, None] * lg_lane[None, :])
    q_dec = np.exp((idx + 1.0)[:, None] * lg_lane[None, :]) * QK_SCALE
    even = (_lane_head() % 2 == 0)[None, :]
    g_chunk = np.exp(CHUNK * log_g)
    g_state = np.broadcast_to(np.repeat(g_chunk, RET_V_DIM).reshape(N_PAIRS, 1, PAIR_V), (N_PAIRS, LANES, PAIR_V))
    row_par = np.arange(LANES) // RET_QK_DIM
    col_par = np.arange(PAIR_V) // RET_V_DIM
    bd_mask = row_par[:, None] == col_par[None, :]
    key_order = np.concatenate([CHUNK_ROW_TIME, CHUNK + CHUNK_ROW_TIME])
    dmask_blocked = dmask_pair[:, CHUNK_ROW_TIME][:, :, key_order]
    as_f32 = lambda a: jnp.asarray(np.asarray(a, np.float32))
    return tuple(as_f32(t) for t in (k_dec, dmask_blocked, k_dec[CHUNK_ROW_TIME], q_dec[CHUNK_ROW_TIME],
                                     even, ~even, g_state, bd_mask))


def _rotary_tables(n_pos):
    half = RET_QK_DIM // 2
    inv = (np.float32(ROPE_BASE) ** (-np.arange(half, dtype=np.float32) / half)).astype(np.float32)
    ang = np.arange(n_pos).astype(np.float32)[:, None] * inv[None, :]
    cos, sin = np.cos(ang), np.sin(ang)
    cos_t = np.concatenate([cos, cos, cos, cos], axis=-1).astype(np.float32)
    sin_t = np.concatenate([-sin, sin, -sin, sin], axis=-1).astype(np.float32)
    return cos_t, sin_t


def _rmsnorm_rows(x, gain_row):
    ms = jnp.mean(x * x, axis=-1, keepdims=True)
    return x * lax.rsqrt(ms + EPS) * gain_row


def _sigmoid(x):
    return 0.5 * jnp.tanh(0.5 * x) + 0.5


def _silu(x):
    hx = 0.5 * x
    return hx * jnp.tanh(hx) + hx


def _conv(n_rows, lx_scr, convw_ref, convb_ref, xc_scr, xcb_scr):
    base = SUBLANES
    xc = convb_ref[...] + convw_ref[3:4, :] * lx_scr[pl.ds(base, n_rows), :]
    xc = xc + convw_ref[2:3, :] * lx_scr[pl.ds(base - 1, n_rows), :]
    xc = xc + convw_ref[1:2, :] * lx_scr[pl.ds(base - 2, n_rows), :]
    xc = xc + convw_ref[0:1, :] * lx_scr[pl.ds(base - 3, n_rows), :]
    xc_scr[...] = xc
    xcb_scr[...] = xc.astype(bf16)


CONV_HIST = (CONV_WIDTH - 1) * SUBLANES


def _conv_blocked(lx_scr, convw_ref, convb_ref, xc_scr, xcb_scr):
    last_sublane = lax.broadcasted_iota(jnp.int32, (SUBLANES, LRU_WIDTH), 0) == SUBLANES - 1
    for c in range(TILE_T // CHUNK):
        base = CONV_HIST + c * CHUNK
        rows = lambda group, n: lx_scr[base + group * SUBLANES:base + (group + n) * SUBLANES, :]
        wrapped = []
        for k in range(1, CONV_WIDTH):
            own, prev = rows(BLOCK_STEPS - k, 1), rows(-k, 1)
            wrapped.append(pltpu.roll(jnp.where(last_sublane, prev, own), 1, 0))
        xc = convb_ref[...] + convw_ref[3:4, :] * rows(0, BLOCK_STEPS)
        for k in range(1, CONV_WIDTH):
            shifted = jnp.concatenate(wrapped[:k][::-1] + [rows(0, BLOCK_STEPS - k)], axis=0)
            xc = xc + convw_ref[CONV_WIDTH - 1 - k:CONV_WIDTH - k, :] * shifted
        xc_scr[c * CHUNK:(c + 1) * CHUNK, :] = xc
        xcb_scr[c * CHUNK:(c + 1) * CHUNK, :] = xc.astype(bf16)


def _gates_head(h, xcb_scr, wrg_ref, wig_ref, pre_scr):
    lanes = slice(h * LRU_BLOCK, (h + 1) * LRU_BLOCK)
    wg = jnp.concatenate([wrg_ref[h], wig_ref[h]], axis=-1).astype(bf16)
    pre = jnp.dot(xcb_scr[:, lanes], wg, preferred_element_type=f32)
    pre_scr[:, lanes] = pre[:, :LRU_BLOCK]
    pre_scr[:, LRU_WIDTH + h * LRU_BLOCK:LRU_WIDTH + (h + 1) * LRU_BLOCK] = pre[:, LRU_BLOCK:]


def _lru_maps(xc, pre_r, pre_i, brg, big, c_sp):
    r = _sigmoid(pre_r + brg)
    i = _sigmoid(pre_i + big)
    nl = r * c_sp
    a = jnp.exp(-nl)
    z = jnp.tanh(nl) * (1.0 + a * a)
    beta = jnp.where(z > 0.0, z * lax.rsqrt(z), 0.0)
    return a, beta * i * xc


def _scan_sublanes(a, b):
    rowid = lax.broadcasted_iota(jnp.int32, a.shape, 0)
    for s in (1, 2, 4):
        keep = rowid >= s
        a_s = jnp.where(keep, pltpu.roll(a, s, 0), 1.0)
        b_s = jnp.where(keep, pltpu.roll(b, s, 0), 0.0)
        b = a * b_s + b
        a = a * a_s
    return a, b


def _lru_chunk_blocked(xc, pre_r, pre_i, brg, big, c_sp, carry):
    a, b = _lru_maps(xc, pre_r, pre_i, brg, big, c_sp)
    width = xc.shape[1]
    a3 = a.reshape(BLOCK_STEPS, SUBLANES, width)
    b3 = b.reshape(BLOCK_STEPS, SUBLANES, width)
    decay, local = [a3[0]], [b3[0]]
    for j in range(1, BLOCK_STEPS):
        local.append(a3[j] * local[-1] + b3[j])
        decay.append(a3[j] * decay[-1])
    a_blocks, b_blocks = _scan_sublanes(decay[-1], local[-1])
    after = a_blocks * carry + b_blocks
    rowid = lax.broadcasted_iota(jnp.int32, carry.shape, 0)
    before = jnp.where(rowid >= 1, pltpu.roll(after, 1, 0), carry)
    hs = [local[j] + decay[j] * before for j in range(BLOCK_STEPS)]
    new_carry = jnp.broadcast_to(after[SUBLANES - 1:SUBLANES, :], carry.shape)
    return jnp.concatenate(hs, axis=0), new_carry


def _lru_block(xc, pre_r, pre_i, brg, big, c_sp, carry):
    rows, width = xc.shape
    a, b = _lru_maps(xc, pre_r, pre_i, brg, big, c_sp)
    nv = rows // SUBLANES
    a3 = a.reshape(nv, SUBLANES, width)
    b3 = b.reshape(nv, SUBLANES, width)
    rowid = lax.broadcasted_iota(jnp.int32, (nv, SUBLANES, width), 1)
    for s in (1, 2, 4):
        keep = rowid >= s
        a_s = jnp.where(keep, pltpu.roll(a3, s, 1), 1.0)
        b_s = jnp.where(keep, pltpu.roll(b3, s, 1), 0.0)
        b3 = a3 * b_s + b3
        a3 = a3 * a_s
    hs = []
    for v in range(nv):
        h_v = a3[v] * carry + b3[v]
        carry = h_v[SUBLANES - 1:SUBLANES, :]
        hs.append(h_v)
    return jnp.concatenate(hs, axis=0), carry


def _rotary(t, cos, sin):
    half = RET_QK_DIM // 2
    in_first_half = (lax.broadcasted_iota(jnp.int32, t.shape, 1) // half) % 2 == 0
    partner = jnp.where(in_first_half, pltpu.roll(t, LANES - half, 1), pltpu.roll(t, half, 1))
    return t * cos + partner * sin


def _dot_t0(a, b):
    return lax.dot_general(a, b, (((0,), (0,)), ((), ())), preferred_element_type=f32)


def _dot_nt(a, b):
    return lax.dot_general(a, b, (((1,), (1,)), ((), ())), preferred_element_type=f32)


def _meta_kernel(meta_ref, cos_ref, sin_ref, g1_ref, wlx_ref, wk_ref, wv_ref, convw_ref, convb_ref,
                 wrg_ref, wig_ref, brg_ref, big_ref, lam_ref, kdec_ref, bdm_ref,
                 lx_out, h_out, st_out,
                 lx_scr, xc_scr, xcb_scr, pre_scr):
    u = _rmsnorm_rows(meta_ref[...], g1_ref[...]).astype(bf16)
    lx_scr[0:SUBLANES, :] = jnp.zeros((SUBLANES, LRU_WIDTH), f32)
    lx_scr[SUBLANES:SUBLANES + N_META, :] = jnp.dot(u, wlx_ref[...].astype(bf16), preferred_element_type=f32)
    k = jnp.dot(u, wk_ref[...].astype(bf16), preferred_element_type=f32)
    v = jnp.dot(u, wv_ref[...].astype(bf16), preferred_element_type=f32).astype(bf16)

    _conv(N_META, lx_scr, convw_ref, convb_ref, xc_scr, xcb_scr)
    for h in range(LRU_HEADS):
        _gates_head(h, xcb_scr, wrg_ref, wig_ref, pre_scr)
    for back in range(1, CONV_WIDTH):
        row = lx_scr[SUBLANES + N_META - back:SUBLANES + N_META - back + 1, :]
        group = CONV_WIDTH - 1 - back
        lx_out[group * SUBLANES:(group + 1) * SUBLANES, :] = jnp.broadcast_to(row, (SUBLANES, LRU_WIDTH))
    c_sp = LRU_C * jax.nn.softplus(-lam_ref[...])
    _, carry = _lru_block(xc_scr[...], pre_scr[:, :LRU_WIDTH], pre_scr[:, LRU_WIDTH:],
                          brg_ref[...], big_ref[...], c_sp, jnp.zeros((1, LRU_WIDTH), f32))
    h_out[...] = jnp.broadcast_to(carry, (SUBLANES, LRU_WIDTH))

    cos, sin = cos_ref[...], sin_ref[...]
    for p in range(N_PAIRS):
        sl = slice(p * LANES, (p + 1) * LANES)
        k_rot = _rotary(k[:, sl], cos, sin)
        kd = (k_rot * kdec_ref[CHUNK - N_META:CHUNK, sl]).astype(bf16)
        st_out[p] = _dot_t0(kd, v[:, p * PAIR_V:(p + 1) * PAIR_V]) * bdm_ref[...]


def _norm_tile(x_tile, g1_ref, u_scr):
    u_scr[...] = _rmsnorm_rows(x_tile, g1_ref[...]).astype(bf16)


MXU_COLS = 256
N_IN_TILES = IN_WIDTH // MXU_COLS


def _project_col_tile(t, u_scr, win_ref, slot):
    lx_scr, gate_scr, q_scr, k_scr, v_scr, rg_scr = slot
    off = t * MXU_COLS
    res = jnp.dot(u_scr[...], win_ref[:, off:off + MXU_COLS].astype(bf16), preferred_element_type=f32)
    for dst, start, rows, dt in ((lx_scr, OFF_LX, slice(CONV_HIST, CONV_HIST + TILE_T), f32), (gate_scr, OFF_GATE, slice(None), f32),
                                 (q_scr, OFF_Q, slice(None), f32), (k_scr, OFF_K, slice(None), f32),
                                 (v_scr, OFF_V, slice(None), bf16), (rg_scr, OFF_RG, slice(None), f32)):
        width = dst.shape[1]
        if start <= off < start + width:
            dst[rows, off - start:off - start + MXU_COLS] = res.astype(dt)
            return
    raise AssertionError("column tile outside the projection")


def _finish_tile(load_x, cos_ref, sin_ref, convw_ref, convb_ref, wrg_ref, wig_ref, brg_ref, big_ref, lam_ref, rng_ref,
                 wout_ref, fng_ref, dmaskp_ref, kdec_ref, qdec_ref, even_ref, odd_ref, gst_ref, bdm_ref,
                 slot, xc_scr, xcb_scr, pre_scr, y_scr, lxtail_scr, h_scr, st_scr, after_gates):
    lx_scr, gate_scr, q_scr, k_scr, v_scr, rg_scr = slot
    tt = TILE_T

    lx_scr[0:CONV_HIST, :] = lxtail_scr[...]
    _conv_blocked(lx_scr, convw_ref, convb_ref, xc_scr, xcb_scr)
    lxtail_scr[...] = lx_scr[pl.ds(tt, CONV_HIST), :]
    for h in range(LRU_HEADS):
        lanes = slice(h * LRU_BLOCK, (h + 1) * LRU_BLOCK)
        ilanes = slice(LRU_WIDTH + h * LRU_BLOCK, LRU_WIDTH + (h + 1) * LRU_BLOCK)
        _gates_head(h, xcb_scr, wrg_ref, wig_ref, pre_scr)
        after_gates(h)
        full = (CHUNK, LRU_BLOCK)
        c_sp = jnp.broadcast_to(LRU_C * jax.nn.softplus(-lam_ref[:, lanes]), full)
        brg, big = jnp.broadcast_to(brg_ref[:, lanes], full), jnp.broadcast_to(big_ref[:, lanes], full)
        carry = h_scr[:, lanes]
        for c in range(tt // CHUNK):
            rows = slice(c * CHUNK, (c + 1) * CHUNK)
            hb, carry = _lru_chunk_blocked(xc_scr[rows, lanes], pre_scr[rows, lanes], pre_scr[rows, ilanes],
                                           brg, big, c_sp, carry)
            y_scr[rows, lanes] = (hb * _silu(gate_scr[rows, lanes])).astype(bf16)
        h_scr[:, lanes] = carry

    for c in range(tt // CHUNK):
        rows = slice(c * CHUNK, (c + 1) * CHUNK)
        cos, sin = cos_ref[rows, :], sin_ref[rows, :]
        for p in range(N_PAIRS):
            sl = slice(p * LANES, (p + 1) * LANES)
            q_rot = _rotary(q_scr[rows, sl], cos, sin)
            k_rot = _rotary(k_scr[rows, sl], cos, sin)
            q_b = (q_rot * QK_SCALE).astype(bf16)
            q_d = (q_rot * qdec_ref[:, sl]).astype(bf16)
            k_d = (k_rot * kdec_ref[:, sl]).astype(bf16)
            kk = jnp.concatenate([(k_rot * even_ref[:, sl]).astype(bf16), (k_rot * odd_ref[:, sl]).astype(bf16)], axis=0)
            v_pair = v_scr[rows, p * PAIR_V:(p + 1) * PAIR_V]
            zeros = jnp.zeros((CHUNK, RET_V_DIM), bf16)
            v_bd = jnp.concatenate([jnp.concatenate([v_pair[:, :RET_V_DIM], zeros], axis=1),
                                    jnp.concatenate([zeros, v_pair[:, RET_V_DIM:]], axis=1)], axis=0)
            st_pair = st_scr[p]
            s = _dot_nt(q_b, kk) * dmaskp_ref[p]
            o = jnp.dot(s.astype(bf16), v_bd, preferred_element_type=f32)
            o = o + jnp.dot(q_d, st_pair.astype(bf16), preferred_element_type=f32)
            for e in range(2):
                h = 2 * p + e
                hs = slice(h * RET_V_DIM, (h + 1) * RET_V_DIM)
                o_h = o[:, e * RET_V_DIM:(e + 1) * RET_V_DIM]
                mu = jnp.mean(o_h, axis=-1, keepdims=True)
                oc = o_h - mu
                var = jnp.mean(oc * oc, axis=-1, keepdims=True)
                on = oc * lax.rsqrt(var + EPS) * rng_ref[:, hs]
                y_scr[rows, LRU_WIDTH + h * RET_V_DIM:LRU_WIDTH + (h + 1) * RET_V_DIM] = (
                    on * _silu(rg_scr[rows, hs])).astype(bf16)
            kv = _dot_t0(k_d, v_pair)
            st_scr[p] = gst_ref[p] * st_pair + bdm_ref[...] * kv

    res = load_x() + jnp.dot(y_scr[...], wout_ref[...].astype(bf16), preferred_element_type=f32)
    return _rmsnorm_rows(res, fng_ref[...])


X_SLOTS = 3
OUT_SLOTS = 2
ROW_GROUPS = TILE_T // SUBLANES


def _tile_block_copies(hbm_ref, vmem_ref, sem_ref, tile, slot, n_tiles, to_hbm):
    b = lax.div(tile, n_tiles)
    first_row = (tile - b * n_tiles) * TILE_T
    copies = []
    for c in range(TILE_T // CHUNK):
        for s in range(SUBLANES):
            hbm = hbm_ref.at[b, pl.ds(first_row + c * CHUNK + s * BLOCK_STEPS, BLOCK_STEPS), :]
            vmem = vmem_ref.at[slot, pl.ds(c * BLOCK_STEPS, BLOCK_STEPS), s, :]
            src, dst = (vmem, hbm) if to_hbm else (hbm, vmem)
            copies.append(pltpu.make_async_copy(src, dst, sem_ref.at[slot]))
    return copies


def _main_kernel(n_tiles, n_total,
                 x_hbm, cos_ref, sin_ref, g1_ref, win_ref, convw_ref, convb_ref, wrg_ref, wig_ref,
                 brg_ref, big_ref, lam_ref, rng_ref, wout_ref, fng_ref, dmaskp_ref, kdec_ref, qdec_ref, even_ref, odd_ref,
                 gst_ref, bdm_ref, lx0_ref, h0_ref, st0_ref,
                 o_hbm,
                 u_scr,
                 lx_a, gate_a, q_a, k_a, v_a, rg_a,
                 lx_b, gate_b, q_b, k_b, v_b, rg_b,
                 xc_scr, xcb_scr, pre_scr, y_scr, lxtail_scr, h_scr, st_scr,
                 xin_scr, out_scr, xin_sem, out_sem):
    g = pl.program_id(0)
    slots = ((lx_a, gate_a, q_a, k_a, v_a, rg_a), (lx_b, gate_b, q_b, k_b, v_b, rg_b))
    x_slot = lambda tile: lax.rem(tile + X_SLOTS, X_SLOTS)
    out_slot = lambda tile: lax.rem(tile + OUT_SLOTS, OUT_SLOTS)
    x_tile = lambda tile: xin_scr[x_slot(tile)].reshape(TILE_T, D_MODEL)
    fetch = lambda tile: _tile_block_copies(x_hbm, xin_scr, xin_sem, jnp.minimum(tile, n_total - 1), x_slot(tile),
                                            n_tiles, to_hbm=False)
    write_back = lambda tile: _tile_block_copies(o_hbm, out_scr, out_sem, tile, out_slot(tile), n_tiles, to_hbm=True)

    @pl.when(g >= 1 + OUT_SLOTS)
    def _():
        for copy in write_back(g - 1 - OUT_SLOTS):
            copy.wait()

    @pl.when(lax.rem(g + n_tiles - 1, n_tiles) == 0)
    def _():
        lxtail_scr[...] = lx0_ref[...]
        h_scr[...] = h0_ref[...]
        st_scr[...] = st0_ref[...]

    tiles_after_head = (4, 4, 3, 3, 2, 2, 1, 1)
    assert sum(tiles_after_head) == N_IN_TILES and len(tiles_after_head) == LRU_HEADS

    def finish(slot_finish, after_gates):
        out = _finish_tile(lambda: x_tile(g - 1), cos_ref, sin_ref, convw_ref, convb_ref, wrg_ref, wig_ref, brg_ref, big_ref,
                           lam_ref, rng_ref, wout_ref, fng_ref, dmaskp_ref, kdec_ref, qdec_ref, even_ref, odd_ref, gst_ref,
                           bdm_ref, slot_finish, xc_scr, xcb_scr, pre_scr, y_scr, lxtail_scr, h_scr, st_scr, after_gates)
        out_scr[out_slot(g - 1)] = out.reshape(ROW_GROUPS, SUBLANES, D_MODEL)
        for copy in write_back(g - 1):
            copy.start()

    def step(slot_project, slot_finish):
        for copy in fetch(g):
            copy.wait()
        for copy in fetch(g + 1):
            copy.start()
        _norm_tile(x_tile(g), g1_ref, u_scr)

        def after_gates(h):
            first = sum(tiles_after_head[:h])
            for t in range(first, first + tiles_after_head[h]):
                _project_col_tile(t, u_scr, win_ref, slot_project)

        finish(slot_finish, after_gates)

    @pl.when(g == 0)
    def _():
        for copy in fetch(g):
            copy.start()
        for copy in fetch(g):
            copy.wait()
        for copy in fetch(g + 1):
            copy.start()
        _norm_tile(x_tile(g), g1_ref, u_scr)
        for t in range(N_IN_TILES):
            _project_col_tile(t, u_scr, win_ref, slots[0])

    regular = jnp.logical_and(g >= 1, g < n_total)

    @pl.when(jnp.logical_and(regular, lax.rem(g, 2) == 0))
    def _():
        step(slots[0], slots[1])

    @pl.when(jnp.logical_and(regular, lax.rem(g, 2) == 1))
    def _():
        step(slots[1], slots[0])

    @pl.when(g == n_total)
    def _():
        for copy in fetch(g):
            copy.wait()
        finish(slots[(n_total - 1) % 2], lambda h: None)
        for tile in (g - 2, g - 1):
            for copy in write_back(tile):
                copy.wait()


def _const_spec(shape):
    nd = len(shape)
    return pl.BlockSpec(shape, lambda g, _nd=nd: (0,) * _nd, pipeline_mode=pl.Buffered(1))


def kernel(x, meta_tokens, norm_gain, w_in, conv_w, conv_b, w_rg, b_rg, w_ig, b_ig,
           lru_lambda, ret_norm_gain, w_out, final_norm_gain):
    B, S, D = x.shape
    assert D == D_MODEL and S % TILE_T == 0 and TILE_T % CHUNK == 0
    assert norm_gain.shape[0] == 1, "single-layer block"
    assert meta_tokens.shape == (N_META, D_MODEL)

    assert w_in.shape == (1, D_MODEL, IN_WIDTH) and w_out.shape == (1, MIX_WIDTH, D_MODEL)
    assert w_rg.shape == w_ig.shape == (1, LRU_HEADS, LRU_BLOCK, LRU_BLOCK)

    g1 = norm_gain[0].reshape(1, D_MODEL)
    fng = final_norm_gain.reshape(1, D_MODEL)
    convw = conv_w[0]
    convb = conv_b[0].reshape(1, LRU_WIDTH)
    brg = b_rg[0].reshape(1, LRU_WIDTH)
    big = b_ig[0].reshape(1, LRU_WIDTH)
    lam = lru_lambda[0].reshape(1, LRU_WIDTH)
    rng = ret_norm_gain[0].reshape(1, RET_WIDTH)

    k_dec, dmask_blocked, k_dec_blocked, q_dec_blocked, even, odd, g_state, bd_mask = _retention_tables()
    cos_t, sin_t = _rotary_tables(N_META + S)
    cos_m, sin_m = jnp.asarray(cos_t[:N_META]), jnp.asarray(sin_t[:N_META])
    cos_blocked = jnp.asarray(_chunk_rows_blocked(cos_t[N_META:]))
    sin_blocked = jnp.asarray(_chunk_rows_blocked(sin_t[N_META:]))

    n_lx = SUBLANES + N_META
    whole = lambda a: pl.BlockSpec(a.shape, lambda i, _nd=a.ndim: (0,) * _nd)
    w_in_cols = lambda off, width: pl.BlockSpec((None, D_MODEL, width), lambda i, _b=off // width: (0, 0, _b))
    gate_w_spec = pl.BlockSpec((None, LRU_HEADS, LRU_BLOCK, LRU_BLOCK), lambda i: (0, 0, 0, 0))
    meta_out_shapes = ((CONV_HIST, LRU_WIDTH), (SUBLANES, LRU_WIDTH), (N_PAIRS, LANES, PAIR_V))
    lx0, h0, st0 = pl.pallas_call(
        _meta_kernel,
        grid=(1,),
        out_specs=tuple(pl.BlockSpec(s, lambda i, _nd=len(s): (0,) * _nd) for s in meta_out_shapes),
        in_specs=[whole(meta_tokens), whole(cos_m), whole(sin_m), whole(g1),
                  w_in_cols(OFF_LX, LRU_WIDTH), w_in_cols(OFF_K, RET_QK_WIDTH), w_in_cols(OFF_V, RET_WIDTH),
                  whole(convw), whole(convb), gate_w_spec, gate_w_spec,
                  whole(brg), whole(big), whole(lam), whole(k_dec), whole(bd_mask)],
        out_shape=tuple(jax.ShapeDtypeStruct(s, f32) for s in meta_out_shapes),
        scratch_shapes=[pltpu.VMEM((n_lx, LRU_WIDTH), f32),
                        pltpu.VMEM((N_META, LRU_WIDTH), f32),
                        pltpu.VMEM((N_META, LRU_WIDTH), bf16),
                        pltpu.VMEM((N_META, 2 * LRU_WIDTH), f32)],
        compiler_params=pltpu.CompilerParams(vmem_limit_bytes=VMEM_LIMIT_BYTES),
        name="hybrid_meta_state",
    )(meta_tokens, cos_m, sin_m, g1, w_in, w_in, w_in, convw, convb, w_rg, w_ig, brg, big, lam, k_dec, bd_mask)

    tt = TILE_T
    n_tiles = S // tt
    n_total = B * n_tiles

    rot_spec = pl.BlockSpec((tt, LANES), lambda g: (jnp.maximum(g - 1, 0) % n_tiles, 0))
    in_specs = [
        pl.BlockSpec(memory_space=pl.ANY),
        rot_spec, rot_spec,
        _const_spec((1, D_MODEL)),
        _const_spec((None, D_MODEL, IN_WIDTH)),
        _const_spec((CONV_WIDTH, LRU_WIDTH)), _const_spec((1, LRU_WIDTH)),
        _const_spec((None, LRU_HEADS, LRU_BLOCK, LRU_BLOCK)),
        _const_spec((None, LRU_HEADS, LRU_BLOCK, LRU_BLOCK)),
        _const_spec((1, LRU_WIDTH)), _const_spec((1, LRU_WIDTH)), _const_spec((1, LRU_WIDTH)),
        _const_spec((1, RET_WIDTH)),
        _const_spec((None, MIX_WIDTH, D_MODEL)),
        _const_spec((1, D_MODEL)),
        _const_spec((N_PAIRS, CHUNK, 2 * CHUNK)),
        _const_spec((CHUNK, RET_QK_WIDTH)), _const_spec((CHUNK, RET_QK_WIDTH)),
        _const_spec((1, RET_QK_WIDTH)), _const_spec((1, RET_QK_WIDTH)),
        _const_spec((N_PAIRS, LANES, PAIR_V)),
        _const_spec((LANES, PAIR_V)),
        _const_spec((CONV_HIST, LRU_WIDTH)), _const_spec((SUBLANES, LRU_WIDTH)),
        _const_spec((N_PAIRS, LANES, PAIR_V)),
    ]
    slot_scratch = [
        pltpu.VMEM((CONV_HIST + tt, LRU_WIDTH), f32),
        pltpu.VMEM((tt, LRU_WIDTH), f32),
        pltpu.VMEM((tt, RET_QK_WIDTH), f32),
        pltpu.VMEM((tt, RET_QK_WIDTH), f32),
        pltpu.VMEM((tt, RET_WIDTH), bf16),
        pltpu.VMEM((tt, RET_WIDTH), f32),
    ]
    scratch = [pltpu.VMEM((tt, D_MODEL), bf16)] + slot_scratch + slot_scratch + [
        pltpu.VMEM((tt, LRU_WIDTH), f32),
        pltpu.VMEM((tt, LRU_WIDTH), bf16),
        pltpu.VMEM((tt, 2 * LRU_WIDTH), f32),
        pltpu.VMEM((tt, MIX_WIDTH), bf16),
        pltpu.VMEM((CONV_HIST, LRU_WIDTH), f32),
        pltpu.VMEM((SUBLANES, LRU_WIDTH), f32),
        pltpu.VMEM((N_PAIRS, LANES, PAIR_V), f32),
        pltpu.VMEM((X_SLOTS, ROW_GROUPS, SUBLANES, D_MODEL), f32),
        pltpu.VMEM((OUT_SLOTS, ROW_GROUPS, SUBLANES, D_MODEL), f32),
        pltpu.SemaphoreType.DMA((X_SLOTS,)),
        pltpu.SemaphoreType.DMA((OUT_SLOTS,)),
    ]
    out = pl.pallas_call(
        functools.partial(_main_kernel, n_tiles, n_total),
        grid=(n_total + 1,),
        in_specs=in_specs,
        out_specs=pl.BlockSpec(memory_space=pl.ANY),
        out_shape=jax.ShapeDtypeStruct((B, S, D_MODEL), x.dtype),
        scratch_shapes=scratch,
        compiler_params=pltpu.CompilerParams(
            dimension_semantics=("arbitrary",),
            vmem_limit_bytes=VMEM_LIMIT_BYTES),
        name="hybrid_main",
    )(x, cos_blocked, sin_blocked, g1, w_in, convw, convb, w_rg, w_ig, brg, big, lam, rng, w_out, fng,
      dmask_blocked, k_dec_blocked, q_dec_blocked, even, odd, g_state, bd_mask, lx0, h0, st0)
    return out
```

```python
import functools

import numpy as np
import jax
import jax.numpy as jnp
from jax import lax
from jax.experimental import pallas as pl
from jax.experimental.pallas import tpu as pltpu

f32 = jnp.float32
bf16 = jnp.bfloat16

D_MODEL = 1024
N_META = 16
LRU_WIDTH = 1024
LRU_HEADS = 8
LRU_BLOCK = 128
CONV_WIDTH = 4
LRU_C = 8.0
RET_HEADS = 8
RET_QK_DIM = 64
RET_V_DIM = 128
RET_QK_WIDTH = 512
RET_WIDTH = 1024
CHUNK = 128
ROPE_BASE = 10000.0
MIX_WIDTH = 2048
EPS = 1e-6
QK_SCALE = RET_QK_DIM ** -0.5

OFF_LX, OFF_GATE, OFF_Q, OFF_K, OFF_V, OFF_RG = 0, 1024, 2048, 2560, 3072, 4096
IN_WIDTH = 5120

LANES = 128
SUBLANES = 8
N_PAIRS = RET_HEADS // 2
PAIR_V = 2 * RET_V_DIM
TILE_T = 256
VMEM_LIMIT_BYTES = 60000 * 1024


def _lane_head():
    return np.arange(RET_QK_WIDTH) // RET_QK_DIM


BLOCK_STEPS = CHUNK // SUBLANES
CHUNK_ROW_TIME = np.arange(CHUNK).reshape(SUBLANES, BLOCK_STEPS).T.reshape(-1)


def _chunk_rows_blocked(table):
    n = table.shape[0]
    return table.reshape((n // CHUNK, CHUNK) + table.shape[1:])[:, CHUNK_ROW_TIME].reshape(table.shape)


def _retention_tables():
    log_g = np.log1p(-np.exp2(-5.0 - np.arange(RET_HEADS, dtype=np.float32))).astype(np.float32)
    idx = np.arange(CHUNK, dtype=np.float32)
    diff = idx[:, None] - idx[None, :]
    dmask = np.where(diff[None] >= 0.0, np.exp(np.maximum(diff, 0.0)[None] * log_g[:, None, None]), 0.0)
    dmask_pair = np.concatenate([dmask[0::2], dmask[1::2]], axis=-1)
    lg_lane = log_g[_lane_head()]
    k_dec = np.exp((CHUNK - 1.0 - idx)[:, None] * lg_lane[None, :])
    q_dec = np.exp((idx + 1.0)[:, None] * lg_lane[None, :]) * QK_SCALE
    even = (_lane_head() % 2 == 0)[None, :]
    g_chunk = np.exp(CHUNK * log_g)
    g_state = np.broadcast_to(np.repeat(g_chunk, RET_V_DIM).reshape(N_PAIRS, 1, PAIR_V), (N_PAIRS, LANES, PAIR_V))
    row_par = np.arange(LANES) // RET_QK_DIM
    col_par = np.arange(PAIR_V) // RET_V_DIM
    bd_mask = row_par[:, None] == col_par[None, :]
    key_order = np.concatenate([CHUNK_ROW_TIME, CHUNK + CHUNK_ROW_TIME])
    dmask_blocked = dmask_pair[:, CHUNK_ROW_TIME][:, :, key_order]
    as_f32 = lambda a: jnp.asarray(np.asarray(a, np.float32))
    return tuple(as_f32(t) for t in (k_dec, dmask_blocked, k_dec[CHUNK_ROW_TIME], q_dec[CHUNK_ROW_TIME],
                                     even, ~even, g_state, bd_mask))


def _rotary_tables(n_pos):
    half = RET_QK_DIM // 2
    inv = (np.float32(ROPE_BASE) ** (-np.arange(half, dtype=np.float32) / half)).astype(np.float32)
    ang = np.arange(n_pos).astype(np.float32)[:, None] * inv[None, :]
    cos, sin = np.cos(ang), np.sin(ang)
    cos_t = np.concatenate([cos, cos, cos, cos], axis=-1).astype(np.float32)
    sin_t = np.concatenate([-sin, sin, -sin, sin], axis=-1).astype(np.float32)
    return cos_t, sin_t


def _rmsnorm_rows(x, gain_row):
    ms = jnp.mean(x * x, axis=-1, keepdims=True)
    return x * lax.rsqrt(ms + EPS) * gain_row


def _sigmoid(x):
    return 0.5 * jnp.tanh(0.5 * x) + 0.5


def _silu(x):
    hx = 0.5 * x
    return hx * jnp.tanh(hx) + hx


def _conv(n_rows, lx_scr, convw_ref, convb_ref, xc_scr, xcb_scr):
    base = SUBLANES
    xc = convb_ref[...] + convw_ref[3:4, :] * lx_scr[pl.ds(base, n_rows), :]
    xc = xc + convw_ref[2:3, :] * lx_scr[pl.ds(base - 1, n_rows), :]
    xc = xc + convw_ref[1:2, :] * lx_scr[pl.ds(base - 2, n_rows), :]
    xc = xc + convw_ref[0:1, :] * lx_scr[pl.ds(base - 3, n_rows), :]
    xc_scr[...] = xc
    xcb_scr[...] = xc.astype(bf16)


CONV_HIST = (CONV_WIDTH - 1) * SUBLANES


def _conv_blocked(lx_scr, convw_ref, convb_ref, xc_scr, xcb_scr):
    last_sublane = lax.broadcasted_iota(jnp.int32, (SUBLANES, LRU_WIDTH), 0) == SUBLANES - 1
    for c in range(TILE_T // CHUNK):
        base = CONV_HIST + c * CHUNK
        rows = lambda group, n: lx_scr[base + group * SUBLANES:base + (group + n) * SUBLANES, :]
        wrapped = []
        for k in range(1, CONV_WIDTH):
            own, prev = rows(BLOCK_STEPS - k, 1), rows(-k, 1)
            wrapped.append(pltpu.roll(jnp.where(last_sublane, prev, own), 1, 0))
        xc = convb_ref[...] + convw_ref[3:4, :] * rows(0, BLOCK_STEPS)
        for k in range(1, CONV_WIDTH):
            shifted = jnp.concatenate(wrapped[:k][::-1] + [rows(0, BLOCK_STEPS - k)], axis=0)
            xc = xc + convw_ref[CONV_WIDTH - 1 - k:CONV_WIDTH - k, :] * shifted
        xc_scr[c * CHUNK:(c + 1) * CHUNK, :] = xc
        xcb_scr[c * CHUNK:(c + 1) * CHUNK, :] = xc.astype(bf16)


def _gates_head(h, xcb_scr, wrg_ref, wig_ref, pre_scr):
    lanes = slice(h * LRU_BLOCK, (h + 1) * LRU_BLOCK)
    wg = jnp.concatenate([wrg_ref[h], wig_ref[h]], axis=-1).astype(bf16)
    pre = jnp.dot(xcb_scr[:, lanes], wg, preferred_element_type=f32)
    pre_scr[:, lanes] = pre[:, :LRU_BLOCK]
    pre_scr[:, LRU_WIDTH + h * LRU_BLOCK:LRU_WIDTH + (h + 1) * LRU_BLOCK] = pre[:, LRU_BLOCK:]


def _lru_maps(xc, pre_r, pre_i, brg, big, c_sp):
    r = _sigmoid(pre_r + brg)
    i = _sigmoid(pre_i + big)
    nl = r * c_sp
    a = jnp.exp(-nl)
    z = jnp.tanh(nl) * (1.0 + a * a)
    beta = jnp.where(z > 0.0, z * lax.rsqrt(z), 0.0)
    return a, beta * i * xc


def _scan_sublanes(a, b):
    rowid = lax.broadcasted_iota(jnp.int32, a.shape, 0)
    for s in (1, 2, 4):
        keep = rowid >= s
        a_s = jnp.where(keep, pltpu.roll(a, s, 0), 1.0)
        b_s = jnp.where(keep, pltpu.roll(b, s, 0), 0.0)
        b = a * b_s + b
        a = a * a_s
    return a, b


def _lru_chunk_blocked(xc, pre_r, pre_i, brg, big, c_sp, carry):
    a, b = _lru_maps(xc, pre_r, pre_i, brg, big, c_sp)
    width = xc.shape[1]
    a3 = a.reshape(BLOCK_STEPS, SUBLANES, width)
    b3 = b.reshape(BLOCK_STEPS, SUBLANES, width)
    decay, local = [a3[0]], [b3[0]]
    for j in range(1, BLOCK_STEPS):
        local.append(a3[j] * local[-1] + b3[j])
        decay.append(a3[j] * decay[-1])
    a_blocks, b_blocks = _scan_sublanes(decay[-1], local[-1])
    after = a_blocks * carry + b_blocks
    rowid = lax.broadcasted_iota(jnp.int32, carry.shape, 0)
    before = jnp.where(rowid >= 1, pltpu.roll(after, 1, 0), carry)
    hs = [local[j] + decay[j] * before for j in range(BLOCK_STEPS)]
    new_carry = jnp.broadcast_to(after[SUBLANES - 1:SUBLANES, :], carry.shape)
    return jnp.concatenate(hs, axis=0), new_carry


def _lru_block(xc, pre_r, pre_i, brg, big, c_sp, carry):
    rows, width = xc.shape
    a, b = _lru_maps(xc, pre_r, pre_i, brg, big, c_sp)
    nv = rows // SUBLANES
    a3 = a.reshape(nv, SUBLANES, width)
    b3 = b.reshape(nv, SUBLANES, width)
    rowid = lax.broadcasted_iota(jnp.int32, (nv, SUBLANES, width), 1)
    for s in (1, 2, 4):
        keep = rowid >= s
        a_s = jnp.where(keep, pltpu.roll(a3, s, 1), 1.0)
        b_s = jnp.where(keep, pltpu.roll(b3, s, 1), 0.0)
        b3 = a3 * b_s + b3
        a3 = a3 * a_s
    hs = []
    for v in range(nv):
        h_v = a3[v] * carry + b3[v]
        carry = h_v[SUBLANES - 1:SUBLANES, :]
        hs.append(h_v)
    return jnp.concatenate(hs, axis=0), carry


def _rotary(t, cos, sin):
    half = RET_QK_DIM // 2
    in_first_half = (lax.broadcasted_iota(jnp.int32, t.shape, 1) // half) % 2 == 0
    partner = jnp.where(in_first_half, pltpu.roll(t, LANES - half, 1), pltpu.roll(t, half, 1))
    return t * cos + partner * sin


def _dot_t0(a, b):
    return lax.dot_general(a, b, (((0,), (0,)), ((), ())), preferred_element_type=f32)


def _dot_nt(a, b):
    return lax.dot_general(a, b, (((1,), (1,)), ((), ())), preferred_element_type=f32)


def _meta_states(meta_ref, cos_ref, sin_ref, g1_ref, wlx_ref, wk_ref, wv_ref, convw_ref, convb_ref,
                 wrg_ref, wig_ref, brg_ref, big_ref, lam_ref, kdec_ref, bdm_ref,
                 lx_out, h_out, st_out,
                 lx_scr, xc_scr, xcb_scr, pre_scr):
    u = _rmsnorm_rows(meta_ref[...], g1_ref[...]).astype(bf16)
    lx_scr[0:SUBLANES, :] = jnp.zeros((SUBLANES, LRU_WIDTH), f32)
    lx_scr[SUBLANES:SUBLANES + N_META, :] = jnp.dot(u, wlx_ref[...].astype(bf16), preferred_element_type=f32)
    k = jnp.dot(u, wk_ref[...].astype(bf16), preferred_element_type=f32)
    v = jnp.dot(u, wv_ref[...].astype(bf16), preferred_element_type=f32).astype(bf16)

    _conv(N_META, lx_scr, convw_ref, convb_ref, xc_scr, xcb_scr)
    for h in range(LRU_HEADS):
        _gates_head(h, xcb_scr, wrg_ref, wig_ref, pre_scr)
    for back in range(1, CONV_WIDTH):
        row = lx_scr[SUBLANES + N_META - back:SUBLANES + N_META - back + 1, :]
        group = CONV_WIDTH - 1 - back
        lx_out[group * SUBLANES:(group + 1) * SUBLANES, :] = jnp.broadcast_to(row, (SUBLANES, LRU_WIDTH))
    c_sp = LRU_C * jax.nn.softplus(-lam_ref[...])
    _, carry = _lru_block(xc_scr[...], pre_scr[:, :LRU_WIDTH], pre_scr[:, LRU_WIDTH:],
                          brg_ref[...], big_ref[...], c_sp, jnp.zeros((1, LRU_WIDTH), f32))
    h_out[...] = jnp.broadcast_to(carry, (SUBLANES, LRU_WIDTH))

    cos, sin = cos_ref[...], sin_ref[...]
    for p in range(N_PAIRS):
        sl = slice(p * LANES, (p + 1) * LANES)
        k_rot = _rotary(k[:, sl], cos, sin)
        kd = (k_rot * kdec_ref[CHUNK - N_META:CHUNK, sl]).astype(bf16)
        st_out[p] = _dot_t0(kd, v[:, p * PAIR_V:(p + 1) * PAIR_V]) * bdm_ref[...]


def _norm_tile(x_tile, g1_ref, u_scr):
    u_scr[...] = _rmsnorm_rows(x_tile, g1_ref[...]).astype(bf16)


MXU_COLS = 256
N_IN_TILES = IN_WIDTH // MXU_COLS


def _project_col_tile(t, u_scr, win_ref, slot):
    lx_scr, gate_scr, q_scr, k_scr, v_scr, rg_scr = slot
    off = t * MXU_COLS
    res = jnp.dot(u_scr[...], win_ref[:, off:off + MXU_COLS].astype(bf16), preferred_element_type=f32)
    for dst, start, rows, dt in ((lx_scr, OFF_LX, slice(CONV_HIST, CONV_HIST + TILE_T), f32), (gate_scr, OFF_GATE, slice(None), f32),
                                 (q_scr, OFF_Q, slice(None), f32), (k_scr, OFF_K, slice(None), f32),
                                 (v_scr, OFF_V, slice(None), bf16), (rg_scr, OFF_RG, slice(None), f32)):
        width = dst.shape[1]
        if start <= off < start + width:
            dst[rows, off - start:off - start + MXU_COLS] = res.astype(dt)
            return
    raise AssertionError("column tile outside the projection")


def _finish_tile(load_x, cos_ref, sin_ref, convw_ref, convb_ref, wrg_ref, wig_ref, brg_ref, big_ref, lam_ref, rng_ref,
                 wout_ref, fng_ref, dmaskp_ref, kdec_ref, qdec_ref, even_ref, odd_ref, gst_ref, bdm_ref,
                 slot, xc_scr, xcb_scr, pre_scr, y_scr, lxtail_scr, h_scr, st_scr, after_gates):
    lx_scr, gate_scr, q_scr, k_scr, v_scr, rg_scr = slot
    tt = TILE_T

    def retention_unit(c, p):
        rows = slice(c * CHUNK, (c + 1) * CHUNK)
        cos, sin = cos_ref[rows, :], sin_ref[rows, :]
        sl = slice(p * LANES, (p + 1) * LANES)
        q_rot = _rotary(q_scr[rows, sl], cos, sin)
        k_rot = _rotary(k_scr[rows, sl], cos, sin)
        q_b = (q_rot * QK_SCALE).astype(bf16)
        q_d = (q_rot * qdec_ref[:, sl]).astype(bf16)
        k_d = (k_rot * kdec_ref[:, sl]).astype(bf16)
        kk = jnp.concatenate([(k_rot * even_ref[:, sl]).astype(bf16), (k_rot * odd_ref[:, sl]).astype(bf16)], axis=0)
        v_pair = v_scr[rows, p * PAIR_V:(p + 1) * PAIR_V]
        zeros = jnp.zeros((CHUNK, RET_V_DIM), bf16)
        v_bd = jnp.concatenate([jnp.concatenate([v_pair[:, :RET_V_DIM], zeros], axis=1),
                                jnp.concatenate([zeros, v_pair[:, RET_V_DIM:]], axis=1)], axis=0)
        st_pair = st_scr[p]
        s = _dot_nt(q_b, kk) * dmaskp_ref[p]
        o = jnp.dot(s.astype(bf16), v_bd, preferred_element_type=f32)
        o = o + jnp.dot(q_d, st_pair.astype(bf16), preferred_element_type=f32)
        for e in range(2):
            h = 2 * p + e
            hs = slice(h * RET_V_DIM, (h + 1) * RET_V_DIM)
            o_h = o[:, e * RET_V_DIM:(e + 1) * RET_V_DIM]
            mu = jnp.mean(o_h, axis=-1, keepdims=True)
            oc = o_h - mu
            var = jnp.mean(oc * oc, axis=-1, keepdims=True)
            on = oc * lax.rsqrt(var + EPS) * rng_ref[:, hs]
            y_scr[rows, LRU_WIDTH + h * RET_V_DIM:LRU_WIDTH + (h + 1) * RET_V_DIM] = (
                on * _silu(rg_scr[rows, hs])).astype(bf16)
        kv = _dot_t0(k_d, v_pair)
        st_scr[p] = gst_ref[p] * st_pair + bdm_ref[...] * kv

    lx_scr[0:CONV_HIST, :] = lxtail_scr[...]
    _conv_blocked(lx_scr, convw_ref, convb_ref, xc_scr, xcb_scr)
    lxtail_scr[...] = lx_scr[pl.ds(tt, CONV_HIST), :]
    for h in range(LRU_HEADS):
        lanes = slice(h * LRU_BLOCK, (h + 1) * LRU_BLOCK)
        ilanes = slice(LRU_WIDTH + h * LRU_BLOCK, LRU_WIDTH + (h + 1) * LRU_BLOCK)
        _gates_head(h, xcb_scr, wrg_ref, wig_ref, pre_scr)
        after_gates(h)
        full = (CHUNK, LRU_BLOCK)
        c_sp = jnp.broadcast_to(LRU_C * jax.nn.softplus(-lam_ref[:, lanes]), full)
        brg, big = jnp.broadcast_to(brg_ref[:, lanes], full), jnp.broadcast_to(big_ref[:, lanes], full)
        carry = h_scr[:, lanes]
        for c in range(tt // CHUNK):
            rows = slice(c * CHUNK, (c + 1) * CHUNK)
            hb, carry = _lru_chunk_blocked(xc_scr[rows, lanes], pre_scr[rows, lanes], pre_scr[rows, ilanes],
                                           brg, big, c_sp, carry)
            y_scr[rows, lanes] = (hb * _silu(gate_scr[rows, lanes])).astype(bf16)
        h_scr[:, lanes] = carry

    for c in range(tt // CHUNK):
        for p in range(N_PAIRS):
            retention_unit(c, p)

    res = load_x() + jnp.dot(y_scr[...], wout_ref[...].astype(bf16), preferred_element_type=f32)
    return _rmsnorm_rows(res, fng_ref[...])


X_SLOTS = 3
OUT_SLOTS = 2
ROW_GROUPS = TILE_T // SUBLANES


def _tile_block_copies(hbm_ref, vmem_ref, sem_ref, tile, slot, n_tiles, to_hbm):
    b = lax.div(tile, n_tiles)
    first_row = (tile - b * n_tiles) * TILE_T
    copies = []
    for c in range(TILE_T // CHUNK):
        for s in range(SUBLANES):
            hbm = hbm_ref.at[b, pl.ds(first_row + c * CHUNK + s * BLOCK_STEPS, BLOCK_STEPS), :]
            vmem = vmem_ref.at[slot, pl.ds(c * BLOCK_STEPS, BLOCK_STEPS), s, :]
            src, dst = (vmem, hbm) if to_hbm else (hbm, vmem)
            copies.append(pltpu.make_async_copy(src, dst, sem_ref.at[slot]))
    return copies


def _main_kernel(n_tiles, n_total,
                 x_hbm, cos_ref, sin_ref, g1_ref, win_ref, convw_ref, convb_ref, wrg_ref, wig_ref,
                 brg_ref, big_ref, lam_ref, rng_ref, wout_ref, fng_ref, dmaskp_ref, kdec_ref, qdec_ref, even_ref, odd_ref,
                 gst_ref, bdm_ref, meta_ref, cos_meta_ref, sin_meta_ref, kdec_time_ref,
                 o_hbm,
                 u_scr,
                 lx_a, gate_a, q_a, k_a, v_a, rg_a,
                 lx_b, gate_b, q_b, k_b, v_b, rg_b,
                 xc_scr, xcb_scr, pre_scr, y_scr, lxtail_scr, h_scr, st_scr,
                 lx0_ref, h0_ref, st0_ref,
                 xin_scr, out_scr, xin_sem, out_sem):
    g = pl.program_id(0)
    slots = ((lx_a, gate_a, q_a, k_a, v_a, rg_a), (lx_b, gate_b, q_b, k_b, v_b, rg_b))
    x_slot = lambda tile: lax.rem(tile + X_SLOTS, X_SLOTS)
    out_slot = lambda tile: lax.rem(tile + OUT_SLOTS, OUT_SLOTS)
    x_tile = lambda tile: xin_scr[x_slot(tile)].reshape(TILE_T, D_MODEL)
    fetch = lambda tile: _tile_block_copies(x_hbm, xin_scr, xin_sem, jnp.minimum(tile, n_total - 1), x_slot(tile),
                                            n_tiles, to_hbm=False)
    write_back = lambda tile: _tile_block_copies(o_hbm, out_scr, out_sem, tile, out_slot(tile), n_tiles, to_hbm=True)

    @pl.when(g >= 1 + OUT_SLOTS)
    def _():
        for copy in write_back(g - 1 - OUT_SLOTS):
            copy.wait()

    @pl.when(lax.rem(g + n_tiles - 1, n_tiles) == 0)
    def _():
        lxtail_scr[...] = lx0_ref[...]
        h_scr[...] = h0_ref[...]
        st_scr[...] = st0_ref[...]

    tiles_after_head = (4, 4, 3, 3, 2, 2, 1, 1)
    assert sum(tiles_after_head) == N_IN_TILES and len(tiles_after_head) == LRU_HEADS

    def finish(slot_finish, after_gates):
        out = _finish_tile(lambda: x_tile(g - 1), cos_ref, sin_ref, convw_ref, convb_ref, wrg_ref, wig_ref, brg_ref, big_ref,
                           lam_ref, rng_ref, wout_ref, fng_ref, dmaskp_ref, kdec_ref, qdec_ref, even_ref, odd_ref, gst_ref,
                           bdm_ref, slot_finish, xc_scr, xcb_scr, pre_scr, y_scr, lxtail_scr, h_scr, st_scr, after_gates)
        out_scr[out_slot(g - 1)] = out.reshape(ROW_GROUPS, SUBLANES, D_MODEL)
        for copy in write_back(g - 1):
            copy.start()

    def step(slot_project, slot_finish):
        for copy in fetch(g):
            copy.wait()
        for copy in fetch(g + 1):
            copy.start()
        _norm_tile(x_tile(g), g1_ref, u_scr)

        def after_gates(h):
            first = sum(tiles_after_head[:h])
            for t in range(first, first + tiles_after_head[h]):
                _project_col_tile(t, u_scr, win_ref, slot_project)

        finish(slot_finish, after_gates)

    @pl.when(g == 0)
    def _():
        for copy in fetch(g):
            copy.start()
        for copy in fetch(g):
            copy.wait()
        for copy in fetch(g + 1):
            copy.start()
        _norm_tile(x_tile(g), g1_ref, u_scr)
        for t in range(N_IN_TILES):
            _project_col_tile(t, u_scr, win_ref, slots[0])
        meta_rows = lambda ref, n: ref.at[pl.ds(0, n)]
        _meta_states(meta_ref, cos_meta_ref, sin_meta_ref, g1_ref,
                     win_ref.at[:, pl.ds(OFF_LX, LRU_WIDTH)], win_ref.at[:, pl.ds(OFF_K, RET_QK_WIDTH)],
                     win_ref.at[:, pl.ds(OFF_V, RET_WIDTH)], convw_ref, convb_ref, wrg_ref, wig_ref, brg_ref, big_ref,
                     lam_ref, kdec_time_ref, bdm_ref, lx0_ref, h0_ref, st0_ref,
                     meta_rows(slots[1][0], SUBLANES + N_META), meta_rows(xc_scr, N_META), meta_rows(xcb_scr, N_META),
                     meta_rows(pre_scr, N_META))

    regular = jnp.logical_and(g >= 1, g < n_total)

    @pl.when(jnp.logical_and(regular, lax.rem(g, 2) == 0))
    def _():
        step(slots[0], slots[1])

    @pl.when(jnp.logical_and(regular, lax.rem(g, 2) == 1))
    def _():
        step(slots[1], slots[0])

    @pl.when(g == n_total)
    def _():
        for copy in fetch(g):
            copy.wait()
        finish(slots[(n_total - 1) % 2], lambda h: None)
        for tile in (g - 2, g - 1):
            for copy in write_back(tile):
                copy.wait()


def _const_spec(shape):
    nd = len(shape)
    return pl.BlockSpec(shape, lambda g, _nd=nd: (0,) * _nd, pipeline_mode=pl.Buffered(1))


def kernel(x, meta_tokens, norm_gain, w_in, conv_w, conv_b, w_rg, b_rg, w_ig, b_ig,
           lru_lambda, ret_norm_gain, w_out, final_norm_gain):
    B, S, D = x.shape
    assert D == D_MODEL and S % TILE_T == 0 and TILE_T % CHUNK == 0
    assert norm_gain.shape[0] == 1, "single-layer block"
    assert meta_tokens.shape == (N_META, D_MODEL)

    assert w_in.shape == (1, D_MODEL, IN_WIDTH) and w_out.shape == (1, MIX_WIDTH, D_MODEL)
    assert w_rg.shape == w_ig.shape == (1, LRU_HEADS, LRU_BLOCK, LRU_BLOCK)

    g1 = norm_gain[0].reshape(1, D_MODEL)
    fng = final_norm_gain.reshape(1, D_MODEL)
    convw = conv_w[0]
    convb = conv_b[0].reshape(1, LRU_WIDTH)
    brg = b_rg[0].reshape(1, LRU_WIDTH)
    big = b_ig[0].reshape(1, LRU_WIDTH)
    lam = lru_lambda[0].reshape(1, LRU_WIDTH)
    rng = ret_norm_gain[0].reshape(1, RET_WIDTH)

    k_dec, dmask_blocked, k_dec_blocked, q_dec_blocked, even, odd, g_state, bd_mask = _retention_tables()
    cos_t, sin_t = _rotary_tables(N_META + S)
    cos_m, sin_m = jnp.asarray(cos_t[:N_META]), jnp.asarray(sin_t[:N_META])
    cos_blocked = jnp.asarray(_chunk_rows_blocked(cos_t[N_META:]))
    sin_blocked = jnp.asarray(_chunk_rows_blocked(sin_t[N_META:]))

    tt = TILE_T
    n_tiles = S // tt
    n_total = B * n_tiles

    rot_spec = pl.BlockSpec((tt, LANES), lambda g: (jnp.maximum(g - 1, 0) % n_tiles, 0))
    in_specs = [
        pl.BlockSpec(memory_space=pl.ANY),
        rot_spec, rot_spec,
        _const_spec((1, D_MODEL)),
        _const_spec((None, D_MODEL, IN_WIDTH)),
        _const_spec((CONV_WIDTH, LRU_WIDTH)), _const_spec((1, LRU_WIDTH)),
        _const_spec((None, LRU_HEADS, LRU_BLOCK, LRU_BLOCK)),
        _const_spec((None, LRU_HEADS, LRU_BLOCK, LRU_BLOCK)),
        _const_spec((1, LRU_WIDTH)), _const_spec((1, LRU_WIDTH)), _const_spec((1, LRU_WIDTH)),
        _const_spec((1, RET_WIDTH)),
        _const_spec((None, MIX_WIDTH, D_MODEL)),
        _const_spec((1, D_MODEL)),
        _const_spec((N_PAIRS, CHUNK, 2 * CHUNK)),
        _const_spec((CHUNK, RET_QK_WIDTH)), _const_spec((CHUNK, RET_QK_WIDTH)),
        _const_spec((1, RET_QK_WIDTH)), _const_spec((1, RET_QK_WIDTH)),
        _const_spec((N_PAIRS, LANES, PAIR_V)),
        _const_spec((LANES, PAIR_V)),
        _const_spec((N_META, D_MODEL)),
        _const_spec((N_META, LANES)), _const_spec((N_META, LANES)),
        _const_spec((CHUNK, RET_QK_WIDTH)),
    ]
    slot_scratch = [
        pltpu.VMEM((CONV_HIST + tt, LRU_WIDTH), f32),
        pltpu.VMEM((tt, LRU_WIDTH), f32),
        pltpu.VMEM((tt, RET_QK_WIDTH), f32),
        pltpu.VMEM((tt, RET_QK_WIDTH), f32),
        pltpu.VMEM((tt, RET_WIDTH), bf16),
        pltpu.VMEM((tt, RET_WIDTH), f32),
    ]
    scratch = [pltpu.VMEM((tt, D_MODEL), bf16)] + slot_scratch + slot_scratch + [
        pltpu.VMEM((tt, LRU_WIDTH), f32),
        pltpu.VMEM((tt, LRU_WIDTH), bf16),
        pltpu.VMEM((tt, 2 * LRU_WIDTH), f32),
        pltpu.VMEM((tt, MIX_WIDTH), bf16),
        pltpu.VMEM((CONV_HIST, LRU_WIDTH), f32),
        pltpu.VMEM((SUBLANES, LRU_WIDTH), f32),
        pltpu.VMEM((N_PAIRS, LANES, PAIR_V), f32),
        pltpu.VMEM((CONV_HIST, LRU_WIDTH), f32),
        pltpu.VMEM((SUBLANES, LRU_WIDTH), f32),
        pltpu.VMEM((N_PAIRS, LANES, PAIR_V), f32),
        pltpu.VMEM((X_SLOTS, ROW_GROUPS, SUBLANES, D_MODEL), f32),
        pltpu.VMEM((OUT_SLOTS, ROW_GROUPS, SUBLANES, D_MODEL), f32),
        pltpu.SemaphoreType.DMA((X_SLOTS,)),
        pltpu.SemaphoreType.DMA((OUT_SLOTS,)),
    ]
    out = pl.pallas_call(
        functools.partial(_main_kernel, n_tiles, n_total),
        grid=(n_total + 1,),
        in_specs=in_specs,
        out_specs=pl.BlockSpec(memory_space=pl.ANY),
        out_shape=jax.ShapeDtypeStruct((B, S, D_MODEL), x.dtype),
        scratch_shapes=scratch,
        compiler_params=pltpu.CompilerParams(
            dimension_semantics=("arbitrary",),
            vmem_limit_bytes=VMEM_LIMIT_BYTES),
        name="hybrid_main",
    )(x, cos_blocked, sin_blocked, g1, w_in, convw, convb, w_rg, w_ig, brg, big, lam, rng, w_out, fng,
      dmask_blocked, k_dec_blocked, q_dec_blocked, even, odd, g_state, bd_mask, meta_tokens, cos_m, sin_m, k_dec)
    return out
```

```python
import functools

import numpy as np
import jax
import jax.numpy as jnp
from jax import lax
from jax.experimental import pallas as pl
from jax.experimental.pallas import tpu as pltpu

f32 = jnp.float32
bf16 = jnp.bfloat16

D_MODEL = 1024
N_META = 16
LRU_WIDTH = 1024
LRU_HEADS = 8
LRU_BLOCK = 128
CONV_WIDTH = 4
LRU_C = 8.0
RET_HEADS = 8
RET_QK_DIM = 64
RET_V_DIM = 128
RET_QK_WIDTH = 512
RET_WIDTH = 1024
CHUNK = 128
ROPE_BASE = 10000.0
MIX_WIDTH = 2048
EPS = 1e-6
QK_SCALE = RET_QK_DIM ** -0.5

OFF_LX, OFF_GATE, OFF_Q, OFF_K, OFF_V, OFF_RG = 0, 1024, 2048, 2560, 3072, 4096
IN_WIDTH = 5120

LANES = 128
SUBLANES = 8
N_PAIRS = RET_HEADS // 2
PAIR_V = 2 * RET_V_DIM
TILE_T = 256
VMEM_LIMIT_BYTES = 60000 * 1024


def _lane_head():
    return np.arange(RET_QK_WIDTH) // RET_QK_DIM


BLOCK_STEPS = CHUNK // SUBLANES
CHUNK_ROW_TIME = np.arange(CHUNK).reshape(SUBLANES, BLOCK_STEPS).T.reshape(-1)


def _chunk_rows_blocked(table):
    n = table.shape[0]
    return table.reshape((n // CHUNK, CHUNK) + table.shape[1:])[:, CHUNK_ROW_TIME].reshape(table.shape)


def _retention_tables():
    log_g = np.log1p(-np.exp2(-5.0 - np.arange(RET_HEADS, dtype=np.float32))).astype(np.float32)
    idx = np.arange(CHUNK, dtype=np.float32)
    diff = idx[:, None] - idx[None, :]
    dmask = np.where(diff[None] >= 0.0, np.exp(np.maximum(diff, 0.0)[None] * log_g[:, None, None]), 0.0)
    dmask_pair = np.concatenate([dmask[0::2], dmask[1::2]], axis=-1)
    lg_lane = log_g[_lane_head()]
    k_dec = np.exp((CHUNK - 1.0 - idx)[:, None] * lg_lane[None, :])
    q_dec = np.exp((idx + 1.0)[:, None] * lg_lane[None, :]) * QK_SCALE
    even = (_lane_head() % 2 == 0)[None, :]
    g_chunk = np.exp(CHUNK * log_g)
    g_state = np.broadcast_to(np.repeat(g_chunk, RET_V_DIM).reshape(N_PAIRS, 1, PAIR_V), (N_PAIRS, LANES, PAIR_V))
    row_par = np.arange(LANES) // RET_QK_DIM
    col_par = np.arange(PAIR_V) // RET_V_DIM
    bd_mask = row_par[:, None] == col_par[None, :]
    key_order = np.concatenate([CHUNK_ROW_TIME, CHUNK + CHUNK_ROW_TIME])
    dmask_blocked = dmask_pair[:, CHUNK_ROW_TIME][:, :, key_order]
    as_f32 = lambda a: jnp.asarray(np.asarray(a, np.float32))
    return tuple(as_f32(t) for t in (k_dec, dmask_blocked, k_dec[CHUNK_ROW_TIME], q_dec[CHUNK_ROW_TIME],
                                     even, ~even, g_state, bd_mask))


def _rotary_tables(n_pos):
    half = RET_QK_DIM // 2
    inv = (np.float32(ROPE_BASE) ** (-np.arange(half, dtype=np.float32) / half)).astype(np.float32)
    ang = np.arange(n_pos).astype(np.float32)[:, None] * inv[None, :]
    cos, sin = np.cos(ang), np.sin(ang)
    cos_t = np.concatenate([cos, cos, cos, cos], axis=-1).astype(np.float32)
    sin_t = np.concatenate([-sin, sin, -sin, sin], axis=-1).astype(np.float32)
    return cos_t, sin_t


def _rmsnorm_rows(x, gain_row):
    ms = jnp.mean(x * x, axis=-1, keepdims=True)
    return x * lax.rsqrt(ms + EPS) * gain_row


def _silu(x):
    hx = 0.5 * x
    return hx * jnp.tanh(hx) + hx


def _conv(n_rows, lx_scr, convw_ref, convb_ref, xc_scr, xcb_scr):
    base = SUBLANES
    xc = convb_ref[...] + convw_ref[3:4, :] * lx_scr[pl.ds(base, n_rows), :]
    xc = xc + convw_ref[2:3, :] * lx_scr[pl.ds(base - 1, n_rows), :]
    xc = xc + convw_ref[1:2, :] * lx_scr[pl.ds(base - 2, n_rows), :]
    xc = xc + convw_ref[0:1, :] * lx_scr[pl.ds(base - 3, n_rows), :]
    xc_scr[...] = xc
    xcb_scr[...] = xc.astype(bf16)


CONV_HIST = (CONV_WIDTH - 1) * SUBLANES


def _conv_blocked(lx_scr, convw_ref, convb_ref, xc_scr, xcb_scr):
    last_sublane = lax.broadcasted_iota(jnp.int32, (SUBLANES, LRU_WIDTH), 0) == SUBLANES - 1
    for c in range(TILE_T // CHUNK):
        base = CONV_HIST + c * CHUNK
        rows = lambda group, n: lx_scr[base + group * SUBLANES:base + (group + n) * SUBLANES, :]
        wrapped = []
        for k in range(1, CONV_WIDTH):
            own, prev = rows(BLOCK_STEPS - k, 1), rows(-k, 1)
            wrapped.append(pltpu.roll(jnp.where(last_sublane, prev, own), 1, 0))
        xc = convb_ref[...] + convw_ref[3:4, :] * rows(0, BLOCK_STEPS)
        for k in range(1, CONV_WIDTH):
            shifted = jnp.concatenate(wrapped[:k][::-1] + [rows(0, BLOCK_STEPS - k)], axis=0)
            xc = xc + convw_ref[CONV_WIDTH - 1 - k:CONV_WIDTH - k, :] * shifted
        xc_scr[c * CHUNK:(c + 1) * CHUNK, :] = xc
        xcb_scr[c * CHUNK:(c + 1) * CHUNK, :] = xc.astype(bf16)


def _gates_head(h, xcb_scr, wrg_ref, wig_ref, pre_scr):
    lanes = slice(h * LRU_BLOCK, (h + 1) * LRU_BLOCK)
    wg = (0.5 * jnp.concatenate([wrg_ref[h], wig_ref[h]], axis=-1)).astype(bf16)
    pre = jnp.dot(xcb_scr[:, lanes], wg, preferred_element_type=f32)
    pre_scr[:, lanes] = pre[:, :LRU_BLOCK]
    pre_scr[:, LRU_WIDTH + h * LRU_BLOCK:LRU_WIDTH + (h + 1) * LRU_BLOCK] = pre[:, LRU_BLOCK:]


def _lru_maps(xc, half_pre_r, half_pre_i, half_brg, half_big, half_c_sp):
    nl = jnp.tanh(half_pre_r + half_brg) * half_c_sp + half_c_sp
    i = 0.5 * jnp.tanh(half_pre_i + half_big) + 0.5
    a = jnp.exp(-nl)
    z = jnp.tanh(nl) * (1.0 + a * a)
    beta = jnp.where(z > 0.0, z * lax.rsqrt(z), 0.0)
    return a, beta * i * xc


def _scan_sublanes(a, b):
    rowid = lax.broadcasted_iota(jnp.int32, a.shape, 0)
    for s in (1, 2, 4):
        keep = rowid >= s
        a_s = jnp.where(keep, pltpu.roll(a, s, 0), 1.0)
        b_s = jnp.where(keep, pltpu.roll(b, s, 0), 0.0)
        b = a * b_s + b
        a = a * a_s
    return a, b


def _lru_chunk_blocked(xc, pre_r, pre_i, brg, big, c_sp, carry):
    a, b = _lru_maps(xc, pre_r, pre_i, brg, big, c_sp)
    width = xc.shape[1]
    a3 = a.reshape(BLOCK_STEPS, SUBLANES, width)
    b3 = b.reshape(BLOCK_STEPS, SUBLANES, width)
    decay, local = [a3[0]], [b3[0]]
    for j in range(1, BLOCK_STEPS):
        local.append(a3[j] * local[-1] + b3[j])
        decay.append(a3[j] * decay[-1])
    a_blocks, b_blocks = _scan_sublanes(decay[-1], local[-1])
    after = a_blocks * carry + b_blocks
    rowid = lax.broadcasted_iota(jnp.int32, carry.shape, 0)
    before = jnp.where(rowid >= 1, pltpu.roll(after, 1, 0), carry)
    hs = [local[j] + decay[j] * before for j in range(BLOCK_STEPS)]
    new_carry = jnp.broadcast_to(after[SUBLANES - 1:SUBLANES, :], carry.shape)
    return jnp.concatenate(hs, axis=0), new_carry


def _lru_block(xc, pre_r, pre_i, brg, big, c_sp, carry):
    rows, width = xc.shape
    a, b = _lru_maps(xc, pre_r, pre_i, brg, big, c_sp)
    nv = rows // SUBLANES
    a3 = a.reshape(nv, SUBLANES, width)
    b3 = b.reshape(nv, SUBLANES, width)
    rowid = lax.broadcasted_iota(jnp.int32, (nv, SUBLANES, width), 1)
    for s in (1, 2, 4):
        keep = rowid >= s
        a_s = jnp.where(keep, pltpu.roll(a3, s, 1), 1.0)
        b_s = jnp.where(keep, pltpu.roll(b3, s, 1), 0.0)
        b3 = a3 * b_s + b3
        a3 = a3 * a_s
    hs = []
    for v in range(nv):
        h_v = a3[v] * carry + b3[v]
        carry = h_v[SUBLANES - 1:SUBLANES, :]
        hs.append(h_v)
    return jnp.concatenate(hs, axis=0), carry


def _rotary(t, cos, sin):
    half = RET_QK_DIM // 2
    in_first_half = (lax.broadcasted_iota(jnp.int32, t.shape, 1) // half) % 2 == 0
    partner = jnp.where(in_first_half, pltpu.roll(t, LANES - half, 1), pltpu.roll(t, half, 1))
    return t * cos + partner * sin


def _dot_t0(a, b):
    return lax.dot_general(a, b, (((0,), (0,)), ((), ())), preferred_element_type=f32)


def _dot_nt(a, b):
    return lax.dot_general(a, b, (((1,), (1,)), ((), ())), preferred_element_type=f32)


def _meta_states(meta_ref, cos_ref, sin_ref, g1_ref, wlx_ref, wk_ref, wv_ref, convw_ref, convb_ref,
                 wrg_ref, wig_ref, brg_ref, big_ref, lam_ref, kdec_ref, bdm_ref,
                 lx_out, h_out, st_out,
                 lx_scr, xc_scr, xcb_scr, pre_scr):
    u = _rmsnorm_rows(meta_ref[...], g1_ref[...]).astype(bf16)
    lx_scr[0:SUBLANES, :] = jnp.zeros((SUBLANES, LRU_WIDTH), f32)
    lx_scr[SUBLANES:SUBLANES + N_META, :] = jnp.dot(u, wlx_ref[...].astype(bf16), preferred_element_type=f32)
    k = jnp.dot(u, wk_ref[...].astype(bf16), preferred_element_type=f32)
    v = jnp.dot(u, wv_ref[...].astype(bf16), preferred_element_type=f32).astype(bf16)

    _conv(N_META, lx_scr, convw_ref, convb_ref, xc_scr, xcb_scr)
    for h in range(LRU_HEADS):
        _gates_head(h, xcb_scr, wrg_ref, wig_ref, pre_scr)
    for back in range(1, CONV_WIDTH):
        row = lx_scr[SUBLANES + N_META - back:SUBLANES + N_META - back + 1, :]
        group = CONV_WIDTH - 1 - back
        lx_out[group * SUBLANES:(group + 1) * SUBLANES, :] = jnp.broadcast_to(row, (SUBLANES, LRU_WIDTH))
    half_c_sp = 0.5 * LRU_C * jax.nn.softplus(-lam_ref[...])
    _, carry = _lru_block(xc_scr[...], pre_scr[:, :LRU_WIDTH], pre_scr[:, LRU_WIDTH:],
                          0.5 * brg_ref[...], 0.5 * big_ref[...], half_c_sp, jnp.zeros((1, LRU_WIDTH), f32))
    h_out[...] = jnp.broadcast_to(carry, (SUBLANES, LRU_WIDTH))

    cos, sin = cos_ref[...], sin_ref[...]
    for p in range(N_PAIRS):
        sl = slice(p * LANES, (p + 1) * LANES)
        k_rot = _rotary(k[:, sl], cos, sin)
        kd = (k_rot * kdec_ref[CHUNK - N_META:CHUNK, sl]).astype(bf16)
        st_out[p] = _dot_t0(kd, v[:, p * PAIR_V:(p + 1) * PAIR_V]) * bdm_ref[...]


def _norm_tile(x_tile, g1_ref, u_scr):
    u_scr[...] = _rmsnorm_rows(x_tile, g1_ref[...]).astype(bf16)


MXU_COLS = 256
N_IN_TILES = IN_WIDTH // MXU_COLS


def _project_col_tile(t, u_scr, win_ref, slot):
    lx_scr, gate_scr, q_scr, k_scr, v_scr, rg_scr = slot
    off = t * MXU_COLS
    res = jnp.dot(u_scr[...], win_ref[:, off:off + MXU_COLS].astype(bf16), preferred_element_type=f32)
    for dst, start, rows, dt in ((lx_scr, OFF_LX, slice(CONV_HIST, CONV_HIST + TILE_T), f32), (gate_scr, OFF_GATE, slice(None), f32),
                                 (q_scr, OFF_Q, slice(None), f32), (k_scr, OFF_K, slice(None), f32),
                                 (v_scr, OFF_V, slice(None), bf16), (rg_scr, OFF_RG, slice(None), f32)):
        width = dst.shape[1]
        if start <= off < start + width:
            dst[rows, off - start:off - start + MXU_COLS] = res.astype(dt)
            return
    raise AssertionError("column tile outside the projection")


def _finish_tile(load_x, cos_ref, sin_ref, convw_ref, convb_ref, wrg_ref, wig_ref, brg_ref, big_ref, lam_ref, rng_ref,
                 wout_ref, fng_ref, dmaskp_ref, kdec_ref, qdec_ref, even_ref, odd_ref, gst_ref, bdm_ref,
                 slot, xc_scr, xcb_scr, pre_scr, y_scr, lxtail_scr, h_scr, st_scr, after_gates):
    lx_scr, gate_scr, q_scr, k_scr, v_scr, rg_scr = slot
    tt = TILE_T

    def retention_unit(c, p):
        rows = slice(c * CHUNK, (c + 1) * CHUNK)
        cos, sin = cos_ref[rows, :], sin_ref[rows, :]
        sl = slice(p * LANES, (p + 1) * LANES)
        q_rot = _rotary(q_scr[rows, sl], cos, sin)
        k_rot = _rotary(k_scr[rows, sl], cos, sin)
        q_b = (q_rot * QK_SCALE).astype(bf16)
        q_d = (q_rot * qdec_ref[:, sl]).astype(bf16)
        k_d = (k_rot * kdec_ref[:, sl]).astype(bf16)
        kk = jnp.concatenate([(k_rot * even_ref[:, sl]).astype(bf16), (k_rot * odd_ref[:, sl]).astype(bf16)], axis=0)
        v_pair = v_scr[rows, p * PAIR_V:(p + 1) * PAIR_V]
        zeros = jnp.zeros((CHUNK, RET_V_DIM), bf16)
        v_bd = jnp.concatenate([jnp.concatenate([v_pair[:, :RET_V_DIM], zeros], axis=1),
                                jnp.concatenate([zeros, v_pair[:, RET_V_DIM:]], axis=1)], axis=0)
        st_pair = st_scr[p]
        s = _dot_nt(q_b, kk) * dmaskp_ref[p]
        o = jnp.dot(jnp.concatenate([s.astype(bf16), q_d], axis=1),
                    jnp.concatenate([v_bd, st_pair.astype(bf16)], axis=0), preferred_element_type=f32)
        for e in range(2):
            h = 2 * p + e
            hs = slice(h * RET_V_DIM, (h + 1) * RET_V_DIM)
            o_h = o[:, e * RET_V_DIM:(e + 1) * RET_V_DIM]
            mu = jnp.mean(o_h, axis=-1, keepdims=True)
            oc = o_h - mu
            var = jnp.mean(oc * oc, axis=-1, keepdims=True)
            on = oc * lax.rsqrt(var + EPS) * rng_ref[:, hs]
            y_scr[rows, LRU_WIDTH + h * RET_V_DIM:LRU_WIDTH + (h + 1) * RET_V_DIM] = (
                on * _silu(rg_scr[rows, hs])).astype(bf16)
        kv = _dot_t0(k_d, v_pair)
        st_scr[p] = gst_ref[p] * st_pair + bdm_ref[...] * kv

    lx_scr[0:CONV_HIST, :] = lxtail_scr[...]
    _conv_blocked(lx_scr, convw_ref, convb_ref, xc_scr, xcb_scr)
    lxtail_scr[...] = lx_scr[pl.ds(tt, CONV_HIST), :]
    for h in range(LRU_HEADS):
        lanes = slice(h * LRU_BLOCK, (h + 1) * LRU_BLOCK)
        ilanes = slice(LRU_WIDTH + h * LRU_BLOCK, LRU_WIDTH + (h + 1) * LRU_BLOCK)
        _gates_head(h, xcb_scr, wrg_ref, wig_ref, pre_scr)
        after_gates(h)
        full = (CHUNK, LRU_BLOCK)
        c_sp = jnp.broadcast_to(0.5 * LRU_C * jax.nn.softplus(-lam_ref[:, lanes]), full)
        brg, big = jnp.broadcast_to(0.5 * brg_ref[:, lanes], full), jnp.broadcast_to(0.5 * big_ref[:, lanes], full)
        carry = h_scr[:, lanes]
        for c in range(tt // CHUNK):
            rows = slice(c * CHUNK, (c + 1) * CHUNK)
            hb, carry = _lru_chunk_blocked(xc_scr[rows, lanes], pre_scr[rows, lanes], pre_scr[rows, ilanes],
                                           brg, big, c_sp, carry)
            y_scr[rows, lanes] = (hb * _silu(gate_scr[rows, lanes])).astype(bf16)
        h_scr[:, lanes] = carry

    for c in range(tt // CHUNK):
        for p in range(N_PAIRS):
            retention_unit(c, p)

    res = load_x() + jnp.dot(y_scr[...], wout_ref[...].astype(bf16), preferred_element_type=f32)
    return _rmsnorm_rows(res, fng_ref[...])


X_SLOTS = 3
OUT_SLOTS = 2
ROW_GROUPS = TILE_T // SUBLANES


def _tile_block_copies(hbm_ref, vmem_ref, sem_ref, tile, slot, n_tiles, to_hbm):
    b = lax.div(tile, n_tiles)
    first_row = (tile - b * n_tiles) * TILE_T
    copies = []
    for c in range(TILE_T // CHUNK):
        for s in range(SUBLANES):
            hbm = hbm_ref.at[b, pl.ds(first_row + c * CHUNK + s * BLOCK_STEPS, BLOCK_STEPS), :]
            vmem = vmem_ref.at[slot, pl.ds(c * BLOCK_STEPS, BLOCK_STEPS), s, :]
            src, dst = (vmem, hbm) if to_hbm else (hbm, vmem)
            copies.append(pltpu.make_async_copy(src, dst, sem_ref.at[slot]))
    return copies


def _main_kernel(n_tiles, n_total,
                 x_hbm, cos_ref, sin_ref, g1_ref, win_ref, convw_ref, convb_ref, wrg_ref, wig_ref,
                 brg_ref, big_ref, lam_ref, rng_ref, wout_ref, fng_ref, dmaskp_ref, kdec_ref, qdec_ref, even_ref, odd_ref,
                 gst_ref, bdm_ref, meta_ref, cos_meta_ref, sin_meta_ref, kdec_time_ref,
                 o_hbm,
                 u_scr,
                 lx_a, gate_a, q_a, k_a, v_a, rg_a,
                 lx_b, gate_b, q_b, k_b, v_b, rg_b,
                 xc_scr, xcb_scr, pre_scr, y_scr, lxtail_scr, h_scr, st_scr,
                 lx0_ref, h0_ref, st0_ref,
                 xin_scr, out_scr, xin_sem, out_sem):
    g = pl.program_id(0)
    slots = ((lx_a, gate_a, q_a, k_a, v_a, rg_a), (lx_b, gate_b, q_b, k_b, v_b, rg_b))
    x_slot = lambda tile: lax.rem(tile + X_SLOTS, X_SLOTS)
    out_slot = lambda tile: lax.rem(tile + OUT_SLOTS, OUT_SLOTS)
    x_tile = lambda tile: xin_scr[x_slot(tile)].reshape(TILE_T, D_MODEL)
    fetch = lambda tile: _tile_block_copies(x_hbm, xin_scr, xin_sem, jnp.minimum(tile, n_total - 1), x_slot(tile),
                                            n_tiles, to_hbm=False)
    write_back = lambda tile: _tile_block_copies(o_hbm, out_scr, out_sem, tile, out_slot(tile), n_tiles, to_hbm=True)

    @pl.when(g >= 1 + OUT_SLOTS)
    def _():
        for copy in write_back(g - 1 - OUT_SLOTS):
            copy.wait()

    @pl.when(lax.rem(g + n_tiles - 1, n_tiles) == 0)
    def _():
        lxtail_scr[...] = lx0_ref[...]
        h_scr[...] = h0_ref[...]
        st_scr[...] = st0_ref[...]

    tiles_after_head = (4, 4, 3, 3, 2, 2, 1, 1)
    assert sum(tiles_after_head) == N_IN_TILES and len(tiles_after_head) == LRU_HEADS

    def finish(slot_finish, after_gates):
        out = _finish_tile(lambda: x_tile(g - 1), cos_ref, sin_ref, convw_ref, convb_ref, wrg_ref, wig_ref, brg_ref, big_ref,
                           lam_ref, rng_ref, wout_ref, fng_ref, dmaskp_ref, kdec_ref, qdec_ref, even_ref, odd_ref, gst_ref,
                           bdm_ref, slot_finish, xc_scr, xcb_scr, pre_scr, y_scr, lxtail_scr, h_scr, st_scr, after_gates)
        out_scr[out_slot(g - 1)] = out.reshape(ROW_GROUPS, SUBLANES, D_MODEL)
        for copy in write_back(g - 1):
            copy.start()

    def step(slot_project, slot_finish):
        for copy in fetch(g):
            copy.wait()
        for copy in fetch(g + 1):
            copy.start()
        _norm_tile(x_tile(g), g1_ref, u_scr)

        def after_gates(h):
            first = sum(tiles_after_head[:h])
            for t in range(first, first + tiles_after_head[h]):
                _project_col_tile(t, u_scr, win_ref, slot_project)

        finish(slot_finish, after_gates)

    @pl.when(g == 0)
    def _():
        for copy in fetch(g):
            copy.start()
        for copy in fetch(g):
            copy.wait()
        for copy in fetch(g + 1):
            copy.start()
        _norm_tile(x_tile(g), g1_ref, u_scr)
        for t in range(N_IN_TILES):
            _project_col_tile(t, u_scr, win_ref, slots[0])
        meta_rows = lambda ref, n: ref.at[pl.ds(0, n)]
        _meta_states(meta_ref, cos_meta_ref, sin_meta_ref, g1_ref,
                     win_ref.at[:, pl.ds(OFF_LX, LRU_WIDTH)], win_ref.at[:, pl.ds(OFF_K, RET_QK_WIDTH)],
                     win_ref.at[:, pl.ds(OFF_V, RET_WIDTH)], convw_ref, convb_ref, wrg_ref, wig_ref, brg_ref, big_ref,
                     lam_ref, kdec_time_ref, bdm_ref, lx0_ref, h0_ref, st0_ref,
                     meta_rows(slots[1][0], SUBLANES + N_META), meta_rows(xc_scr, N_META), meta_rows(xcb_scr, N_META),
                     meta_rows(pre_scr, N_META))

    regular = jnp.logical_and(g >= 1, g < n_total)

    @pl.when(jnp.logical_and(regular, lax.rem(g, 2) == 0))
    def _():
        step(slots[0], slots[1])

    @pl.when(jnp.logical_and(regular, lax.rem(g, 2) == 1))
    def _():
        step(slots[1], slots[0])

    @pl.when(g == n_total)
    def _():
        for copy in fetch(g):
            copy.wait()
        finish(slots[(n_total - 1) % 2], lambda h: None)
        for tile in (g - 2, g - 1):
            for copy in write_back(tile):
                copy.wait()


def _const_spec(shape):
    nd = len(shape)
    return pl.BlockSpec(shape, lambda g, _nd=nd: (0,) * _nd, pipeline_mode=pl.Buffered(1))


def kernel(x, meta_tokens, norm_gain, w_in, conv_w, conv_b, w_rg, b_rg, w_ig, b_ig,
           lru_lambda, ret_norm_gain, w_out, final_norm_gain):
    B, S, D = x.shape
    assert D == D_MODEL and S % TILE_T == 0 and TILE_T % CHUNK == 0
    assert norm_gain.shape[0] == 1, "single-layer block"
    assert meta_tokens.shape == (N_META, D_MODEL)

    assert w_in.shape == (1, D_MODEL, IN_WIDTH) and w_out.shape == (1, MIX_WIDTH, D_MODEL)
    assert w_rg.shape == w_ig.shape == (1, LRU_HEADS, LRU_BLOCK, LRU_BLOCK)

    g1 = norm_gain[0].reshape(1, D_MODEL)
    fng = final_norm_gain.reshape(1, D_MODEL)
    convw = conv_w[0]
    convb = conv_b[0].reshape(1, LRU_WIDTH)
    brg = b_rg[0].reshape(1, LRU_WIDTH)
    big = b_ig[0].reshape(1, LRU_WIDTH)
    lam = lru_lambda[0].reshape(1, LRU_WIDTH)
    rng = ret_norm_gain[0].reshape(1, RET_WIDTH)

    k_dec, dmask_blocked, k_dec_blocked, q_dec_blocked, even, odd, g_state, bd_mask = _retention_tables()
    cos_t, sin_t = _rotary_tables(N_META + S)
    cos_m, sin_m = jnp.asarray(cos_t[:N_META]), jnp.asarray(sin_t[:N_META])
    cos_blocked = jnp.asarray(_chunk_rows_blocked(cos_t[N_META:]))
    sin_blocked = jnp.asarray(_chunk_rows_blocked(sin_t[N_META:]))

    tt = TILE_T
    n_tiles = S // tt
    n_total = B * n_tiles

    rot_spec = pl.BlockSpec((tt, LANES), lambda g: (jnp.maximum(g - 1, 0) % n_tiles, 0))
    in_specs = [
        pl.BlockSpec(memory_space=pl.ANY),
        rot_spec, rot_spec,
        _const_spec((1, D_MODEL)),
        _const_spec((None, D_MODEL, IN_WIDTH)),
        _const_spec((CONV_WIDTH, LRU_WIDTH)), _const_spec((1, LRU_WIDTH)),
        _const_spec((None, LRU_HEADS, LRU_BLOCK, LRU_BLOCK)),
        _const_spec((None, LRU_HEADS, LRU_BLOCK, LRU_BLOCK)),
        _const_spec((1, LRU_WIDTH)), _const_spec((1, LRU_WIDTH)), _const_spec((1, LRU_WIDTH)),
        _const_spec((1, RET_WIDTH)),
        _const_spec((None, MIX_WIDTH, D_MODEL)),
        _const_spec((1, D_MODEL)),
        _const_spec((N_PAIRS, CHUNK, 2 * CHUNK)),
        _const_spec((CHUNK, RET_QK_WIDTH)), _const_spec((CHUNK, RET_QK_WIDTH)),
        _const_spec((1, RET_QK_WIDTH)), _const_spec((1, RET_QK_WIDTH)),
        _const_spec((N_PAIRS, LANES, PAIR_V)),
        _const_spec((LANES, PAIR_V)),
        _const_spec((N_META, D_MODEL)),
        _const_spec((N_META, LANES)), _const_spec((N_META, LANES)),
        _const_spec((CHUNK, RET_QK_WIDTH)),
    ]
    slot_scratch = [
        pltpu.VMEM((CONV_HIST + tt, LRU_WIDTH), f32),
        pltpu.VMEM((tt, LRU_WIDTH), f32),
        pltpu.VMEM((tt, RET_QK_WIDTH), f32),
        pltpu.VMEM((tt, RET_QK_WIDTH), f32),
        pltpu.VMEM((tt, RET_WIDTH), bf16),
        pltpu.VMEM((tt, RET_WIDTH), f32),
    ]
    scratch = [pltpu.VMEM((tt, D_MODEL), bf16)] + slot_scratch + slot_scratch + [
        pltpu.VMEM((tt, LRU_WIDTH), f32),
        pltpu.VMEM((tt, LRU_WIDTH), bf16),
        pltpu.VMEM((tt, 2 * LRU_WIDTH), f32),
        pltpu.VMEM((tt, MIX_WIDTH), bf16),
        pltpu.VMEM((CONV_HIST, LRU_WIDTH), f32),
        pltpu.VMEM((SUBLANES, LRU_WIDTH), f32),
        pltpu.VMEM((N_PAIRS, LANES, PAIR_V), f32),
        pltpu.VMEM((CONV_HIST, LRU_WIDTH), f32),
        pltpu.VMEM((SUBLANES, LRU_WIDTH), f32),
        pltpu.VMEM((N_PAIRS, LANES, PAIR_V), f32),
        pltpu.VMEM((X_SLOTS, ROW_GROUPS, SUBLANES, D_MODEL), f32),
        pltpu.VMEM((OUT_SLOTS, ROW_GROUPS, SUBLANES, D_MODEL), f32),
        pltpu.SemaphoreType.DMA((X_SLOTS,)),
        pltpu.SemaphoreType.DMA((OUT_SLOTS,)),
    ]
    out = pl.pallas_call(
        functools.partial(_main_kernel, n_tiles, n_total),
        grid=(n_total + 1,),
        in_specs=in_specs,
        out_specs=pl.BlockSpec(memory_space=pl.ANY),
        out_shape=jax.ShapeDtypeStruct((B, S, D_MODEL), x.dtype),
        scratch_shapes=scratch,
        compiler_params=pltpu.CompilerParams(
            dimension_semantics=("arbitrary",),
            vmem_limit_bytes=VMEM_LIMIT_BYTES),
        name="hybrid_main",
    )(x, cos_blocked, sin_blocked, g1, w_in, convw, convb, w_rg, w_ig, brg, big, lam, rng, w_out, fng,
      dmask_blocked, k_dec_blocked, q_dec_blocked, even, odd, g_state, bd_mask, meta_tokens, cos_m, sin_m, k_dec)
    return out
```

```python
import functools

import numpy as np
import jax
import jax.numpy as jnp
from jax import lax
from jax.experimental import pallas as pl
from jax.experimental.pallas import tpu as pltpu

f32 = jnp.float32
bf16 = jnp.bfloat16

D_MODEL = 1024
N_META = 16
LRU_WIDTH = 1024
LRU_HEADS = 8
LRU_BLOCK = 128
CONV_WIDTH = 4
LRU_C = 8.0
RET_HEADS = 8
RET_QK_DIM = 64
RET_V_DIM = 128
RET_QK_WIDTH = 512
RET_WIDTH = 1024
CHUNK = 128
ROPE_BASE = 10000.0
MIX_WIDTH = 2048
EPS = 1e-6
QK_SCALE = RET_QK_DIM ** -0.5

OFF_LX, OFF_GATE, OFF_Q, OFF_K, OFF_V, OFF_RG = 0, 1024, 2048, 2560, 3072, 4096
IN_WIDTH = 5120

LANES = 128
SUBLANES = 8
N_PAIRS = RET_HEADS // 2
PAIR_V = 2 * RET_V_DIM
TILE_T = 256
VMEM_LIMIT_BYTES = 60000 * 1024


def _lane_head():
    return np.arange(RET_QK_WIDTH) // RET_QK_DIM


BLOCK_STEPS = CHUNK // SUBLANES
CHUNK_ROW_TIME = np.arange(CHUNK).reshape(SUBLANES, BLOCK_STEPS).T.reshape(-1)


def _chunk_rows_blocked(table):
    n = table.shape[0]
    return table.reshape((n // CHUNK, CHUNK) + table.shape[1:])[:, CHUNK_ROW_TIME].reshape(table.shape)


def _retention_tables():
    log_g = np.log1p(-np.exp2(-5.0 - np.arange(RET_HEADS, dtype=np.float32))).astype(np.float32)
    idx = np.arange(CHUNK, dtype=np.float32)
    diff = idx[:, None] - idx[None, :]
    dmask = np.where(diff[None] >= 0.0, np.exp(np.maximum(diff, 0.0)[None] * log_g[:, None, None]), 0.0)
    dmask_pair = np.concatenate([dmask[0::2], dmask[1::2]], axis=-1)
    lg_lane = log_g[_lane_head()]
    k_dec = np.exp((CHUNK - 1.0 - idx)[:, None] * lg_lane[None, :])
    q_dec = np.exp((idx + 1.0)[:, None] * lg_lane[None, :]) * QK_SCALE
    even = (_lane_head() % 2 == 0)[None, :]
    g_chunk = np.exp(CHUNK * log_g)
    g_state = np.broadcast_to(np.repeat(g_chunk, RET_V_DIM).reshape(N_PAIRS, 1, PAIR_V), (N_PAIRS, LANES, PAIR_V))
    row_par = np.arange(LANES) // RET_QK_DIM
    col_par = np.arange(PAIR_V) // RET_V_DIM
    bd_mask = row_par[:, None] == col_par[None, :]
    key_order = np.concatenate([CHUNK_ROW_TIME, CHUNK + CHUNK_ROW_TIME])
    dmask_blocked = dmask_pair[:, CHUNK_ROW_TIME][:, :, key_order]
    as_f32 = lambda a: jnp.asarray(np.asarray(a, np.float32))
    return tuple(as_f32(t) for t in (k_dec, dmask_blocked, k_dec[CHUNK_ROW_TIME], q_dec[CHUNK_ROW_TIME],
                                     even, ~even, g_state, bd_mask))


def _rotary_tables(n_pos):
    half = RET_QK_DIM // 2
    inv = (np.float32(ROPE_BASE) ** (-np.arange(half, dtype=np.float32) / half)).astype(np.float32)
    ang = np.arange(n_pos).astype(np.float32)[:, None] * inv[None, :]
    cos, sin = np.cos(ang), np.sin(ang)
    cos_t = np.concatenate([cos, cos, cos, cos], axis=-1).astype(np.float32)
    sin_t = np.concatenate([-sin, sin, -sin, sin], axis=-1).astype(np.float32)
    return cos_t, sin_t


def _rmsnorm_rows(x, gain_row):
    ms = jnp.mean(x * x, axis=-1, keepdims=True)
    return x * lax.rsqrt(ms + EPS) * gain_row


def _silu(x):
    hx = 0.5 * x
    return hx * jnp.tanh(hx) + hx


def _half_conv_params(convw_ref, convb_ref):
    return [0.5 * convw_ref[k:k + 1, :] for k in range(CONV_WIDTH)], 0.5 * convb_ref[...]


def _conv(n_rows, lx_scr, convw_ref, convb_ref, xc_scr, xcb_scr):
    base = SUBLANES
    taps, bias = _half_conv_params(convw_ref, convb_ref)
    xc = bias + taps[3] * lx_scr[pl.ds(base, n_rows), :]
    xc = xc + taps[2] * lx_scr[pl.ds(base - 1, n_rows), :]
    xc = xc + taps[1] * lx_scr[pl.ds(base - 2, n_rows), :]
    xc = xc + taps[0] * lx_scr[pl.ds(base - 3, n_rows), :]
    xc_scr[...] = xc
    xcb_scr[...] = xc.astype(bf16)


CONV_HIST = (CONV_WIDTH - 1) * SUBLANES


def _conv_blocked(lx_scr, convw_ref, convb_ref, xc_scr, xcb_scr):
    taps, bias = _half_conv_params(convw_ref, convb_ref)
    last_sublane = lax.broadcasted_iota(jnp.int32, (SUBLANES, LRU_WIDTH), 0) == SUBLANES - 1
    for c in range(TILE_T // CHUNK):
        base = CONV_HIST + c * CHUNK
        rows = lambda group, n: lx_scr[base + group * SUBLANES:base + (group + n) * SUBLANES, :]
        wrapped = []
        for k in range(1, CONV_WIDTH):
            own, prev = rows(BLOCK_STEPS - k, 1), rows(-k, 1)
            wrapped.append(pltpu.roll(jnp.where(last_sublane, prev, own), 1, 0))
        xc = bias + taps[CONV_WIDTH - 1] * rows(0, BLOCK_STEPS)
        for k in range(1, CONV_WIDTH):
            shifted = jnp.concatenate(wrapped[:k][::-1] + [rows(0, BLOCK_STEPS - k)], axis=0)
            xc = xc + taps[CONV_WIDTH - 1 - k] * shifted
        xc_scr[c * CHUNK:(c + 1) * CHUNK, :] = xc
        xcb_scr[c * CHUNK:(c + 1) * CHUNK, :] = xc.astype(bf16)


def _gates_head(h, xcb_scr, wrg_ref, wig_ref, pre_scr):
    lanes = slice(h * LRU_BLOCK, (h + 1) * LRU_BLOCK)
    wg = jnp.concatenate([wrg_ref[h], wig_ref[h]], axis=-1).astype(bf16)
    pre = jnp.dot(xcb_scr[:, lanes], wg, preferred_element_type=f32)
    pre_scr[:, lanes] = pre[:, :LRU_BLOCK]
    pre_scr[:, LRU_WIDTH + h * LRU_BLOCK:LRU_WIDTH + (h + 1) * LRU_BLOCK] = pre[:, LRU_BLOCK:]


def _lru_maps(half_xc, half_pre_r, half_pre_i, half_brg, half_big, half_c_sp):
    nl = jnp.tanh(half_pre_r + half_brg) * half_c_sp + half_c_sp
    twice_i = jnp.tanh(half_pre_i + half_big) + 1.0
    a = jnp.exp(-nl)
    z = jnp.tanh(nl) * (1.0 + a * a)
    beta = jnp.where(z > 0.0, z * lax.rsqrt(z), 0.0)
    return a, beta * twice_i * half_xc


def _scan_sublanes(a, b):
    rowid = lax.broadcasted_iota(jnp.int32, a.shape, 0)
    for s in (1, 2, 4):
        keep = rowid >= s
        a_s = jnp.where(keep, pltpu.roll(a, s, 0), 1.0)
        b_s = jnp.where(keep, pltpu.roll(b, s, 0), 0.0)
        b = a * b_s + b
        a = a * a_s
    return a, b


def _lru_chunk_blocked(xc, pre_r, pre_i, brg, big, c_sp, carry):
    a, b = _lru_maps(xc, pre_r, pre_i, brg, big, c_sp)
    width = xc.shape[1]
    a3 = a.reshape(BLOCK_STEPS, SUBLANES, width)
    b3 = b.reshape(BLOCK_STEPS, SUBLANES, width)
    decay, local = [a3[0]], [b3[0]]
    for j in range(1, BLOCK_STEPS):
        local.append(a3[j] * local[-1] + b3[j])
        decay.append(a3[j] * decay[-1])
    a_blocks, b_blocks = _scan_sublanes(decay[-1], local[-1])
    after = a_blocks * carry + b_blocks
    rowid = lax.broadcasted_iota(jnp.int32, carry.shape, 0)
    before = jnp.where(rowid >= 1, pltpu.roll(after, 1, 0), carry)
    hs = [local[j] + decay[j] * before for j in range(BLOCK_STEPS)]
    new_carry = jnp.broadcast_to(after[SUBLANES - 1:SUBLANES, :], carry.shape)
    return jnp.concatenate(hs, axis=0), new_carry


def _lru_block(xc, pre_r, pre_i, brg, big, c_sp, carry):
    rows, width = xc.shape
    a, b = _lru_maps(xc, pre_r, pre_i, brg, big, c_sp)
    nv = rows // SUBLANES
    a3 = a.reshape(nv, SUBLANES, width)
    b3 = b.reshape(nv, SUBLANES, width)
    rowid = lax.broadcasted_iota(jnp.int32, (nv, SUBLANES, width), 1)
    for s in (1, 2, 4):
        keep = rowid >= s
        a_s = jnp.where(keep, pltpu.roll(a3, s, 1), 1.0)
        b_s = jnp.where(keep, pltpu.roll(b3, s, 1), 0.0)
        b3 = a3 * b_s + b3
        a3 = a3 * a_s
    hs = []
    for v in range(nv):
        h_v = a3[v] * carry + b3[v]
        carry = h_v[SUBLANES - 1:SUBLANES, :]
        hs.append(h_v)
    return jnp.concatenate(hs, axis=0), carry


def _rotary(t, cos, sin):
    half = RET_QK_DIM // 2
    in_first_half = (lax.broadcasted_iota(jnp.int32, t.shape, 1) // half) % 2 == 0
    partner = jnp.where(in_first_half, pltpu.roll(t, LANES - half, 1), pltpu.roll(t, half, 1))
    return t * cos + partner * sin


def _dot_t0(a, b):
    return lax.dot_general(a, b, (((0,), (0,)), ((), ())), preferred_element_type=f32)


def _dot_nt(a, b):
    return lax.dot_general(a, b, (((1,), (1,)), ((), ())), preferred_element_type=f32)


def _meta_states(meta_ref, cos_ref, sin_ref, g1_ref, wlx_ref, wk_ref, wv_ref, convw_ref, convb_ref,
                 wrg_ref, wig_ref, brg_ref, big_ref, lam_ref, kdec_ref, bdm_ref,
                 lx_out, h_out, st_out,
                 lx_scr, xc_scr, xcb_scr, pre_scr):
    u = _rmsnorm_rows(meta_ref[...], g1_ref[...]).astype(bf16)
    lx_scr[0:SUBLANES, :] = jnp.zeros((SUBLANES, LRU_WIDTH), f32)
    lx_scr[SUBLANES:SUBLANES + N_META, :] = jnp.dot(u, wlx_ref[...].astype(bf16), preferred_element_type=f32)
    k = jnp.dot(u, wk_ref[...].astype(bf16), preferred_element_type=f32)
    v = jnp.dot(u, wv_ref[...].astype(bf16), preferred_element_type=f32).astype(bf16)

    _conv(N_META, lx_scr, convw_ref, convb_ref, xc_scr, xcb_scr)
    for h in range(LRU_HEADS):
        _gates_head(h, xcb_scr, wrg_ref, wig_ref, pre_scr)
    for back in range(1, CONV_WIDTH):
        row = lx_scr[SUBLANES + N_META - back:SUBLANES + N_META - back + 1, :]
        group = CONV_WIDTH - 1 - back
        lx_out[group * SUBLANES:(group + 1) * SUBLANES, :] = jnp.broadcast_to(row, (SUBLANES, LRU_WIDTH))
    half_c_sp = 0.5 * LRU_C * jax.nn.softplus(-lam_ref[...])
    _, carry = _lru_block(xc_scr[...], pre_scr[:, :LRU_WIDTH], pre_scr[:, LRU_WIDTH:],
                          0.5 * brg_ref[...], 0.5 * big_ref[...], half_c_sp, jnp.zeros((1, LRU_WIDTH), f32))
    h_out[...] = jnp.broadcast_to(carry, (SUBLANES, LRU_WIDTH))

    cos, sin = cos_ref[...], sin_ref[...]
    for p in range(N_PAIRS):
        sl = slice(p * LANES, (p + 1) * LANES)
        k_rot = _rotary(k[:, sl], cos, sin)
        kd = (k_rot * kdec_ref[CHUNK - N_META:CHUNK, sl]).astype(bf16)
        st_out[p] = _dot_t0(kd, v[:, p * PAIR_V:(p + 1) * PAIR_V]) * bdm_ref[...]


def _norm_tile(x_tile, g1_ref, u_scr):
    u_scr[...] = _rmsnorm_rows(x_tile, g1_ref[...]).astype(bf16)


MXU_COLS = 256
N_IN_TILES = IN_WIDTH // MXU_COLS


def _project_col_tile(t, u_scr, win_ref, slot):
    lx_scr, gate_scr, q_scr, k_scr, v_scr, rg_scr = slot
    off = t * MXU_COLS
    res = jnp.dot(u_scr[...], win_ref[:, off:off + MXU_COLS].astype(bf16), preferred_element_type=f32)
    for dst, start, rows, dt in ((lx_scr, OFF_LX, slice(CONV_HIST, CONV_HIST + TILE_T), f32), (gate_scr, OFF_GATE, slice(None), f32),
                                 (q_scr, OFF_Q, slice(None), f32), (k_scr, OFF_K, slice(None), f32),
                                 (v_scr, OFF_V, slice(None), bf16), (rg_scr, OFF_RG, slice(None), f32)):
        width = dst.shape[1]
        if start <= off < start + width:
            dst[rows, off - start:off - start + MXU_COLS] = res.astype(dt)
            return
    raise AssertionError("column tile outside the projection")


def _finish_tile(load_x, cos_ref, sin_ref, convw_ref, convb_ref, wrg_ref, wig_ref, brg_ref, big_ref, lam_ref, rng_ref,
                 wout_ref, fng_ref, dmaskp_ref, kdec_ref, qdec_ref, even_ref, odd_ref, gst_ref, bdm_ref,
                 slot, xc_scr, xcb_scr, pre_scr, y_scr, lxtail_scr, h_scr, st_scr, after_gates):
    lx_scr, gate_scr, q_scr, k_scr, v_scr, rg_scr = slot
    tt = TILE_T

    def retention_unit(c, p):
        rows = slice(c * CHUNK, (c + 1) * CHUNK)
        cos, sin = cos_ref[rows, :], sin_ref[rows, :]
        sl = slice(p * LANES, (p + 1) * LANES)
        q_rot = _rotary(q_scr[rows, sl], cos, sin)
        k_rot = _rotary(k_scr[rows, sl], cos, sin)
        q_b = (q_rot * QK_SCALE).astype(bf16)
        q_d = (q_rot * qdec_ref[:, sl]).astype(bf16)
        k_d = (k_rot * kdec_ref[:, sl]).astype(bf16)
        k_b = k_rot.astype(bf16)
        kk = jnp.concatenate([k_b * even_ref[:, sl].astype(bf16), k_b * odd_ref[:, sl].astype(bf16)], axis=0)
        v_pair = v_scr[rows, p * PAIR_V:(p + 1) * PAIR_V]
        zeros = jnp.zeros((CHUNK, RET_V_DIM), bf16)
        v_bd = jnp.concatenate([jnp.concatenate([v_pair[:, :RET_V_DIM], zeros], axis=1),
                                jnp.concatenate([zeros, v_pair[:, RET_V_DIM:]], axis=1)], axis=0)
        st_pair = st_scr[p]
        s = _dot_nt(q_b, kk) * dmaskp_ref[p]
        o = jnp.dot(jnp.concatenate([s.astype(bf16), q_d], axis=1),
                    jnp.concatenate([v_bd, st_pair.astype(bf16)], axis=0), preferred_element_type=f32)
        for e in range(2):
            h = 2 * p + e
            hs = slice(h * RET_V_DIM, (h + 1) * RET_V_DIM)
            o_h = o[:, e * RET_V_DIM:(e + 1) * RET_V_DIM]
            mu = jnp.mean(o_h, axis=-1, keepdims=True)
            oc = o_h - mu
            var = jnp.mean(oc * oc, axis=-1, keepdims=True)
            on = oc * lax.rsqrt(var + EPS) * rng_ref[:, hs]
            y_scr[rows, LRU_WIDTH + h * RET_V_DIM:LRU_WIDTH + (h + 1) * RET_V_DIM] = (
                on * _silu(rg_scr[rows, hs])).astype(bf16)
        kv = _dot_t0(k_d, v_pair)
        st_scr[p] = gst_ref[p] * st_pair + bdm_ref[...] * kv

    lx_scr[0:CONV_HIST, :] = lxtail_scr[...]
    _conv_blocked(lx_scr, convw_ref, convb_ref, xc_scr, xcb_scr)
    lxtail_scr[...] = lx_scr[pl.ds(tt, CONV_HIST), :]
    for h in range(LRU_HEADS):
        lanes = slice(h * LRU_BLOCK, (h + 1) * LRU_BLOCK)
        ilanes = slice(LRU_WIDTH + h * LRU_BLOCK, LRU_WIDTH + (h + 1) * LRU_BLOCK)
        _gates_head(h, xcb_scr, wrg_ref, wig_ref, pre_scr)
        after_gates(h)
        full = (CHUNK, LRU_BLOCK)
        c_sp = jnp.broadcast_to(0.5 * LRU_C * jax.nn.softplus(-lam_ref[:, lanes]), full)
        brg, big = jnp.broadcast_to(0.5 * brg_ref[:, lanes], full), jnp.broadcast_to(0.5 * big_ref[:, lanes], full)
        carry = h_scr[:, lanes]
        for c in range(tt // CHUNK):
            rows = slice(c * CHUNK, (c + 1) * CHUNK)
            hb, carry = _lru_chunk_blocked(xc_scr[rows, lanes], pre_scr[rows, lanes], pre_scr[rows, ilanes],
                                           brg, big, c_sp, carry)
            y_scr[rows, lanes] = (hb * _silu(gate_scr[rows, lanes])).astype(bf16)
        h_scr[:, lanes] = carry

    for c in range(tt // CHUNK):
        for p in range(N_PAIRS):
            retention_unit(c, p)

    res = load_x() + jnp.dot(y_scr[...], wout_ref[...].astype(bf16), preferred_element_type=f32)
    return _rmsnorm_rows(res, fng_ref[...])


X_SLOTS = 3
OUT_SLOTS = 2
ROW_GROUPS = TILE_T // SUBLANES


def _tile_block_copies(hbm_ref, vmem_ref, sem_ref, tile, slot, n_tiles, to_hbm):
    b = lax.div(tile, n_tiles)
    first_row = (tile - b * n_tiles) * TILE_T
    copies = []
    for c in range(TILE_T // CHUNK):
        for s in range(SUBLANES):
            hbm = hbm_ref.at[b, pl.ds(first_row + c * CHUNK + s * BLOCK_STEPS, BLOCK_STEPS), :]
            vmem = vmem_ref.at[slot, pl.ds(c * BLOCK_STEPS, BLOCK_STEPS), s, :]
            src, dst = (vmem, hbm) if to_hbm else (hbm, vmem)
            copies.append(pltpu.make_async_copy(src, dst, sem_ref.at[slot]))
    return copies


def _main_kernel(n_tiles, n_total,
                 x_hbm, cos_ref, sin_ref, g1_ref, win_ref, convw_ref, convb_ref, wrg_ref, wig_ref,
                 brg_ref, big_ref, lam_ref, rng_ref, wout_ref, fng_ref, dmaskp_ref, kdec_ref, qdec_ref, even_ref, odd_ref,
                 gst_ref, bdm_ref, meta_ref, cos_meta_ref, sin_meta_ref, kdec_time_ref,
                 o_hbm,
                 u_scr,
                 lx_a, gate_a, q_a, k_a, v_a, rg_a,
                 lx_b, gate_b, q_b, k_b, v_b, rg_b,
                 xc_scr, xcb_scr, pre_scr, y_scr, lxtail_scr, h_scr, st_scr,
                 lx0_ref, h0_ref, st0_ref,
                 xin_scr, out_scr, xin_sem, out_sem):
    g = pl.program_id(0)
    slots = ((lx_a, gate_a, q_a, k_a, v_a, rg_a), (lx_b, gate_b, q_b, k_b, v_b, rg_b))
    x_slot = lambda tile: lax.rem(tile + X_SLOTS, X_SLOTS)
    out_slot = lambda tile: lax.rem(tile + OUT_SLOTS, OUT_SLOTS)
    x_tile = lambda tile: xin_scr[x_slot(tile)].reshape(TILE_T, D_MODEL)
    fetch = lambda tile: _tile_block_copies(x_hbm, xin_scr, xin_sem, jnp.minimum(tile, n_total - 1), x_slot(tile),
                                            n_tiles, to_hbm=False)
    write_back = lambda tile: _tile_block_copies(o_hbm, out_scr, out_sem, tile, out_slot(tile), n_tiles, to_hbm=True)

    @pl.when(g >= 1 + OUT_SLOTS)
    def _():
        for copy in write_back(g - 1 - OUT_SLOTS):
            copy.wait()

    @pl.when(lax.rem(g + n_tiles - 1, n_tiles) == 0)
    def _():
        lxtail_scr[...] = lx0_ref[...]
        h_scr[...] = h0_ref[...]
        st_scr[...] = st0_ref[...]

    tiles_after_head = (4, 4, 3, 3, 2, 2, 1, 1)
    assert sum(tiles_after_head) == N_IN_TILES and len(tiles_after_head) == LRU_HEADS

    def finish(slot_finish, after_gates):
        out = _finish_tile(lambda: x_tile(g - 1), cos_ref, sin_ref, convw_ref, convb_ref, wrg_ref, wig_ref, brg_ref, big_ref,
                           lam_ref, rng_ref, wout_ref, fng_ref, dmaskp_ref, kdec_ref, qdec_ref, even_ref, odd_ref, gst_ref,
                           bdm_ref, slot_finish, xc_scr, xcb_scr, pre_scr, y_scr, lxtail_scr, h_scr, st_scr, after_gates)
        out_scr[out_slot(g - 1)] = out.reshape(ROW_GROUPS, SUBLANES, D_MODEL)
        for copy in write_back(g - 1):
            copy.start()

    def step(slot_project, slot_finish):
        for copy in fetch(g):
            copy.wait()
        for copy in fetch(g + 1):
            copy.start()
        _norm_tile(x_tile(g), g1_ref, u_scr)

        def after_gates(h):
            first = sum(tiles_after_head[:h])
            for t in range(first, first + tiles_after_head[h]):
                _project_col_tile(t, u_scr, win_ref, slot_project)

        finish(slot_finish, after_gates)

    @pl.when(g == 0)
    def _():
        for copy in fetch(g):
            copy.start()
        for copy in fetch(g):
            copy.wait()
        for copy in fetch(g + 1):
            copy.start()
        _norm_tile(x_tile(g), g1_ref, u_scr)
        for t in range(N_IN_TILES):
            _project_col_tile(t, u_scr, win_ref, slots[0])
        meta_rows = lambda ref, n: ref.at[pl.ds(0, n)]
        _meta_states(meta_ref, cos_meta_ref, sin_meta_ref, g1_ref,
                     win_ref.at[:, pl.ds(OFF_LX, LRU_WIDTH)], win_ref.at[:, pl.ds(OFF_K, RET_QK_WIDTH)],
                     win_ref.at[:, pl.ds(OFF_V, RET_WIDTH)], convw_ref, convb_ref, wrg_ref, wig_ref, brg_ref, big_ref,
                     lam_ref, kdec_time_ref, bdm_ref, lx0_ref, h0_ref, st0_ref,
                     meta_rows(slots[1][0], SUBLANES + N_META), meta_rows(xc_scr, N_META), meta_rows(xcb_scr, N_META),
                     meta_rows(pre_scr, N_META))

    regular = jnp.logical_and(g >= 1, g < n_total)

    @pl.when(jnp.logical_and(regular, lax.rem(g, 2) == 0))
    def _():
        step(slots[0], slots[1])

    @pl.when(jnp.logical_and(regular, lax.rem(g, 2) == 1))
    def _():
        step(slots[1], slots[0])

    @pl.when(g == n_total)
    def _():
        for copy in fetch(g):
            copy.wait()
        finish(slots[(n_total - 1) % 2], lambda h: None)
        for tile in (g - 2, g - 1):
            for copy in write_back(tile):
                copy.wait()


def _const_spec(shape):
    nd = len(shape)
    return pl.BlockSpec(shape, lambda g, _nd=nd: (0,) * _nd, pipeline_mode=pl.Buffered(1))


def kernel(x, meta_tokens, norm_gain, w_in, conv_w, conv_b, w_rg, b_rg, w_ig, b_ig,
           lru_lambda, ret_norm_gain, w_out, final_norm_gain):
    B, S, D = x.shape
    assert D == D_MODEL and S % TILE_T == 0 and TILE_T % CHUNK == 0
    assert norm_gain.shape[0] == 1, "single-layer block"
    assert meta_tokens.shape == (N_META, D_MODEL)

    assert w_in.shape == (1, D_MODEL, IN_WIDTH) and w_out.shape == (1, MIX_WIDTH, D_MODEL)
    assert w_rg.shape == w_ig.shape == (1, LRU_HEADS, LRU_BLOCK, LRU_BLOCK)

    g1 = norm_gain[0].reshape(1, D_MODEL)
    fng = final_norm_gain.reshape(1, D_MODEL)
    convw = conv_w[0]
    convb = conv_b[0].reshape(1, LRU_WIDTH)
    brg = b_rg[0].reshape(1, LRU_WIDTH)
    big = b_ig[0].reshape(1, LRU_WIDTH)
    lam = lru_lambda[0].reshape(1, LRU_WIDTH)
    rng = ret_norm_gain[0].reshape(1, RET_WIDTH)

    k_dec, dmask_blocked, k_dec_blocked, q_dec_blocked, even, odd, g_state, bd_mask = _retention_tables()
    cos_t, sin_t = _rotary_tables(N_META + S)
    cos_m, sin_m = jnp.asarray(cos_t[:N_META]), jnp.asarray(sin_t[:N_META])
    cos_blocked = jnp.asarray(_chunk_rows_blocked(cos_t[N_META:]))
    sin_blocked = jnp.asarray(_chunk_rows_blocked(sin_t[N_META:]))

    tt = TILE_T
    n_tiles = S // tt
    n_total = B * n_tiles

    rot_spec = pl.BlockSpec((tt, LANES), lambda g: (jnp.maximum(g - 1, 0) % n_tiles, 0))
    in_specs = [
        pl.BlockSpec(memory_space=pl.ANY),
        rot_spec, rot_spec,
        _const_spec((1, D_MODEL)),
        _const_spec((None, D_MODEL, IN_WIDTH)),
        _const_spec((CONV_WIDTH, LRU_WIDTH)), _const_spec((1, LRU_WIDTH)),
        _const_spec((None, LRU_HEADS, LRU_BLOCK, LRU_BLOCK)),
        _const_spec((None, LRU_HEADS, LRU_BLOCK, LRU_BLOCK)),
        _const_spec((1, LRU_WIDTH)), _const_spec((1, LRU_WIDTH)), _const_spec((1, LRU_WIDTH)),
        _const_spec((1, RET_WIDTH)),
        _const_spec((None, MIX_WIDTH, D_MODEL)),
        _const_spec((1, D_MODEL)),
        _const_spec((N_PAIRS, CHUNK, 2 * CHUNK)),
        _const_spec((CHUNK, RET_QK_WIDTH)), _const_spec((CHUNK, RET_QK_WIDTH)),
        _const_spec((1, RET_QK_WIDTH)), _const_spec((1, RET_QK_WIDTH)),
        _const_spec((N_PAIRS, LANES, PAIR_V)),
        _const_spec((LANES, PAIR_V)),
        _const_spec((N_META, D_MODEL)),
        _const_spec((N_META, LANES)), _const_spec((N_META, LANES)),
        _const_spec((CHUNK, RET_QK_WIDTH)),
    ]
    slot_scratch = [
        pltpu.VMEM((CONV_HIST + tt, LRU_WIDTH), f32),
        pltpu.VMEM((tt, LRU_WIDTH), f32),
        pltpu.VMEM((tt, RET_QK_WIDTH), f32),
        pltpu.VMEM((tt, RET_QK_WIDTH), f32),
        pltpu.VMEM((tt, RET_WIDTH), bf16),
        pltpu.VMEM((tt, RET_WIDTH), f32),
    ]
    scratch = [pltpu.VMEM((tt, D_MODEL), bf16)] + slot_scratch + slot_scratch + [
        pltpu.VMEM((tt, LRU_WIDTH), f32),
        pltpu.VMEM((tt, LRU_WIDTH), bf16),
        pltpu.VMEM((tt, 2 * LRU_WIDTH), f32),
        pltpu.VMEM((tt, MIX_WIDTH), bf16),
        pltpu.VMEM((CONV_HIST, LRU_WIDTH), f32),
        pltpu.VMEM((SUBLANES, LRU_WIDTH), f32),
        pltpu.VMEM((N_PAIRS, LANES, PAIR_V), f32),
        pltpu.VMEM((CONV_HIST, LRU_WIDTH), f32),
        pltpu.VMEM((SUBLANES, LRU_WIDTH), f32),
        pltpu.VMEM((N_PAIRS, LANES, PAIR_V), f32),
        pltpu.VMEM((X_SLOTS, ROW_GROUPS, SUBLANES, D_MODEL), f32),
        pltpu.VMEM((OUT_SLOTS, ROW_GROUPS, SUBLANES, D_MODEL), f32),
        pltpu.SemaphoreType.DMA((X_SLOTS,)),
        pltpu.SemaphoreType.DMA((OUT_SLOTS,)),
    ]
    out = pl.pallas_call(
        functools.partial(_main_kernel, n_tiles, n_total),
        grid=(n_total + 1,),
        in_specs=in_specs,
        out_specs=pl.BlockSpec(memory_space=pl.ANY),
        out_shape=jax.ShapeDtypeStruct((B, S, D_MODEL), x.dtype),
        scratch_shapes=scratch,
        compiler_params=pltpu.CompilerParams(
            dimension_semantics=("arbitrary",),
            vmem_limit_bytes=VMEM_LIMIT_BYTES),
        name="hybrid_main",
    )(x, cos_blocked, sin_blocked, g1, w_in, convw, convb, w_rg, w_ig, brg, big, lam, rng, w_out, fng,
      dmask_blocked, k_dec_blocked, q_dec_blocked, even, odd, g_state, bd_mask, meta_tokens, cos_m, sin_m, k_dec)
    return out
```

```python
import functools

import numpy as np
import jax
import jax.numpy as jnp
from jax import lax
from jax.experimental import pallas as pl
from jax.experimental.pallas import tpu as pltpu

f32 = jnp.float32
bf16 = jnp.bfloat16

D_MODEL = 1024
N_META = 16
LRU_WIDTH = 1024
LRU_HEADS = 8
LRU_BLOCK = 128
CONV_WIDTH = 4
LRU_C = 8.0
RET_HEADS = 8
RET_QK_DIM = 64
RET_V_DIM = 128
RET_QK_WIDTH = 512
RET_WIDTH = 1024
CHUNK = 128
ROPE_BASE = 10000.0
MIX_WIDTH = 2048
EPS = 1e-6
QK_SCALE = RET_QK_DIM ** -0.5
F32_TINY = float(np.finfo(np.float32).tiny)

OFF_LX, OFF_GATE, OFF_Q, OFF_K, OFF_V, OFF_RG = 0, 1024, 2048, 2560, 3072, 4096
IN_WIDTH = 5120

LANES = 128
SUBLANES = 8
N_PAIRS = RET_HEADS // 2
PAIR_V = 2 * RET_V_DIM
TILE_T = 256
VMEM_LIMIT_BYTES = 60000 * 1024


def _lane_head():
    return np.arange(RET_QK_WIDTH) // RET_QK_DIM


BLOCK_STEPS = CHUNK // SUBLANES
CHUNK_ROW_TIME = np.arange(CHUNK).reshape(SUBLANES, BLOCK_STEPS).T.reshape(-1)


def _chunk_rows_blocked(table):
    n = table.shape[0]
    return table.reshape((n // CHUNK, CHUNK) + table.shape[1:])[:, CHUNK_ROW_TIME].reshape(table.shape)


def _retention_tables():
    log_g = np.log1p(-np.exp2(-5.0 - np.arange(RET_HEADS, dtype=np.float32))).astype(np.float32)
    idx = np.arange(CHUNK, dtype=np.float32)
    diff = idx[:, None] - idx[None, :]
    dmask = np.where(diff[None] >= 0.0, np.exp(np.maximum(diff, 0.0)[None] * log_g[:, None, None]), 0.0)
    dmask_pair = np.concatenate([dmask[0::2], dmask[1::2]], axis=-1)
    lg_lane = log_g[_lane_head()]
    k_dec = np.exp((CHUNK - 1.0 - idx)[:, None] * lg_lane[None, :])
    q_dec = np.exp((idx + 1.0)[:, None] * lg_lane[None, :]) * QK_SCALE
    even = (_lane_head() % 2 == 0)[None, :]
    g_chunk = np.exp(CHUNK * log_g)
    g_state = np.broadcast_to(np.repeat(g_chunk, RET_V_DIM).reshape(N_PAIRS, 1, PAIR_V), (N_PAIRS, LANES, PAIR_V))
    row_par = np.arange(LANES) // RET_QK_DIM
    col_par = np.arange(PAIR_V) // RET_V_DIM
    bd_mask = row_par[:, None] == col_par[None, :]
    key_order = np.concatenate([CHUNK_ROW_TIME, CHUNK + CHUNK_ROW_TIME])
    dmask_blocked = dmask_pair[:, CHUNK_ROW_TIME][:, :, key_order] * QK_SCALE
    as_f32 = lambda a: jnp.asarray(np.asarray(a, np.float32))
    return tuple(as_f32(t) for t in (k_dec, dmask_blocked, k_dec[CHUNK_ROW_TIME], q_dec[CHUNK_ROW_TIME],
                                     even, ~even, g_state, bd_mask))


def _rotary_tables(n_pos):
    half = RET_QK_DIM // 2
    inv = (np.float32(ROPE_BASE) ** (-np.arange(half, dtype=np.float32) / half)).astype(np.float32)
    ang = np.arange(n_pos).astype(np.float32)[:, None] * inv[None, :]
    cos, sin = np.cos(ang), np.sin(ang)
    cos_t = np.concatenate([cos, cos, cos, cos], axis=-1).astype(np.float32)
    sin_t = np.concatenate([-sin, sin, -sin, sin], axis=-1).astype(np.float32)
    return cos_t, sin_t


def _rmsnorm_rows(x, gain_row):
    ms = jnp.mean(x * x, axis=-1, keepdims=True)
    return x * lax.rsqrt(ms + EPS) * gain_row


def _silu(x):
    hx = 0.5 * x
    return hx * jnp.tanh(hx) + hx


def _half_conv_params(convw_ref, convb_ref):
    return [0.5 * convw_ref[k:k + 1, :] for k in range(CONV_WIDTH)], 0.5 * convb_ref[...]


def _conv(n_rows, lx_scr, convw_ref, convb_ref, xc_scr, xcb_scr):
    base = SUBLANES
    taps, bias = _half_conv_params(convw_ref, convb_ref)
    xc = bias + taps[3] * lx_scr[pl.ds(base, n_rows), :]
    xc = xc + taps[2] * lx_scr[pl.ds(base - 1, n_rows), :]
    xc = xc + taps[1] * lx_scr[pl.ds(base - 2, n_rows), :]
    xc = xc + taps[0] * lx_scr[pl.ds(base - 3, n_rows), :]
    xc_scr[...] = xc
    xcb_scr[...] = xc.astype(bf16)


CONV_HIST = (CONV_WIDTH - 1) * SUBLANES


def _conv_blocked(lx_scr, convw_ref, convb_ref, xc_scr, xcb_scr):
    taps, bias = _half_conv_params(convw_ref, convb_ref)
    last_sublane = lax.broadcasted_iota(jnp.int32, (SUBLANES, LRU_WIDTH), 0) == SUBLANES - 1
    for c in range(TILE_T // CHUNK):
        base = CONV_HIST + c * CHUNK
        rows = lambda group, n: lx_scr[base + group * SUBLANES:base + (group + n) * SUBLANES, :]
        wrapped = []
        for k in range(1, CONV_WIDTH):
            own, prev = rows(BLOCK_STEPS - k, 1), rows(-k, 1)
            wrapped.append(pltpu.roll(jnp.where(last_sublane, prev, own), 1, 0))
        xc = bias + taps[CONV_WIDTH - 1] * rows(0, BLOCK_STEPS)
        for k in range(1, CONV_WIDTH):
            shifted = jnp.concatenate(wrapped[:k][::-1] + [rows(0, BLOCK_STEPS - k)], axis=0)
            xc = xc + taps[CONV_WIDTH - 1 - k] * shifted
        xc_scr[c * CHUNK:(c + 1) * CHUNK, :] = xc
        xcb_scr[c * CHUNK:(c + 1) * CHUNK, :] = xc.astype(bf16)


def _gates_head(h, xcb_scr, wrg_ref, wig_ref, pre_scr):
    lanes = slice(h * LRU_BLOCK, (h + 1) * LRU_BLOCK)
    wg = jnp.concatenate([wrg_ref[h], wig_ref[h]], axis=-1).astype(bf16)
    pre = jnp.dot(xcb_scr[:, lanes], wg, preferred_element_type=f32)
    pre_scr[:, lanes] = pre[:, :LRU_BLOCK]
    pre_scr[:, LRU_WIDTH + h * LRU_BLOCK:LRU_WIDTH + (h + 1) * LRU_BLOCK] = pre[:, LRU_BLOCK:]


def _lru_maps(half_xc, half_pre_r, half_pre_i, half_brg, half_big, half_c_sp):
    nl = jnp.tanh(half_pre_r + half_brg) * half_c_sp + half_c_sp
    twice_i = jnp.tanh(half_pre_i + half_big) + 1.0
    a = jnp.exp(-nl)
    z = jnp.tanh(nl) * (1.0 + a * a)
    beta = z * lax.rsqrt(jnp.maximum(z, F32_TINY))
    return a, beta * twice_i * half_xc


def _scan_sublanes(a, b):
    rowid = lax.broadcasted_iota(jnp.int32, a.shape, 0)
    for s in (1, 2, 4):
        keep = rowid >= s
        a_s = jnp.where(keep, pltpu.roll(a, s, 0), 1.0)
        b_s = jnp.where(keep, pltpu.roll(b, s, 0), 0.0)
        b = a * b_s + b
        a = a * a_s
    return a, b


def _lru_chunk_blocked(xc, pre_r, pre_i, brg, big, c_sp, carry):
    a, b = _lru_maps(xc, pre_r, pre_i, brg, big, c_sp)
    width = xc.shape[1]
    a3 = a.reshape(BLOCK_STEPS, SUBLANES, width)
    b3 = b.reshape(BLOCK_STEPS, SUBLANES, width)
    decay, local = [a3[0]], [b3[0]]
    for j in range(1, BLOCK_STEPS):
        local.append(a3[j] * local[-1] + b3[j])
        decay.append(a3[j] * decay[-1])
    a_blocks, b_blocks = _scan_sublanes(decay[-1], local[-1])
    after = a_blocks * carry + b_blocks
    rowid = lax.broadcasted_iota(jnp.int32, carry.shape, 0)
    before = jnp.where(rowid >= 1, pltpu.roll(after, 1, 0), carry)
    hs = [local[j] + decay[j] * before for j in range(BLOCK_STEPS)]
    new_carry = jnp.broadcast_to(after[SUBLANES - 1:SUBLANES, :], carry.shape)
    return jnp.concatenate(hs, axis=0), new_carry


def _lru_block(xc, pre_r, pre_i, brg, big, c_sp, carry):
    rows, width = xc.shape
    a, b = _lru_maps(xc, pre_r, pre_i, brg, big, c_sp)
    nv = rows // SUBLANES
    a3 = a.reshape(nv, SUBLANES, width)
    b3 = b.reshape(nv, SUBLANES, width)
    rowid = lax.broadcasted_iota(jnp.int32, (nv, SUBLANES, width), 1)
    for s in (1, 2, 4):
        keep = rowid >= s
        a_s = jnp.where(keep, pltpu.roll(a3, s, 1), 1.0)
        b_s = jnp.where(keep, pltpu.roll(b3, s, 1), 0.0)
        b3 = a3 * b_s + b3
        a3 = a3 * a_s
    hs = []
    for v in range(nv):
        h_v = a3[v] * carry + b3[v]
        carry = h_v[SUBLANES - 1:SUBLANES, :]
        hs.append(h_v)
    return jnp.concatenate(hs, axis=0), carry


def _rotary(t, cos, sin):
    half = RET_QK_DIM // 2
    in_first_half = (lax.broadcasted_iota(jnp.int32, t.shape, 1) // half) % 2 == 0
    partner = jnp.where(in_first_half, pltpu.roll(t, LANES - half, 1), pltpu.roll(t, half, 1))
    return t * cos + partner * sin


def _dot_t0(a, b):
    return lax.dot_general(a, b, (((0,), (0,)), ((), ())), preferred_element_type=f32)


def _dot_nt(a, b):
    return lax.dot_general(a, b, (((1,), (1,)), ((), ())), preferred_element_type=f32)


def _meta_states(meta_ref, cos_ref, sin_ref, g1_ref, wlx_ref, wk_ref, wv_ref, convw_ref, convb_ref,
                 wrg_ref, wig_ref, brg_ref, big_ref, lam_ref, kdec_ref, bdm_ref,
                 lx_out, h_out, st_out,
                 lx_scr, xc_scr, xcb_scr, pre_scr):
    u = _rmsnorm_rows(meta_ref[...], g1_ref[...]).astype(bf16)
    lx_scr[0:SUBLANES, :] = jnp.zeros((SUBLANES, LRU_WIDTH), f32)
    lx_scr[SUBLANES:SUBLANES + N_META, :] = jnp.dot(u, wlx_ref[...].astype(bf16), preferred_element_type=f32)
    k = jnp.dot(u, wk_ref[...].astype(bf16), preferred_element_type=f32)
    v = jnp.dot(u, wv_ref[...].astype(bf16), preferred_element_type=f32).astype(bf16)

    _conv(N_META, lx_scr, convw_ref, convb_ref, xc_scr, xcb_scr)
    for h in range(LRU_HEADS):
        _gates_head(h, xcb_scr, wrg_ref, wig_ref, pre_scr)
    for back in range(1, CONV_WIDTH):
        row = lx_scr[SUBLANES + N_META - back:SUBLANES + N_META - back + 1, :]
        group = CONV_WIDTH - 1 - back
        lx_out[group * SUBLANES:(group + 1) * SUBLANES, :] = jnp.broadcast_to(row, (SUBLANES, LRU_WIDTH))
    half_c_sp = 0.5 * LRU_C * jax.nn.softplus(-lam_ref[...])
    _, carry = _lru_block(xc_scr[...], pre_scr[:, :LRU_WIDTH], pre_scr[:, LRU_WIDTH:],
                          0.5 * brg_ref[...], 0.5 * big_ref[...], half_c_sp, jnp.zeros((1, LRU_WIDTH), f32))
    h_out[...] = jnp.broadcast_to(carry, (SUBLANES, LRU_WIDTH))

    cos, sin = cos_ref[...], sin_ref[...]
    for p in range(N_PAIRS):
        sl = slice(p * LANES, (p + 1) * LANES)
        k_rot = _rotary(k[:, sl], cos, sin)
        kd = (k_rot * kdec_ref[CHUNK - N_META:CHUNK, sl]).astype(bf16)
        st_out[p] = _dot_t0(kd, v[:, p * PAIR_V:(p + 1) * PAIR_V]) * bdm_ref[...]


def _norm_tile(x_tile, g1_ref, u_scr):
    u_scr[...] = _rmsnorm_rows(x_tile, g1_ref[...]).astype(bf16)


MXU_COLS = 256
N_IN_TILES = IN_WIDTH // MXU_COLS


def _project_col_tile(t, u_scr, win_ref, slot):
    lx_scr, gate_scr, q_scr, k_scr, v_scr, rg_scr = slot
    off = t * MXU_COLS
    res = jnp.dot(u_scr[...], win_ref[:, off:off + MXU_COLS].astype(bf16), preferred_element_type=f32)
    for dst, start, rows, dt in ((lx_scr, OFF_LX, slice(CONV_HIST, CONV_HIST + TILE_T), f32), (gate_scr, OFF_GATE, slice(None), f32),
                                 (q_scr, OFF_Q, slice(None), f32), (k_scr, OFF_K, slice(None), f32),
                                 (v_scr, OFF_V, slice(None), bf16), (rg_scr, OFF_RG, slice(None), f32)):
        width = dst.shape[1]
        if start <= off < start + width:
            dst[rows, off - start:off - start + MXU_COLS] = res.astype(dt)
            return
    raise AssertionError("column tile outside the projection")


def _finish_tile(load_x, cos_ref, sin_ref, convw_ref, convb_ref, wrg_ref, wig_ref, brg_ref, big_ref, lam_ref, rng_ref,
                 wout_ref, fng_ref, dmaskp_ref, kdec_ref, qdec_ref, even_ref, odd_ref, gst_ref, bdm_ref,
                 slot, xc_scr, xcb_scr, pre_scr, y_scr, lxtail_scr, h_scr, st_scr, after_gates):
    lx_scr, gate_scr, q_scr, k_scr, v_scr, rg_scr = slot
    tt = TILE_T

    def retention_unit(c, p):
        rows = slice(c * CHUNK, (c + 1) * CHUNK)
        cos, sin = cos_ref[rows, :], sin_ref[rows, :]
        sl = slice(p * LANES, (p + 1) * LANES)
        q_rot = _rotary(q_scr[rows, sl], cos, sin)
        k_rot = _rotary(k_scr[rows, sl], cos, sin)
        q_b = q_rot.astype(bf16)
        q_d = (q_rot * qdec_ref[:, sl]).astype(bf16)
        k_d = (k_rot * kdec_ref[:, sl]).astype(bf16)
        k_b = k_rot.astype(bf16)
        kk = jnp.concatenate([k_b * even_ref[:, sl].astype(bf16), k_b * odd_ref[:, sl].astype(bf16)], axis=0)
        v_pair = v_scr[rows, p * PAIR_V:(p + 1) * PAIR_V]
        zeros = jnp.zeros((CHUNK, RET_V_DIM), bf16)
        v_bd = jnp.concatenate([jnp.concatenate([v_pair[:, :RET_V_DIM], zeros], axis=1),
                                jnp.concatenate([zeros, v_pair[:, RET_V_DIM:]], axis=1)], axis=0)
        st_pair = st_scr[p]
        s = _dot_nt(q_b, kk) * dmaskp_ref[p]
        o = jnp.dot(jnp.concatenate([s.astype(bf16), q_d], axis=1),
                    jnp.concatenate([v_bd, st_pair.astype(bf16)], axis=0), preferred_element_type=f32)
        for e in range(2):
            h = 2 * p + e
            hs = slice(h * RET_V_DIM, (h + 1) * RET_V_DIM)
            o_h = o[:, e * RET_V_DIM:(e + 1) * RET_V_DIM]
            mu = jnp.mean(o_h, axis=-1, keepdims=True)
            oc = o_h - mu
            var = jnp.mean(oc * oc, axis=-1, keepdims=True)
            on = oc * lax.rsqrt(var + EPS) * rng_ref[:, hs]
            y_scr[rows, LRU_WIDTH + h * RET_V_DIM:LRU_WIDTH + (h + 1) * RET_V_DIM] = (
                on * _silu(rg_scr[rows, hs])).astype(bf16)
        kv = _dot_t0(k_d, v_pair)
        st_scr[p] = gst_ref[p] * st_pair + bdm_ref[...] * kv

    lx_scr[0:CONV_HIST, :] = lxtail_scr[...]
    _conv_blocked(lx_scr, convw_ref, convb_ref, xc_scr, xcb_scr)
    lxtail_scr[...] = lx_scr[pl.ds(tt, CONV_HIST), :]
    for h in range(LRU_HEADS):
        lanes = slice(h * LRU_BLOCK, (h + 1) * LRU_BLOCK)
        ilanes = slice(LRU_WIDTH + h * LRU_BLOCK, LRU_WIDTH + (h + 1) * LRU_BLOCK)
        _gates_head(h, xcb_scr, wrg_ref, wig_ref, pre_scr)
        after_gates(h)
        full = (CHUNK, LRU_BLOCK)
        c_sp = jnp.broadcast_to(0.5 * LRU_C * jax.nn.softplus(-lam_ref[:, lanes]), full)
        brg, big = jnp.broadcast_to(0.5 * brg_ref[:, lanes], full), jnp.broadcast_to(0.5 * big_ref[:, lanes], full)
        carry = h_scr[:, lanes]
        for c in range(tt // CHUNK):
            rows = slice(c * CHUNK, (c + 1) * CHUNK)
            hb, carry = _lru_chunk_blocked(xc_scr[rows, lanes], pre_scr[rows, lanes], pre_scr[rows, ilanes],
                                           brg, big, c_sp, carry)
            y_scr[rows, lanes] = (hb * _silu(gate_scr[rows, lanes])).astype(bf16)
        h_scr[:, lanes] = carry

    for c in range(tt // CHUNK):
        for p in range(N_PAIRS):
            retention_unit(c, p)

    res = load_x() + jnp.dot(y_scr[...], wout_ref[...].astype(bf16), preferred_element_type=f32)
    return _rmsnorm_rows(res, fng_ref[...])


X_SLOTS = 3
OUT_SLOTS = 2
ROW_GROUPS = TILE_T // SUBLANES


def _tile_block_copies(hbm_ref, vmem_ref, sem_ref, tile, slot, n_tiles, to_hbm):
    b = lax.div(tile, n_tiles)
    first_row = (tile - b * n_tiles) * TILE_T
    copies = []
    for c in range(TILE_T // CHUNK):
        for s in range(SUBLANES):
            hbm = hbm_ref.at[b, pl.ds(first_row + c * CHUNK + s * BLOCK_STEPS, BLOCK_STEPS), :]
            vmem = vmem_ref.at[slot, pl.ds(c * BLOCK_STEPS, BLOCK_STEPS), s, :]
            src, dst = (vmem, hbm) if to_hbm else (hbm, vmem)
            copies.append(pltpu.make_async_copy(src, dst, sem_ref.at[slot]))
    return copies


def _main_kernel(n_tiles, n_total,
                 x_hbm, cos_ref, sin_ref, g1_ref, win_ref, convw_ref, convb_ref, wrg_ref, wig_ref,
                 brg_ref, big_ref, lam_ref, rng_ref, wout_ref, fng_ref, dmaskp_ref, kdec_ref, qdec_ref, even_ref, odd_ref,
                 gst_ref, bdm_ref, meta_ref, cos_meta_ref, sin_meta_ref, kdec_time_ref,
                 o_hbm,
                 u_scr,
                 lx_a, gate_a, q_a, k_a, v_a, rg_a,
                 lx_b, gate_b, q_b, k_b, v_b, rg_b,
                 xc_scr, xcb_scr, pre_scr, y_scr, lxtail_scr, h_scr, st_scr,
                 lx0_ref, h0_ref, st0_ref,
                 xin_scr, out_scr, xin_sem, out_sem):
    g = pl.program_id(0)
    slots = ((lx_a, gate_a, q_a, k_a, v_a, rg_a), (lx_b, gate_b, q_b, k_b, v_b, rg_b))
    x_slot = lambda tile: lax.rem(tile + X_SLOTS, X_SLOTS)
    out_slot = lambda tile: lax.rem(tile + OUT_SLOTS, OUT_SLOTS)
    x_tile = lambda tile: xin_scr[x_slot(tile)].reshape(TILE_T, D_MODEL)
    fetch = lambda tile: _tile_block_copies(x_hbm, xin_scr, xin_sem, jnp.minimum(tile, n_total - 1), x_slot(tile),
                                            n_tiles, to_hbm=False)
    write_back = lambda tile: _tile_block_copies(o_hbm, out_scr, out_sem, tile, out_slot(tile), n_tiles, to_hbm=True)

    @pl.when(g >= 1 + OUT_SLOTS)
    def _():
        for copy in write_back(g - 1 - OUT_SLOTS):
            copy.wait()

    @pl.when(lax.rem(g + n_tiles - 1, n_tiles) == 0)
    def _():
        lxtail_scr[...] = lx0_ref[...]
        h_scr[...] = h0_ref[...]
        st_scr[...] = st0_ref[...]

    tiles_after_head = (4, 4, 3, 3, 2, 2, 1, 1)
    assert sum(tiles_after_head) == N_IN_TILES and len(tiles_after_head) == LRU_HEADS

    def finish(slot_finish, after_gates):
        out = _finish_tile(lambda: x_tile(g - 1), cos_ref, sin_ref, convw_ref, convb_ref, wrg_ref, wig_ref, brg_ref, big_ref,
                           lam_ref, rng_ref, wout_ref, fng_ref, dmaskp_ref, kdec_ref, qdec_ref, even_ref, odd_ref, gst_ref,
                           bdm_ref, slot_finish, xc_scr, xcb_scr, pre_scr, y_scr, lxtail_scr, h_scr, st_scr, after_gates)
        out_scr[out_slot(g - 1)] = out.reshape(ROW_GROUPS, SUBLANES, D_MODEL)
        for copy in write_back(g - 1):
            copy.start()

    def step(slot_project, slot_finish):
        for copy in fetch(g):
            copy.wait()
        for copy in fetch(g + 1):
            copy.start()
        _norm_tile(x_tile(g), g1_ref, u_scr)

        def after_gates(h):
            first = sum(tiles_after_head[:h])
            for t in range(first, first + tiles_after_head[h]):
                _project_col_tile(t, u_scr, win_ref, slot_project)

        finish(slot_finish, after_gates)

    @pl.when(g == 0)
    def _():
        for copy in fetch(g):
            copy.start()
        for copy in fetch(g):
            copy.wait()
        for copy in fetch(g + 1):
            copy.start()
        _norm_tile(x_tile(g), g1_ref, u_scr)
        for t in range(N_IN_TILES):
            _project_col_tile(t, u_scr, win_ref, slots[0])
        meta_rows = lambda ref, n: ref.at[pl.ds(0, n)]
        _meta_states(meta_ref, cos_meta_ref, sin_meta_ref, g1_ref,
                     win_ref.at[:, pl.ds(OFF_LX, LRU_WIDTH)], win_ref.at[:, pl.ds(OFF_K, RET_QK_WIDTH)],
                     win_ref.at[:, pl.ds(OFF_V, RET_WIDTH)], convw_ref, convb_ref, wrg_ref, wig_ref, brg_ref, big_ref,
                     lam_ref, kdec_time_ref, bdm_ref, lx0_ref, h0_ref, st0_ref,
                     meta_rows(slots[1][0], SUBLANES + N_META), meta_rows(xc_scr, N_META), meta_rows(xcb_scr, N_META),
                     meta_rows(pre_scr, N_META))

    regular = jnp.logical_and(g >= 1, g < n_total)

    @pl.when(jnp.logical_and(regular, lax.rem(g, 2) == 0))
    def _():
        step(slots[0], slots[1])

    @pl.when(jnp.logical_and(regular, lax.rem(g, 2) == 1))
    def _():
        step(slots[1], slots[0])

    @pl.when(g == n_total)
    def _():
        for copy in fetch(g):
            copy.wait()
        finish(slots[(n_total - 1) % 2], lambda h: None)
        for tile in (g - 2, g - 1):
            for copy in write_back(tile):
                copy.wait()


def _const_spec(shape):
    nd = len(shape)
    return pl.BlockSpec(shape, lambda g, _nd=nd: (0,) * _nd, pipeline_mode=pl.Buffered(1))


def kernel(x, meta_tokens, norm_gain, w_in, conv_w, conv_b, w_rg, b_rg, w_ig, b_ig,
           lru_lambda, ret_norm_gain, w_out, final_norm_gain):
    B, S, D = x.shape
    assert D == D_MODEL and S % TILE_T == 0 and TILE_T % CHUNK == 0
    assert norm_gain.shape[0] == 1, "single-layer block"
    assert meta_tokens.shape == (N_META, D_MODEL)

    assert w_in.shape == (1, D_MODEL, IN_WIDTH) and w_out.shape == (1, MIX_WIDTH, D_MODEL)
    assert w_rg.shape == w_ig.shape == (1, LRU_HEADS, LRU_BLOCK, LRU_BLOCK)

    g1 = norm_gain[0].reshape(1, D_MODEL)
    fng = final_norm_gain.reshape(1, D_MODEL)
    convw = conv_w[0]
    convb = conv_b[0].reshape(1, LRU_WIDTH)
    brg = b_rg[0].reshape(1, LRU_WIDTH)
    big = b_ig[0].reshape(1, LRU_WIDTH)
    lam = lru_lambda[0].reshape(1, LRU_WIDTH)
    rng = ret_norm_gain[0].reshape(1, RET_WIDTH)

    k_dec, dmask_blocked, k_dec_blocked, q_dec_blocked, even, odd, g_state, bd_mask = _retention_tables()
    cos_t, sin_t = _rotary_tables(N_META + S)
    cos_m, sin_m = jnp.asarray(cos_t[:N_META]), jnp.asarray(sin_t[:N_META])
    cos_blocked = jnp.asarray(_chunk_rows_blocked(cos_t[N_META:]))
    sin_blocked = jnp.asarray(_chunk_rows_blocked(sin_t[N_META:]))

    tt = TILE_T
    n_tiles = S // tt
    n_total = B * n_tiles

    rot_spec = pl.BlockSpec((tt, LANES), lambda g: (jnp.maximum(g - 1, 0) % n_tiles, 0))
    in_specs = [
        pl.BlockSpec(memory_space=pl.ANY),
        rot_spec, rot_spec,
        _const_spec((1, D_MODEL)),
        _const_spec((None, D_MODEL, IN_WIDTH)),
        _const_spec((CONV_WIDTH, LRU_WIDTH)), _const_spec((1, LRU_WIDTH)),
        _const_spec((None, LRU_HEADS, LRU_BLOCK, LRU_BLOCK)),
        _const_spec((None, LRU_HEADS, LRU_BLOCK, LRU_BLOCK)),
        _const_spec((1, LRU_WIDTH)), _const_spec((1, LRU_WIDTH)), _const_spec((1, LRU_WIDTH)),
        _const_spec((1, RET_WIDTH)),
        _const_spec((None, MIX_WIDTH, D_MODEL)),
        _const_spec((1, D_MODEL)),
        _const_spec((N_PAIRS, CHUNK, 2 * CHUNK)),
        _const_spec((CHUNK, RET_QK_WIDTH)), _const_spec((CHUNK, RET_QK_WIDTH)),
        _const_spec((1, RET_QK_WIDTH)), _const_spec((1, RET_QK_WIDTH)),
        _const_spec((N_PAIRS, LANES, PAIR_V)),
        _const_spec((LANES, PAIR_V)),
        _const_spec((N_META, D_MODEL)),
        _const_spec((N_META, LANES)), _const_spec((N_META, LANES)),
        _const_spec((CHUNK, RET_QK_WIDTH)),
    ]
    slot_scratch = [
        pltpu.VMEM((CONV_HIST + tt, LRU_WIDTH), f32),
        pltpu.VMEM((tt, LRU_WIDTH), f32),
        pltpu.VMEM((tt, RET_QK_WIDTH), f32),
        pltpu.VMEM((tt, RET_QK_WIDTH), f32),
        pltpu.VMEM((tt, RET_WIDTH), bf16),
        pltpu.VMEM((tt, RET_WIDTH), f32),
    ]
    scratch = [pltpu.VMEM((tt, D_MODEL), bf16)] + slot_scratch + slot_scratch + [
        pltpu.VMEM((tt, LRU_WIDTH), f32),
        pltpu.VMEM((tt, LRU_WIDTH), bf16),
        pltpu.VMEM((tt, 2 * LRU_WIDTH), f32),
        pltpu.VMEM((tt, MIX_WIDTH), bf16),
        pltpu.VMEM((CONV_HIST, LRU_WIDTH), f32),
        pltpu.VMEM((SUBLANES, LRU_WIDTH), f32),
        pltpu.VMEM((N_PAIRS, LANES, PAIR_V), f32),
        pltpu.VMEM((CONV_HIST, LRU_WIDTH), f32),
        pltpu.VMEM((SUBLANES, LRU_WIDTH), f32),
        pltpu.VMEM((N_PAIRS, LANES, PAIR_V), f32),
        pltpu.VMEM((X_SLOTS, ROW_GROUPS, SUBLANES, D_MODEL), f32),
        pltpu.VMEM((OUT_SLOTS, ROW_GROUPS, SUBLANES, D_MODEL), f32),
        pltpu.SemaphoreType.DMA((X_SLOTS,)),
        pltpu.SemaphoreType.DMA((OUT_SLOTS,)),
    ]
    out = pl.pallas_call(
        functools.partial(_main_kernel, n_tiles, n_total),
        grid=(n_total + 1,),
        in_specs=in_specs,
        out_specs=pl.BlockSpec(memory_space=pl.ANY),
        out_shape=jax.ShapeDtypeStruct((B, S, D_MODEL), x.dtype),
        scratch_shapes=scratch,
        compiler_params=pltpu.CompilerParams(
            dimension_semantics=("arbitrary",),
            vmem_limit_bytes=VMEM_LIMIT_BYTES),
        name="hybrid_main",
    )(x, cos_blocked, sin_blocked, g1, w_in, convw, convb, w_rg, w_ig, brg, big, lam, rng, w_out, fng,
      dmask_blocked, k_dec_blocked, q_dec_blocked, even, odd, g_state, bd_mask, meta_tokens, cos_m, sin_m, k_dec)
    return out
```

```python
import functools

import numpy as np
import jax
import jax.numpy as jnp
from jax import lax
from jax.experimental import pallas as pl
from jax.experimental.pallas import tpu as pltpu

f32 = jnp.float32
bf16 = jnp.bfloat16

D_MODEL = 1024
N_META = 16
LRU_WIDTH = 1024
LRU_HEADS = 8
LRU_BLOCK = 128
CONV_WIDTH = 4
LRU_C = 8.0
RET_HEADS = 8
RET_QK_DIM = 64
RET_V_DIM = 128
RET_QK_WIDTH = 512
RET_WIDTH = 1024
CHUNK = 128
ROPE_BASE = 10000.0
MIX_WIDTH = 2048
EPS = 1e-6
QK_SCALE = RET_QK_DIM ** -0.5

OFF_LX, OFF_GATE, OFF_Q, OFF_K, OFF_V, OFF_RG = 0, 1024, 2048, 2560, 3072, 4096
IN_WIDTH = 5120

LANES = 128
SUBLANES = 8
N_PAIRS = RET_HEADS // 2
PAIR_V = 2 * RET_V_DIM
TILE_T = 256
VMEM_LIMIT_BYTES = 60000 * 1024


def _lane_head():
    return np.arange(RET_QK_WIDTH) // RET_QK_DIM


BLOCK_STEPS = CHUNK // SUBLANES
CHUNK_ROW_TIME = np.arange(CHUNK).reshape(SUBLANES, BLOCK_STEPS).T.reshape(-1)


def _chunk_rows_blocked(table):
    n = table.shape[0]
    return table.reshape((n // CHUNK, CHUNK) + table.shape[1:])[:, CHUNK_ROW_TIME].reshape(table.shape)


def _retention_tables():
    log_g = np.log1p(-np.exp2(-5.0 - np.arange(RET_HEADS, dtype=np.float32))).astype(np.float32)
    idx = np.arange(CHUNK, dtype=np.float32)
    diff = idx[:, None] - idx[None, :]
    dmask = np.where(diff[None] >= 0.0, np.exp(np.maximum(diff, 0.0)[None] * log_g[:, None, None]), 0.0)
    dmask_pair = np.concatenate([dmask[0::2], dmask[1::2]], axis=-1)
    lg_lane = log_g[_lane_head()]
    k_dec = np.exp((CHUNK - 1.0 - idx)[:, None] * lg_lane[None, :])
    q_dec = np.exp((idx + 1.0)[:, None] * lg_lane[None, :]) * QK_SCALE
    even = (_lane_head() % 2 == 0)[None, :]
    g_chunk = np.exp(CHUNK * log_g)
    g_state = np.broadcast_to(np.repeat(g_chunk, RET_V_DIM).reshape(N_PAIRS, 1, PAIR_V), (N_PAIRS, LANES, PAIR_V))
    row_par = np.arange(LANES) // RET_QK_DIM
    col_par = np.arange(PAIR_V) // RET_V_DIM
    bd_mask = row_par[:, None] == col_par[None, :]
    key_order = np.concatenate([CHUNK_ROW_TIME, CHUNK + CHUNK_ROW_TIME])
    dmask_blocked = dmask_pair[:, CHUNK_ROW_TIME][:, :, key_order]
    as_f32 = lambda a: jnp.asarray(np.asarray(a, np.float32))
    return tuple(as_f32(t) for t in (k_dec, dmask_blocked, k_dec[CHUNK_ROW_TIME], q_dec[CHUNK_ROW_TIME],
                                     even, ~even, g_state, bd_mask))


def _rotary_tables(n_pos):
    half = RET_QK_DIM // 2
    inv = (np.float32(ROPE_BASE) ** (-np.arange(half, dtype=np.float32) / half)).astype(np.float32)
    ang = np.arange(n_pos).astype(np.float32)[:, None] * inv[None, :]
    cos, sin = np.cos(ang), np.sin(ang)
    cos_t = np.concatenate([cos, cos, cos, cos], axis=-1).astype(np.float32)
    sin_t = np.concatenate([-sin, sin, -sin, sin], axis=-1).astype(np.float32)
    return cos_t, sin_t


def _rmsnorm_rows(x, gain_row):
    ms = jnp.mean(x * x, axis=-1, keepdims=True)
    return x * lax.rsqrt(ms + EPS) * gain_row


def _silu(x):
    hx = 0.5 * x
    return hx * jnp.tanh(hx) + hx


def _half_conv_params(convw_ref, convb_ref):
    return [0.5 * convw_ref[k:k + 1, :] for k in range(CONV_WIDTH)], 0.5 * convb_ref[...]


def _conv(n_rows, lx_scr, convw_ref, convb_ref, xc_scr, xcb_scr):
    base = SUBLANES
    taps, bias = _half_conv_params(convw_ref, convb_ref)
    xc = bias + taps[3] * lx_scr[pl.ds(base, n_rows), :]
    xc = xc + taps[2] * lx_scr[pl.ds(base - 1, n_rows), :]
    xc = xc + taps[1] * lx_scr[pl.ds(base - 2, n_rows), :]
    xc = xc + taps[0] * lx_scr[pl.ds(base - 3, n_rows), :]
    xc_scr[...] = xc
    xcb_scr[...] = xc.astype(bf16)


CONV_HIST = (CONV_WIDTH - 1) * SUBLANES


def _conv_blocked(lx_scr, convw_ref, convb_ref, xc_scr, xcb_scr):
    taps, bias = _half_conv_params(convw_ref, convb_ref)
    last_sublane = lax.broadcasted_iota(jnp.int32, (SUBLANES, LRU_WIDTH), 0) == SUBLANES - 1
    for c in range(TILE_T // CHUNK):
        base = CONV_HIST + c * CHUNK
        rows = lambda group, n: lx_scr[base + group * SUBLANES:base + (group + n) * SUBLANES, :]
        wrapped = []
        for k in range(1, CONV_WIDTH):
            own, prev = rows(BLOCK_STEPS - k, 1), rows(-k, 1)
            wrapped.append(pltpu.roll(jnp.where(last_sublane, prev, own), 1, 0))
        xc = bias + taps[CONV_WIDTH - 1] * rows(0, BLOCK_STEPS)
        for k in range(1, CONV_WIDTH):
            shifted = jnp.concatenate(wrapped[:k][::-1] + [rows(0, BLOCK_STEPS - k)], axis=0)
            xc = xc + taps[CONV_WIDTH - 1 - k] * shifted
        xc_scr[c * CHUNK:(c + 1) * CHUNK, :] = xc
        xcb_scr[c * CHUNK:(c + 1) * CHUNK, :] = xc.astype(bf16)


def _gates_head(h, xcb_scr, wrg_ref, wig_ref, pre_scr):
    lanes = slice(h * LRU_BLOCK, (h + 1) * LRU_BLOCK)
    wg = jnp.concatenate([wrg_ref[h], wig_ref[h]], axis=-1).astype(bf16)
    pre = jnp.dot(xcb_scr[:, lanes], wg, preferred_element_type=f32)
    pre_scr[:, lanes] = pre[:, :LRU_BLOCK]
    pre_scr[:, LRU_WIDTH + h * LRU_BLOCK:LRU_WIDTH + (h + 1) * LRU_BLOCK] = pre[:, LRU_BLOCK:]


def _lru_maps(half_xc, half_pre_r, half_pre_i, half_brg, half_big, half_c_sp):
    nl = jnp.tanh(half_pre_r + half_brg) * half_c_sp + half_c_sp
    twice_i = jnp.tanh(half_pre_i + half_big) + 1.0
    a = jnp.exp(-nl)
    z = jnp.tanh(nl) * (1.0 + a * a)
    beta = jnp.where(z > 0.0, z * lax.rsqrt(z), 0.0)
    return a, beta * twice_i * half_xc


def _scan_sublanes(a, b):
    rowid = lax.broadcasted_iota(jnp.int32, a.shape, 0)
    for s in (1, 2, 4):
        keep = rowid >= s
        a_s = jnp.where(keep, pltpu.roll(a, s, 0), 1.0)
        b_s = jnp.where(keep, pltpu.roll(b, s, 0), 0.0)
        b = a * b_s + b
        a = a * a_s
    return a, b


def _lru_chunk_blocked(xc, pre_r, pre_i, brg, big, c_sp, carry):
    a, b = _lru_maps(xc, pre_r, pre_i, brg, big, c_sp)
    width = xc.shape[1]
    a3 = a.reshape(BLOCK_STEPS, SUBLANES, width)
    b3 = b.reshape(BLOCK_STEPS, SUBLANES, width)
    decay, local = [a3[0]], [b3[0]]
    for j in range(1, BLOCK_STEPS):
        local.append(a3[j] * local[-1] + b3[j])
        decay.append(a3[j] * decay[-1])
    a_blocks, b_blocks = _scan_sublanes(decay[-1], local[-1])
    after = a_blocks * carry + b_blocks
    rowid = lax.broadcasted_iota(jnp.int32, carry.shape, 0)
    before = jnp.where(rowid >= 1, pltpu.roll(after, 1, 0), carry)
    hs = [local[j] + decay[j] * before for j in range(BLOCK_STEPS)]
    new_carry = jnp.broadcast_to(after[SUBLANES - 1:SUBLANES, :], carry.shape)
    return jnp.concatenate(hs, axis=0), new_carry


def _lru_block(xc, pre_r, pre_i, brg, big, c_sp, carry):
    rows, width = xc.shape
    a, b = _lru_maps(xc, pre_r, pre_i, brg, big, c_sp)
    nv = rows // SUBLANES
    a3 = a.reshape(nv, SUBLANES, width)
    b3 = b.reshape(nv, SUBLANES, width)
    rowid = lax.broadcasted_iota(jnp.int32, (nv, SUBLANES, width), 1)
    for s in (1, 2, 4):
        keep = rowid >= s
        a_s = jnp.where(keep, pltpu.roll(a3, s, 1), 1.0)
        b_s = jnp.where(keep, pltpu.roll(b3, s, 1), 0.0)
        b3 = a3 * b_s + b3
        a3 = a3 * a_s
    hs = []
    for v in range(nv):
        h_v = a3[v] * carry + b3[v]
        carry = h_v[SUBLANES - 1:SUBLANES, :]
        hs.append(h_v)
    return jnp.concatenate(hs, axis=0), carry


def _rotary(t, cos, sin):
    half = RET_QK_DIM // 2
    in_first_half = (lax.broadcasted_iota(jnp.int32, t.shape, 1) // half) % 2 == 0
    partner = jnp.where(in_first_half, pltpu.roll(t, LANES - half, 1), pltpu.roll(t, half, 1))
    return t * cos + partner * sin


def _dot_t0(a, b):
    return lax.dot_general(a, b, (((0,), (0,)), ((), ())), preferred_element_type=f32)


def _dot_nt(a, b):
    return lax.dot_general(a, b, (((1,), (1,)), ((), ())), preferred_element_type=f32)


def _meta_states(meta_ref, cos_ref, sin_ref, g1_ref, wlx_ref, wk_ref, wv_ref, convw_ref, convb_ref,
                 wrg_ref, wig_ref, brg_ref, big_ref, lam_ref, kdec_ref, bdm_ref,
                 lx_out, h_out, st_out,
                 lx_scr, xc_scr, xcb_scr, pre_scr):
    u = _rmsnorm_rows(meta_ref[...], g1_ref[...]).astype(bf16)
    lx_scr[0:SUBLANES, :] = jnp.zeros((SUBLANES, LRU_WIDTH), f32)
    lx_scr[SUBLANES:SUBLANES + N_META, :] = jnp.dot(u, wlx_ref[...].astype(bf16), preferred_element_type=f32)
    k = jnp.dot(u, wk_ref[...].astype(bf16), preferred_element_type=f32)
    v = jnp.dot(u, wv_ref[...].astype(bf16), preferred_element_type=f32).astype(bf16)

    _conv(N_META, lx_scr, convw_ref, convb_ref, xc_scr, xcb_scr)
    for h in range(LRU_HEADS):
        _gates_head(h, xcb_scr, wrg_ref, wig_ref, pre_scr)
    for back in range(1, CONV_WIDTH):
        row = lx_scr[SUBLANES + N_META - back:SUBLANES + N_META - back + 1, :]
        group = CONV_WIDTH - 1 - back
        lx_out[group * SUBLANES:(group + 1) * SUBLANES, :] = jnp.broadcast_to(row, (SUBLANES, LRU_WIDTH))
    half_c_sp = 0.5 * LRU_C * jax.nn.softplus(-lam_ref[...])
    _, carry = _lru_block(xc_scr[...], pre_scr[:, :LRU_WIDTH], pre_scr[:, LRU_WIDTH:],
                          0.5 * brg_ref[...], 0.5 * big_ref[...], half_c_sp, jnp.zeros((1, LRU_WIDTH), f32))
    h_out[...] = jnp.broadcast_to(carry, (SUBLANES, LRU_WIDTH))

    cos, sin = cos_ref[...], sin_ref[...]
    for p in range(N_PAIRS):
        sl = slice(p * LANES, (p + 1) * LANES)
        k_rot = _rotary(k[:, sl], cos, sin)
        kd = (k_rot * kdec_ref[CHUNK - N_META:CHUNK, sl]).astype(bf16)
        st_out[p] = _dot_t0(kd, v[:, p * PAIR_V:(p + 1) * PAIR_V]) * bdm_ref[...]


def _norm_tile(x_tile, g1_ref, u_scr):
    u_scr[...] = _rmsnorm_rows(x_tile, g1_ref[...]).astype(bf16)


MXU_COLS = 256
N_IN_TILES = IN_WIDTH // MXU_COLS


def _project_col_tile(t, u_scr, win_ref, slot):
    lx_scr, gate_scr, q_scr, k_scr, v_scr, rg_scr = slot
    off = t * MXU_COLS
    res = jnp.dot(u_scr[...], win_ref[:, off:off + MXU_COLS].astype(bf16), preferred_element_type=f32)
    for dst, start, rows, dt in ((lx_scr, OFF_LX, slice(CONV_HIST, CONV_HIST + TILE_T), f32), (gate_scr, OFF_GATE, slice(None), f32),
                                 (q_scr, OFF_Q, slice(None), f32), (k_scr, OFF_K, slice(None), f32),
                                 (v_scr, OFF_V, slice(None), bf16), (rg_scr, OFF_RG, slice(None), f32)):
        width = dst.shape[1]
        if start <= off < start + width:
            dst[rows, off - start:off - start + MXU_COLS] = res.astype(dt)
            return
    raise AssertionError("column tile outside the projection")


def _finish_tile(load_x, cos_ref, sin_ref, convw_ref, convb_ref, wrg_ref, wig_ref, brg_ref, big_ref, lam_ref, rng_ref,
                 wout_ref, dmaskp_ref, kdec_ref, qdec_ref, even_ref, odd_ref, gst_ref, bdm_ref,
                 slot, xc_scr, xcb_scr, pre_scr, y_scr, lxtail_scr, h_scr, st_scr, after_gates):
    lx_scr, gate_scr, q_scr, k_scr, v_scr, rg_scr = slot
    tt = TILE_T

    def retention_unit(c, p):
        rows = slice(c * CHUNK, (c + 1) * CHUNK)
        cos, sin = cos_ref[rows, :], sin_ref[rows, :]
        sl = slice(p * LANES, (p + 1) * LANES)
        q_rot = _rotary(q_scr[rows, sl], cos, sin)
        k_rot = _rotary(k_scr[rows, sl], cos, sin)
        q_b = (q_rot * QK_SCALE).astype(bf16)
        q_d = (q_rot * qdec_ref[:, sl]).astype(bf16)
        k_d = (k_rot * kdec_ref[:, sl]).astype(bf16)
        k_b = k_rot.astype(bf16)
        kk = jnp.concatenate([k_b * even_ref[:, sl].astype(bf16), k_b * odd_ref[:, sl].astype(bf16)], axis=0)
        v_pair = v_scr[rows, p * PAIR_V:(p + 1) * PAIR_V]
        zeros = jnp.zeros((CHUNK, RET_V_DIM), bf16)
        v_bd = jnp.concatenate([jnp.concatenate([v_pair[:, :RET_V_DIM], zeros], axis=1),
                                jnp.concatenate([zeros, v_pair[:, RET_V_DIM:]], axis=1)], axis=0)
        st_pair = st_scr[p]
        s = _dot_nt(q_b, kk) * dmaskp_ref[p]
        o = jnp.dot(jnp.concatenate([s.astype(bf16), q_d], axis=1),
                    jnp.concatenate([v_bd, st_pair.astype(bf16)], axis=0), preferred_element_type=f32)
        for e in range(2):
            h = 2 * p + e
            hs = slice(h * RET_V_DIM, (h + 1) * RET_V_DIM)
            o_h = o[:, e * RET_V_DIM:(e + 1) * RET_V_DIM]
            mu = jnp.mean(o_h, axis=-1, keepdims=True)
            oc = o_h - mu
            var = jnp.mean(oc * oc, axis=-1, keepdims=True)
            on = oc * lax.rsqrt(var + EPS) * rng_ref[:, hs]
            y_scr[rows, LRU_WIDTH + h * RET_V_DIM:LRU_WIDTH + (h + 1) * RET_V_DIM] = (
                on * _silu(rg_scr[rows, hs])).astype(bf16)
        kv = _dot_t0(k_d, v_pair)
        st_scr[p] = gst_ref[p] * st_pair + bdm_ref[...] * kv

    lx_scr[0:CONV_HIST, :] = lxtail_scr[...]
    _conv_blocked(lx_scr, convw_ref, convb_ref, xc_scr, xcb_scr)
    lxtail_scr[...] = lx_scr[pl.ds(tt, CONV_HIST), :]
    for h in range(LRU_HEADS):
        lanes = slice(h * LRU_BLOCK, (h + 1) * LRU_BLOCK)
        ilanes = slice(LRU_WIDTH + h * LRU_BLOCK, LRU_WIDTH + (h + 1) * LRU_BLOCK)
        _gates_head(h, xcb_scr, wrg_ref, wig_ref, pre_scr)
        after_gates(h)
        full = (CHUNK, LRU_BLOCK)
        c_sp = jnp.broadcast_to(0.5 * LRU_C * jax.nn.softplus(-lam_ref[:, lanes]), full)
        brg, big = jnp.broadcast_to(0.5 * brg_ref[:, lanes], full), jnp.broadcast_to(0.5 * big_ref[:, lanes], full)
        carry = h_scr[:, lanes]
        for c in range(tt // CHUNK):
            rows = slice(c * CHUNK, (c + 1) * CHUNK)
            hb, carry = _lru_chunk_blocked(xc_scr[rows, lanes], pre_scr[rows, lanes], pre_scr[rows, ilanes],
                                           brg, big, c_sp, carry)
            y_scr[rows, lanes] = (hb * _silu(gate_scr[rows, lanes])).astype(bf16)
        h_scr[:, lanes] = carry

    for c in range(tt // CHUNK):
        for p in range(N_PAIRS):
            retention_unit(c, p)

    return load_x() + jnp.dot(y_scr[...], wout_ref[...].astype(bf16), preferred_element_type=f32)


X_SLOTS = 3
OUT_SLOTS = 2
ROW_GROUPS = TILE_T // SUBLANES


def _tile_block_copies(hbm_ref, vmem_ref, sem_ref, tile, slot, n_tiles, to_hbm):
    b = lax.div(tile, n_tiles)
    first_row = (tile - b * n_tiles) * TILE_T
    copies = []
    for c in range(TILE_T // CHUNK):
        for s in range(SUBLANES):
            hbm = hbm_ref.at[b, pl.ds(first_row + c * CHUNK + s * BLOCK_STEPS, BLOCK_STEPS), :]
            vmem = vmem_ref.at[slot, pl.ds(c * BLOCK_STEPS, BLOCK_STEPS), s, :]
            src, dst = (vmem, hbm) if to_hbm else (hbm, vmem)
            copies.append(pltpu.make_async_copy(src, dst, sem_ref.at[slot]))
    return copies


def _main_kernel(n_tiles, n_total,
                 x_hbm, cos_ref, sin_ref, g1_ref, win_ref, convw_ref, convb_ref, wrg_ref, wig_ref,
                 brg_ref, big_ref, lam_ref, rng_ref, wout_ref, fng_ref, dmaskp_ref, kdec_ref, qdec_ref, even_ref, odd_ref,
                 gst_ref, bdm_ref, meta_ref, cos_meta_ref, sin_meta_ref, kdec_time_ref,
                 o_hbm,
                 u_scr,
                 lx_a, gate_a, q_a, k_a, v_a, rg_a,
                 lx_b, gate_b, q_b, k_b, v_b, rg_b,
                 xc_scr, xcb_scr, pre_scr, y_scr, lxtail_scr, h_scr, st_scr,
                 lx0_ref, h0_ref, st0_ref, res_scr,
                 xin_scr, out_scr, xin_sem, out_sem):
    g = pl.program_id(0)
    slots = ((lx_a, gate_a, q_a, k_a, v_a, rg_a), (lx_b, gate_b, q_b, k_b, v_b, rg_b))
    x_slot = lambda tile: lax.rem(tile + X_SLOTS, X_SLOTS)
    out_slot = lambda tile: lax.rem(tile + OUT_SLOTS, OUT_SLOTS)
    x_tile = lambda tile: xin_scr[x_slot(tile)].reshape(TILE_T, D_MODEL)
    fetch = lambda tile: _tile_block_copies(x_hbm, xin_scr, xin_sem, jnp.minimum(tile, n_total - 1), x_slot(tile),
                                            n_tiles, to_hbm=False)
    write_back = lambda tile: _tile_block_copies(o_hbm, out_scr, out_sem, tile, out_slot(tile), n_tiles, to_hbm=True)

    @pl.when(g >= 2 + OUT_SLOTS)
    def _():
        for copy in write_back(g - 2 - OUT_SLOTS):
            copy.wait()

    @pl.when(lax.rem(g + n_tiles - 1, n_tiles) == 0)
    def _():
        lxtail_scr[...] = lx0_ref[...]
        h_scr[...] = h0_ref[...]
        st_scr[...] = st0_ref[...]

    tiles_after_head = (4, 4, 3, 3, 2, 2, 1, 1)
    assert sum(tiles_after_head) == N_IN_TILES and len(tiles_after_head) == LRU_HEADS

    def finish(slot_finish, after_gates):
        res_scr[...] = _finish_tile(
            lambda: x_tile(g - 1), cos_ref, sin_ref, convw_ref, convb_ref, wrg_ref, wig_ref, brg_ref, big_ref,
            lam_ref, rng_ref, wout_ref, dmaskp_ref, kdec_ref, qdec_ref, even_ref, odd_ref, gst_ref,
            bdm_ref, slot_finish, xc_scr, xcb_scr, pre_scr, y_scr, lxtail_scr, h_scr, st_scr, after_gates)

    def final_norm(tile):
        out = _rmsnorm_rows(res_scr[...], fng_ref[...])
        out_scr[out_slot(tile)] = out.reshape(ROW_GROUPS, SUBLANES, D_MODEL)

    def norm_and_write_back(tile):
        final_norm(tile)
        for copy in write_back(tile):
            copy.start()

    norm_after_head = 5

    def step(slot_project, slot_finish, norm_previous=True):
        for copy in fetch(g):
            copy.wait()
        for copy in fetch(g + 1):
            copy.start()
        _norm_tile(x_tile(g), g1_ref, u_scr)

        def after_gates(h):
            first = sum(tiles_after_head[:h])
            for t in range(first, first + tiles_after_head[h]):
                _project_col_tile(t, u_scr, win_ref, slot_project)
            if norm_previous and h == norm_after_head:
                final_norm(g - 2)

        finish(slot_finish, after_gates)
        if norm_previous:
            for copy in write_back(g - 2):
                copy.start()

    @pl.when(g == 0)
    def _():
        for copy in fetch(g):
            copy.start()
        for copy in fetch(g):
            copy.wait()
        for copy in fetch(g + 1):
            copy.start()
        _norm_tile(x_tile(g), g1_ref, u_scr)
        for t in range(N_IN_TILES):
            _project_col_tile(t, u_scr, win_ref, slots[0])
        meta_rows = lambda ref, n: ref.at[pl.ds(0, n)]
        _meta_states(meta_ref, cos_meta_ref, sin_meta_ref, g1_ref,
                     win_ref.at[:, pl.ds(OFF_LX, LRU_WIDTH)], win_ref.at[:, pl.ds(OFF_K, RET_QK_WIDTH)],
                     win_ref.at[:, pl.ds(OFF_V, RET_WIDTH)], convw_ref, convb_ref, wrg_ref, wig_ref, brg_ref, big_ref,
                     lam_ref, kdec_time_ref, bdm_ref, lx0_ref, h0_ref, st0_ref,
                     meta_rows(slots[1][0], SUBLANES + N_META), meta_rows(xc_scr, N_META), meta_rows(xcb_scr, N_META),
                     meta_rows(pre_scr, N_META))

    @pl.when(g == 1)
    def _():
        step(slots[1], slots[0], norm_previous=False)

    regular = jnp.logical_and(g >= 2, g < n_total)

    @pl.when(jnp.logical_and(regular, lax.rem(g, 2) == 0))
    def _():
        step(slots[0], slots[1])

    @pl.when(jnp.logical_and(regular, lax.rem(g, 2) == 1))
    def _():
        step(slots[1], slots[0])

    @pl.when(g == n_total)
    def _():
        for copy in fetch(g):
            copy.wait()
        norm_and_write_back(g - 2)
        finish(slots[(n_total - 1) % 2], lambda h: None)
        for copy in write_back(g - 1 - OUT_SLOTS):
            copy.wait()
        norm_and_write_back(g - 1)
        for tile in (g - 2, g - 1):
            for copy in write_back(tile):
                copy.wait()


def _const_spec(shape):
    nd = len(shape)
    return pl.BlockSpec(shape, lambda g, _nd=nd: (0,) * _nd, pipeline_mode=pl.Buffered(1))


def kernel(x, meta_tokens, norm_gain, w_in, conv_w, conv_b, w_rg, b_rg, w_ig, b_ig,
           lru_lambda, ret_norm_gain, w_out, final_norm_gain):
    B, S, D = x.shape
    assert D == D_MODEL and S % TILE_T == 0 and TILE_T % CHUNK == 0
    assert norm_gain.shape[0] == 1, "single-layer block"
    assert meta_tokens.shape == (N_META, D_MODEL)

    assert w_in.shape == (1, D_MODEL, IN_WIDTH) and w_out.shape == (1, MIX_WIDTH, D_MODEL)
    assert w_rg.shape == w_ig.shape == (1, LRU_HEADS, LRU_BLOCK, LRU_BLOCK)

    g1 = norm_gain[0].reshape(1, D_MODEL)
    fng = final_norm_gain.reshape(1, D_MODEL)
    convw = conv_w[0]
    convb = conv_b[0].reshape(1, LRU_WIDTH)
    brg = b_rg[0].reshape(1, LRU_WIDTH)
    big = b_ig[0].reshape(1, LRU_WIDTH)
    lam = lru_lambda[0].reshape(1, LRU_WIDTH)
    rng = ret_norm_gain[0].reshape(1, RET_WIDTH)

    k_dec, dmask_blocked, k_dec_blocked, q_dec_blocked, even, odd, g_state, bd_mask = _retention_tables()
    cos_t, sin_t = _rotary_tables(N_META + S)
    cos_m, sin_m = jnp.asarray(cos_t[:N_META]), jnp.asarray(sin_t[:N_META])
    cos_blocked = jnp.asarray(_chunk_rows_blocked(cos_t[N_META:]))
    sin_blocked = jnp.asarray(_chunk_rows_blocked(sin_t[N_META:]))

    tt = TILE_T
    n_tiles = S // tt
    n_total = B * n_tiles

    rot_spec = pl.BlockSpec((tt, LANES), lambda g: (jnp.maximum(g - 1, 0) % n_tiles, 0))
    in_specs = [
        pl.BlockSpec(memory_space=pl.ANY),
        rot_spec, rot_spec,
        _const_spec((1, D_MODEL)),
        _const_spec((None, D_MODEL, IN_WIDTH)),
        _const_spec((CONV_WIDTH, LRU_WIDTH)), _const_spec((1, LRU_WIDTH)),
        _const_spec((None, LRU_HEADS, LRU_BLOCK, LRU_BLOCK)),
        _const_spec((None, LRU_HEADS, LRU_BLOCK, LRU_BLOCK)),
        _const_spec((1, LRU_WIDTH)), _const_spec((1, LRU_WIDTH)), _const_spec((1, LRU_WIDTH)),
        _const_spec((1, RET_WIDTH)),
        _const_spec((None, MIX_WIDTH, D_MODEL)),
        _const_spec((1, D_MODEL)),
        _const_spec((N_PAIRS, CHUNK, 2 * CHUNK)),
        _const_spec((CHUNK, RET_QK_WIDTH)), _const_spec((CHUNK, RET_QK_WIDTH)),
        _const_spec((1, RET_QK_WIDTH)), _const_spec((1, RET_QK_WIDTH)),
        _const_spec((N_PAIRS, LANES, PAIR_V)),
        _const_spec((LANES, PAIR_V)),
        _const_spec((N_META, D_MODEL)),
        _const_spec((N_META, LANES)), _const_spec((N_META, LANES)),
        _const_spec((CHUNK, RET_QK_WIDTH)),
    ]
    slot_scratch = [
        pltpu.VMEM((CONV_HIST + tt, LRU_WIDTH), f32),
        pltpu.VMEM((tt, LRU_WIDTH), f32),
        pltpu.VMEM((tt, RET_QK_WIDTH), f32),
        pltpu.VMEM((tt, RET_QK_WIDTH), f32),
        pltpu.VMEM((tt, RET_WIDTH), bf16),
        pltpu.VMEM((tt, RET_WIDTH), f32),
    ]
    scratch = [pltpu.VMEM((tt, D_MODEL), bf16)] + slot_scratch + slot_scratch + [
        pltpu.VMEM((tt, LRU_WIDTH), f32),
        pltpu.VMEM((tt, LRU_WIDTH), bf16),
        pltpu.VMEM((tt, 2 * LRU_WIDTH), f32),
        pltpu.VMEM((tt, MIX_WIDTH), bf16),
        pltpu.VMEM((CONV_HIST, LRU_WIDTH), f32),
        pltpu.VMEM((SUBLANES, LRU_WIDTH), f32),
        pltpu.VMEM((N_PAIRS, LANES, PAIR_V), f32),
        pltpu.VMEM((CONV_HIST, LRU_WIDTH), f32),
        pltpu.VMEM((SUBLANES, LRU_WIDTH), f32),
        pltpu.VMEM((N_PAIRS, LANES, PAIR_V), f32),
        pltpu.VMEM((tt, D_MODEL), f32),
        pltpu.VMEM((X_SLOTS, ROW_GROUPS, SUBLANES, D_MODEL), f32),
        pltpu.VMEM((OUT_SLOTS, ROW_GROUPS, SUBLANES, D_MODEL), f32),
        pltpu.SemaphoreType.DMA((X_SLOTS,)),
        pltpu.SemaphoreType.DMA((OUT_SLOTS,)),
    ]
    out = pl.pallas_call(
        functools.partial(_main_kernel, n_tiles, n_total),
        grid=(n_total + 1,),
        in_specs=in_specs,
        out_specs=pl.BlockSpec(memory_space=pl.ANY),
        out_shape=jax.ShapeDtypeStruct((B, S, D_MODEL), x.dtype),
        scratch_shapes=scratch,
        compiler_params=pltpu.CompilerParams(
            dimension_semantics=("arbitrary",),
            vmem_limit_bytes=VMEM_LIMIT_BYTES),
        name="hybrid_main",
    )(x, cos_blocked, sin_blocked, g1, w_in, convw, convb, w_rg, w_ig, brg, big, lam, rng, w_out, fng,
      dmask_blocked, k_dec_blocked, q_dec_blocked, even, odd, g_state, bd_mask, meta_tokens, cos_m, sin_m, k_dec)
    return out
```

```python
import functools

import numpy as np
import jax
import jax.numpy as jnp
from jax import lax
from jax.experimental import pallas as pl
from jax.experimental.pallas import tpu as pltpu

f32 = jnp.float32
bf16 = jnp.bfloat16

D_MODEL = 1024
N_META = 16
LRU_WIDTH = 1024
LRU_HEADS = 8
LRU_BLOCK = 128
CONV_WIDTH = 4
LRU_C = 8.0
RET_HEADS = 8
RET_QK_DIM = 64
RET_V_DIM = 128
RET_QK_WIDTH = 512
RET_WIDTH = 1024
CHUNK = 128
ROPE_BASE = 10000.0
MIX_WIDTH = 2048
EPS = 1e-6
QK_SCALE = RET_QK_DIM ** -0.5

OFF_LX, OFF_GATE, OFF_Q, OFF_K, OFF_V, OFF_RG = 0, 1024, 2048, 2560, 3072, 4096
IN_WIDTH = 5120

LANES = 128
SUBLANES = 8
N_PAIRS = RET_HEADS // 2
PAIR_V = 2 * RET_V_DIM
TILE_T = 256
VMEM_LIMIT_BYTES = 60000 * 1024


def _lane_head():
    return np.arange(RET_QK_WIDTH) // RET_QK_DIM


BLOCK_STEPS = CHUNK // SUBLANES
CHUNK_ROW_TIME = np.arange(CHUNK).reshape(SUBLANES, BLOCK_STEPS).T.reshape(-1)


def _chunk_rows_blocked(table):
    n = table.shape[0]
    return table.reshape((n // CHUNK, CHUNK) + table.shape[1:])[:, CHUNK_ROW_TIME].reshape(table.shape)


def _retention_tables():
    log_g = np.log1p(-np.exp2(-5.0 - np.arange(RET_HEADS, dtype=np.float32))).astype(np.float32)
    idx = np.arange(CHUNK, dtype=np.float32)
    diff = idx[:, None] - idx[None, :]
    dmask = np.where(diff[None] >= 0.0, np.exp(np.maximum(diff, 0.0)[None] * log_g[:, None, None]), 0.0)
    dmask_pair = np.concatenate([dmask[0::2], dmask[1::2]], axis=-1)
    lg_lane = log_g[_lane_head()]
    k_dec = np.exp((CHUNK - 1.0 - idx)[:, None] * lg_lane[None, :])
    q_dec = np.exp((idx + 1.0)[:, None] * lg_lane[None, :]) * QK_SCALE
    even = (_lane_head() % 2 == 0)[None, :]
    g_chunk = np.exp(CHUNK * log_g)
    g_state = np.broadcast_to(np.repeat(g_chunk, RET_V_DIM).reshape(N_PAIRS, 1, PAIR_V), (N_PAIRS, LANES, PAIR_V))
    row_par = np.arange(LANES) // RET_QK_DIM
    col_par = np.arange(PAIR_V) // RET_V_DIM
    bd_mask = row_par[:, None] == col_par[None, :]
    key_order = np.concatenate([CHUNK_ROW_TIME, CHUNK + CHUNK_ROW_TIME])
    dmask_blocked = dmask_pair[:, CHUNK_ROW_TIME][:, :, key_order]
    as_f32 = lambda a: jnp.asarray(np.asarray(a, np.float32))
    return tuple(as_f32(t) for t in (k_dec, dmask_blocked, k_dec[CHUNK_ROW_TIME], q_dec[CHUNK_ROW_TIME],
                                     even, ~even, g_state, bd_mask))


def _rotary_tables(n_pos):
    half = RET_QK_DIM // 2
    inv = (np.float32(ROPE_BASE) ** (-np.arange(half, dtype=np.float32) / half)).astype(np.float32)
    ang = np.arange(n_pos).astype(np.float32)[:, None] * inv[None, :]
    cos, sin = np.cos(ang), np.sin(ang)
    cos_t = np.concatenate([cos, cos, cos, cos], axis=-1).astype(np.float32)
    sin_t = np.concatenate([-sin, sin, -sin, sin], axis=-1).astype(np.float32)
    return cos_t, sin_t


def _rmsnorm_rows(x, gain_row):
    ms = jnp.mean(x * x, axis=-1, keepdims=True)
    return x * lax.rsqrt(ms + EPS) * gain_row


def _silu(x):
    hx = 0.5 * x
    return hx * jnp.tanh(hx) + hx


def _half_conv_params(convw_ref, convb_ref):
    return [0.5 * convw_ref[k:k + 1, :] for k in range(CONV_WIDTH)], 0.5 * convb_ref[...]


def _conv(n_rows, lx_scr, convw_ref, convb_ref, xc_scr, xcb_scr):
    base = SUBLANES
    taps, bias = _half_conv_params(convw_ref, convb_ref)
    xc = bias + taps[3] * lx_scr[pl.ds(base, n_rows), :]
    xc = xc + taps[2] * lx_scr[pl.ds(base - 1, n_rows), :]
    xc = xc + taps[1] * lx_scr[pl.ds(base - 2, n_rows), :]
    xc = xc + taps[0] * lx_scr[pl.ds(base - 3, n_rows), :]
    xc_scr[...] = xc
    xcb_scr[...] = xc.astype(bf16)


CONV_HIST = (CONV_WIDTH - 1) * SUBLANES


def _conv_blocked(lx_scr, convw_ref, convb_ref, xc_scr, xcb_scr):
    taps, bias = _half_conv_params(convw_ref, convb_ref)
    last_sublane = lax.broadcasted_iota(jnp.int32, (SUBLANES, LRU_WIDTH), 0) == SUBLANES - 1
    for c in range(TILE_T // CHUNK):
        base = CONV_HIST + c * CHUNK
        rows = lambda group, n: lx_scr[base + group * SUBLANES:base + (group + n) * SUBLANES, :]
        wrapped = []
        for k in range(1, CONV_WIDTH):
            own, prev = rows(BLOCK_STEPS - k, 1), rows(-k, 1)
            wrapped.append(pltpu.roll(jnp.where(last_sublane, prev, own), 1, 0))
        xc = bias + taps[CONV_WIDTH - 1] * rows(0, BLOCK_STEPS)
        for k in range(1, CONV_WIDTH):
            shifted = jnp.concatenate(wrapped[:k][::-1] + [rows(0, BLOCK_STEPS - k)], axis=0)
            xc = xc + taps[CONV_WIDTH - 1 - k] * shifted
        xc_scr[c * CHUNK:(c + 1) * CHUNK, :] = xc
        xcb_scr[c * CHUNK:(c + 1) * CHUNK, :] = xc.astype(bf16)


def _gates_head(h, xcb_scr, wrg_ref, wig_ref, pre_scr):
    lanes = slice(h * LRU_BLOCK, (h + 1) * LRU_BLOCK)
    wg = jnp.concatenate([wrg_ref[h], wig_ref[h]], axis=-1).astype(bf16)
    pre = jnp.dot(xcb_scr[:, lanes], wg, preferred_element_type=f32)
    pre_scr[:, lanes] = pre[:, :LRU_BLOCK]
    pre_scr[:, LRU_WIDTH + h * LRU_BLOCK:LRU_WIDTH + (h + 1) * LRU_BLOCK] = pre[:, LRU_BLOCK:]


def _lru_maps(half_xc, half_pre_r, half_pre_i, half_brg, half_big, half_c_sp):
    nl = jnp.tanh(half_pre_r + half_brg) * half_c_sp + half_c_sp
    twice_i = jnp.tanh(half_pre_i + half_big) + 1.0
    a = jnp.exp(-nl)
    z = jnp.tanh(nl) * (1.0 + a * a)
    beta = jnp.where(z > 0.0, z * lax.rsqrt(z), 0.0)
    return a, beta * twice_i * half_xc


def _scan_sublanes(a, b):
    rowid = lax.broadcasted_iota(jnp.int32, a.shape, 0)
    for s in (1, 2, 4):
        keep = rowid >= s
        a_s = jnp.where(keep, pltpu.roll(a, s, 0), 1.0)
        b_s = jnp.where(keep, pltpu.roll(b, s, 0), 0.0)
        b = a * b_s + b
        a = a * a_s
    return a, b


def _lru_chunk_blocked(xc, pre_r, pre_i, brg, big, c_sp, carry):
    a, b = _lru_maps(xc, pre_r, pre_i, brg, big, c_sp)
    width = xc.shape[1]
    a3 = a.reshape(BLOCK_STEPS, SUBLANES, width)
    b3 = b.reshape(BLOCK_STEPS, SUBLANES, width)
    decay, local = [a3[0]], [b3[0]]
    for j in range(1, BLOCK_STEPS):
        local.append(a3[j] * local[-1] + b3[j])
        decay.append(a3[j] * decay[-1])
    a_blocks, b_blocks = _scan_sublanes(decay[-1], local[-1])
    after = a_blocks * carry + b_blocks
    rowid = lax.broadcasted_iota(jnp.int32, carry.shape, 0)
    before = jnp.where(rowid >= 1, pltpu.roll(after, 1, 0), carry)
    hs = [local[j] + decay[j] * before for j in range(BLOCK_STEPS)]
    new_carry = jnp.broadcast_to(after[SUBLANES - 1:SUBLANES, :], carry.shape)
    return jnp.concatenate(hs, axis=0), new_carry


def _lru_block(xc, pre_r, pre_i, brg, big, c_sp, carry):
    rows, width = xc.shape
    a, b = _lru_maps(xc, pre_r, pre_i, brg, big, c_sp)
    nv = rows // SUBLANES
    a3 = a.reshape(nv, SUBLANES, width)
    b3 = b.reshape(nv, SUBLANES, width)
    rowid = lax.broadcasted_iota(jnp.int32, (nv, SUBLANES, width), 1)
    for s in (1, 2, 4):
        keep = rowid >= s
        a_s = jnp.where(keep, pltpu.roll(a3, s, 1), 1.0)
        b_s = jnp.where(keep, pltpu.roll(b3, s, 1), 0.0)
        b3 = a3 * b_s + b3
        a3 = a3 * a_s
    hs = []
    for v in range(nv):
        h_v = a3[v] * carry + b3[v]
        carry = h_v[SUBLANES - 1:SUBLANES, :]
        hs.append(h_v)
    return jnp.concatenate(hs, axis=0), carry


def _rotary(t, cos, sin):
    half = RET_QK_DIM // 2
    in_first_half = (lax.broadcasted_iota(jnp.int32, t.shape, 1) // half) % 2 == 0
    partner = jnp.where(in_first_half, pltpu.roll(t, LANES - half, 1), pltpu.roll(t, half, 1))
    return t * cos + partner * sin


def _dot_t0(a, b):
    return lax.dot_general(a, b, (((0,), (0,)), ((), ())), preferred_element_type=f32)


def _dot_nt(a, b):
    return lax.dot_general(a, b, (((1,), (1,)), ((), ())), preferred_element_type=f32)


def _meta_states(meta_ref, cos_ref, sin_ref, g1_ref, wlx_ref, wk_ref, wv_ref, convw_ref, convb_ref,
                 wrg_ref, wig_ref, brg_ref, big_ref, lam_ref, kdec_ref, bdm_ref,
                 lx_out, h_out, st_out,
                 lx_scr, xc_scr, xcb_scr, pre_scr):
    u = _rmsnorm_rows(meta_ref[...], g1_ref[...]).astype(bf16)
    lx_scr[0:SUBLANES, :] = jnp.zeros((SUBLANES, LRU_WIDTH), f32)
    lx_scr[SUBLANES:SUBLANES + N_META, :] = jnp.dot(u, wlx_ref[...].astype(bf16), preferred_element_type=f32)
    k = jnp.dot(u, wk_ref[...].astype(bf16), preferred_element_type=f32)
    v = jnp.dot(u, wv_ref[...].astype(bf16), preferred_element_type=f32).astype(bf16)

    _conv(N_META, lx_scr, convw_ref, convb_ref, xc_scr, xcb_scr)
    for h in range(LRU_HEADS):
        _gates_head(h, xcb_scr, wrg_ref, wig_ref, pre_scr)
    for back in range(1, CONV_WIDTH):
        row = lx_scr[SUBLANES + N_META - back:SUBLANES + N_META - back + 1, :]
        group = CONV_WIDTH - 1 - back
        lx_out[group * SUBLANES:(group + 1) * SUBLANES, :] = jnp.broadcast_to(row, (SUBLANES, LRU_WIDTH))
    half_c_sp = 0.5 * LRU_C * jax.nn.softplus(-lam_ref[...])
    _, carry = _lru_block(xc_scr[...], pre_scr[:, :LRU_WIDTH], pre_scr[:, LRU_WIDTH:],
                          0.5 * brg_ref[...], 0.5 * big_ref[...], half_c_sp, jnp.zeros((1, LRU_WIDTH), f32))
    h_out[...] = jnp.broadcast_to(carry, (SUBLANES, LRU_WIDTH))

    cos, sin = cos_ref[...], sin_ref[...]
    for p in range(N_PAIRS):
        sl = slice(p * LANES, (p + 1) * LANES)
        k_rot = _rotary(k[:, sl], cos, sin)
        kd = (k_rot * kdec_ref[CHUNK - N_META:CHUNK, sl]).astype(bf16)
        st_out[p] = _dot_t0(kd, v[:, p * PAIR_V:(p + 1) * PAIR_V]) * bdm_ref[...]


def _norm_tile(x_tile, g1_ref, u_scr):
    u_scr[...] = _rmsnorm_rows(x_tile, g1_ref[...]).astype(bf16)


MXU_COLS = 256
N_IN_TILES = IN_WIDTH // MXU_COLS


def _project_col_tile(t, u_scr, win_ref, slot):
    lx_scr, gate_scr, q_scr, k_scr, v_scr, rg_scr = slot
    off = t * MXU_COLS
    res = jnp.dot(u_scr[...], win_ref[:, off:off + MXU_COLS].astype(bf16), preferred_element_type=f32)
    for dst, start, rows, dt in ((lx_scr, OFF_LX, slice(CONV_HIST, CONV_HIST + TILE_T), f32), (gate_scr, OFF_GATE, slice(None), f32),
                                 (q_scr, OFF_Q, slice(None), f32), (k_scr, OFF_K, slice(None), f32),
                                 (v_scr, OFF_V, slice(None), bf16), (rg_scr, OFF_RG, slice(None), f32)):
        width = dst.shape[1]
        if start <= off < start + width:
            dst[rows, off - start:off - start + MXU_COLS] = res.astype(dt)
            return
    raise AssertionError("column tile outside the projection")


def _finish_tile(load_x, cos_ref, sin_ref, convw_ref, convb_ref, wrg_ref, wig_ref, brg_ref, big_ref, lam_ref, rng_ref,
                 wout_ref, dmaskp_ref, kdec_ref, qdec_ref, even_ref, odd_ref, gst_ref, bdm_ref,
                 slot, xc_scr, xcb_scr, pre_scr, y_scr, lxtail_scr, h_scr, st_scr, after_gates):
    lx_scr, gate_scr, q_scr, k_scr, v_scr, rg_scr = slot
    tt = TILE_T

    def retention_unit(c, p):
        rows = slice(c * CHUNK, (c + 1) * CHUNK)
        cos, sin = cos_ref[rows, :], sin_ref[rows, :]
        sl = slice(p * LANES, (p + 1) * LANES)
        q_rot = _rotary(q_scr[rows, sl], cos, sin)
        k_rot = _rotary(k_scr[rows, sl], cos, sin)
        q_b = (q_rot * QK_SCALE).astype(bf16)
        q_d = (q_rot * qdec_ref[:, sl]).astype(bf16)
        k_d = (k_rot * kdec_ref[:, sl]).astype(bf16)
        k_b = k_rot.astype(bf16)
        kk = jnp.concatenate([k_b * even_ref[:, sl].astype(bf16), k_b * odd_ref[:, sl].astype(bf16)], axis=0)
        v_pair = v_scr[rows, p * PAIR_V:(p + 1) * PAIR_V]
        zeros = jnp.zeros((CHUNK, RET_V_DIM), bf16)
        v_bd = jnp.concatenate([jnp.concatenate([v_pair[:, :RET_V_DIM], zeros], axis=1),
                                jnp.concatenate([zeros, v_pair[:, RET_V_DIM:]], axis=1)], axis=0)
        st_pair = st_scr[p]
        s = _dot_nt(q_b, kk) * dmaskp_ref[p]
        o = jnp.dot(jnp.concatenate([s.astype(bf16), q_d], axis=1),
                    jnp.concatenate([v_bd, st_pair.astype(bf16)], axis=0), preferred_element_type=f32)
        for e in range(2):
            h = 2 * p + e
            hs = slice(h * RET_V_DIM, (h + 1) * RET_V_DIM)
            o_h = o[:, e * RET_V_DIM:(e + 1) * RET_V_DIM]
            mu = jnp.mean(o_h, axis=-1, keepdims=True)
            oc = o_h - mu
            var = jnp.mean(oc * oc, axis=-1, keepdims=True)
            on = oc * lax.rsqrt(var + EPS) * rng_ref[:, hs]
            y_scr[rows, LRU_WIDTH + h * RET_V_DIM:LRU_WIDTH + (h + 1) * RET_V_DIM] = (
                on * _silu(rg_scr[rows, hs])).astype(bf16)
        kv = _dot_t0(k_d, v_pair)
        st_scr[p] = gst_ref[p] * st_pair + bdm_ref[...] * kv

    lx_scr[0:CONV_HIST, :] = lxtail_scr[...]
    _conv_blocked(lx_scr, convw_ref, convb_ref, xc_scr, xcb_scr)
    lxtail_scr[...] = lx_scr[pl.ds(tt, CONV_HIST), :]
    for h in range(LRU_HEADS):
        lanes = slice(h * LRU_BLOCK, (h + 1) * LRU_BLOCK)
        ilanes = slice(LRU_WIDTH + h * LRU_BLOCK, LRU_WIDTH + (h + 1) * LRU_BLOCK)
        _gates_head(h, xcb_scr, wrg_ref, wig_ref, pre_scr)
        after_gates(h)
        full = (CHUNK, LRU_BLOCK)
        c_sp = jnp.broadcast_to(0.5 * LRU_C * jax.nn.softplus(-lam_ref[:, lanes]), full)
        brg, big = jnp.broadcast_to(0.5 * brg_ref[:, lanes], full), jnp.broadcast_to(0.5 * big_ref[:, lanes], full)
        carry = h_scr[:, lanes]
        for c in range(tt // CHUNK):
            rows = slice(c * CHUNK, (c + 1) * CHUNK)
            hb, carry = _lru_chunk_blocked(xc_scr[rows, lanes], pre_scr[rows, lanes], pre_scr[rows, ilanes],
                                           brg, big, c_sp, carry)
            y_scr[rows, lanes] = (hb * _silu(gate_scr[rows, lanes])).astype(bf16)
        h_scr[:, lanes] = carry

    for c in range(tt // CHUNK):
        for p in range(N_PAIRS):
            retention_unit(c, p)

    return load_x() + jnp.dot(y_scr[...], wout_ref[...].astype(bf16), preferred_element_type=f32)


X_SLOTS = 4
OUT_SLOTS = 2
ROW_GROUPS = TILE_T // SUBLANES


def _tile_block_copies(hbm_ref, vmem_ref, sem_ref, tile, slot, n_tiles, to_hbm):
    b = lax.div(tile, n_tiles)
    first_row = (tile - b * n_tiles) * TILE_T
    copies = []
    for c in range(TILE_T // CHUNK):
        for s in range(SUBLANES):
            hbm = hbm_ref.at[b, pl.ds(first_row + c * CHUNK + s * BLOCK_STEPS, BLOCK_STEPS), :]
            vmem = vmem_ref.at[slot, pl.ds(c * BLOCK_STEPS, BLOCK_STEPS), s, :]
            src, dst = (vmem, hbm) if to_hbm else (hbm, vmem)
            copies.append(pltpu.make_async_copy(src, dst, sem_ref.at[slot]))
    return copies


def _main_kernel(n_tiles, n_total,
                 x_hbm, cos_ref, sin_ref, g1_ref, win_ref, convw_ref, convb_ref, wrg_ref, wig_ref,
                 brg_ref, big_ref, lam_ref, rng_ref, wout_ref, fng_ref, dmaskp_ref, kdec_ref, qdec_ref, even_ref, odd_ref,
                 gst_ref, bdm_ref, meta_ref, cos_meta_ref, sin_meta_ref, kdec_time_ref,
                 o_hbm,
                 u_a, u_b,
                 lx_a, gate_a, q_a, k_a, v_a, rg_a,
                 lx_b, gate_b, q_b, k_b, v_b, rg_b,
                 xc_scr, xcb_scr, pre_scr, y_scr, lxtail_scr, h_scr, st_scr,
                 lx0_ref, h0_ref, st0_ref, res_scr,
                 xin_scr, out_scr, xin_sem, out_sem):
    g = pl.program_id(0)
    slots = ((lx_a, gate_a, q_a, k_a, v_a, rg_a), (lx_b, gate_b, q_b, k_b, v_b, rg_b))
    x_slot = lambda tile: lax.rem(tile + X_SLOTS, X_SLOTS)
    out_slot = lambda tile: lax.rem(tile + OUT_SLOTS, OUT_SLOTS)
    x_tile = lambda tile: xin_scr[x_slot(tile)].reshape(TILE_T, D_MODEL)
    fetch = lambda tile: _tile_block_copies(x_hbm, xin_scr, xin_sem, jnp.minimum(tile, n_total - 1), x_slot(tile),
                                            n_tiles, to_hbm=False)
    write_back = lambda tile: _tile_block_copies(o_hbm, out_scr, out_sem, tile, out_slot(tile), n_tiles, to_hbm=True)

    @pl.when(g >= 2 + OUT_SLOTS)
    def _():
        for copy in write_back(g - 2 - OUT_SLOTS):
            copy.wait()

    @pl.when(lax.rem(g + n_tiles - 1, n_tiles) == 0)
    def _():
        lxtail_scr[...] = lx0_ref[...]
        h_scr[...] = h0_ref[...]
        st_scr[...] = st0_ref[...]

    tiles_after_head = (4, 4, 3, 3, 2, 2, 1, 1)
    assert sum(tiles_after_head) == N_IN_TILES and len(tiles_after_head) == LRU_HEADS

    def finish(slot_finish, after_gates):
        res_scr[...] = _finish_tile(
            lambda: x_tile(g - 1), cos_ref, sin_ref, convw_ref, convb_ref, wrg_ref, wig_ref, brg_ref, big_ref,
            lam_ref, rng_ref, wout_ref, dmaskp_ref, kdec_ref, qdec_ref, even_ref, odd_ref, gst_ref,
            bdm_ref, slot_finish, xc_scr, xcb_scr, pre_scr, y_scr, lxtail_scr, h_scr, st_scr, after_gates)

    def final_norm(tile):
        out = _rmsnorm_rows(res_scr[...], fng_ref[...])
        out_scr[out_slot(tile)] = out.reshape(ROW_GROUPS, SUBLANES, D_MODEL)

    def norm_and_write_back(tile):
        final_norm(tile)
        for copy in write_back(tile):
            copy.start()

    final_norm_after_head, pre_norm_after_head = 4, 6

    def step(u_project, u_next, slot_project, slot_finish, norm_previous=True):
        for copy in fetch(g + 1):
            copy.wait()
        for copy in fetch(g + 2):
            copy.start()

        def after_gates(h):
            first = sum(tiles_after_head[:h])
            for t in range(first, first + tiles_after_head[h]):
                _project_col_tile(t, u_project, win_ref, slot_project)
            if norm_previous and h == final_norm_after_head:
                final_norm(g - 2)
            if h == pre_norm_after_head:
                _norm_tile(x_tile(g + 1), g1_ref, u_next)

        finish(slot_finish, after_gates)
        if norm_previous:
            for copy in write_back(g - 2):
                copy.start()

    @pl.when(g == 0)
    def _():
        for copy in fetch(g):
            copy.start()
        for copy in fetch(g):
            copy.wait()
        for tile in (g + 1, g + 2):
            for copy in fetch(tile):
                copy.start()
        _norm_tile(x_tile(g), g1_ref, u_a)
        for t in range(N_IN_TILES):
            _project_col_tile(t, u_a, win_ref, slots[0])
        meta_rows = lambda ref, n: ref.at[pl.ds(0, n)]
        _meta_states(meta_ref, cos_meta_ref, sin_meta_ref, g1_ref,
                     win_ref.at[:, pl.ds(OFF_LX, LRU_WIDTH)], win_ref.at[:, pl.ds(OFF_K, RET_QK_WIDTH)],
                     win_ref.at[:, pl.ds(OFF_V, RET_WIDTH)], convw_ref, convb_ref, wrg_ref, wig_ref, brg_ref, big_ref,
                     lam_ref, kdec_time_ref, bdm_ref, lx0_ref, h0_ref, st0_ref,
                     meta_rows(slots[1][0], SUBLANES + N_META), meta_rows(xc_scr, N_META), meta_rows(xcb_scr, N_META),
                     meta_rows(pre_scr, N_META))
        for copy in fetch(g + 1):
            copy.wait()
        _norm_tile(x_tile(g + 1), g1_ref, u_b)

    @pl.when(g == 1)
    def _():
        step(u_b, u_a, slots[1], slots[0], norm_previous=False)

    regular = jnp.logical_and(g >= 2, g < n_total)

    @pl.when(jnp.logical_and(regular, lax.rem(g, 2) == 0))
    def _():
        step(u_a, u_b, slots[0], slots[1])

    @pl.when(jnp.logical_and(regular, lax.rem(g, 2) == 1))
    def _():
        step(u_b, u_a, slots[1], slots[0])

    @pl.when(g == n_total)
    def _():
        for copy in fetch(g + 1):
            copy.wait()
        norm_and_write_back(g - 2)
        finish(slots[(n_total - 1) % 2], lambda h: None)
        for copy in write_back(g - 1 - OUT_SLOTS):
            copy.wait()
        norm_and_write_back(g - 1)
        for tile in (g - 2, g - 1):
            for copy in write_back(tile):
                copy.wait()


def _const_spec(shape):
    nd = len(shape)
    return pl.BlockSpec(shape, lambda g, _nd=nd: (0,) * _nd, pipeline_mode=pl.Buffered(1))


def kernel(x, meta_tokens, norm_gain, w_in, conv_w, conv_b, w_rg, b_rg, w_ig, b_ig,
           lru_lambda, ret_norm_gain, w_out, final_norm_gain):
    B, S, D = x.shape
    assert D == D_MODEL and S % TILE_T == 0 and TILE_T % CHUNK == 0
    assert norm_gain.shape[0] == 1, "single-layer block"
    assert meta_tokens.shape == (N_META, D_MODEL)

    assert w_in.shape == (1, D_MODEL, IN_WIDTH) and w_out.shape == (1, MIX_WIDTH, D_MODEL)
    assert w_rg.shape == w_ig.shape == (1, LRU_HEADS, LRU_BLOCK, LRU_BLOCK)

    g1 = norm_gain[0].reshape(1, D_MODEL)
    fng = final_norm_gain.reshape(1, D_MODEL)
    convw = conv_w[0]
    convb = conv_b[0].reshape(1, LRU_WIDTH)
    brg = b_rg[0].reshape(1, LRU_WIDTH)
    big = b_ig[0].reshape(1, LRU_WIDTH)
    lam = lru_lambda[0].reshape(1, LRU_WIDTH)
    rng = ret_norm_gain[0].reshape(1, RET_WIDTH)

    k_dec, dmask_blocked, k_dec_blocked, q_dec_blocked, even, odd, g_state, bd_mask = _retention_tables()
    cos_t, sin_t = _rotary_tables(N_META + S)
    cos_m, sin_m = jnp.asarray(cos_t[:N_META]), jnp.asarray(sin_t[:N_META])
    cos_blocked = jnp.asarray(_chunk_rows_blocked(cos_t[N_META:]))
    sin_blocked = jnp.asarray(_chunk_rows_blocked(sin_t[N_META:]))

    tt = TILE_T
    n_tiles = S // tt
    n_total = B * n_tiles

    rot_spec = pl.BlockSpec((tt, LANES), lambda g: (jnp.maximum(g - 1, 0) % n_tiles, 0))
    in_specs = [
        pl.BlockSpec(memory_space=pl.ANY),
        rot_spec, rot_spec,
        _const_spec((1, D_MODEL)),
        _const_spec((None, D_MODEL, IN_WIDTH)),
        _const_spec((CONV_WIDTH, LRU_WIDTH)), _const_spec((1, LRU_WIDTH)),
        _const_spec((None, LRU_HEADS, LRU_BLOCK, LRU_BLOCK)),
        _const_spec((None, LRU_HEADS, LRU_BLOCK, LRU_BLOCK)),
        _const_spec((1, LRU_WIDTH)), _const_spec((1, LRU_WIDTH)), _const_spec((1, LRU_WIDTH)),
        _const_spec((1, RET_WIDTH)),
        _const_spec((None, MIX_WIDTH, D_MODEL)),
        _const_spec((1, D_MODEL)),
        _const_spec((N_PAIRS, CHUNK, 2 * CHUNK)),
        _const_spec((CHUNK, RET_QK_WIDTH)), _const_spec((CHUNK, RET_QK_WIDTH)),
        _const_spec((1, RET_QK_WIDTH)), _const_spec((1, RET_QK_WIDTH)),
        _const_spec((N_PAIRS, LANES, PAIR_V)),
        _const_spec((LANES, PAIR_V)),
        _const_spec((N_META, D_MODEL)),
        _const_spec((N_META, LANES)), _const_spec((N_META, LANES)),
        _const_spec((CHUNK, RET_QK_WIDTH)),
    ]
    slot_scratch = [
        pltpu.VMEM((CONV_HIST + tt, LRU_WIDTH), f32),
        pltpu.VMEM((tt, LRU_WIDTH), f32),
        pltpu.VMEM((tt, RET_QK_WIDTH), f32),
        pltpu.VMEM((tt, RET_QK_WIDTH), f32),
        pltpu.VMEM((tt, RET_WIDTH), bf16),
        pltpu.VMEM((tt, RET_WIDTH), f32),
    ]
    scratch = [pltpu.VMEM((tt, D_MODEL), bf16)] * 2 + slot_scratch + slot_scratch + [
        pltpu.VMEM((tt, LRU_WIDTH), f32),
        pltpu.VMEM((tt, LRU_WIDTH), bf16),
        pltpu.VMEM((tt, 2 * LRU_WIDTH), f32),
        pltpu.VMEM((tt, MIX_WIDTH), bf16),
        pltpu.VMEM((CONV_HIST, LRU_WIDTH), f32),
        pltpu.VMEM((SUBLANES, LRU_WIDTH), f32),
        pltpu.VMEM((N_PAIRS, LANES, PAIR_V), f32),
        pltpu.VMEM((CONV_HIST, LRU_WIDTH), f32),
        pltpu.VMEM((SUBLANES, LRU_WIDTH), f32),
        pltpu.VMEM((N_PAIRS, LANES, PAIR_V), f32),
        pltpu.VMEM((tt, D_MODEL), f32),
        pltpu.VMEM((X_SLOTS, ROW_GROUPS, SUBLANES, D_MODEL), f32),
        pltpu.VMEM((OUT_SLOTS, ROW_GROUPS, SUBLANES, D_MODEL), f32),
        pltpu.SemaphoreType.DMA((X_SLOTS,)),
        pltpu.SemaphoreType.DMA((OUT_SLOTS,)),
    ]
    out = pl.pallas_call(
        functools.partial(_main_kernel, n_tiles, n_total),
        grid=(n_total + 1,),
        in_specs=in_specs,
        out_specs=pl.BlockSpec(memory_space=pl.ANY),
        out_shape=jax.ShapeDtypeStruct((B, S, D_MODEL), x.dtype),
        scratch_shapes=scratch,
        compiler_params=pltpu.CompilerParams(
            dimension_semantics=("arbitrary",),
            vmem_limit_bytes=VMEM_LIMIT_BYTES),
        name="hybrid_main",
    )(x, cos_blocked, sin_blocked, g1, w_in, convw, convb, w_rg, w_ig, brg, big, lam, rng, w_out, fng,
      dmask_blocked, k_dec_blocked, q_dec_blocked, even, odd, g_state, bd_mask, meta_tokens, cos_m, sin_m, k_dec)
    return out
```

```python
import functools

import numpy as np
import jax
import jax.numpy as jnp
from jax import lax
from jax.experimental import pallas as pl
from jax.experimental.pallas import tpu as pltpu

f32 = jnp.float32
bf16 = jnp.bfloat16

D_MODEL = 1024
N_META = 16
LRU_WIDTH = 1024
LRU_HEADS = 8
LRU_BLOCK = 128
CONV_WIDTH = 4
LRU_C = 8.0
RET_HEADS = 8
RET_QK_DIM = 64
RET_V_DIM = 128
RET_QK_WIDTH = 512
RET_WIDTH = 1024
CHUNK = 128
ROPE_BASE = 10000.0
MIX_WIDTH = 2048
EPS = 1e-6
QK_SCALE = RET_QK_DIM ** -0.5

OFF_LX, OFF_GATE, OFF_Q, OFF_K, OFF_V, OFF_RG = 0, 1024, 2048, 2560, 3072, 4096
IN_WIDTH = 5120

LANES = 128
SUBLANES = 8
N_PAIRS = RET_HEADS // 2
PAIR_V = 2 * RET_V_DIM
TILE_T = 256
VMEM_LIMIT_BYTES = 60000 * 1024


def _lane_head():
    return np.arange(RET_QK_WIDTH) // RET_QK_DIM


BLOCK_STEPS = CHUNK // SUBLANES
CHUNK_ROW_TIME = np.arange(CHUNK).reshape(SUBLANES, BLOCK_STEPS).T.reshape(-1)


def _chunk_rows_blocked(table):
    n = table.shape[0]
    return table.reshape((n // CHUNK, CHUNK) + table.shape[1:])[:, CHUNK_ROW_TIME].reshape(table.shape)


def _retention_tables():
    log_g = np.log1p(-np.exp2(-5.0 - np.arange(RET_HEADS, dtype=np.float32))).astype(np.float32)
    idx = np.arange(CHUNK, dtype=np.float32)
    diff = idx[:, None] - idx[None, :]
    dmask = np.where(diff[None] >= 0.0, np.exp(np.maximum(diff, 0.0)[None] * log_g[:, None, None]), 0.0)
    dmask_pair = np.concatenate([dmask[0::2], dmask[1::2]], axis=-1)
    lg_lane = log_g[_lane_head()]
    k_dec = np.exp((CHUNK - 1.0 - idx)[:, None] * lg_lane[None, :])
    q_dec = np.exp((idx + 1.0)[:, None] * lg_lane[None, :]) * QK_SCALE
    even = (_lane_head() % 2 == 0)[None, :]
    g_chunk = np.exp(CHUNK * log_g)
    g_state = np.broadcast_to(np.repeat(g_chunk, RET_V_DIM).reshape(N_PAIRS, 1, PAIR_V), (N_PAIRS, LANES, PAIR_V))
    row_par = np.arange(LANES) // RET_QK_DIM
    col_par = np.arange(PAIR_V) // RET_V_DIM
    bd_mask = row_par[:, None] == col_par[None, :]
    key_order = np.concatenate([CHUNK_ROW_TIME, CHUNK + CHUNK_ROW_TIME])
    dmask_blocked = dmask_pair[:, CHUNK_ROW_TIME][:, :, key_order]
    as_f32 = lambda a: jnp.asarray(np.asarray(a, np.float32))
    return tuple(as_f32(t) for t in (k_dec, dmask_blocked, k_dec[CHUNK_ROW_TIME], q_dec[CHUNK_ROW_TIME],
                                     even, ~even, g_state, bd_mask))


def _rotary_tables(n_pos):
    half = RET_QK_DIM // 2
    inv = (np.float32(ROPE_BASE) ** (-np.arange(half, dtype=np.float32) / half)).astype(np.float32)
    ang = np.arange(n_pos).astype(np.float32)[:, None] * inv[None, :]
    cos, sin = np.cos(ang), np.sin(ang)
    cos_t = np.concatenate([cos, cos, cos, cos], axis=-1).astype(np.float32)
    sin_t = np.concatenate([-sin, sin, -sin, sin], axis=-1).astype(np.float32)
    return cos_t, sin_t


def _rmsnorm_rows(x, gain_row):
    ms = jnp.mean(x * x, axis=-1, keepdims=True)
    return x * lax.rsqrt(ms + EPS) * gain_row


def _silu(x):
    hx = 0.5 * x
    return hx * jnp.tanh(hx) + hx


def _half_conv_params(convw_ref, convb_ref):
    return [0.5 * convw_ref[k:k + 1, :] for k in range(CONV_WIDTH)], 0.5 * convb_ref[...]


def _conv(n_rows, lx_scr, convw_ref, convb_ref, xc_scr, xcb_scr):
    base = SUBLANES
    taps, bias = _half_conv_params(convw_ref, convb_ref)
    xc = bias + taps[3] * lx_scr[pl.ds(base, n_rows), :]
    xc = xc + taps[2] * lx_scr[pl.ds(base - 1, n_rows), :]
    xc = xc + taps[1] * lx_scr[pl.ds(base - 2, n_rows), :]
    xc = xc + taps[0] * lx_scr[pl.ds(base - 3, n_rows), :]
    xc_scr[...] = xc
    xcb_scr[...] = xc.astype(bf16)


CONV_HIST = (CONV_WIDTH - 1) * SUBLANES


def _conv_blocked(lx_scr, convw_ref, convb_ref, xc_scr, xcb_scr):
    taps, bias = _half_conv_params(convw_ref, convb_ref)
    last_sublane = lax.broadcasted_iota(jnp.int32, (SUBLANES, LRU_WIDTH), 0) == SUBLANES - 1
    for c in range(TILE_T // CHUNK):
        base = CONV_HIST + c * CHUNK
        rows = lambda group, n: lx_scr[base + group * SUBLANES:base + (group + n) * SUBLANES, :]
        wrapped = []
        for k in range(1, CONV_WIDTH):
            own, prev = rows(BLOCK_STEPS - k, 1), rows(-k, 1)
            wrapped.append(pltpu.roll(jnp.where(last_sublane, prev, own), 1, 0))
        xc = bias + taps[CONV_WIDTH - 1] * rows(0, BLOCK_STEPS)
        for k in range(1, CONV_WIDTH):
            shifted = jnp.concatenate(wrapped[:k][::-1] + [rows(0, BLOCK_STEPS - k)], axis=0)
            xc = xc + taps[CONV_WIDTH - 1 - k] * shifted
        xc_scr[c * CHUNK:(c + 1) * CHUNK, :] = xc
        xcb_scr[c * CHUNK:(c + 1) * CHUNK, :] = xc.astype(bf16)


def _gates_head(h, xcb_scr, wrg_ref, wig_ref, pre_scr):
    lanes = slice(h * LRU_BLOCK, (h + 1) * LRU_BLOCK)
    wg = jnp.concatenate([wrg_ref[h], wig_ref[h]], axis=-1).astype(bf16)
    pre = jnp.dot(xcb_scr[:, lanes], wg, preferred_element_type=f32)
    pre_scr[:, lanes] = pre[:, :LRU_BLOCK]
    pre_scr[:, LRU_WIDTH + h * LRU_BLOCK:LRU_WIDTH + (h + 1) * LRU_BLOCK] = pre[:, LRU_BLOCK:]


def _lru_maps(half_xc, half_pre_r, half_pre_i, half_brg, half_big, half_c_sp):
    nl = jnp.tanh(half_pre_r + half_brg) * half_c_sp + half_c_sp
    twice_i = jnp.tanh(half_pre_i + half_big) + 1.0
    a = jnp.exp(-nl)
    z = jnp.tanh(nl) * (1.0 + a * a)
    beta = jnp.where(z > 0.0, z * lax.rsqrt(z), 0.0)
    return a, beta * twice_i * half_xc


def _scan_sublanes(a, b):
    rowid = lax.broadcasted_iota(jnp.int32, a.shape, 0)
    for s in (1, 2, 4):
        keep = rowid >= s
        a_s = jnp.where(keep, pltpu.roll(a, s, 0), 1.0)
        b_s = jnp.where(keep, pltpu.roll(b, s, 0), 0.0)
        b = a * b_s + b
        a = a * a_s
    return a, b


def _lru_chunk_blocked(xc, pre_r, pre_i, brg, big, c_sp, carry):
    a, b = _lru_maps(xc, pre_r, pre_i, brg, big, c_sp)
    width = xc.shape[1]
    a3 = a.reshape(BLOCK_STEPS, SUBLANES, width)
    b3 = b.reshape(BLOCK_STEPS, SUBLANES, width)
    decay, local = [a3[0]], [b3[0]]
    for j in range(1, BLOCK_STEPS):
        local.append(a3[j] * local[-1] + b3[j])
        decay.append(a3[j] * decay[-1])
    a_blocks, b_blocks = _scan_sublanes(decay[-1], local[-1])
    after = a_blocks * carry + b_blocks
    rowid = lax.broadcasted_iota(jnp.int32, carry.shape, 0)
    before = jnp.where(rowid >= 1, pltpu.roll(after, 1, 0), carry)
    hs = [local[j] + decay[j] * before for j in range(BLOCK_STEPS)]
    new_carry = jnp.broadcast_to(after[SUBLANES - 1:SUBLANES, :], carry.shape)
    return jnp.concatenate(hs, axis=0), new_carry


def _lru_block(xc, pre_r, pre_i, brg, big, c_sp, carry):
    rows, width = xc.shape
    a, b = _lru_maps(xc, pre_r, pre_i, brg, big, c_sp)
    nv = rows // SUBLANES
    a3 = a.reshape(nv, SUBLANES, width)
    b3 = b.reshape(nv, SUBLANES, width)
    rowid = lax.broadcasted_iota(jnp.int32, (nv, SUBLANES, width), 1)
    for s in (1, 2, 4):
        keep = rowid >= s
        a_s = jnp.where(keep, pltpu.roll(a3, s, 1), 1.0)
        b_s = jnp.where(keep, pltpu.roll(b3, s, 1), 0.0)
        b3 = a3 * b_s + b3
        a3 = a3 * a_s
    hs = []
    for v in range(nv):
        h_v = a3[v] * carry + b3[v]
        carry = h_v[SUBLANES - 1:SUBLANES, :]
        hs.append(h_v)
    return jnp.concatenate(hs, axis=0), carry


def _rotary(t, cos, sin):
    half = RET_QK_DIM // 2
    in_first_half = (lax.broadcasted_iota(jnp.int32, t.shape, 1) // half) % 2 == 0
    partner = jnp.where(in_first_half, pltpu.roll(t, LANES - half, 1), pltpu.roll(t, half, 1))
    return t * cos + partner * sin


def _dot_t0(a, b):
    return lax.dot_general(a, b, (((0,), (0,)), ((), ())), preferred_element_type=f32)


def _dot_nt(a, b):
    return lax.dot_general(a, b, (((1,), (1,)), ((), ())), preferred_element_type=f32)


def _meta_states(meta_ref, cos_ref, sin_ref, g1_ref, wlx_ref, wk_ref, wv_ref, convw_ref, convb_ref,
                 wrg_ref, wig_ref, brg_ref, big_ref, lam_ref, kdec_ref, bdm_ref,
                 lx_out, h_out, st_out,
                 lx_scr, xc_scr, xcb_scr, pre_scr):
    u = _rmsnorm_rows(meta_ref[...], g1_ref[...]).astype(bf16)
    lx_scr[0:SUBLANES, :] = jnp.zeros((SUBLANES, LRU_WIDTH), f32)
    lx_scr[SUBLANES:SUBLANES + N_META, :] = jnp.dot(u, wlx_ref[...].astype(bf16), preferred_element_type=f32)
    k = jnp.dot(u, wk_ref[...].astype(bf16), preferred_element_type=f32)
    v = jnp.dot(u, wv_ref[...].astype(bf16), preferred_element_type=f32).astype(bf16)

    _conv(N_META, lx_scr, convw_ref, convb_ref, xc_scr, xcb_scr)
    for h in range(LRU_HEADS):
        _gates_head(h, xcb_scr, wrg_ref, wig_ref, pre_scr)
    for back in range(1, CONV_WIDTH):
        row = lx_scr[SUBLANES + N_META - back:SUBLANES + N_META - back + 1, :]
        group = CONV_WIDTH - 1 - back
        lx_out[group * SUBLANES:(group + 1) * SUBLANES, :] = jnp.broadcast_to(row, (SUBLANES, LRU_WIDTH))
    half_c_sp = 0.5 * LRU_C * jax.nn.softplus(-lam_ref[...])
    _, carry = _lru_block(xc_scr[...], pre_scr[:, :LRU_WIDTH], pre_scr[:, LRU_WIDTH:],
                          0.5 * brg_ref[...], 0.5 * big_ref[...], half_c_sp, jnp.zeros((1, LRU_WIDTH), f32))
    h_out[...] = jnp.broadcast_to(carry, (SUBLANES, LRU_WIDTH))

    cos, sin = cos_ref[...], sin_ref[...]
    for p in range(N_PAIRS):
        sl = slice(p * LANES, (p + 1) * LANES)
        k_rot = _rotary(k[:, sl], cos, sin)
        kd = (k_rot * kdec_ref[CHUNK - N_META:CHUNK, sl]).astype(bf16)
        st_out[p] = _dot_t0(kd, v[:, p * PAIR_V:(p + 1) * PAIR_V]) * bdm_ref[...]


def _norm_tile(x_tile, g1_ref, u_scr):
    u_scr[...] = _rmsnorm_rows(x_tile, g1_ref[...]).astype(bf16)


MXU_COLS = 256
N_IN_TILES = IN_WIDTH // MXU_COLS
W_CHUNK_COLS = 4 * MXU_COLS
N_W_CHUNKS = IN_WIDTH // W_CHUNK_COLS


def _project_col_tile(t, u_scr, win_ref, slot):
    lx_scr, gate_scr, q_scr, k_scr, v_scr, rg_scr = slot
    off = t * MXU_COLS
    res = jnp.dot(u_scr[...], win_ref[:, off:off + MXU_COLS].astype(bf16), preferred_element_type=f32)
    for dst, start, rows, dt in ((lx_scr, OFF_LX, slice(CONV_HIST, CONV_HIST + TILE_T), f32), (gate_scr, OFF_GATE, slice(None), f32),
                                 (q_scr, OFF_Q, slice(None), f32), (k_scr, OFF_K, slice(None), f32),
                                 (v_scr, OFF_V, slice(None), bf16), (rg_scr, OFF_RG, slice(None), f32)):
        width = dst.shape[1]
        if start <= off < start + width:
            dst[rows, off - start:off - start + MXU_COLS] = res.astype(dt)
            return
    raise AssertionError("column tile outside the projection")


def _finish_tile(load_x, cos_ref, sin_ref, convw_ref, convb_ref, wrg_ref, wig_ref, brg_ref, big_ref, lam_ref, rng_ref,
                 wout_ref, dmaskp_ref, kdec_ref, qdec_ref, even_ref, odd_ref, gst_ref, bdm_ref,
                 slot, xc_scr, xcb_scr, pre_scr, y_scr, lxtail_scr, h_scr, st_scr, after_gates):
    lx_scr, gate_scr, q_scr, k_scr, v_scr, rg_scr = slot
    tt = TILE_T

    def retention_unit(c, p):
        rows = slice(c * CHUNK, (c + 1) * CHUNK)
        cos, sin = cos_ref[rows, :], sin_ref[rows, :]
        sl = slice(p * LANES, (p + 1) * LANES)
        q_rot = _rotary(q_scr[rows, sl], cos, sin)
        k_rot = _rotary(k_scr[rows, sl], cos, sin)
        q_b = (q_rot * QK_SCALE).astype(bf16)
        q_d = (q_rot * qdec_ref[:, sl]).astype(bf16)
        k_d = (k_rot * kdec_ref[:, sl]).astype(bf16)
        k_b = k_rot.astype(bf16)
        kk = jnp.concatenate([k_b * even_ref[:, sl].astype(bf16), k_b * odd_ref[:, sl].astype(bf16)], axis=0)
        v_pair = v_scr[rows, p * PAIR_V:(p + 1) * PAIR_V]
        zeros = jnp.zeros((CHUNK, RET_V_DIM), bf16)
        v_bd = jnp.concatenate([jnp.concatenate([v_pair[:, :RET_V_DIM], zeros], axis=1),
                                jnp.concatenate([zeros, v_pair[:, RET_V_DIM:]], axis=1)], axis=0)
        st_pair = st_scr[p]
        s = _dot_nt(q_b, kk) * dmaskp_ref[p]
        o = jnp.dot(jnp.concatenate([s.astype(bf16), q_d], axis=1),
                    jnp.concatenate([v_bd, st_pair.astype(bf16)], axis=0), preferred_element_type=f32)
        for e in range(2):
            h = 2 * p + e
            hs = slice(h * RET_V_DIM, (h + 1) * RET_V_DIM)
            o_h = o[:, e * RET_V_DIM:(e + 1) * RET_V_DIM]
            mu = jnp.mean(o_h, axis=-1, keepdims=True)
            oc = o_h - mu
            var = jnp.mean(oc * oc, axis=-1, keepdims=True)
            on = oc * lax.rsqrt(var + EPS) * rng_ref[:, hs]
            y_scr[rows, LRU_WIDTH + h * RET_V_DIM:LRU_WIDTH + (h + 1) * RET_V_DIM] = (
                on * _silu(rg_scr[rows, hs])).astype(bf16)
        kv = _dot_t0(k_d, v_pair)
        st_scr[p] = gst_ref[p] * st_pair + bdm_ref[...] * kv

    lx_scr[0:CONV_HIST, :] = lxtail_scr[...]
    _conv_blocked(lx_scr, convw_ref, convb_ref, xc_scr, xcb_scr)
    lxtail_scr[...] = lx_scr[pl.ds(tt, CONV_HIST), :]
    for h in range(LRU_HEADS):
        lanes = slice(h * LRU_BLOCK, (h + 1) * LRU_BLOCK)
        ilanes = slice(LRU_WIDTH + h * LRU_BLOCK, LRU_WIDTH + (h + 1) * LRU_BLOCK)
        _gates_head(h, xcb_scr, wrg_ref, wig_ref, pre_scr)
        after_gates(h)
        full = (CHUNK, LRU_BLOCK)
        c_sp = jnp.broadcast_to(0.5 * LRU_C * jax.nn.softplus(-lam_ref[:, lanes]), full)
        brg, big = jnp.broadcast_to(0.5 * brg_ref[:, lanes], full), jnp.broadcast_to(0.5 * big_ref[:, lanes], full)
        carry = h_scr[:, lanes]
        for c in range(tt // CHUNK):
            rows = slice(c * CHUNK, (c + 1) * CHUNK)
            hb, carry = _lru_chunk_blocked(xc_scr[rows, lanes], pre_scr[rows, lanes], pre_scr[rows, ilanes],
                                           brg, big, c_sp, carry)
            y_scr[rows, lanes] = (hb * _silu(gate_scr[rows, lanes])).astype(bf16)
        h_scr[:, lanes] = carry

    for c in range(tt // CHUNK):
        for p in range(N_PAIRS):
            retention_unit(c, p)

    return load_x() + jnp.dot(y_scr[...], wout_ref[...].astype(bf16), preferred_element_type=f32)


X_SLOTS = 4
OUT_SLOTS = 2
ROW_GROUPS = TILE_T // SUBLANES


def _tile_block_copies(hbm_ref, vmem_ref, sem_ref, tile, slot, n_tiles, to_hbm):
    b = lax.div(tile, n_tiles)
    first_row = (tile - b * n_tiles) * TILE_T
    copies = []
    for c in range(TILE_T // CHUNK):
        for s in range(SUBLANES):
            hbm = hbm_ref.at[b, pl.ds(first_row + c * CHUNK + s * BLOCK_STEPS, BLOCK_STEPS), :]
            vmem = vmem_ref.at[slot, pl.ds(c * BLOCK_STEPS, BLOCK_STEPS), s, :]
            src, dst = (vmem, hbm) if to_hbm else (hbm, vmem)
            copies.append(pltpu.make_async_copy(src, dst, sem_ref.at[slot]))
    return copies


def _main_kernel(n_tiles, n_total,
                 x_hbm, cos_ref, sin_ref, g1_ref, win_hbm, convw_ref, convb_ref, wrg_ref, wig_ref,
                 brg_ref, big_ref, lam_ref, rng_ref, wout_hbm, fng_ref, dmaskp_ref, kdec_ref, qdec_ref, even_ref, odd_ref,
                 gst_ref, bdm_ref, meta_ref, cos_meta_ref, sin_meta_ref, kdec_time_ref,
                 o_hbm,
                 u_a, u_b,
                 lx_a, gate_a, q_a, k_a, v_a, rg_a,
                 lx_b, gate_b, q_b, k_b, v_b, rg_b,
                 xc_scr, xcb_scr, pre_scr, y_scr, lxtail_scr, h_scr, st_scr,
                 lx0_ref, h0_ref, st0_ref, res_scr,
                 xin_scr, out_scr, xin_sem, out_sem,
                 win_ref, wout_ref, w_sem):
    g = pl.program_id(0)
    slots = ((lx_a, gate_a, q_a, k_a, v_a, rg_a), (lx_b, gate_b, q_b, k_b, v_b, rg_b))
    win_copies = [pltpu.make_async_copy(win_hbm.at[0, :, pl.ds(c * W_CHUNK_COLS, W_CHUNK_COLS)],
                                        win_ref.at[:, pl.ds(c * W_CHUNK_COLS, W_CHUNK_COLS)], w_sem.at[c])
                  for c in range(N_W_CHUNKS)]
    wout_copy = pltpu.make_async_copy(wout_hbm.at[0], wout_ref, w_sem.at[N_W_CHUNKS])
    x_slot = lambda tile: lax.rem(tile + X_SLOTS, X_SLOTS)
    out_slot = lambda tile: lax.rem(tile + OUT_SLOTS, OUT_SLOTS)
    x_tile = lambda tile: xin_scr[x_slot(tile)].reshape(TILE_T, D_MODEL)
    fetch = lambda tile: _tile_block_copies(x_hbm, xin_scr, xin_sem, jnp.minimum(tile, n_total - 1), x_slot(tile),
                                            n_tiles, to_hbm=False)
    write_back = lambda tile: _tile_block_copies(o_hbm, out_scr, out_sem, tile, out_slot(tile), n_tiles, to_hbm=True)

    @pl.when(g >= 2 + OUT_SLOTS)
    def _():
        for copy in write_back(g - 2 - OUT_SLOTS):
            copy.wait()

    @pl.when(lax.rem(g + n_tiles - 1, n_tiles) == 0)
    def _():
        lxtail_scr[...] = lx0_ref[...]
        h_scr[...] = h0_ref[...]
        st_scr[...] = st0_ref[...]

    tiles_after_head = (4, 4, 3, 3, 2, 2, 1, 1)
    assert sum(tiles_after_head) == N_IN_TILES and len(tiles_after_head) == LRU_HEADS

    def finish(slot_finish, after_gates):
        res_scr[...] = _finish_tile(
            lambda: x_tile(g - 1), cos_ref, sin_ref, convw_ref, convb_ref, wrg_ref, wig_ref, brg_ref, big_ref,
            lam_ref, rng_ref, wout_ref, dmaskp_ref, kdec_ref, qdec_ref, even_ref, odd_ref, gst_ref,
            bdm_ref, slot_finish, xc_scr, xcb_scr, pre_scr, y_scr, lxtail_scr, h_scr, st_scr, after_gates)

    def final_norm(tile):
        out = _rmsnorm_rows(res_scr[...], fng_ref[...])
        out_scr[out_slot(tile)] = out.reshape(ROW_GROUPS, SUBLANES, D_MODEL)

    def norm_and_write_back(tile):
        final_norm(tile)
        for copy in write_back(tile):
            copy.start()

    final_norm_after_head, pre_norm_after_head = 4, 6

    def step(u_project, u_next, slot_project, slot_finish, norm_previous=True):
        for copy in fetch(g + 1):
            copy.wait()
        for copy in fetch(g + 2):
            copy.start()

        def after_gates(h):
            first = sum(tiles_after_head[:h])
            for t in range(first, first + tiles_after_head[h]):
                _project_col_tile(t, u_project, win_ref, slot_project)
            if norm_previous and h == final_norm_after_head:
                final_norm(g - 2)
            if h == pre_norm_after_head:
                _norm_tile(x_tile(g + 1), g1_ref, u_next)

        finish(slot_finish, after_gates)
        if norm_previous:
            for copy in write_back(g - 2):
                copy.start()

    @pl.when(g == 0)
    def _():
        for copy in fetch(g):
            copy.start()
        for copy in win_copies + [wout_copy]:
            copy.start()
        for copy in fetch(g):
            copy.wait()
        for tile in (g + 1, g + 2):
            for copy in fetch(tile):
                copy.start()
        _norm_tile(x_tile(g), g1_ref, u_a)
        for t in range(N_IN_TILES):
            if t % (W_CHUNK_COLS // MXU_COLS) == 0:
                win_copies[t // (W_CHUNK_COLS // MXU_COLS)].wait()
            _project_col_tile(t, u_a, win_ref, slots[0])
        meta_rows = lambda ref, n: ref.at[pl.ds(0, n)]
        _meta_states(meta_ref, cos_meta_ref, sin_meta_ref, g1_ref,
                     win_ref.at[:, pl.ds(OFF_LX, LRU_WIDTH)], win_ref.at[:, pl.ds(OFF_K, RET_QK_WIDTH)],
                     win_ref.at[:, pl.ds(OFF_V, RET_WIDTH)], convw_ref, convb_ref, wrg_ref, wig_ref, brg_ref, big_ref,
                     lam_ref, kdec_time_ref, bdm_ref, lx0_ref, h0_ref, st0_ref,
                     meta_rows(slots[1][0], SUBLANES + N_META), meta_rows(xc_scr, N_META), meta_rows(xcb_scr, N_META),
                     meta_rows(pre_scr, N_META))
        for copy in fetch(g + 1):
            copy.wait()
        _norm_tile(x_tile(g + 1), g1_ref, u_b)

    @pl.when(g == 1)
    def _():
        wout_copy.wait()
        step(u_b, u_a, slots[1], slots[0], norm_previous=False)

    regular = jnp.logical_and(g >= 2, g < n_total)

    @pl.when(jnp.logical_and(regular, lax.rem(g, 2) == 0))
    def _():
        step(u_a, u_b, slots[0], slots[1])

    @pl.when(jnp.logical_and(regular, lax.rem(g, 2) == 1))
    def _():
        step(u_b, u_a, slots[1], slots[0])

    @pl.when(g == n_total)
    def _():
        for copy in fetch(g + 1):
            copy.wait()
        norm_and_write_back(g - 2)
        finish(slots[(n_total - 1) % 2], lambda h: None)
        for copy in write_back(g - 1 - OUT_SLOTS):
            copy.wait()
        norm_and_write_back(g - 1)
        for tile in (g - 2, g - 1):
            for copy in write_back(tile):
                copy.wait()


def _const_spec(shape):
    nd = len(shape)
    return pl.BlockSpec(shape, lambda g, _nd=nd: (0,) * _nd, pipeline_mode=pl.Buffered(1))


def kernel(x, meta_tokens, norm_gain, w_in, conv_w, conv_b, w_rg, b_rg, w_ig, b_ig,
           lru_lambda, ret_norm_gain, w_out, final_norm_gain):
    B, S, D = x.shape
    assert D == D_MODEL and S % TILE_T == 0 and TILE_T % CHUNK == 0
    assert norm_gain.shape[0] == 1, "single-layer block"
    assert meta_tokens.shape == (N_META, D_MODEL)

    assert w_in.shape == (1, D_MODEL, IN_WIDTH) and w_out.shape == (1, MIX_WIDTH, D_MODEL)
    assert w_rg.shape == w_ig.shape == (1, LRU_HEADS, LRU_BLOCK, LRU_BLOCK)

    g1 = norm_gain[0].reshape(1, D_MODEL)
    fng = final_norm_gain.reshape(1, D_MODEL)
    convw = conv_w[0]
    convb = conv_b[0].reshape(1, LRU_WIDTH)
    brg = b_rg[0].reshape(1, LRU_WIDTH)
    big = b_ig[0].reshape(1, LRU_WIDTH)
    lam = lru_lambda[0].reshape(1, LRU_WIDTH)
    rng = ret_norm_gain[0].reshape(1, RET_WIDTH)

    k_dec, dmask_blocked, k_dec_blocked, q_dec_blocked, even, odd, g_state, bd_mask = _retention_tables()
    cos_t, sin_t = _rotary_tables(N_META + S)
    cos_m, sin_m = jnp.asarray(cos_t[:N_META]), jnp.asarray(sin_t[:N_META])
    cos_blocked = jnp.asarray(_chunk_rows_blocked(cos_t[N_META:]))
    sin_blocked = jnp.asarray(_chunk_rows_blocked(sin_t[N_META:]))

    tt = TILE_T
    n_tiles = S // tt
    n_total = B * n_tiles

    rot_spec = pl.BlockSpec((tt, LANES), lambda g: (jnp.maximum(g - 1, 0) % n_tiles, 0))
    in_specs = [
        pl.BlockSpec(memory_space=pl.ANY),
        rot_spec, rot_spec,
        _const_spec((1, D_MODEL)),
        pl.BlockSpec(memory_space=pl.ANY),
        _const_spec((CONV_WIDTH, LRU_WIDTH)), _const_spec((1, LRU_WIDTH)),
        _const_spec((None, LRU_HEADS, LRU_BLOCK, LRU_BLOCK)),
        _const_spec((None, LRU_HEADS, LRU_BLOCK, LRU_BLOCK)),
        _const_spec((1, LRU_WIDTH)), _const_spec((1, LRU_WIDTH)), _const_spec((1, LRU_WIDTH)),
        _const_spec((1, RET_WIDTH)),
        pl.BlockSpec(memory_space=pl.ANY),
        _const_spec((1, D_MODEL)),
        _const_spec((N_PAIRS, CHUNK, 2 * CHUNK)),
        _const_spec((CHUNK, RET_QK_WIDTH)), _const_spec((CHUNK, RET_QK_WIDTH)),
        _const_spec((1, RET_QK_WIDTH)), _const_spec((1, RET_QK_WIDTH)),
        _const_spec((N_PAIRS, LANES, PAIR_V)),
        _const_spec((LANES, PAIR_V)),
        _const_spec((N_META, D_MODEL)),
        _const_spec((N_META, LANES)), _const_spec((N_META, LANES)),
        _const_spec((CHUNK, RET_QK_WIDTH)),
    ]
    slot_scratch = [
        pltpu.VMEM((CONV_HIST + tt, LRU_WIDTH), f32),
        pltpu.VMEM((tt, LRU_WIDTH), f32),
        pltpu.VMEM((tt, RET_QK_WIDTH), f32),
        pltpu.VMEM((tt, RET_QK_WIDTH), f32),
        pltpu.VMEM((tt, RET_WIDTH), bf16),
        pltpu.VMEM((tt, RET_WIDTH), f32),
    ]
    scratch = [pltpu.VMEM((tt, D_MODEL), bf16)] * 2 + slot_scratch + slot_scratch + [
        pltpu.VMEM((tt, LRU_WIDTH), f32),
        pltpu.VMEM((tt, LRU_WIDTH), bf16),
        pltpu.VMEM((tt, 2 * LRU_WIDTH), f32),
        pltpu.VMEM((tt, MIX_WIDTH), bf16),
        pltpu.VMEM((CONV_HIST, LRU_WIDTH), f32),
        pltpu.VMEM((SUBLANES, LRU_WIDTH), f32),
        pltpu.VMEM((N_PAIRS, LANES, PAIR_V), f32),
        pltpu.VMEM((CONV_HIST, LRU_WIDTH), f32),
        pltpu.VMEM((SUBLANES, LRU_WIDTH), f32),
        pltpu.VMEM((N_PAIRS, LANES, PAIR_V), f32),
        pltpu.VMEM((tt, D_MODEL), f32),
        pltpu.VMEM((X_SLOTS, ROW_GROUPS, SUBLANES, D_MODEL), f32),
        pltpu.VMEM((OUT_SLOTS, ROW_GROUPS, SUBLANES, D_MODEL), f32),
        pltpu.SemaphoreType.DMA((X_SLOTS,)),
        pltpu.SemaphoreType.DMA((OUT_SLOTS,)),
        pltpu.VMEM((D_MODEL, IN_WIDTH), f32),
        pltpu.VMEM((MIX_WIDTH, D_MODEL), f32),
        pltpu.SemaphoreType.DMA((N_W_CHUNKS + 1,)),
    ]
    out = pl.pallas_call(
        functools.partial(_main_kernel, n_tiles, n_total),
        grid=(n_total + 1,),
        in_specs=in_specs,
        out_specs=pl.BlockSpec(memory_space=pl.ANY),
        out_shape=jax.ShapeDtypeStruct((B, S, D_MODEL), x.dtype),
        scratch_shapes=scratch,
        compiler_params=pltpu.CompilerParams(
            dimension_semantics=("arbitrary",),
            vmem_limit_bytes=VMEM_LIMIT_BYTES),
        name="hybrid_main",
    )(x, cos_blocked, sin_blocked, g1, w_in, convw, convb, w_rg, w_ig, brg, big, lam, rng, w_out, fng,
      dmask_blocked, k_dec_blocked, q_dec_blocked, even, odd, g_state, bd_mask, meta_tokens, cos_m, sin_m, k_dec)
    return out
```

```python
import functools

import numpy as np
import jax
import jax.numpy as jnp
from jax import lax
from jax.experimental import pallas as pl
from jax.experimental.pallas import tpu as pltpu

f32 = jnp.float32
bf16 = jnp.bfloat16

D_MODEL = 1024
N_META = 16
LRU_WIDTH = 1024
LRU_HEADS = 8
LRU_BLOCK = 128
CONV_WIDTH = 4
LRU_C = 8.0
RET_HEADS = 8
RET_QK_DIM = 64
RET_V_DIM = 128
RET_QK_WIDTH = 512
RET_WIDTH = 1024
CHUNK = 128
ROPE_BASE = 10000.0
MIX_WIDTH = 2048
EPS = 1e-6
QK_SCALE = RET_QK_DIM ** -0.5

OFF_LX, OFF_GATE, OFF_Q, OFF_K, OFF_V, OFF_RG = 0, 1024, 2048, 2560, 3072, 4096
IN_WIDTH = 5120

LANES = 128
SUBLANES = 8
N_PAIRS = RET_HEADS // 2
PAIR_V = 2 * RET_V_DIM
TILE_T = 256
VMEM_LIMIT_BYTES = 60000 * 1024


def _lane_head():
    return np.arange(RET_QK_WIDTH) // RET_QK_DIM


BLOCK_STEPS = CHUNK // SUBLANES
CHUNK_ROW_TIME = np.arange(CHUNK).reshape(SUBLANES, BLOCK_STEPS).T.reshape(-1)


def _chunk_rows_blocked(table):
    n = table.shape[0]
    return table.reshape((n // CHUNK, CHUNK) + table.shape[1:])[:, CHUNK_ROW_TIME].reshape(table.shape)


def _retention_tables():
    log_g = np.log1p(-np.exp2(-5.0 - np.arange(RET_HEADS, dtype=np.float32))).astype(np.float32)
    idx = np.arange(CHUNK, dtype=np.float32)
    diff = idx[:, None] - idx[None, :]
    dmask = np.where(diff[None] >= 0.0, np.exp(np.maximum(diff, 0.0)[None] * log_g[:, None, None]), 0.0)
    dmask_pair = np.concatenate([dmask[0::2], dmask[1::2]], axis=-1)
    lg_lane = log_g[_lane_head()]
    k_dec = np.exp((CHUNK - 1.0 - idx)[:, None] * lg_lane[None, :])
    q_dec = np.exp((idx + 1.0)[:, None] * lg_lane[None, :]) * QK_SCALE
    even = (_lane_head() % 2 == 0)[None, :]
    g_chunk = np.exp(CHUNK * log_g)
    g_state = np.broadcast_to(np.repeat(g_chunk, RET_V_DIM).reshape(N_PAIRS, 1, PAIR_V), (N_PAIRS, LANES, PAIR_V))
    row_par = np.arange(LANES) // RET_QK_DIM
    col_par = np.arange(PAIR_V) // RET_V_DIM
    bd_mask = row_par[:, None] == col_par[None, :]
    key_order = np.concatenate([CHUNK_ROW_TIME, CHUNK + CHUNK_ROW_TIME])
    dmask_blocked = dmask_pair[:, CHUNK_ROW_TIME][:, :, key_order]
    as_f32 = lambda a: jnp.asarray(np.asarray(a, np.float32))
    return tuple(as_f32(t) for t in (k_dec, dmask_blocked, k_dec[CHUNK_ROW_TIME], q_dec[CHUNK_ROW_TIME],
                                     even, ~even, g_state, bd_mask))


def _rotary_tables(n_pos):
    half = RET_QK_DIM // 2
    inv = (np.float32(ROPE_BASE) ** (-np.arange(half, dtype=np.float32) / half)).astype(np.float32)
    ang = np.arange(n_pos).astype(np.float32)[:, None] * inv[None, :]
    cos, sin = np.cos(ang), np.sin(ang)
    cos_t = np.concatenate([cos, cos, cos, cos], axis=-1).astype(np.float32)
    sin_t = np.concatenate([-sin, sin, -sin, sin], axis=-1).astype(np.float32)
    return cos_t, sin_t


def _rmsnorm_rows(x, gain_row):
    ms = jnp.mean(x * x, axis=-1, keepdims=True)
    return x * lax.rsqrt(ms + EPS) * gain_row


def _silu(x):
    hx = 0.5 * x
    return hx * jnp.tanh(hx) + hx


def _half_conv_params(convw_ref, convb_ref):
    return [0.5 * convw_ref[k:k + 1, :] for k in range(CONV_WIDTH)], 0.5 * convb_ref[...]


def _conv(n_rows, lx_scr, convw_ref, convb_ref, xc_scr, xcb_scr):
    base = SUBLANES
    taps, bias = _half_conv_params(convw_ref, convb_ref)
    xc = bias + taps[3] * lx_scr[pl.ds(base, n_rows), :]
    xc = xc + taps[2] * lx_scr[pl.ds(base - 1, n_rows), :]
    xc = xc + taps[1] * lx_scr[pl.ds(base - 2, n_rows), :]
    xc = xc + taps[0] * lx_scr[pl.ds(base - 3, n_rows), :]
    xc_scr[...] = xc
    xcb_scr[...] = xc.astype(bf16)


CONV_HIST = (CONV_WIDTH - 1) * SUBLANES


def _conv_blocked(lx_scr, convw_ref, convb_ref, xc_scr, xcb_scr):
    taps, bias = _half_conv_params(convw_ref, convb_ref)
    last_sublane = lax.broadcasted_iota(jnp.int32, (SUBLANES, LRU_WIDTH), 0) == SUBLANES - 1
    for c in range(TILE_T // CHUNK):
        base = CONV_HIST + c * CHUNK
        rows = lambda group, n: lx_scr[base + group * SUBLANES:base + (group + n) * SUBLANES, :]
        wrapped = []
        for k in range(1, CONV_WIDTH):
            own, prev = rows(BLOCK_STEPS - k, 1), rows(-k, 1)
            wrapped.append(pltpu.roll(jnp.where(last_sublane, prev, own), 1, 0))
        xc = bias + taps[CONV_WIDTH - 1] * rows(0, BLOCK_STEPS)
        for k in range(1, CONV_WIDTH):
            shifted = jnp.concatenate(wrapped[:k][::-1] + [rows(0, BLOCK_STEPS - k)], axis=0)
            xc = xc + taps[CONV_WIDTH - 1 - k] * shifted
        xc_scr[c * CHUNK:(c + 1) * CHUNK, :] = xc
        xcb_scr[c * CHUNK:(c + 1) * CHUNK, :] = xc.astype(bf16)


def _gates_head(h, xcb_scr, wrg_ref, wig_ref, pre_scr):
    lanes = slice(h * LRU_BLOCK, (h + 1) * LRU_BLOCK)
    wg = jnp.concatenate([wrg_ref[h], wig_ref[h]], axis=-1).astype(bf16)
    pre = jnp.dot(xcb_scr[:, lanes], wg, preferred_element_type=f32)
    pre_scr[:, lanes] = pre[:, :LRU_BLOCK]
    pre_scr[:, LRU_WIDTH + h * LRU_BLOCK:LRU_WIDTH + (h + 1) * LRU_BLOCK] = pre[:, LRU_BLOCK:]


def _lru_maps(half_xc, half_pre_r, half_pre_i, half_brg, half_big, half_c_sp):
    nl = jnp.tanh(half_pre_r + half_brg) * half_c_sp + half_c_sp
    twice_i = jnp.tanh(half_pre_i + half_big) + 1.0
    a = jnp.exp(-nl)
    z = jnp.tanh(nl) * (1.0 + a * a)
    beta = jnp.where(z > 0.0, z * lax.rsqrt(z), 0.0)
    return a, beta * twice_i * half_xc


def _scan_sublanes(a, b):
    rowid = lax.broadcasted_iota(jnp.int32, a.shape, 0)
    for s in (1, 2, 4):
        keep = rowid >= s
        a_s = jnp.where(keep, pltpu.roll(a, s, 0), 1.0)
        b_s = jnp.where(keep, pltpu.roll(b, s, 0), 0.0)
        b = a * b_s + b
        a = a * a_s
    return a, b


def _lru_chunk_blocked(xc, pre_r, pre_i, brg, big, c_sp, carry):
    a, b = _lru_maps(xc, pre_r, pre_i, brg, big, c_sp)
    width = xc.shape[1]
    a3 = a.reshape(BLOCK_STEPS, SUBLANES, width)
    b3 = b.reshape(BLOCK_STEPS, SUBLANES, width)
    decay, local = [a3[0]], [b3[0]]
    for j in range(1, BLOCK_STEPS):
        local.append(a3[j] * local[-1] + b3[j])
        decay.append(a3[j] * decay[-1])
    a_blocks, b_blocks = _scan_sublanes(decay[-1], local[-1])
    after = a_blocks * carry + b_blocks
    rowid = lax.broadcasted_iota(jnp.int32, carry.shape, 0)
    before = jnp.where(rowid >= 1, pltpu.roll(after, 1, 0), carry)
    hs = [local[j] + decay[j] * before for j in range(BLOCK_STEPS)]
    new_carry = jnp.broadcast_to(after[SUBLANES - 1:SUBLANES, :], carry.shape)
    return jnp.concatenate(hs, axis=0), new_carry


def _lru_block(xc, pre_r, pre_i, brg, big, c_sp, carry):
    rows, width = xc.shape
    a, b = _lru_maps(xc, pre_r, pre_i, brg, big, c_sp)
    nv = rows // SUBLANES
    a3 = a.reshape(nv, SUBLANES, width)
    b3 = b.reshape(nv, SUBLANES, width)
    rowid = lax.broadcasted_iota(jnp.int32, (nv, SUBLANES, width), 1)
    for s in (1, 2, 4):
        keep = rowid >= s
        a_s = jnp.where(keep, pltpu.roll(a3, s, 1), 1.0)
        b_s = jnp.where(keep, pltpu.roll(b3, s, 1), 0.0)
        b3 = a3 * b_s + b3
        a3 = a3 * a_s
    hs = []
    for v in range(nv):
        h_v = a3[v] * carry + b3[v]
        carry = h_v[SUBLANES - 1:SUBLANES, :]
        hs.append(h_v)
    return jnp.concatenate(hs, axis=0), carry


def _rotary(t, cos, sin):
    half = RET_QK_DIM // 2
    in_first_half = (lax.broadcasted_iota(jnp.int32, t.shape, 1) // half) % 2 == 0
    partner = jnp.where(in_first_half, pltpu.roll(t, LANES - half, 1), pltpu.roll(t, half, 1))
    return t * cos + partner * sin


def _dot_t0(a, b):
    return lax.dot_general(a, b, (((0,), (0,)), ((), ())), preferred_element_type=f32)


def _dot_nt(a, b):
    return lax.dot_general(a, b, (((1,), (1,)), ((), ())), preferred_element_type=f32)


def _meta_states(meta_ref, cos_ref, sin_ref, g1_ref, wlx_ref, wk_ref, wv_ref, convw_ref, convb_ref,
                 wrg_ref, wig_ref, brg_ref, big_ref, lam_ref, kdec_ref, bdm_ref,
                 lx_out, h_out, st_out,
                 lx_scr, xc_scr, xcb_scr, pre_scr):
    u = _rmsnorm_rows(meta_ref[...], g1_ref[...]).astype(bf16)
    lx_scr[0:SUBLANES, :] = jnp.zeros((SUBLANES, LRU_WIDTH), f32)
    lx_scr[SUBLANES:SUBLANES + N_META, :] = jnp.dot(u, wlx_ref[...].astype(bf16), preferred_element_type=f32)
    k = jnp.dot(u, wk_ref[...].astype(bf16), preferred_element_type=f32)
    v = jnp.dot(u, wv_ref[...].astype(bf16), preferred_element_type=f32).astype(bf16)

    _conv(N_META, lx_scr, convw_ref, convb_ref, xc_scr, xcb_scr)
    for h in range(LRU_HEADS):
        _gates_head(h, xcb_scr, wrg_ref, wig_ref, pre_scr)
    for back in range(1, CONV_WIDTH):
        row = lx_scr[SUBLANES + N_META - back:SUBLANES + N_META - back + 1, :]
        group = CONV_WIDTH - 1 - back
        lx_out[group * SUBLANES:(group + 1) * SUBLANES, :] = jnp.broadcast_to(row, (SUBLANES, LRU_WIDTH))
    half_c_sp = 0.5 * LRU_C * jax.nn.softplus(-lam_ref[...])
    _, carry = _lru_block(xc_scr[...], pre_scr[:, :LRU_WIDTH], pre_scr[:, LRU_WIDTH:],
                          0.5 * brg_ref[...], 0.5 * big_ref[...], half_c_sp, jnp.zeros((1, LRU_WIDTH), f32))
    h_out[...] = jnp.broadcast_to(carry, (SUBLANES, LRU_WIDTH))

    cos, sin = cos_ref[...], sin_ref[...]
    for p in range(N_PAIRS):
        sl = slice(p * LANES, (p + 1) * LANES)
        k_rot = _rotary(k[:, sl], cos, sin)
        kd = (k_rot * kdec_ref[CHUNK - N_META:CHUNK, sl]).astype(bf16)
        st_out[p] = _dot_t0(kd, v[:, p * PAIR_V:(p + 1) * PAIR_V]) * bdm_ref[...]


def _norm_tile(x_tile, g1_ref, u_scr):
    u_scr[...] = _rmsnorm_rows(x_tile, g1_ref[...]).astype(bf16)


MXU_COLS = 256
N_IN_TILES = IN_WIDTH // MXU_COLS
W_CHUNK_COLS = 4 * MXU_COLS
N_W_CHUNKS = IN_WIDTH // W_CHUNK_COLS


def _project_col_tile(t, u_scr, win_ref, slot):
    lx_scr, gate_scr, q_scr, k_scr, v_scr, rg_scr = slot
    off = t * MXU_COLS
    res = jnp.dot(u_scr[...], win_ref[:, off:off + MXU_COLS].astype(bf16), preferred_element_type=f32)
    for dst, start, rows, dt in ((lx_scr, OFF_LX, slice(CONV_HIST, CONV_HIST + TILE_T), f32), (gate_scr, OFF_GATE, slice(None), f32),
                                 (q_scr, OFF_Q, slice(None), f32), (k_scr, OFF_K, slice(None), f32),
                                 (v_scr, OFF_V, slice(None), bf16), (rg_scr, OFF_RG, slice(None), f32)):
        width = dst.shape[1]
        if start <= off < start + width:
            dst[rows, off - start:off - start + MXU_COLS] = res.astype(dt)
            return
    raise AssertionError("column tile outside the projection")


def _finish_tile(load_x, cos_ref, sin_ref, convw_ref, convb_ref, wrg_ref, wig_ref, brg_ref, big_ref, lam_ref, rng_ref,
                 wout_ref, dmaskp_ref, kdec_ref, qdec_ref, even_ref, odd_ref, gst_ref, bdm_ref,
                 slot, xc_scr, xcb_scr, pre_scr, y_scr, lxtail_scr, h_scr, st_scr, after_gates):
    lx_scr, gate_scr, q_scr, k_scr, v_scr, rg_scr = slot
    tt = TILE_T

    def retention_unit(c, p):
        rows = slice(c * CHUNK, (c + 1) * CHUNK)
        cos, sin = cos_ref[rows, :], sin_ref[rows, :]
        sl = slice(p * LANES, (p + 1) * LANES)
        q_rot = _rotary(q_scr[rows, sl], cos, sin)
        k_rot = _rotary(k_scr[rows, sl], cos, sin)
        q_b = (q_rot * QK_SCALE).astype(bf16)
        q_d = (q_rot * qdec_ref[:, sl]).astype(bf16)
        k_d = (k_rot * kdec_ref[:, sl]).astype(bf16)
        k_b = k_rot.astype(bf16)
        kk = jnp.concatenate([k_b * even_ref[:, sl].astype(bf16), k_b * odd_ref[:, sl].astype(bf16)], axis=0)
        v_pair = v_scr[rows, p * PAIR_V:(p + 1) * PAIR_V]
        zeros = jnp.zeros((CHUNK, RET_V_DIM), bf16)
        v_bd = jnp.concatenate([jnp.concatenate([v_pair[:, :RET_V_DIM], zeros], axis=1),
                                jnp.concatenate([zeros, v_pair[:, RET_V_DIM:]], axis=1)], axis=0)
        st_pair = st_scr[p]
        s = _dot_nt(q_b, kk) * dmaskp_ref[p]
        o = jnp.dot(jnp.concatenate([s.astype(bf16), q_d], axis=1),
                    jnp.concatenate([v_bd, st_pair.astype(bf16)], axis=0), preferred_element_type=f32)
        for e in range(2):
            h = 2 * p + e
            hs = slice(h * RET_V_DIM, (h + 1) * RET_V_DIM)
            o_h = o[:, e * RET_V_DIM:(e + 1) * RET_V_DIM]
            mu = jnp.mean(o_h, axis=-1, keepdims=True)
            oc = o_h - mu
            var = jnp.mean(oc * oc, axis=-1, keepdims=True)
            on = oc * lax.rsqrt(var + EPS) * rng_ref[:, hs]
            y_scr[rows, LRU_WIDTH + h * RET_V_DIM:LRU_WIDTH + (h + 1) * RET_V_DIM] = (
                on * _silu(rg_scr[rows, hs])).astype(bf16)
        kv = _dot_t0(k_d, v_pair)
        st_scr[p] = gst_ref[p] * st_pair + bdm_ref[...] * kv

    lx_scr[0:CONV_HIST, :] = lxtail_scr[...]
    _conv_blocked(lx_scr, convw_ref, convb_ref, xc_scr, xcb_scr)
    lxtail_scr[...] = lx_scr[pl.ds(tt, CONV_HIST), :]
    for h in range(LRU_HEADS):
        lanes = slice(h * LRU_BLOCK, (h + 1) * LRU_BLOCK)
        ilanes = slice(LRU_WIDTH + h * LRU_BLOCK, LRU_WIDTH + (h + 1) * LRU_BLOCK)
        _gates_head(h, xcb_scr, wrg_ref, wig_ref, pre_scr)
        after_gates(h)
        full = (CHUNK, LRU_BLOCK)
        c_sp = jnp.broadcast_to(0.5 * LRU_C * jax.nn.softplus(-lam_ref[:, lanes]), full)
        brg, big = jnp.broadcast_to(0.5 * brg_ref[:, lanes], full), jnp.broadcast_to(0.5 * big_ref[:, lanes], full)
        carry = h_scr[:, lanes]
        for c in range(tt // CHUNK):
            rows = slice(c * CHUNK, (c + 1) * CHUNK)
            hb, carry = _lru_chunk_blocked(xc_scr[rows, lanes], pre_scr[rows, lanes], pre_scr[rows, ilanes],
                                           brg, big, c_sp, carry)
            y_scr[rows, lanes] = (hb * _silu(gate_scr[rows, lanes])).astype(bf16)
        h_scr[:, lanes] = carry

    for c in range(tt // CHUNK):
        for p in range(N_PAIRS):
            retention_unit(c, p)

    return load_x() + jnp.dot(y_scr[...], wout_ref[...].astype(bf16), preferred_element_type=f32)


X_SLOTS = 4
OUT_SLOTS = 2
ROW_GROUPS = TILE_T // SUBLANES


def _tile_block_copies(hbm_ref, vmem_ref, sem_ref, tile, slot, n_tiles, to_hbm):
    b = lax.div(tile, n_tiles)
    first_row = (tile - b * n_tiles) * TILE_T
    copies = []
    for c in range(TILE_T // CHUNK):
        for s in range(SUBLANES):
            hbm = hbm_ref.at[b, pl.ds(first_row + c * CHUNK + s * BLOCK_STEPS, BLOCK_STEPS), :]
            vmem = vmem_ref.at[slot, pl.ds(c * BLOCK_STEPS, BLOCK_STEPS), s, :]
            src, dst = (vmem, hbm) if to_hbm else (hbm, vmem)
            copies.append(pltpu.make_async_copy(src, dst, sem_ref.at[slot]))
    return copies


def _main_kernel(n_tiles, n_total,
                 x_hbm, cos_ref, sin_ref, g1_ref, win_hbm, convw_ref, convb_ref, wrg_ref, wig_ref,
                 brg_ref, big_ref, lam_ref, rng_ref, wout_hbm, fng_ref, dmaskp_ref, kdec_ref, qdec_ref, even_ref, odd_ref,
                 gst_ref, bdm_ref, meta_ref, cos_meta_ref, sin_meta_ref, kdec_time_ref,
                 o_hbm,
                 u_a, u_b,
                 lx_a, gate_a, q_a, k_a, v_a, rg_a,
                 lx_b, gate_b, q_b, k_b, v_b, rg_b,
                 xc_scr, xcb_scr, pre_scr, y_scr, lxtail_scr, h_scr, st_scr,
                 lx0_ref, h0_ref, st0_ref, res_scr,
                 xin_scr, out_scr, xin_sem, out_sem,
                 win_ref, wout_ref, w_sem):
    g = pl.program_id(0)
    slots = ((lx_a, gate_a, q_a, k_a, v_a, rg_a), (lx_b, gate_b, q_b, k_b, v_b, rg_b))
    win_copies = [pltpu.make_async_copy(win_hbm.at[0, :, pl.ds(c * W_CHUNK_COLS, W_CHUNK_COLS)],
                                        win_ref.at[:, pl.ds(c * W_CHUNK_COLS, W_CHUNK_COLS)], w_sem.at[c])
                  for c in range(N_W_CHUNKS)]
    wout_copy = pltpu.make_async_copy(wout_hbm.at[0], wout_ref, w_sem.at[N_W_CHUNKS])
    x_slot = lambda tile: lax.rem(tile + X_SLOTS, X_SLOTS)
    out_slot = lambda tile: lax.rem(tile + OUT_SLOTS, OUT_SLOTS)
    x_tile = lambda tile: xin_scr[x_slot(tile)].reshape(TILE_T, D_MODEL)
    fetch = lambda tile: _tile_block_copies(x_hbm, xin_scr, xin_sem, jnp.minimum(tile, n_total - 1), x_slot(tile),
                                            n_tiles, to_hbm=False)
    write_back = lambda tile: _tile_block_copies(o_hbm, out_scr, out_sem, tile, out_slot(tile), n_tiles, to_hbm=True)

    @pl.when(g >= 2 + OUT_SLOTS)
    def _():
        for copy in write_back(g - 2 - OUT_SLOTS):
            copy.wait()

    @pl.when(lax.rem(g + n_tiles - 1, n_tiles) == 0)
    def _():
        lxtail_scr[...] = lx0_ref[...]
        h_scr[...] = h0_ref[...]
        st_scr[...] = st0_ref[...]

    tiles_after_head = (4, 4, 3, 3, 2, 2, 1, 1)
    assert sum(tiles_after_head) == N_IN_TILES and len(tiles_after_head) == LRU_HEADS

    def finish(slot_finish, after_gates):
        res_scr[...] = _finish_tile(
            lambda: x_tile(g - 1), cos_ref, sin_ref, convw_ref, convb_ref, wrg_ref, wig_ref, brg_ref, big_ref,
            lam_ref, rng_ref, wout_ref, dmaskp_ref, kdec_ref, qdec_ref, even_ref, odd_ref, gst_ref,
            bdm_ref, slot_finish, xc_scr, xcb_scr, pre_scr, y_scr, lxtail_scr, h_scr, st_scr, after_gates)

    def final_norm(tile):
        out = _rmsnorm_rows(res_scr[...], fng_ref[...])
        out_scr[out_slot(tile)] = out.reshape(ROW_GROUPS, SUBLANES, D_MODEL)

    def norm_and_write_back(tile):
        final_norm(tile)
        for copy in write_back(tile):
            copy.start()

    final_norm_after_head, pre_norm_after_head = 4, 6

    def step(u_project, u_next, slot_project, slot_finish, norm_previous=True):
        for copy in fetch(g + 1):
            copy.wait()
        for copy in fetch(g + 2):
            copy.start()

        def after_gates(h):
            first = sum(tiles_after_head[:h])
            for t in range(first, first + tiles_after_head[h]):
                _project_col_tile(t, u_project, win_ref, slot_project)
            if norm_previous and h == final_norm_after_head:
                final_norm(g - 2)
            if h == pre_norm_after_head:
                _norm_tile(x_tile(g + 1), g1_ref, u_next)

        finish(slot_finish, after_gates)
        if norm_previous:
            for copy in write_back(g - 2):
                copy.start()

    @pl.when(g == 0)
    def _():
        for copy in fetch(g):
            copy.start()
        for copy in win_copies + [wout_copy]:
            copy.start()
        for copy in fetch(g):
            copy.wait()
        for tile in (g + 1, g + 2):
            for copy in fetch(tile):
                copy.start()
        _norm_tile(x_tile(g), g1_ref, u_a)
        for t in range(N_IN_TILES):
            if t % (W_CHUNK_COLS // MXU_COLS) == 0:
                win_copies[t // (W_CHUNK_COLS // MXU_COLS)].wait()
            _project_col_tile(t, u_a, win_ref, slots[0])
        meta_rows = lambda ref, n: ref.at[pl.ds(0, n)]
        _meta_states(meta_ref, cos_meta_ref, sin_meta_ref, g1_ref,
                     win_ref.at[:, pl.ds(OFF_LX, LRU_WIDTH)], win_ref.at[:, pl.ds(OFF_K, RET_QK_WIDTH)],
                     win_ref.at[:, pl.ds(OFF_V, RET_WIDTH)], convw_ref, convb_ref, wrg_ref, wig_ref, brg_ref, big_ref,
                     lam_ref, kdec_time_ref, bdm_ref, lx0_ref, h0_ref, st0_ref,
                     meta_rows(slots[1][0], SUBLANES + N_META), meta_rows(xc_scr, N_META), meta_rows(xcb_scr, N_META),
                     meta_rows(pre_scr, N_META))
        for copy in fetch(g + 1):
            copy.wait()
        _norm_tile(x_tile(g + 1), g1_ref, u_b)

    @pl.when(g == 1)
    def _():
        wout_copy.wait()
        step(u_b, u_a, slots[1], slots[0], norm_previous=False)

    regular = jnp.logical_and(g >= 2, g < n_total)

    @pl.when(jnp.logical_and(regular, lax.rem(g, 2) == 0))
    def _():
        step(u_a, u_b, slots[0], slots[1])

    @pl.when(jnp.logical_and(regular, lax.rem(g, 2) == 1))
    def _():
        step(u_b, u_a, slots[1], slots[0])

    @pl.when(g == n_total)
    def _():
        for copy in fetch(g + 1):
            copy.wait()
        norm_and_write_back(g - 2)
        finish(slots[(n_total - 1) % 2], lambda h: None)
        for copy in write_back(g - 1 - OUT_SLOTS):
            copy.wait()
        norm_and_write_back(g - 1)
        for tile in (g - 2, g - 1):
            for copy in write_back(tile):
                copy.wait()


def _const_spec(shape):
    nd = len(shape)
    return pl.BlockSpec(shape, lambda g, _nd=nd: (0,) * _nd, pipeline_mode=pl.Buffered(1))


def kernel(x, meta_tokens, norm_gain, w_in, conv_w, conv_b, w_rg, b_rg, w_ig, b_ig,
           lru_lambda, ret_norm_gain, w_out, final_norm_gain):
    B, S, D = x.shape
    assert D == D_MODEL and S % TILE_T == 0 and TILE_T % CHUNK == 0
    assert norm_gain.shape[0] == 1, "single-layer block"
    assert meta_tokens.shape == (N_META, D_MODEL)

    assert w_in.shape == (1, D_MODEL, IN_WIDTH) and w_out.shape == (1, MIX_WIDTH, D_MODEL)
    assert w_rg.shape == w_ig.shape == (1, LRU_HEADS, LRU_BLOCK, LRU_BLOCK)

    g1 = norm_gain[0].reshape(1, D_MODEL)
    fng = final_norm_gain.reshape(1, D_MODEL)
    convw = conv_w[0]
    convb = conv_b[0].reshape(1, LRU_WIDTH)
    brg = b_rg[0].reshape(1, LRU_WIDTH)
    big = b_ig[0].reshape(1, LRU_WIDTH)
    lam = lru_lambda[0].reshape(1, LRU_WIDTH)
    rng = ret_norm_gain[0].reshape(1, RET_WIDTH)

    k_dec, dmask_blocked, k_dec_blocked, q_dec_blocked, even, odd, g_state, bd_mask = _retention_tables()
    cos_t, sin_t = _rotary_tables(N_META + S)
    cos_m, sin_m = jnp.asarray(cos_t[:N_META]), jnp.asarray(sin_t[:N_META])
    cos_blocked = jnp.asarray(_chunk_rows_blocked(cos_t[N_META:]))
    sin_blocked = jnp.asarray(_chunk_rows_blocked(sin_t[N_META:]))

    tt = TILE_T
    n_tiles = S // tt
    n_total = B * n_tiles
    assert n_total >= 3, "the four-stage tile pipeline needs at least three tiles"

    rot_spec = pl.BlockSpec((tt, LANES), lambda g: (jnp.maximum(g - 1, 0) % n_tiles, 0))
    in_specs = [
        pl.BlockSpec(memory_space=pl.ANY),
        rot_spec, rot_spec,
        _const_spec((1, D_MODEL)),
        pl.BlockSpec(memory_space=pl.ANY),
        _const_spec((CONV_WIDTH, LRU_WIDTH)), _const_spec((1, LRU_WIDTH)),
        _const_spec((None, LRU_HEADS, LRU_BLOCK, LRU_BLOCK)),
        _const_spec((None, LRU_HEADS, LRU_BLOCK, LRU_BLOCK)),
        _const_spec((1, LRU_WIDTH)), _const_spec((1, LRU_WIDTH)), _const_spec((1, LRU_WIDTH)),
        _const_spec((1, RET_WIDTH)),
        pl.BlockSpec(memory_space=pl.ANY),
        _const_spec((1, D_MODEL)),
        _const_spec((N_PAIRS, CHUNK, 2 * CHUNK)),
        _const_spec((CHUNK, RET_QK_WIDTH)), _const_spec((CHUNK, RET_QK_WIDTH)),
        _const_spec((1, RET_QK_WIDTH)), _const_spec((1, RET_QK_WIDTH)),
        _const_spec((N_PAIRS, LANES, PAIR_V)),
        _const_spec((LANES, PAIR_V)),
        _const_spec((N_META, D_MODEL)),
        _const_spec((N_META, LANES)), _const_spec((N_META, LANES)),
        _const_spec((CHUNK, RET_QK_WIDTH)),
    ]
    slot_scratch = [
        pltpu.VMEM((CONV_HIST + tt, LRU_WIDTH), f32),
        pltpu.VMEM((tt, LRU_WIDTH), f32),
        pltpu.VMEM((tt, RET_QK_WIDTH), f32),
        pltpu.VMEM((tt, RET_QK_WIDTH), f32),
        pltpu.VMEM((tt, RET_WIDTH), bf16),
        pltpu.VMEM((tt, RET_WIDTH), f32),
    ]
    scratch = [pltpu.VMEM((tt, D_MODEL), bf16)] * 2 + slot_scratch + slot_scratch + [
        pltpu.VMEM((tt, LRU_WIDTH), f32),
        pltpu.VMEM((tt, LRU_WIDTH), bf16),
        pltpu.VMEM((tt, 2 * LRU_WIDTH), f32),
        pltpu.VMEM((tt, MIX_WIDTH), bf16),
        pltpu.VMEM((CONV_HIST, LRU_WIDTH), f32),
        pltpu.VMEM((SUBLANES, LRU_WIDTH), f32),
        pltpu.VMEM((N_PAIRS, LANES, PAIR_V), f32),
        pltpu.VMEM((CONV_HIST, LRU_WIDTH), f32),
        pltpu.VMEM((SUBLANES, LRU_WIDTH), f32),
        pltpu.VMEM((N_PAIRS, LANES, PAIR_V), f32),
        pltpu.VMEM((tt, D_MODEL), f32),
        pltpu.VMEM((X_SLOTS, ROW_GROUPS, SUBLANES, D_MODEL), f32),
        pltpu.VMEM((OUT_SLOTS, ROW_GROUPS, SUBLANES, D_MODEL), f32),
        pltpu.SemaphoreType.DMA((X_SLOTS,)),
        pltpu.SemaphoreType.DMA((OUT_SLOTS,)),
        pltpu.VMEM((D_MODEL, IN_WIDTH), f32),
        pltpu.VMEM((MIX_WIDTH, D_MODEL), f32),
        pltpu.SemaphoreType.DMA((N_W_CHUNKS + 1,)),
    ]
    out = pl.pallas_call(
        functools.partial(_main_kernel, n_tiles, n_total),
        grid=(n_total + 1,),
        in_specs=in_specs,
        out_specs=pl.BlockSpec(memory_space=pl.ANY),
        out_shape=jax.ShapeDtypeStruct((B, S, D_MODEL), x.dtype),
        scratch_shapes=scratch,
        compiler_params=pltpu.CompilerParams(
            dimension_semantics=("arbitrary",),
            vmem_limit_bytes=VMEM_LIMIT_BYTES),
        name="hybrid_main",
    )(x, cos_blocked, sin_blocked, g1, w_in, convw, convb, w_rg, w_ig, brg, big, lam, rng, w_out, fng,
      dmask_blocked, k_dec_blocked, q_dec_blocked, even, odd, g_state, bd_mask, meta_tokens, cos_m, sin_m, k_dec)
    return out
```

```python
import functools

import numpy as np
import jax
import jax.numpy as jnp
from jax import lax
from jax.experimental import pallas as pl
from jax.experimental.pallas import tpu as pltpu

f32 = jnp.float32
bf16 = jnp.bfloat16

D_MODEL = 1024
N_META = 16
LRU_WIDTH = 1024
LRU_HEADS = 8
LRU_BLOCK = 128
CONV_WIDTH = 4
LRU_C = 8.0
RET_HEADS = 8
RET_QK_DIM = 64
RET_V_DIM = 128
RET_QK_WIDTH = 512
RET_WIDTH = 1024
CHUNK = 128
ROPE_BASE = 10000.0
MIX_WIDTH = 2048
EPS = 1e-6
QK_SCALE = RET_QK_DIM ** -0.5

OFF_LX, OFF_GATE, OFF_Q, OFF_K, OFF_V, OFF_RG = 0, 1024, 2048, 2560, 3072, 4096
IN_WIDTH = 5120

LANES = 128
SUBLANES = 8
N_PAIRS = RET_HEADS // 2
PAIR_V = 2 * RET_V_DIM
TILE_T = 256
VMEM_LIMIT_BYTES = 60000 * 1024


def _lane_head():
    return np.arange(RET_QK_WIDTH) // RET_QK_DIM


BLOCK_STEPS = CHUNK // SUBLANES
CHUNK_ROW_TIME = np.arange(CHUNK).reshape(SUBLANES, BLOCK_STEPS).T.reshape(-1)


def _chunk_rows_blocked(table):
    n = table.shape[0]
    return table.reshape((n // CHUNK, CHUNK) + table.shape[1:])[:, CHUNK_ROW_TIME].reshape(table.shape)


def _retention_tables():
    log_g = np.log1p(-np.exp2(-5.0 - np.arange(RET_HEADS, dtype=np.float32))).astype(np.float32)
    idx = np.arange(CHUNK, dtype=np.float32)
    diff = idx[:, None] - idx[None, :]
    dmask = np.where(diff[None] >= 0.0, np.exp(np.maximum(diff, 0.0)[None] * log_g[:, None, None]), 0.0)
    dmask_pair = np.concatenate([dmask[0::2], dmask[1::2]], axis=-1)
    lg_lane = log_g[_lane_head()]
    k_dec = np.exp((CHUNK - 1.0 - idx)[:, None] * lg_lane[None, :])
    q_dec = np.exp((idx + 1.0)[:, None] * lg_lane[None, :]) * QK_SCALE
    even = (_lane_head() % 2 == 0)[None, :]
    g_chunk = np.exp(CHUNK * log_g)
    g_state = np.broadcast_to(np.repeat(g_chunk, RET_V_DIM).reshape(N_PAIRS, 1, PAIR_V), (N_PAIRS, LANES, PAIR_V))
    row_par = np.arange(LANES) // RET_QK_DIM
    col_par = np.arange(PAIR_V) // RET_V_DIM
    bd_mask = row_par[:, None] == col_par[None, :]
    key_order = np.concatenate([CHUNK_ROW_TIME, CHUNK + CHUNK_ROW_TIME])
    dmask_blocked = dmask_pair[:, CHUNK_ROW_TIME][:, :, key_order]
    as_f32 = lambda a: jnp.asarray(np.asarray(a, np.float32))
    return tuple(as_f32(t) for t in (k_dec, dmask_blocked, k_dec[CHUNK_ROW_TIME], q_dec[CHUNK_ROW_TIME],
                                     even, ~even, g_state, bd_mask))


def _rotary_tables(n_pos):
    half = RET_QK_DIM // 2
    inv = (np.float32(ROPE_BASE) ** (-np.arange(half, dtype=np.float32) / half)).astype(np.float32)
    ang = np.arange(n_pos).astype(np.float32)[:, None] * inv[None, :]
    cos, sin = np.cos(ang), np.sin(ang)
    cos_t = np.concatenate([cos, cos, cos, cos], axis=-1).astype(np.float32)
    sin_t = np.concatenate([-sin, sin, -sin, sin], axis=-1).astype(np.float32)
    return cos_t, sin_t


def _rmsnorm_rows(x, gain_row):
    ms = jnp.mean(x * x, axis=-1, keepdims=True)
    return x * lax.rsqrt(ms + EPS) * gain_row


def _silu(x):
    hx = 0.5 * x
    return hx * jnp.tanh(hx) + hx


def _half_conv_params(convw_ref, convb_ref):
    return [0.5 * convw_ref[k:k + 1, :] for k in range(CONV_WIDTH)], 0.5 * convb_ref[...]


def _conv(n_rows, lx_scr, convw_ref, convb_ref, xc_scr, xcb_scr):
    base = SUBLANES
    taps, bias = _half_conv_params(convw_ref, convb_ref)
    xc = bias + taps[3] * lx_scr[pl.ds(base, n_rows), :]
    xc = xc + taps[2] * lx_scr[pl.ds(base - 1, n_rows), :]
    xc = xc + taps[1] * lx_scr[pl.ds(base - 2, n_rows), :]
    xc = xc + taps[0] * lx_scr[pl.ds(base - 3, n_rows), :]
    xc_scr[...] = xc
    xcb_scr[...] = xc.astype(bf16)


CONV_HIST = (CONV_WIDTH - 1) * SUBLANES


def _conv_blocked(lx_scr, convw_ref, convb_ref, xc_scr, xcb_scr):
    taps, bias = _half_conv_params(convw_ref, convb_ref)
    last_sublane = lax.broadcasted_iota(jnp.int32, (SUBLANES, LRU_WIDTH), 0) == SUBLANES - 1
    for c in range(TILE_T // CHUNK):
        base = CONV_HIST + c * CHUNK
        rows = lambda group, n: lx_scr[base + group * SUBLANES:base + (group + n) * SUBLANES, :]
        wrapped = []
        for k in range(1, CONV_WIDTH):
            own, prev = rows(BLOCK_STEPS - k, 1), rows(-k, 1)
            wrapped.append(pltpu.roll(jnp.where(last_sublane, prev, own), 1, 0))
        xc = bias + taps[CONV_WIDTH - 1] * rows(0, BLOCK_STEPS)
        for k in range(1, CONV_WIDTH):
            shifted = jnp.concatenate(wrapped[:k][::-1] + [rows(0, BLOCK_STEPS - k)], axis=0)
            xc = xc + taps[CONV_WIDTH - 1 - k] * shifted
        xc_scr[c * CHUNK:(c + 1) * CHUNK, :] = xc
        xcb_scr[c * CHUNK:(c + 1) * CHUNK, :] = xc.astype(bf16)


def _gates_head(h, xcb_scr, wrg_ref, wig_ref, pre_scr):
    lanes = slice(h * LRU_BLOCK, (h + 1) * LRU_BLOCK)
    wg = jnp.concatenate([wrg_ref[h], wig_ref[h]], axis=-1).astype(bf16)
    pre = jnp.dot(xcb_scr[:, lanes], wg, preferred_element_type=f32)
    pre_scr[:, lanes] = pre[:, :LRU_BLOCK]
    pre_scr[:, LRU_WIDTH + h * LRU_BLOCK:LRU_WIDTH + (h + 1) * LRU_BLOCK] = pre[:, LRU_BLOCK:]


def _lru_maps(half_xc, half_pre_r, half_pre_i, half_brg, half_big, half_c_sp):
    nl = jnp.tanh(half_pre_r + half_brg) * half_c_sp + half_c_sp
    twice_i = jnp.tanh(half_pre_i + half_big) + 1.0
    a = jnp.exp(-nl)
    z = jnp.tanh(nl) * (1.0 + a * a)
    beta = jnp.where(z > 0.0, z * lax.rsqrt(z), 0.0)
    return a, beta * twice_i * half_xc


def _scan_sublanes(a, b):
    rowid = lax.broadcasted_iota(jnp.int32, a.shape, 0)
    for s in (1, 2, 4):
        keep = rowid >= s
        a_s = jnp.where(keep, pltpu.roll(a, s, 0), 1.0)
        b_s = jnp.where(keep, pltpu.roll(b, s, 0), 0.0)
        b = a * b_s + b
        a = a * a_s
    return a, b


def _lru_chunk_blocked(xc, pre_r, pre_i, brg, big, c_sp, carry):
    a, b = _lru_maps(xc, pre_r, pre_i, brg, big, c_sp)
    width = xc.shape[1]
    a3 = a.reshape(BLOCK_STEPS, SUBLANES, width)
    b3 = b.reshape(BLOCK_STEPS, SUBLANES, width)
    decay, local = [a3[0]], [b3[0]]
    for j in range(1, BLOCK_STEPS):
        local.append(a3[j] * local[-1] + b3[j])
        decay.append(a3[j] * decay[-1])
    a_blocks, b_blocks = _scan_sublanes(decay[-1], local[-1])
    after = a_blocks * carry + b_blocks
    rowid = lax.broadcasted_iota(jnp.int32, carry.shape, 0)
    before = jnp.where(rowid >= 1, pltpu.roll(after, 1, 0), carry)
    hs = [local[j] + decay[j] * before for j in range(BLOCK_STEPS)]
    new_carry = jnp.broadcast_to(after[SUBLANES - 1:SUBLANES, :], carry.shape)
    return jnp.concatenate(hs, axis=0), new_carry


def _lru_block(xc, pre_r, pre_i, brg, big, c_sp, carry):
    rows, width = xc.shape
    a, b = _lru_maps(xc, pre_r, pre_i, brg, big, c_sp)
    nv = rows // SUBLANES
    a3 = a.reshape(nv, SUBLANES, width)
    b3 = b.reshape(nv, SUBLANES, width)
    rowid = lax.broadcasted_iota(jnp.int32, (nv, SUBLANES, width), 1)
    for s in (1, 2, 4):
        keep = rowid >= s
        a_s = jnp.where(keep, pltpu.roll(a3, s, 1), 1.0)
        b_s = jnp.where(keep, pltpu.roll(b3, s, 1), 0.0)
        b3 = a3 * b_s + b3
        a3 = a3 * a_s
    hs = []
    for v in range(nv):
        h_v = a3[v] * carry + b3[v]
        carry = h_v[SUBLANES - 1:SUBLANES, :]
        hs.append(h_v)
    return jnp.concatenate(hs, axis=0), carry


def _rotary(t, cos, sin):
    half = RET_QK_DIM // 2
    in_first_half = (lax.broadcasted_iota(jnp.int32, t.shape, 1) // half) % 2 == 0
    partner = jnp.where(in_first_half, pltpu.roll(t, LANES - half, 1), pltpu.roll(t, half, 1))
    return t * cos + partner * sin


def _dot_t0(a, b):
    return lax.dot_general(a, b, (((0,), (0,)), ((), ())), preferred_element_type=f32)


def _dot_nt(a, b):
    return lax.dot_general(a, b, (((1,), (1,)), ((), ())), preferred_element_type=f32)


def _meta_states(meta_ref, cos_ref, sin_ref, g1_ref, wlx_ref, wk_ref, wv_ref, convw_ref, convb_ref,
                 wrg_ref, wig_ref, brg_ref, big_ref, lam_ref, kdec_ref, bdm_ref,
                 lx_out, h_out, st_out,
                 lx_scr, xc_scr, xcb_scr, pre_scr):
    u = _rmsnorm_rows(meta_ref[...], g1_ref[...]).astype(bf16)
    lx_scr[0:SUBLANES, :] = jnp.zeros((SUBLANES, LRU_WIDTH), f32)
    lx_scr[SUBLANES:SUBLANES + N_META, :] = jnp.dot(u, wlx_ref[...].astype(bf16), preferred_element_type=f32)
    k = jnp.dot(u, wk_ref[...].astype(bf16), preferred_element_type=f32)
    v = jnp.dot(u, wv_ref[...].astype(bf16), preferred_element_type=f32).astype(bf16)

    _conv(N_META, lx_scr, convw_ref, convb_ref, xc_scr, xcb_scr)
    for h in range(LRU_HEADS):
        _gates_head(h, xcb_scr, wrg_ref, wig_ref, pre_scr)
    for back in range(1, CONV_WIDTH):
        row = lx_scr[SUBLANES + N_META - back:SUBLANES + N_META - back + 1, :]
        group = CONV_WIDTH - 1 - back
        lx_out[group * SUBLANES:(group + 1) * SUBLANES, :] = jnp.broadcast_to(row, (SUBLANES, LRU_WIDTH))
    half_c_sp = 0.5 * LRU_C * jax.nn.softplus(-lam_ref[...])
    _, carry = _lru_block(xc_scr[...], pre_scr[:, :LRU_WIDTH], pre_scr[:, LRU_WIDTH:],
                          0.5 * brg_ref[...], 0.5 * big_ref[...], half_c_sp, jnp.zeros((1, LRU_WIDTH), f32))
    h_out[...] = jnp.broadcast_to(carry, (SUBLANES, LRU_WIDTH))

    cos, sin = cos_ref[...], sin_ref[...]
    for p in range(N_PAIRS):
        sl = slice(p * LANES, (p + 1) * LANES)
        k_rot = _rotary(k[:, sl], cos, sin)
        kd = (k_rot * kdec_ref[CHUNK - N_META:CHUNK, sl]).astype(bf16)
        st_out[p] = _dot_t0(kd, v[:, p * PAIR_V:(p + 1) * PAIR_V]) * bdm_ref[...]


def _norm_tile(x_tile, g1_ref, u_scr):
    u_scr[...] = _rmsnorm_rows(x_tile, g1_ref[...]).astype(bf16)


MXU_COLS = 256
N_IN_TILES = IN_WIDTH // MXU_COLS
W_CHUNK_COLS = 4 * MXU_COLS
N_W_CHUNKS = IN_WIDTH // W_CHUNK_COLS


def _project_col_tile(t, u_scr, win_ref, slot):
    lx_scr, gate_scr, q_scr, k_scr, v_scr, rg_scr = slot
    off = t * MXU_COLS
    res = jnp.dot(u_scr[...], win_ref[:, off:off + MXU_COLS].astype(bf16), preferred_element_type=f32)
    for dst, start, rows, dt in ((lx_scr, OFF_LX, slice(CONV_HIST, CONV_HIST + TILE_T), f32), (gate_scr, OFF_GATE, slice(None), f32),
                                 (q_scr, OFF_Q, slice(None), f32), (k_scr, OFF_K, slice(None), f32),
                                 (v_scr, OFF_V, slice(None), bf16), (rg_scr, OFF_RG, slice(None), f32)):
        width = dst.shape[1]
        if start <= off < start + width:
            dst[rows, off - start:off - start + MXU_COLS] = res.astype(dt)
            return
    raise AssertionError("column tile outside the projection")


def _finish_tile(load_x, cos_ref, sin_ref, convw_ref, convb_ref, wrg_ref, wig_ref, brg_ref, big_ref, lam_ref, rng_ref,
                 wout_ref, dmaskp_ref, kdec_ref, qdec_ref, even_ref, odd_ref, gst_ref, bdm_ref,
                 slot, xc_scr, xcb_scr, pre_scr, y_scr, lxtail_scr, h_scr, st_scr, after_gates):
    lx_scr, gate_scr, q_scr, k_scr, v_scr, rg_scr = slot
    tt = TILE_T

    def retention_unit(c, p):
        rows = slice(c * CHUNK, (c + 1) * CHUNK)
        cos, sin = cos_ref[rows, :], sin_ref[rows, :]
        sl = slice(p * LANES, (p + 1) * LANES)
        q_rot = _rotary(q_scr[rows, sl], cos, sin)
        k_rot = _rotary(k_scr[rows, sl], cos, sin)
        q_b = (q_rot * QK_SCALE).astype(bf16)
        q_d = (q_rot * qdec_ref[:, sl]).astype(bf16)
        k_d = (k_rot * kdec_ref[:, sl]).astype(bf16)
        k_b = k_rot.astype(bf16)
        kk = jnp.concatenate([k_b * even_ref[:, sl].astype(bf16), k_b * odd_ref[:, sl].astype(bf16)], axis=0)
        v_pair = v_scr[rows, p * PAIR_V:(p + 1) * PAIR_V]
        zeros = jnp.zeros((CHUNK, RET_V_DIM), bf16)
        v_bd = jnp.concatenate([jnp.concatenate([v_pair[:, :RET_V_DIM], zeros], axis=1),
                                jnp.concatenate([zeros, v_pair[:, RET_V_DIM:]], axis=1)], axis=0)
        st_pair = st_scr[p]
        s = _dot_nt(q_b, kk) * dmaskp_ref[p]
        o = jnp.dot(jnp.concatenate([s.astype(bf16), q_d], axis=1),
                    jnp.concatenate([v_bd, st_pair.astype(bf16)], axis=0), preferred_element_type=f32)
        for e in range(2):
            h = 2 * p + e
            hs = slice(h * RET_V_DIM, (h + 1) * RET_V_DIM)
            o_h = o[:, e * RET_V_DIM:(e + 1) * RET_V_DIM]
            mu = jnp.mean(o_h, axis=-1, keepdims=True)
            oc = o_h - mu
            var = jnp.mean(oc * oc, axis=-1, keepdims=True)
            on = oc * lax.rsqrt(var + EPS) * rng_ref[:, hs]
            y_scr[rows, LRU_WIDTH + h * RET_V_DIM:LRU_WIDTH + (h + 1) * RET_V_DIM] = (
                on * _silu(rg_scr[rows, hs])).astype(bf16)
        kv = _dot_t0(k_d, v_pair)
        st_scr[p] = gst_ref[p] * st_pair + bdm_ref[...] * kv

    lx_scr[0:CONV_HIST, :] = lxtail_scr[...]
    _conv_blocked(lx_scr, convw_ref, convb_ref, xc_scr, xcb_scr)
    lxtail_scr[...] = lx_scr[pl.ds(tt, CONV_HIST), :]
    for h in range(LRU_HEADS):
        lanes = slice(h * LRU_BLOCK, (h + 1) * LRU_BLOCK)
        ilanes = slice(LRU_WIDTH + h * LRU_BLOCK, LRU_WIDTH + (h + 1) * LRU_BLOCK)
        _gates_head(h, xcb_scr, wrg_ref, wig_ref, pre_scr)
        after_gates(h)
        full = (CHUNK, LRU_BLOCK)
        c_sp = jnp.broadcast_to(0.5 * LRU_C * jax.nn.softplus(-lam_ref[:, lanes]), full)
        brg, big = jnp.broadcast_to(0.5 * brg_ref[:, lanes], full), jnp.broadcast_to(0.5 * big_ref[:, lanes], full)
        carry = h_scr[:, lanes]
        for c in range(tt // CHUNK):
            rows = slice(c * CHUNK, (c + 1) * CHUNK)
            hb, carry = _lru_chunk_blocked(xc_scr[rows, lanes], pre_scr[rows, lanes], pre_scr[rows, ilanes],
                                           brg, big, c_sp, carry)
            y_scr[rows, lanes] = (hb * _silu(gate_scr[rows, lanes])).astype(bf16)
        h_scr[:, lanes] = carry

    for c in range(tt // CHUNK):
        for p in range(N_PAIRS):
            retention_unit(c, p)

    return load_x() + jnp.dot(y_scr[...], wout_ref[...].astype(bf16), preferred_element_type=f32)


X_SLOTS = 4
OUT_SLOTS = 2
SLOT_SHAPE = (TILE_T // CHUNK, BLOCK_STEPS, SUBLANES, D_MODEL)


def _tile_block_copies(hbm_ref, vmem_ref, sem_ref, tile, slot, n_tiles, to_hbm):
    chunks = TILE_T // CHUNK
    b = lax.div(tile, n_tiles)
    first_chunk = (tile - b * n_tiles) * chunks
    copies = []
    for s in range(SUBLANES):
        hbm = hbm_ref.at[b, pl.ds(first_chunk, chunks), pl.ds(s * BLOCK_STEPS, BLOCK_STEPS), :]
        vmem = vmem_ref.at[slot, :, :, s, :]
        src, dst = (vmem, hbm) if to_hbm else (hbm, vmem)
        copies.append(pltpu.make_async_copy(src, dst, sem_ref.at[slot]))
    return copies


def _main_kernel(n_tiles, n_total,
                 x_hbm, cos_ref, sin_ref, g1_ref, win_hbm, convw_ref, convb_ref, wrg_ref, wig_ref,
                 brg_ref, big_ref, lam_ref, rng_ref, wout_hbm, fng_ref, dmaskp_ref, kdec_ref, qdec_ref, even_ref, odd_ref,
                 gst_ref, bdm_ref, meta_ref, cos_meta_ref, sin_meta_ref, kdec_time_ref,
                 o_hbm,
                 u_a, u_b,
                 lx_a, gate_a, q_a, k_a, v_a, rg_a,
                 lx_b, gate_b, q_b, k_b, v_b, rg_b,
                 xc_scr, xcb_scr, pre_scr, y_scr, lxtail_scr, h_scr, st_scr,
                 lx0_ref, h0_ref, st0_ref, res_scr,
                 xin_scr, out_scr, xin_sem, out_sem,
                 win_ref, wout_ref, w_sem):
    g = pl.program_id(0)
    slots = ((lx_a, gate_a, q_a, k_a, v_a, rg_a), (lx_b, gate_b, q_b, k_b, v_b, rg_b))
    win_copies = [pltpu.make_async_copy(win_hbm.at[0, :, pl.ds(c * W_CHUNK_COLS, W_CHUNK_COLS)],
                                        win_ref.at[:, pl.ds(c * W_CHUNK_COLS, W_CHUNK_COLS)], w_sem.at[c])
                  for c in range(N_W_CHUNKS)]
    wout_copy = pltpu.make_async_copy(wout_hbm.at[0], wout_ref, w_sem.at[N_W_CHUNKS])
    x_slot = lambda tile: lax.rem(tile + X_SLOTS, X_SLOTS)
    out_slot = lambda tile: lax.rem(tile + OUT_SLOTS, OUT_SLOTS)
    x_tile = lambda tile: xin_scr[x_slot(tile)].reshape(TILE_T, D_MODEL)
    fetch = lambda tile: _tile_block_copies(x_hbm, xin_scr, xin_sem, jnp.minimum(tile, n_total - 1), x_slot(tile),
                                            n_tiles, to_hbm=False)
    write_back = lambda tile: _tile_block_copies(o_hbm, out_scr, out_sem, tile, out_slot(tile), n_tiles, to_hbm=True)

    @pl.when(g >= 2 + OUT_SLOTS)
    def _():
        for copy in write_back(g - 2 - OUT_SLOTS):
            copy.wait()

    @pl.when(lax.rem(g + n_tiles - 1, n_tiles) == 0)
    def _():
        lxtail_scr[...] = lx0_ref[...]
        h_scr[...] = h0_ref[...]
        st_scr[...] = st0_ref[...]

    tiles_after_head = (4, 4, 3, 3, 2, 2, 1, 1)
    assert sum(tiles_after_head) == N_IN_TILES and len(tiles_after_head) == LRU_HEADS

    def finish(slot_finish, after_gates):
        res_scr[...] = _finish_tile(
            lambda: x_tile(g - 1), cos_ref, sin_ref, convw_ref, convb_ref, wrg_ref, wig_ref, brg_ref, big_ref,
            lam_ref, rng_ref, wout_ref, dmaskp_ref, kdec_ref, qdec_ref, even_ref, odd_ref, gst_ref,
            bdm_ref, slot_finish, xc_scr, xcb_scr, pre_scr, y_scr, lxtail_scr, h_scr, st_scr, after_gates)

    def final_norm(tile):
        out = _rmsnorm_rows(res_scr[...], fng_ref[...])
        out_scr[out_slot(tile)] = out.reshape(SLOT_SHAPE)

    def norm_and_write_back(tile):
        final_norm(tile)
        for copy in write_back(tile):
            copy.start()

    final_norm_after_head, pre_norm_after_head = 4, 6

    def step(u_project, u_next, slot_project, slot_finish, norm_previous=True):
        for copy in fetch(g + 1):
            copy.wait()
        for copy in fetch(g + 2):
            copy.start()

        def after_gates(h):
            first = sum(tiles_after_head[:h])
            for t in range(first, first + tiles_after_head[h]):
                _project_col_tile(t, u_project, win_ref, slot_project)
            if norm_previous and h == final_norm_after_head:
                final_norm(g - 2)
            if h == pre_norm_after_head:
                _norm_tile(x_tile(g + 1), g1_ref, u_next)

        finish(slot_finish, after_gates)
        if norm_previous:
            for copy in write_back(g - 2):
                copy.start()

    @pl.when(g == 0)
    def _():
        for copy in fetch(g):
            copy.start()
        for copy in win_copies + [wout_copy]:
            copy.start()
        for copy in fetch(g):
            copy.wait()
        for tile in (g + 1, g + 2):
            for copy in fetch(tile):
                copy.start()
        _norm_tile(x_tile(g), g1_ref, u_a)
        for t in range(N_IN_TILES):
            if t % (W_CHUNK_COLS // MXU_COLS) == 0:
                win_copies[t // (W_CHUNK_COLS // MXU_COLS)].wait()
            _project_col_tile(t, u_a, win_ref, slots[0])
        meta_rows = lambda ref, n: ref.at[pl.ds(0, n)]
        _meta_states(meta_ref, cos_meta_ref, sin_meta_ref, g1_ref,
                     win_ref.at[:, pl.ds(OFF_LX, LRU_WIDTH)], win_ref.at[:, pl.ds(OFF_K, RET_QK_WIDTH)],
                     win_ref.at[:, pl.ds(OFF_V, RET_WIDTH)], convw_ref, convb_ref, wrg_ref, wig_ref, brg_ref, big_ref,
                     lam_ref, kdec_time_ref, bdm_ref, lx0_ref, h0_ref, st0_ref,
                     meta_rows(slots[1][0], SUBLANES + N_META), meta_rows(xc_scr, N_META), meta_rows(xcb_scr, N_META),
                     meta_rows(pre_scr, N_META))
        for copy in fetch(g + 1):
            copy.wait()
        _norm_tile(x_tile(g + 1), g1_ref, u_b)

    @pl.when(g == 1)
    def _():
        wout_copy.wait()
        step(u_b, u_a, slots[1], slots[0], norm_previous=False)

    regular = jnp.logical_and(g >= 2, g < n_total)

    @pl.when(jnp.logical_and(regular, lax.rem(g, 2) == 0))
    def _():
        step(u_a, u_b, slots[0], slots[1])

    @pl.when(jnp.logical_and(regular, lax.rem(g, 2) == 1))
    def _():
        step(u_b, u_a, slots[1], slots[0])

    @pl.when(g == n_total)
    def _():
        for copy in fetch(g + 1):
            copy.wait()
        norm_and_write_back(g - 2)
        finish(slots[(n_total - 1) % 2], lambda h: None)
        for copy in write_back(g - 1 - OUT_SLOTS):
            copy.wait()
        norm_and_write_back(g - 1)
        for tile in (g - 2, g - 1):
            for copy in write_back(tile):
                copy.wait()


def _const_spec(shape):
    nd = len(shape)
    return pl.BlockSpec(shape, lambda g, _nd=nd: (0,) * _nd, pipeline_mode=pl.Buffered(1))


def kernel(x, meta_tokens, norm_gain, w_in, conv_w, conv_b, w_rg, b_rg, w_ig, b_ig,
           lru_lambda, ret_norm_gain, w_out, final_norm_gain):
    B, S, D = x.shape
    assert D == D_MODEL and S % TILE_T == 0 and TILE_T % CHUNK == 0
    assert norm_gain.shape[0] == 1, "single-layer block"
    assert meta_tokens.shape == (N_META, D_MODEL)

    assert w_in.shape == (1, D_MODEL, IN_WIDTH) and w_out.shape == (1, MIX_WIDTH, D_MODEL)
    assert w_rg.shape == w_ig.shape == (1, LRU_HEADS, LRU_BLOCK, LRU_BLOCK)

    g1 = norm_gain[0].reshape(1, D_MODEL)
    fng = final_norm_gain.reshape(1, D_MODEL)
    convw = conv_w[0]
    convb = conv_b[0].reshape(1, LRU_WIDTH)
    brg = b_rg[0].reshape(1, LRU_WIDTH)
    big = b_ig[0].reshape(1, LRU_WIDTH)
    lam = lru_lambda[0].reshape(1, LRU_WIDTH)
    rng = ret_norm_gain[0].reshape(1, RET_WIDTH)

    k_dec, dmask_blocked, k_dec_blocked, q_dec_blocked, even, odd, g_state, bd_mask = _retention_tables()
    cos_t, sin_t = _rotary_tables(N_META + S)
    cos_m, sin_m = jnp.asarray(cos_t[:N_META]), jnp.asarray(sin_t[:N_META])
    cos_blocked = jnp.asarray(_chunk_rows_blocked(cos_t[N_META:]))
    sin_blocked = jnp.asarray(_chunk_rows_blocked(sin_t[N_META:]))

    tt = TILE_T
    n_tiles = S // tt
    n_total = B * n_tiles
    assert n_total >= 3, "the four-stage tile pipeline needs at least three tiles"

    rot_spec = pl.BlockSpec((tt, LANES), lambda g: (jnp.maximum(g - 1, 0) % n_tiles, 0))
    in_specs = [
        pl.BlockSpec(memory_space=pl.ANY),
        rot_spec, rot_spec,
        _const_spec((1, D_MODEL)),
        pl.BlockSpec(memory_space=pl.ANY),
        _const_spec((CONV_WIDTH, LRU_WIDTH)), _const_spec((1, LRU_WIDTH)),
        _const_spec((None, LRU_HEADS, LRU_BLOCK, LRU_BLOCK)),
        _const_spec((None, LRU_HEADS, LRU_BLOCK, LRU_BLOCK)),
        _const_spec((1, LRU_WIDTH)), _const_spec((1, LRU_WIDTH)), _const_spec((1, LRU_WIDTH)),
        _const_spec((1, RET_WIDTH)),
        pl.BlockSpec(memory_space=pl.ANY),
        _const_spec((1, D_MODEL)),
        _const_spec((N_PAIRS, CHUNK, 2 * CHUNK)),
        _const_spec((CHUNK, RET_QK_WIDTH)), _const_spec((CHUNK, RET_QK_WIDTH)),
        _const_spec((1, RET_QK_WIDTH)), _const_spec((1, RET_QK_WIDTH)),
        _const_spec((N_PAIRS, LANES, PAIR_V)),
        _const_spec((LANES, PAIR_V)),
        _const_spec((N_META, D_MODEL)),
        _const_spec((N_META, LANES)), _const_spec((N_META, LANES)),
        _const_spec((CHUNK, RET_QK_WIDTH)),
    ]
    slot_scratch = [
        pltpu.VMEM((CONV_HIST + tt, LRU_WIDTH), f32),
        pltpu.VMEM((tt, LRU_WIDTH), f32),
        pltpu.VMEM((tt, RET_QK_WIDTH), f32),
        pltpu.VMEM((tt, RET_QK_WIDTH), f32),
        pltpu.VMEM((tt, RET_WIDTH), bf16),
        pltpu.VMEM((tt, RET_WIDTH), f32),
    ]
    scratch = [pltpu.VMEM((tt, D_MODEL), bf16)] * 2 + slot_scratch + slot_scratch + [
        pltpu.VMEM((tt, LRU_WIDTH), f32),
        pltpu.VMEM((tt, LRU_WIDTH), bf16),
        pltpu.VMEM((tt, 2 * LRU_WIDTH), f32),
        pltpu.VMEM((tt, MIX_WIDTH), bf16),
        pltpu.VMEM((CONV_HIST, LRU_WIDTH), f32),
        pltpu.VMEM((SUBLANES, LRU_WIDTH), f32),
        pltpu.VMEM((N_PAIRS, LANES, PAIR_V), f32),
        pltpu.VMEM((CONV_HIST, LRU_WIDTH), f32),
        pltpu.VMEM((SUBLANES, LRU_WIDTH), f32),
        pltpu.VMEM((N_PAIRS, LANES, PAIR_V), f32),
        pltpu.VMEM((tt, D_MODEL), f32),
        pltpu.VMEM((X_SLOTS,) + SLOT_SHAPE, f32),
        pltpu.VMEM((OUT_SLOTS,) + SLOT_SHAPE, f32),
        pltpu.SemaphoreType.DMA((X_SLOTS,)),
        pltpu.SemaphoreType.DMA((OUT_SLOTS,)),
        pltpu.VMEM((D_MODEL, IN_WIDTH), f32),
        pltpu.VMEM((MIX_WIDTH, D_MODEL), f32),
        pltpu.SemaphoreType.DMA((N_W_CHUNKS + 1,)),
    ]
    out = pl.pallas_call(
        functools.partial(_main_kernel, n_tiles, n_total),
        grid=(n_total + 1,),
        in_specs=in_specs,
        out_specs=pl.BlockSpec(memory_space=pl.ANY),
        out_shape=jax.ShapeDtypeStruct((B, S // CHUNK, CHUNK, D_MODEL), x.dtype),
        scratch_shapes=scratch,
        compiler_params=pltpu.CompilerParams(
            dimension_semantics=("arbitrary",),
            vmem_limit_bytes=VMEM_LIMIT_BYTES),
        name="hybrid_main",
    )(x.reshape(B, S // CHUNK, CHUNK, D_MODEL),
      cos_blocked, sin_blocked, g1, w_in, convw, convb, w_rg, w_ig, brg, big, lam, rng, w_out, fng,
      dmask_blocked, k_dec_blocked, q_dec_blocked, even, odd, g_state, bd_mask, meta_tokens, cos_m, sin_m, k_dec)
    return out.reshape(B, S, D_MODEL)
```

```python
import functools

import numpy as np
import jax
import jax.numpy as jnp
from jax import lax
from jax.experimental import pallas as pl
from jax.experimental.pallas import tpu as pltpu

f32 = jnp.float32
bf16 = jnp.bfloat16

D_MODEL = 1024
N_META = 16
LRU_WIDTH = 1024
LRU_HEADS = 8
LRU_BLOCK = 128
CONV_WIDTH = 4
LRU_C = 8.0
RET_HEADS = 8
RET_QK_DIM = 64
RET_V_DIM = 128
RET_QK_WIDTH = 512
RET_WIDTH = 1024
CHUNK = 128
ROPE_BASE = 10000.0
MIX_WIDTH = 2048
EPS = 1e-6
QK_SCALE = RET_QK_DIM ** -0.5

OFF_LX, OFF_GATE, OFF_Q, OFF_K, OFF_V, OFF_RG = 0, 1024, 2048, 2560, 3072, 4096
IN_WIDTH = 5120

LANES = 128
SUBLANES = 8
N_PAIRS = RET_HEADS // 2
PAIR_V = 2 * RET_V_DIM
TILE_T = 256
VMEM_LIMIT_BYTES = 60000 * 1024


def _lane_head():
    return np.arange(RET_QK_WIDTH) // RET_QK_DIM


BLOCK_STEPS = CHUNK // SUBLANES
CHUNK_ROW_TIME = np.arange(CHUNK).reshape(SUBLANES, BLOCK_STEPS).T.reshape(-1)


def _chunk_rows_blocked(table):
    n = table.shape[0]
    return table.reshape((n // CHUNK, CHUNK) + table.shape[1:])[:, CHUNK_ROW_TIME].reshape(table.shape)


def _retention_tables():
    log_g = np.log1p(-np.exp2(-5.0 - np.arange(RET_HEADS, dtype=np.float32))).astype(np.float32)
    idx = np.arange(CHUNK, dtype=np.float32)
    diff = idx[:, None] - idx[None, :]
    dmask = np.where(diff[None] >= 0.0, np.exp(np.maximum(diff, 0.0)[None] * log_g[:, None, None]), 0.0)
    dmask_pair = np.concatenate([dmask[0::2], dmask[1::2]], axis=-1)
    lg_lane = log_g[_lane_head()]
    k_dec = np.exp((CHUNK - 1.0 - idx)[:, None] * lg_lane[None, :])
    q_dec = np.exp((idx + 1.0)[:, None] * lg_lane[None, :]) * QK_SCALE
    even = (_lane_head() % 2 == 0)[None, :]
    g_chunk = np.exp(CHUNK * log_g)
    g_state = np.broadcast_to(np.repeat(g_chunk, RET_V_DIM).reshape(N_PAIRS, 1, PAIR_V), (N_PAIRS, LANES, PAIR_V))
    row_par = np.arange(LANES) // RET_QK_DIM
    col_par = np.arange(PAIR_V) // RET_V_DIM
    bd_mask = row_par[:, None] == col_par[None, :]
    key_order = np.concatenate([CHUNK_ROW_TIME, CHUNK + CHUNK_ROW_TIME])
    dmask_blocked = dmask_pair[:, CHUNK_ROW_TIME][:, :, key_order]
    as_f32 = lambda a: jnp.asarray(np.asarray(a, np.float32))
    return tuple(as_f32(t) for t in (k_dec, dmask_blocked, k_dec[CHUNK_ROW_TIME], q_dec[CHUNK_ROW_TIME],
                                     even, ~even, g_state, bd_mask))


def _rotary_tables(n_pos):
    half = RET_QK_DIM // 2
    inv = (np.float32(ROPE_BASE) ** (-np.arange(half, dtype=np.float32) / half)).astype(np.float32)
    ang = np.arange(n_pos).astype(np.float32)[:, None] * inv[None, :]
    cos, sin = np.cos(ang), np.sin(ang)
    cos_t = np.concatenate([cos, cos, cos, cos], axis=-1).astype(np.float32)
    sin_t = np.concatenate([-sin, sin, -sin, sin], axis=-1).astype(np.float32)
    return cos_t, sin_t


def _rmsnorm_rows(x, gain_row):
    ms = jnp.mean(x * x, axis=-1, keepdims=True)
    return x * lax.rsqrt(ms + EPS) * gain_row


def _silu(x):
    hx = 0.5 * x
    return hx * jnp.tanh(hx) + hx


def _half_conv_params(convw_ref, convb_ref):
    return [0.5 * convw_ref[k:k + 1, :] for k in range(CONV_WIDTH)], 0.5 * convb_ref[...]


def _conv(n_rows, lx_scr, convw_ref, convb_ref, xc_scr, xcb_scr):
    base = SUBLANES
    taps, bias = _half_conv_params(convw_ref, convb_ref)
    xc = bias + taps[3] * lx_scr[pl.ds(base, n_rows), :]
    xc = xc + taps[2] * lx_scr[pl.ds(base - 1, n_rows), :]
    xc = xc + taps[1] * lx_scr[pl.ds(base - 2, n_rows), :]
    xc = xc + taps[0] * lx_scr[pl.ds(base - 3, n_rows), :]
    xc_scr[...] = xc
    xcb_scr[...] = xc.astype(bf16)


CONV_HIST = (CONV_WIDTH - 1) * SUBLANES


def _conv_blocked(lx_scr, convw_ref, convb_ref, xc_scr, xcb_scr):
    taps, bias = _half_conv_params(convw_ref, convb_ref)
    last_sublane = lax.broadcasted_iota(jnp.int32, (SUBLANES, LRU_WIDTH), 0) == SUBLANES - 1
    for c in range(TILE_T // CHUNK):
        base = CONV_HIST + c * CHUNK
        rows = lambda group, n: lx_scr[base + group * SUBLANES:base + (group + n) * SUBLANES, :]
        wrapped = []
        for k in range(1, CONV_WIDTH):
            own, prev = rows(BLOCK_STEPS - k, 1), rows(-k, 1)
            wrapped.append(pltpu.roll(jnp.where(last_sublane, prev, own), 1, 0))
        xc = bias + taps[CONV_WIDTH - 1] * rows(0, BLOCK_STEPS)
        for k in range(1, CONV_WIDTH):
            shifted = jnp.concatenate(wrapped[:k][::-1] + [rows(0, BLOCK_STEPS - k)], axis=0)
            xc = xc + taps[CONV_WIDTH - 1 - k] * shifted
        xc_scr[c * CHUNK:(c + 1) * CHUNK, :] = xc
        xcb_scr[c * CHUNK:(c + 1) * CHUNK, :] = xc.astype(bf16)


def _gates_head(h, xcb_scr, wrg_ref, wig_ref, pre_scr):
    lanes = slice(h * LRU_BLOCK, (h + 1) * LRU_BLOCK)
    wg = jnp.concatenate([wrg_ref[h], wig_ref[h]], axis=-1).astype(bf16)
    pre = jnp.dot(xcb_scr[:, lanes], wg, preferred_element_type=f32)
    pre_scr[:, lanes] = pre[:, :LRU_BLOCK]
    pre_scr[:, LRU_WIDTH + h * LRU_BLOCK:LRU_WIDTH + (h + 1) * LRU_BLOCK] = pre[:, LRU_BLOCK:]


def _lru_maps(half_xc, half_pre_r, half_pre_i, half_brg, half_big, half_c_sp):
    nl = jnp.tanh(half_pre_r + half_brg) * half_c_sp + half_c_sp
    twice_i = jnp.tanh(half_pre_i + half_big) + 1.0
    a = jnp.exp(-nl)
    z = jnp.tanh(nl) * (1.0 + a * a)
    beta = jnp.where(z > 0.0, z * lax.rsqrt(z), 0.0)
    return a, beta * twice_i * half_xc


def _scan_sublanes(a, b):
    rowid = lax.broadcasted_iota(jnp.int32, a.shape, 0)
    for s in (1, 2, 4):
        keep = rowid >= s
        a_s = jnp.where(keep, pltpu.roll(a, s, 0), 1.0)
        b_s = jnp.where(keep, pltpu.roll(b, s, 0), 0.0)
        b = a * b_s + b
        a = a * a_s
    return a, b


def _lru_chunk_blocked(xc, pre_r, pre_i, brg, big, c_sp, carry):
    a, b = _lru_maps(xc, pre_r, pre_i, brg, big, c_sp)
    width = xc.shape[1]
    a3 = a.reshape(BLOCK_STEPS, SUBLANES, width)
    b3 = b.reshape(BLOCK_STEPS, SUBLANES, width)
    decay, local = [a3[0]], [b3[0]]
    for j in range(1, BLOCK_STEPS):
        local.append(a3[j] * local[-1] + b3[j])
        decay.append(a3[j] * decay[-1])
    a_blocks, b_blocks = _scan_sublanes(decay[-1], local[-1])
    after = a_blocks * carry + b_blocks
    rowid = lax.broadcasted_iota(jnp.int32, carry.shape, 0)
    before = jnp.where(rowid >= 1, pltpu.roll(after, 1, 0), carry)
    hs = [local[j] + decay[j] * before for j in range(BLOCK_STEPS)]
    new_carry = jnp.broadcast_to(after[SUBLANES - 1:SUBLANES, :], carry.shape)
    return jnp.concatenate(hs, axis=0), new_carry


def _lru_block(xc, pre_r, pre_i, brg, big, c_sp, carry):
    rows, width = xc.shape
    a, b = _lru_maps(xc, pre_r, pre_i, brg, big, c_sp)
    nv = rows // SUBLANES
    a3 = a.reshape(nv, SUBLANES, width)
    b3 = b.reshape(nv, SUBLANES, width)
    rowid = lax.broadcasted_iota(jnp.int32, (nv, SUBLANES, width), 1)
    for s in (1, 2, 4):
        keep = rowid >= s
        a_s = jnp.where(keep, pltpu.roll(a3, s, 1), 1.0)
        b_s = jnp.where(keep, pltpu.roll(b3, s, 1), 0.0)
        b3 = a3 * b_s + b3
        a3 = a3 * a_s
    hs = []
    for v in range(nv):
        h_v = a3[v] * carry + b3[v]
        carry = h_v[SUBLANES - 1:SUBLANES, :]
        hs.append(h_v)
    return jnp.concatenate(hs, axis=0), carry


def _rotary(t, cos, sin):
    half = RET_QK_DIM // 2
    in_first_half = (lax.broadcasted_iota(jnp.int32, t.shape, 1) // half) % 2 == 0
    partner = jnp.where(in_first_half, pltpu.roll(t, LANES - half, 1), pltpu.roll(t, half, 1))
    return t * cos + partner * sin


def _dot_t0(a, b):
    return lax.dot_general(a, b, (((0,), (0,)), ((), ())), preferred_element_type=f32)


def _dot_nt(a, b):
    return lax.dot_general(a, b, (((1,), (1,)), ((), ())), preferred_element_type=f32)


def _meta_states(meta_ref, cos_ref, sin_ref, g1_ref, wlx_ref, wk_ref, wv_ref, convw_ref, convb_ref,
                 wrg_ref, wig_ref, brg_ref, big_ref, lam_ref, kdec_ref, bdm_ref,
                 lx_out, h_out, st_out,
                 lx_scr, xc_scr, xcb_scr, pre_scr):
    u = _rmsnorm_rows(meta_ref[...], g1_ref[...]).astype(bf16)
    lx_scr[0:SUBLANES, :] = jnp.zeros((SUBLANES, LRU_WIDTH), f32)
    lx_scr[SUBLANES:SUBLANES + N_META, :] = jnp.dot(u, wlx_ref[...].astype(bf16), preferred_element_type=f32)
    k = jnp.dot(u, wk_ref[...].astype(bf16), preferred_element_type=f32)
    v = jnp.dot(u, wv_ref[...].astype(bf16), preferred_element_type=f32).astype(bf16)

    _conv(N_META, lx_scr, convw_ref, convb_ref, xc_scr, xcb_scr)
    for h in range(LRU_HEADS):
        _gates_head(h, xcb_scr, wrg_ref, wig_ref, pre_scr)
    for back in range(1, CONV_WIDTH):
        row = lx_scr[SUBLANES + N_META - back:SUBLANES + N_META - back + 1, :]
        group = CONV_WIDTH - 1 - back
        lx_out[group * SUBLANES:(group + 1) * SUBLANES, :] = jnp.broadcast_to(row, (SUBLANES, LRU_WIDTH))
    half_c_sp = 0.5 * LRU_C * jax.nn.softplus(-lam_ref[...])
    _, carry = _lru_block(xc_scr[...], pre_scr[:, :LRU_WIDTH], pre_scr[:, LRU_WIDTH:],
                          0.5 * brg_ref[...], 0.5 * big_ref[...], half_c_sp, jnp.zeros((1, LRU_WIDTH), f32))
    h_out[...] = jnp.broadcast_to(carry, (SUBLANES, LRU_WIDTH))

    cos, sin = cos_ref[...], sin_ref[...]
    for p in range(N_PAIRS):
        sl = slice(p * LANES, (p + 1) * LANES)
        k_rot = _rotary(k[:, sl], cos, sin)
        kd = (k_rot * kdec_ref[CHUNK - N_META:CHUNK, sl]).astype(bf16)
        st_out[p] = _dot_t0(kd, v[:, p * PAIR_V:(p + 1) * PAIR_V]) * bdm_ref[...]


def _norm_tile(x_tile, g1_ref, u_scr):
    u_scr[...] = _rmsnorm_rows(x_tile, g1_ref[...]).astype(bf16)


MXU_COLS = 256
N_IN_TILES = IN_WIDTH // MXU_COLS
W_CHUNK_COLS = 4 * MXU_COLS
N_W_CHUNKS = IN_WIDTH // W_CHUNK_COLS


def _project_col_tile(t, u_scr, win_ref, slot):
    lx_scr, gate_scr, q_scr, k_scr, v_scr, rg_scr = slot
    off = t * MXU_COLS
    res = jnp.dot(u_scr[...], win_ref[:, off:off + MXU_COLS].astype(bf16), preferred_element_type=f32)
    for dst, start, rows, dt in ((lx_scr, OFF_LX, slice(CONV_HIST, CONV_HIST + TILE_T), f32), (gate_scr, OFF_GATE, slice(None), f32),
                                 (q_scr, OFF_Q, slice(None), f32), (k_scr, OFF_K, slice(None), f32),
                                 (v_scr, OFF_V, slice(None), bf16), (rg_scr, OFF_RG, slice(None), f32)):
        width = dst.shape[1]
        if start <= off < start + width:
            dst[rows, off - start:off - start + MXU_COLS] = res.astype(dt)
            return
    raise AssertionError("column tile outside the projection")


def _finish_tile(load_x, cos_ref, sin_ref, convw_ref, convb_ref, wrg_ref, wig_ref, brg_ref, big_ref, lam_ref, rng_ref,
                 wout_ref, dmaskp_ref, kdec_ref, qdec_ref, even_ref, odd_ref, gst_ref, bdm_ref,
                 slot, xc_scr, xcb_scr, pre_scr, y_scr, lxtail_scr, h_scr, st_scr, after_gates):
    lx_scr, gate_scr, q_scr, k_scr, v_scr, rg_scr = slot
    tt = TILE_T

    def retention_unit(c, p):
        rows = slice(c * CHUNK, (c + 1) * CHUNK)
        cos, sin = cos_ref[rows, :], sin_ref[rows, :]
        sl = slice(p * LANES, (p + 1) * LANES)
        q_rot = _rotary(q_scr[rows, sl], cos, sin)
        k_rot = _rotary(k_scr[rows, sl], cos, sin)
        q_b = (q_rot * QK_SCALE).astype(bf16)
        q_d = (q_rot * qdec_ref[:, sl]).astype(bf16)
        k_d = (k_rot * kdec_ref[:, sl]).astype(bf16)
        k_b = k_rot.astype(bf16)
        kk = jnp.concatenate([k_b * even_ref[:, sl].astype(bf16), k_b * odd_ref[:, sl].astype(bf16)], axis=0)
        v_pair = v_scr[rows, p * PAIR_V:(p + 1) * PAIR_V]
        zeros = jnp.zeros((CHUNK, RET_V_DIM), bf16)
        v_bd = jnp.concatenate([jnp.concatenate([v_pair[:, :RET_V_DIM], zeros], axis=1),
                                jnp.concatenate([zeros, v_pair[:, RET_V_DIM:]], axis=1)], axis=0)
        st_pair = st_scr[p]
        s = _dot_nt(q_b, kk) * dmaskp_ref[p]
        o = jnp.dot(jnp.concatenate([s.astype(bf16), q_d], axis=1),
                    jnp.concatenate([v_bd, st_pair.astype(bf16)], axis=0), preferred_element_type=f32)
        for e in range(2):
            h = 2 * p + e
            hs = slice(h * RET_V_DIM, (h + 1) * RET_V_DIM)
            o_h = o[:, e * RET_V_DIM:(e + 1) * RET_V_DIM]
            mu = jnp.mean(o_h, axis=-1, keepdims=True)
            oc = o_h - mu
            var = jnp.mean(oc * oc, axis=-1, keepdims=True)
            on = oc * lax.rsqrt(var + EPS) * rng_ref[:, hs]
            y_scr[rows, LRU_WIDTH + h * RET_V_DIM:LRU_WIDTH + (h + 1) * RET_V_DIM] = (
                on * _silu(rg_scr[rows, hs])).astype(bf16)
        kv = _dot_t0(k_d, v_pair)
        st_scr[p] = gst_ref[p] * st_pair + bdm_ref[...] * kv

    lx_scr[0:CONV_HIST, :] = lxtail_scr[...]
    _conv_blocked(lx_scr, convw_ref, convb_ref, xc_scr, xcb_scr)
    lxtail_scr[...] = lx_scr[pl.ds(tt, CONV_HIST), :]
    for h in range(LRU_HEADS):
        lanes = slice(h * LRU_BLOCK, (h + 1) * LRU_BLOCK)
        ilanes = slice(LRU_WIDTH + h * LRU_BLOCK, LRU_WIDTH + (h + 1) * LRU_BLOCK)
        _gates_head(h, xcb_scr, wrg_ref, wig_ref, pre_scr)
        after_gates(h)
        full = (CHUNK, LRU_BLOCK)
        c_sp = jnp.broadcast_to(0.5 * LRU_C * jax.nn.softplus(-lam_ref[:, lanes]), full)
        brg, big = jnp.broadcast_to(0.5 * brg_ref[:, lanes], full), jnp.broadcast_to(0.5 * big_ref[:, lanes], full)
        carry = h_scr[:, lanes]
        for c in range(tt // CHUNK):
            rows = slice(c * CHUNK, (c + 1) * CHUNK)
            hb, carry = _lru_chunk_blocked(xc_scr[rows, lanes], pre_scr[rows, lanes], pre_scr[rows, ilanes],
                                           brg, big, c_sp, carry)
            y_scr[rows, lanes] = (hb * _silu(gate_scr[rows, lanes])).astype(bf16)
        h_scr[:, lanes] = carry

    for c in range(tt // CHUNK):
        for p in range(N_PAIRS):
            retention_unit(c, p)

    return load_x() + jnp.dot(y_scr[...], wout_ref[...].astype(bf16), preferred_element_type=f32)


X_SLOTS = 4
OUT_SLOTS = 2
SLOT_SHAPE = (TILE_T // CHUNK, BLOCK_STEPS, SUBLANES, D_MODEL)


def _tile_block_copies(hbm_ref, vmem_ref, sem_ref, tile, slot, n_tiles, to_hbm):
    chunks = TILE_T // CHUNK
    b = lax.div(tile, n_tiles)
    first_chunk = (tile - b * n_tiles) * chunks
    copies = []
    for s in range(SUBLANES):
        hbm = hbm_ref.at[b, pl.ds(first_chunk, chunks), pl.ds(s * BLOCK_STEPS, BLOCK_STEPS), :]
        vmem = vmem_ref.at[slot, :, :, s, :]
        src, dst = (vmem, hbm) if to_hbm else (hbm, vmem)
        copies.append(pltpu.make_async_copy(src, dst, sem_ref.at[slot]))
    return copies


def _main_kernel(n_tiles, n_total,
                 x_hbm, cos_ref, sin_ref, g1_ref, win_hbm, convw_ref, convb_ref, wrg_ref, wig_ref,
                 brg_ref, big_ref, lam_ref, rng_ref, wout_hbm, fng_ref, dmaskp_ref, kdec_ref, qdec_ref, even_ref, odd_ref,
                 gst_ref, bdm_ref, meta_ref, cos_meta_ref, sin_meta_ref, kdec_time_ref,
                 o_hbm,
                 u_a, u_b,
                 lx_a, gate_a, q_a, k_a, v_a, rg_a,
                 lx_b, gate_b, q_b, k_b, v_b, rg_b,
                 xc_scr, xcb_scr, pre_scr, y_scr, lxtail_scr, h_scr, st_scr,
                 lx0_ref, h0_ref, st0_ref, res_scr,
                 xin_scr, out_scr, xin_sem, out_sem,
                 win_ref, wout_ref, w_sem):
    g = pl.program_id(0)
    slots = ((lx_a, gate_a, q_a, k_a, v_a, rg_a), (lx_b, gate_b, q_b, k_b, v_b, rg_b))
    win_copies = [pltpu.make_async_copy(win_hbm.at[0, :, pl.ds(c * W_CHUNK_COLS, W_CHUNK_COLS)],
                                        win_ref.at[:, pl.ds(c * W_CHUNK_COLS, W_CHUNK_COLS)], w_sem.at[c])
                  for c in range(N_W_CHUNKS)]
    wout_copy = pltpu.make_async_copy(wout_hbm.at[0], wout_ref, w_sem.at[N_W_CHUNKS])
    x_slot = lambda tile: lax.rem(tile + X_SLOTS, X_SLOTS)
    out_slot = lambda tile: lax.rem(tile + OUT_SLOTS, OUT_SLOTS)
    x_tile = lambda tile: xin_scr[x_slot(tile)].reshape(TILE_T, D_MODEL)
    fetch = lambda tile: _tile_block_copies(x_hbm, xin_scr, xin_sem, jnp.minimum(tile, n_total - 1), x_slot(tile),
                                            n_tiles, to_hbm=False)
    write_back = lambda tile: _tile_block_copies(o_hbm, out_scr, out_sem, tile, out_slot(tile), n_tiles, to_hbm=True)

    @pl.when(g >= 2 + OUT_SLOTS)
    def _():
        for copy in write_back(g - 2 - OUT_SLOTS):
            copy.wait()

    @pl.when(lax.rem(g + n_tiles - 1, n_tiles) == 0)
    def _():
        lxtail_scr[...] = lx0_ref[...]
        h_scr[...] = h0_ref[...]
        st_scr[...] = st0_ref[...]

    tiles_after_head = (4, 4, 3, 3, 2, 2, 1, 1)
    assert sum(tiles_after_head) == N_IN_TILES and len(tiles_after_head) == LRU_HEADS

    def finish(slot_finish, after_gates):
        res_scr[...] = _finish_tile(
            lambda: x_tile(g - 1), cos_ref, sin_ref, convw_ref, convb_ref, wrg_ref, wig_ref, brg_ref, big_ref,
            lam_ref, rng_ref, wout_ref, dmaskp_ref, kdec_ref, qdec_ref, even_ref, odd_ref, gst_ref,
            bdm_ref, slot_finish, xc_scr, xcb_scr, pre_scr, y_scr, lxtail_scr, h_scr, st_scr, after_gates)

    def final_norm(tile):
        out = _rmsnorm_rows(res_scr[...], fng_ref[...])
        out_scr[out_slot(tile)] = out.reshape(SLOT_SHAPE)

    def norm_and_write_back(tile):
        final_norm(tile)
        for copy in write_back(tile):
            copy.start()

    final_norm_after_head, pre_norm_after_head = 4, 6

    def step(u_project, u_next, slot_project, slot_finish, norm_previous=True):
        for copy in fetch(g + 1):
            copy.wait()

        def after_gates(h):
            first = sum(tiles_after_head[:h])
            for t in range(first, first + tiles_after_head[h]):
                _project_col_tile(t, u_project, win_ref, slot_project)
            if norm_previous and h == final_norm_after_head:
                final_norm(g - 2)
            if h == pre_norm_after_head:
                _norm_tile(x_tile(g + 1), g1_ref, u_next)

        finish(slot_finish, after_gates)
        for copy in fetch(g + 3):
            copy.start()
        if norm_previous:
            for copy in write_back(g - 2):
                copy.start()

    @pl.when(g == 0)
    def _():
        for copy in fetch(g):
            copy.start()
        for copy in win_copies + [wout_copy]:
            copy.start()
        for copy in fetch(g):
            copy.wait()
        for tile in (g + 1, g + 2):
            for copy in fetch(tile):
                copy.start()
        _norm_tile(x_tile(g), g1_ref, u_a)
        for t in range(N_IN_TILES):
            if t % (W_CHUNK_COLS // MXU_COLS) == 0:
                win_copies[t // (W_CHUNK_COLS // MXU_COLS)].wait()
            _project_col_tile(t, u_a, win_ref, slots[0])
        meta_rows = lambda ref, n: ref.at[pl.ds(0, n)]
        _meta_states(meta_ref, cos_meta_ref, sin_meta_ref, g1_ref,
                     win_ref.at[:, pl.ds(OFF_LX, LRU_WIDTH)], win_ref.at[:, pl.ds(OFF_K, RET_QK_WIDTH)],
                     win_ref.at[:, pl.ds(OFF_V, RET_WIDTH)], convw_ref, convb_ref, wrg_ref, wig_ref, brg_ref, big_ref,
                     lam_ref, kdec_time_ref, bdm_ref, lx0_ref, h0_ref, st0_ref,
                     meta_rows(slots[1][0], SUBLANES + N_META), meta_rows(xc_scr, N_META), meta_rows(xcb_scr, N_META),
                     meta_rows(pre_scr, N_META))
        for copy in fetch(g + 1):
            copy.wait()
        _norm_tile(x_tile(g + 1), g1_ref, u_b)
        for copy in fetch(g + 3):
            copy.start()

    @pl.when(g == 1)
    def _():
        wout_copy.wait()
        step(u_b, u_a, slots[1], slots[0], norm_previous=False)

    regular = jnp.logical_and(g >= 2, g < n_total)

    @pl.when(jnp.logical_and(regular, lax.rem(g, 2) == 0))
    def _():
        step(u_a, u_b, slots[0], slots[1])

    @pl.when(jnp.logical_and(regular, lax.rem(g, 2) == 1))
    def _():
        step(u_b, u_a, slots[1], slots[0])

    @pl.when(g == n_total)
    def _():
        for tile in (g + 1, g + 2):
            for copy in fetch(tile):
                copy.wait()
        norm_and_write_back(g - 2)
        finish(slots[(n_total - 1) % 2], lambda h: None)
        for copy in write_back(g - 1 - OUT_SLOTS):
            copy.wait()
        norm_and_write_back(g - 1)
        for tile in (g - 2, g - 1):
            for copy in write_back(tile):
                copy.wait()


def _const_spec(shape):
    nd = len(shape)
    return pl.BlockSpec(shape, lambda g, _nd=nd: (0,) * _nd, pipeline_mode=pl.Buffered(1))


def kernel(x, meta_tokens, norm_gain, w_in, conv_w, conv_b, w_rg, b_rg, w_ig, b_ig,
           lru_lambda, ret_norm_gain, w_out, final_norm_gain):
    B, S, D = x.shape
    assert D == D_MODEL and S % TILE_T == 0 and TILE_T % CHUNK == 0
    assert norm_gain.shape[0] == 1, "single-layer block"
    assert meta_tokens.shape == (N_META, D_MODEL)

    assert w_in.shape == (1, D_MODEL, IN_WIDTH) and w_out.shape == (1, MIX_WIDTH, D_MODEL)
    assert w_rg.shape == w_ig.shape == (1, LRU_HEADS, LRU_BLOCK, LRU_BLOCK)

    g1 = norm_gain[0].reshape(1, D_MODEL)
    fng = final_norm_gain.reshape(1, D_MODEL)
    convw = conv_w[0]
    convb = conv_b[0].reshape(1, LRU_WIDTH)
    brg = b_rg[0].reshape(1, LRU_WIDTH)
    big = b_ig[0].reshape(1, LRU_WIDTH)
    lam = lru_lambda[0].reshape(1, LRU_WIDTH)
    rng = ret_norm_gain[0].reshape(1, RET_WIDTH)

    k_dec, dmask_blocked, k_dec_blocked, q_dec_blocked, even, odd, g_state, bd_mask = _retention_tables()
    cos_t, sin_t = _rotary_tables(N_META + S)
    cos_m, sin_m = jnp.asarray(cos_t[:N_META]), jnp.asarray(sin_t[:N_META])
    cos_blocked = jnp.asarray(_chunk_rows_blocked(cos_t[N_META:]))
    sin_blocked = jnp.asarray(_chunk_rows_blocked(sin_t[N_META:]))

    tt = TILE_T
    n_tiles = S // tt
    n_total = B * n_tiles
    assert n_total >= 3, "the four-stage tile pipeline needs at least three tiles"

    rot_spec = pl.BlockSpec((tt, LANES), lambda g: (jnp.maximum(g - 1, 0) % n_tiles, 0))
    in_specs = [
        pl.BlockSpec(memory_space=pl.ANY),
        rot_spec, rot_spec,
        _const_spec((1, D_MODEL)),
        pl.BlockSpec(memory_space=pl.ANY),
        _const_spec((CONV_WIDTH, LRU_WIDTH)), _const_spec((1, LRU_WIDTH)),
        _const_spec((None, LRU_HEADS, LRU_BLOCK, LRU_BLOCK)),
        _const_spec((None, LRU_HEADS, LRU_BLOCK, LRU_BLOCK)),
        _const_spec((1, LRU_WIDTH)), _const_spec((1, LRU_WIDTH)), _const_spec((1, LRU_WIDTH)),
        _const_spec((1, RET_WIDTH)),
        pl.BlockSpec(memory_space=pl.ANY),
        _const_spec((1, D_MODEL)),
        _const_spec((N_PAIRS, CHUNK, 2 * CHUNK)),
        _const_spec((CHUNK, RET_QK_WIDTH)), _const_spec((CHUNK, RET_QK_WIDTH)),
        _const_spec((1, RET_QK_WIDTH)), _const_spec((1, RET_QK_WIDTH)),
        _const_spec((N_PAIRS, LANES, PAIR_V)),
        _const_spec((LANES, PAIR_V)),
        _const_spec((N_META, D_MODEL)),
        _const_spec((N_META, LANES)), _const_spec((N_META, LANES)),
        _const_spec((CHUNK, RET_QK_WIDTH)),
    ]
    slot_scratch = [
        pltpu.VMEM((CONV_HIST + tt, LRU_WIDTH), f32),
        pltpu.VMEM((tt, LRU_WIDTH), f32),
        pltpu.VMEM((tt, RET_QK_WIDTH), f32),
        pltpu.VMEM((tt, RET_QK_WIDTH), f32),
        pltpu.VMEM((tt, RET_WIDTH), bf16),
        pltpu.VMEM((tt, RET_WIDTH), f32),
    ]
    scratch = [pltpu.VMEM((tt, D_MODEL), bf16)] * 2 + slot_scratch + slot_scratch + [
        pltpu.VMEM((tt, LRU_WIDTH), f32),
        pltpu.VMEM((tt, LRU_WIDTH), bf16),
        pltpu.VMEM((tt, 2 * LRU_WIDTH), f32),
        pltpu.VMEM((tt, MIX_WIDTH), bf16),
        pltpu.VMEM((CONV_HIST, LRU_WIDTH), f32),
        pltpu.VMEM((SUBLANES, LRU_WIDTH), f32),
        pltpu.VMEM((N_PAIRS, LANES, PAIR_V), f32),
        pltpu.VMEM((CONV_HIST, LRU_WIDTH), f32),
        pltpu.VMEM((SUBLANES, LRU_WIDTH), f32),
        pltpu.VMEM((N_PAIRS, LANES, PAIR_V), f32),
        pltpu.VMEM((tt, D_MODEL), f32),
        pltpu.VMEM((X_SLOTS,) + SLOT_SHAPE, f32),
        pltpu.VMEM((OUT_SLOTS,) + SLOT_SHAPE, f32),
        pltpu.SemaphoreType.DMA((X_SLOTS,)),
        pltpu.SemaphoreType.DMA((OUT_SLOTS,)),
        pltpu.VMEM((D_MODEL, IN_WIDTH), f32),
        pltpu.VMEM((MIX_WIDTH, D_MODEL), f32),
        pltpu.SemaphoreType.DMA((N_W_CHUNKS + 1,)),
    ]
    out = pl.pallas_call(
        functools.partial(_main_kernel, n_tiles, n_total),
        grid=(n_total + 1,),
        in_specs=in_specs,
        out_specs=pl.BlockSpec(memory_space=pl.ANY),
        out_shape=jax.ShapeDtypeStruct((B, S // CHUNK, CHUNK, D_MODEL), x.dtype),
        scratch_shapes=scratch,
        compiler_params=pltpu.CompilerParams(
            dimension_semantics=("arbitrary",),
            vmem_limit_bytes=VMEM_LIMIT_BYTES),
        name="hybrid_main",
    )(x.reshape(B, S // CHUNK, CHUNK, D_MODEL),
      cos_blocked, sin_blocked, g1, w_in, convw, convb, w_rg, w_ig, brg, big, lam, rng, w_out, fng,
      dmask_blocked, k_dec_blocked, q_dec_blocked, even, odd, g_state, bd_mask, meta_tokens, cos_m, sin_m, k_dec)
    return out.reshape(B, S, D_MODEL)
```

```python
import functools

import numpy as np
import jax
import jax.numpy as jnp
from jax import lax
from jax.experimental import pallas as pl
from jax.experimental.pallas import tpu as pltpu

f32 = jnp.float32
bf16 = jnp.bfloat16

D_MODEL = 1024
N_META = 16
LRU_WIDTH = 1024
LRU_HEADS = 8
LRU_BLOCK = 128
CONV_WIDTH = 4
LRU_C = 8.0
RET_HEADS = 8
RET_QK_DIM = 64
RET_V_DIM = 128
RET_QK_WIDTH = 512
RET_WIDTH = 1024
CHUNK = 128
ROPE_BASE = 10000.0
MIX_WIDTH = 2048
EPS = 1e-6
QK_SCALE = RET_QK_DIM ** -0.5

OFF_LX, OFF_GATE, OFF_Q, OFF_K, OFF_V, OFF_RG = 0, 1024, 2048, 2560, 3072, 4096
IN_WIDTH = 5120

LANES = 128
SUBLANES = 8
N_PAIRS = RET_HEADS // 2
PAIR_V = 2 * RET_V_DIM
TILE_T = 256
VMEM_LIMIT_BYTES = 60000 * 1024


def _lane_head():
    return np.arange(RET_QK_WIDTH) // RET_QK_DIM


BLOCK_STEPS = CHUNK // SUBLANES
CHUNK_ROW_TIME = np.arange(CHUNK).reshape(SUBLANES, BLOCK_STEPS).T.reshape(-1)


def _chunk_rows_blocked(table):
    n = table.shape[0]
    return table.reshape((n // CHUNK, CHUNK) + table.shape[1:])[:, CHUNK_ROW_TIME].reshape(table.shape)


def _retention_tables():
    log_g = np.log1p(-np.exp2(-5.0 - np.arange(RET_HEADS, dtype=np.float32))).astype(np.float32)
    idx = np.arange(CHUNK, dtype=np.float32)
    diff = idx[:, None] - idx[None, :]
    dmask = np.where(diff[None] >= 0.0, np.exp(np.maximum(diff, 0.0)[None] * log_g[:, None, None]), 0.0)
    dmask_pair = np.concatenate([dmask[0::2], dmask[1::2]], axis=-1)
    lg_lane = log_g[_lane_head()]
    k_dec = np.exp((CHUNK - 1.0 - idx)[:, None] * lg_lane[None, :])
    q_dec = np.exp((idx + 1.0)[:, None] * lg_lane[None, :]) * QK_SCALE
    even = (_lane_head() % 2 == 0)[None, :]
    g_chunk = np.exp(CHUNK * log_g)
    g_state = np.broadcast_to(np.repeat(g_chunk, RET_V_DIM).reshape(N_PAIRS, 1, PAIR_V), (N_PAIRS, LANES, PAIR_V))
    row_par = np.arange(LANES) // RET_QK_DIM
    col_par = np.arange(PAIR_V) // RET_V_DIM
    bd_mask = row_par[:, None] == col_par[None, :]
    key_order = np.concatenate([CHUNK_ROW_TIME, CHUNK + CHUNK_ROW_TIME])
    dmask_blocked = dmask_pair[:, CHUNK_ROW_TIME][:, :, key_order]
    as_f32 = lambda a: jnp.asarray(np.asarray(a, np.float32))
    return tuple(as_f32(t) for t in (k_dec, dmask_blocked, k_dec[CHUNK_ROW_TIME], q_dec[CHUNK_ROW_TIME],
                                     even, ~even, g_state, bd_mask))


def _rotary_tables(n_pos):
    half = RET_QK_DIM // 2
    inv = (np.float32(ROPE_BASE) ** (-np.arange(half, dtype=np.float32) / half)).astype(np.float32)
    ang = np.arange(n_pos).astype(np.float32)[:, None] * inv[None, :]
    cos, sin = np.cos(ang), np.sin(ang)
    cos_t = np.concatenate([cos, cos, cos, cos], axis=-1).astype(np.float32)
    sin_t = np.concatenate([-sin, sin, -sin, sin], axis=-1).astype(np.float32)
    return cos_t, sin_t


def _rmsnorm_rows(x, gain_row):
    ms = jnp.mean(x * x, axis=-1, keepdims=True)
    return x * lax.rsqrt(ms + EPS) * gain_row


def _silu(x):
    hx = 0.5 * x
    return hx * jnp.tanh(hx) + hx


def _half_conv_params(convw_ref, convb_ref):
    return [0.5 * convw_ref[k:k + 1, :] for k in range(CONV_WIDTH)], 0.5 * convb_ref[...]


def _conv(n_rows, lx_scr, convw_ref, convb_ref, xc_scr, xcb_scr):
    base = SUBLANES
    taps, bias = _half_conv_params(convw_ref, convb_ref)
    xc = bias + taps[3] * lx_scr[pl.ds(base, n_rows), :]
    xc = xc + taps[2] * lx_scr[pl.ds(base - 1, n_rows), :]
    xc = xc + taps[1] * lx_scr[pl.ds(base - 2, n_rows), :]
    xc = xc + taps[0] * lx_scr[pl.ds(base - 3, n_rows), :]
    xc_scr[...] = xc
    xcb_scr[...] = xc.astype(bf16)


CONV_HIST = (CONV_WIDTH - 1) * SUBLANES


def _conv_blocked(lx_scr, convw_ref, convb_ref, xc_scr, xcb_scr):
    taps, bias = _half_conv_params(convw_ref, convb_ref)
    last_sublane = lax.broadcasted_iota(jnp.int32, (SUBLANES, LRU_WIDTH), 0) == SUBLANES - 1
    for c in range(TILE_T // CHUNK):
        base = CONV_HIST + c * CHUNK
        rows = lambda group, n: lx_scr[base + group * SUBLANES:base + (group + n) * SUBLANES, :]
        wrapped = []
        for k in range(1, CONV_WIDTH):
            own, prev = rows(BLOCK_STEPS - k, 1), rows(-k, 1)
            wrapped.append(pltpu.roll(jnp.where(last_sublane, prev, own), 1, 0))
        xc = bias + taps[CONV_WIDTH - 1] * rows(0, BLOCK_STEPS)
        for k in range(1, CONV_WIDTH):
            shifted = jnp.concatenate(wrapped[:k][::-1] + [rows(0, BLOCK_STEPS - k)], axis=0)
            xc = xc + taps[CONV_WIDTH - 1 - k] * shifted
        xc_scr[c * CHUNK:(c + 1) * CHUNK, :] = xc
        xcb_scr[c * CHUNK:(c + 1) * CHUNK, :] = xc.astype(bf16)


def _gates_head(h, xcb_scr, wrg_ref, wig_ref, pre_scr):
    lanes = slice(h * LRU_BLOCK, (h + 1) * LRU_BLOCK)
    wg = jnp.concatenate([wrg_ref[h], wig_ref[h]], axis=-1).astype(bf16)
    pre = jnp.dot(xcb_scr[:, lanes], wg, preferred_element_type=f32)
    pre_scr[:, lanes] = pre[:, :LRU_BLOCK]
    pre_scr[:, LRU_WIDTH + h * LRU_BLOCK:LRU_WIDTH + (h + 1) * LRU_BLOCK] = pre[:, LRU_BLOCK:]


def _lru_maps(half_xc, half_pre_r, half_pre_i, half_brg, half_big, half_c_sp):
    nl = jnp.tanh(half_pre_r + half_brg) * half_c_sp + half_c_sp
    twice_i = jnp.tanh(half_pre_i + half_big) + 1.0
    a = jnp.exp(-nl)
    z = jnp.tanh(nl) * (1.0 + a * a)
    beta = jnp.where(z > 0.0, z * lax.rsqrt(z), 0.0)
    return a, beta * twice_i * half_xc


def _scan_sublanes(a, b):
    rowid = lax.broadcasted_iota(jnp.int32, a.shape, 0)
    for s in (1, 2, 4):
        keep = rowid >= s
        a_s = jnp.where(keep, pltpu.roll(a, s, 0), 1.0)
        b_s = jnp.where(keep, pltpu.roll(b, s, 0), 0.0)
        b = a * b_s + b
        a = a * a_s
    return a, b


def _lru_chunk_blocked(xc, pre_r, pre_i, brg, big, c_sp, carry):
    a, b = _lru_maps(xc, pre_r, pre_i, brg, big, c_sp)
    width = xc.shape[1]
    a3 = a.reshape(BLOCK_STEPS, SUBLANES, width)
    b3 = b.reshape(BLOCK_STEPS, SUBLANES, width)
    decay, local = [a3[0]], [b3[0]]
    for j in range(1, BLOCK_STEPS):
        local.append(a3[j] * local[-1] + b3[j])
        decay.append(a3[j] * decay[-1])
    a_blocks, b_blocks = _scan_sublanes(decay[-1], local[-1])
    after = a_blocks * carry + b_blocks
    rowid = lax.broadcasted_iota(jnp.int32, carry.shape, 0)
    before = jnp.where(rowid >= 1, pltpu.roll(after, 1, 0), carry)
    hs = [local[j] + decay[j] * before for j in range(BLOCK_STEPS)]
    new_carry = jnp.broadcast_to(after[SUBLANES - 1:SUBLANES, :], carry.shape)
    return jnp.concatenate(hs, axis=0), new_carry


def _lru_block(xc, pre_r, pre_i, brg, big, c_sp, carry):
    rows, width = xc.shape
    a, b = _lru_maps(xc, pre_r, pre_i, brg, big, c_sp)
    nv = rows // SUBLANES
    a3 = a.reshape(nv, SUBLANES, width)
    b3 = b.reshape(nv, SUBLANES, width)
    rowid = lax.broadcasted_iota(jnp.int32, (nv, SUBLANES, width), 1)
    for s in (1, 2, 4):
        keep = rowid >= s
        a_s = jnp.where(keep, pltpu.roll(a3, s, 1), 1.0)
        b_s = jnp.where(keep, pltpu.roll(b3, s, 1), 0.0)
        b3 = a3 * b_s + b3
        a3 = a3 * a_s
    hs = []
    for v in range(nv):
        h_v = a3[v] * carry + b3[v]
        carry = h_v[SUBLANES - 1:SUBLANES, :]
        hs.append(h_v)
    return jnp.concatenate(hs, axis=0), carry


def _rotary(t, cos, sin):
    half = RET_QK_DIM // 2
    in_first_half = (lax.broadcasted_iota(jnp.int32, t.shape, 1) // half) % 2 == 0
    partner = jnp.where(in_first_half, pltpu.roll(t, LANES - half, 1), pltpu.roll(t, half, 1))
    return t * cos + partner * sin


def _dot_t0(a, b):
    return lax.dot_general(a, b, (((0,), (0,)), ((), ())), preferred_element_type=f32)


def _dot_nt(a, b):
    return lax.dot_general(a, b, (((1,), (1,)), ((), ())), preferred_element_type=f32)


def _meta_states(meta_ref, cos_ref, sin_ref, g1_ref, wlx_ref, wk_ref, wv_ref, convw_ref, convb_ref,
                 wrg_ref, wig_ref, brg_ref, big_ref, lam_ref, kdec_ref, bdm_ref,
                 lx_out, h_out, st_out,
                 lx_scr, xc_scr, xcb_scr, pre_scr):
    u = _rmsnorm_rows(meta_ref[...], g1_ref[...]).astype(bf16)
    lx_scr[0:SUBLANES, :] = jnp.zeros((SUBLANES, LRU_WIDTH), f32)
    lx_scr[SUBLANES:SUBLANES + N_META, :] = jnp.dot(u, wlx_ref[...].astype(bf16), preferred_element_type=f32)
    k = jnp.dot(u, wk_ref[...].astype(bf16), preferred_element_type=f32)
    v = jnp.dot(u, wv_ref[...].astype(bf16), preferred_element_type=f32).astype(bf16)

    _conv(N_META, lx_scr, convw_ref, convb_ref, xc_scr, xcb_scr)
    for h in range(LRU_HEADS):
        _gates_head(h, xcb_scr, wrg_ref, wig_ref, pre_scr)
    for back in range(1, CONV_WIDTH):
        row = lx_scr[SUBLANES + N_META - back:SUBLANES + N_META - back + 1, :]
        group = CONV_WIDTH - 1 - back
        lx_out[group * SUBLANES:(group + 1) * SUBLANES, :] = jnp.broadcast_to(row, (SUBLANES, LRU_WIDTH))
    half_c_sp = 0.5 * LRU_C * jax.nn.softplus(-lam_ref[...])
    _, carry = _lru_block(xc_scr[...], pre_scr[:, :LRU_WIDTH], pre_scr[:, LRU_WIDTH:],
                          0.5 * brg_ref[...], 0.5 * big_ref[...], half_c_sp, jnp.zeros((1, LRU_WIDTH), f32))
    h_out[...] = jnp.broadcast_to(carry, (SUBLANES, LRU_WIDTH))

    cos, sin = cos_ref[...], sin_ref[...]
    for p in range(N_PAIRS):
        sl = slice(p * LANES, (p + 1) * LANES)
        k_rot = _rotary(k[:, sl], cos, sin)
        kd = (k_rot * kdec_ref[CHUNK - N_META:CHUNK, sl]).astype(bf16)
        st_out[p] = _dot_t0(kd, v[:, p * PAIR_V:(p + 1) * PAIR_V]) * bdm_ref[...]


def _norm_tile(x_tile, g1_ref, u_scr):
    u_scr[...] = _rmsnorm_rows(x_tile, g1_ref[...]).astype(bf16)


MXU_COLS = 256
N_IN_TILES = IN_WIDTH // MXU_COLS
W_CHUNK_COLS = 4 * MXU_COLS
N_W_CHUNKS = IN_WIDTH // W_CHUNK_COLS


def _project_col_tile(t, u_scr, win_ref, slot):
    lx_scr, gate_scr, q_scr, k_scr, v_scr, rg_scr = slot
    off = t * MXU_COLS
    res = jnp.dot(u_scr[...], win_ref[:, off:off + MXU_COLS].astype(bf16), preferred_element_type=f32)
    for dst, start, rows, dt in ((lx_scr, OFF_LX, slice(CONV_HIST, CONV_HIST + TILE_T), f32), (gate_scr, OFF_GATE, slice(None), f32),
                                 (q_scr, OFF_Q, slice(None), f32), (k_scr, OFF_K, slice(None), f32),
                                 (v_scr, OFF_V, slice(None), bf16), (rg_scr, OFF_RG, slice(None), f32)):
        width = dst.shape[1]
        if start <= off < start + width:
            dst[rows, off - start:off - start + MXU_COLS] = res.astype(dt)
            return
    raise AssertionError("column tile outside the projection")


def _finish_tile(load_x, cos_ref, sin_ref, convw_ref, convb_ref, wrg_ref, wig_ref, brg_ref, big_ref, lam_ref, rng_ref,
                 wout_ref, dmaskp_ref, kdec_ref, qdec_ref, even_ref, odd_ref, gst_ref, bdm_ref,
                 slot, xc_scr, xcb_scr, pre_scr, y_scr, lxtail_scr, h_scr, st_scr, after_gates):
    lx_scr, gate_scr, q_scr, k_scr, v_scr, rg_scr = slot
    tt = TILE_T

    def retention_unit(c, p):
        rows = slice(c * CHUNK, (c + 1) * CHUNK)
        cos, sin = cos_ref[rows, :], sin_ref[rows, :]
        sl = slice(p * LANES, (p + 1) * LANES)
        q_rot = _rotary(q_scr[rows, sl], cos, sin)
        k_rot = _rotary(k_scr[rows, sl], cos, sin)
        q_b = (q_rot * QK_SCALE).astype(bf16)
        q_d = (q_rot * qdec_ref[:, sl]).astype(bf16)
        k_d = (k_rot * kdec_ref[:, sl]).astype(bf16)
        k_b = k_rot.astype(bf16)
        kk = jnp.concatenate([k_b * even_ref[:, sl].astype(bf16), k_b * odd_ref[:, sl].astype(bf16)], axis=0)
        v_pair = v_scr[rows, p * PAIR_V:(p + 1) * PAIR_V]
        zeros = jnp.zeros((CHUNK, RET_V_DIM), bf16)
        v_bd = jnp.concatenate([jnp.concatenate([v_pair[:, :RET_V_DIM], zeros], axis=1),
                                jnp.concatenate([zeros, v_pair[:, RET_V_DIM:]], axis=1)], axis=0)
        st_pair = st_scr[p]
        s = _dot_nt(q_b, kk) * dmaskp_ref[p]
        o = jnp.dot(jnp.concatenate([s.astype(bf16), q_d], axis=1),
                    jnp.concatenate([v_bd, st_pair.astype(bf16)], axis=0), preferred_element_type=f32)
        for e in range(2):
            h = 2 * p + e
            hs = slice(h * RET_V_DIM, (h + 1) * RET_V_DIM)
            o_h = o[:, e * RET_V_DIM:(e + 1) * RET_V_DIM]
            mu = jnp.mean(o_h, axis=-1, keepdims=True)
            oc = o_h - mu
            var = jnp.mean(oc * oc, axis=-1, keepdims=True)
            on = oc * lax.rsqrt(var + EPS) * rng_ref[:, hs]
            y_scr[rows, LRU_WIDTH + h * RET_V_DIM:LRU_WIDTH + (h + 1) * RET_V_DIM] = (
                on * _silu(rg_scr[rows, hs])).astype(bf16)
        kv = _dot_t0(k_d, v_pair)
        st_scr[p] = gst_ref[p] * st_pair + bdm_ref[...] * kv

    lx_scr[0:CONV_HIST, :] = lxtail_scr[...]
    _conv_blocked(lx_scr, convw_ref, convb_ref, xc_scr, xcb_scr)
    lxtail_scr[...] = lx_scr[pl.ds(tt, CONV_HIST), :]
    for h in range(LRU_HEADS):
        lanes = slice(h * LRU_BLOCK, (h + 1) * LRU_BLOCK)
        ilanes = slice(LRU_WIDTH + h * LRU_BLOCK, LRU_WIDTH + (h + 1) * LRU_BLOCK)
        _gates_head(h, xcb_scr, wrg_ref, wig_ref, pre_scr)
        after_gates(h)
        full = (CHUNK, LRU_BLOCK)
        c_sp = jnp.broadcast_to(0.5 * LRU_C * jax.nn.softplus(-lam_ref[:, lanes]), full)
        brg, big = jnp.broadcast_to(0.5 * brg_ref[:, lanes], full), jnp.broadcast_to(0.5 * big_ref[:, lanes], full)
        carry = h_scr[:, lanes]
        for c in range(tt // CHUNK):
            rows = slice(c * CHUNK, (c + 1) * CHUNK)
            hb, carry = _lru_chunk_blocked(xc_scr[rows, lanes], pre_scr[rows, lanes], pre_scr[rows, ilanes],
                                           brg, big, c_sp, carry)
            y_scr[rows, lanes] = (hb * _silu(gate_scr[rows, lanes])).astype(bf16)
        h_scr[:, lanes] = carry

    for c in range(tt // CHUNK):
        for p in range(N_PAIRS):
            retention_unit(c, p)

    return load_x() + jnp.dot(y_scr[...], wout_ref[...].astype(bf16), preferred_element_type=f32)


X_SLOTS = 4
OUT_SLOTS = 2
SLOT_SHAPE = (TILE_T // CHUNK, BLOCK_STEPS, SUBLANES, D_MODEL)


def _tile_block_copies(hbm_ref, vmem_ref, sem_ref, tile, slot, n_tiles, to_hbm):
    chunks = TILE_T // CHUNK
    b = lax.div(tile, n_tiles)
    first_chunk = (tile - b * n_tiles) * chunks
    copies = []
    for s in range(SUBLANES):
        hbm = hbm_ref.at[b, pl.ds(first_chunk, chunks), pl.ds(s * BLOCK_STEPS, BLOCK_STEPS), :]
        vmem = vmem_ref.at[slot, :, :, s, :]
        src, dst = (vmem, hbm) if to_hbm else (hbm, vmem)
        copies.append(pltpu.make_async_copy(src, dst, sem_ref.at[slot]))
    return copies


def _main_kernel(n_tiles, n_total,
                 x_hbm, cos_ref, sin_ref, g1_ref, win_hbm, convw_ref, convb_ref, wrg_ref, wig_ref,
                 brg_ref, big_ref, lam_ref, rng_ref, wout_hbm, fng_ref, dmaskp_ref, kdec_ref, qdec_ref, even_ref, odd_ref,
                 gst_ref, bdm_ref, meta_ref, cos_meta_ref, sin_meta_ref, kdec_time_ref,
                 o_hbm,
                 u_a, u_b,
                 lx_a, gate_a, q_a, k_a, v_a, rg_a,
                 lx_b, gate_b, q_b, k_b, v_b, rg_b,
                 xc_scr, xcb_scr, pre_scr, y_scr, lxtail_scr, h_scr, st_scr,
                 lx0_ref, h0_ref, st0_ref, res_scr,
                 xin_scr, out_scr, xin_sem, out_sem,
                 win_ref, wout_ref, w_sem):
    g = pl.program_id(0)
    slots = ((lx_a, gate_a, q_a, k_a, v_a, rg_a), (lx_b, gate_b, q_b, k_b, v_b, rg_b))
    win_copies = [pltpu.make_async_copy(win_hbm.at[0, :, pl.ds(c * W_CHUNK_COLS, W_CHUNK_COLS)],
                                        win_ref.at[:, pl.ds(c * W_CHUNK_COLS, W_CHUNK_COLS)], w_sem.at[c])
                  for c in range(N_W_CHUNKS)]
    wout_copy = pltpu.make_async_copy(wout_hbm.at[0], wout_ref, w_sem.at[N_W_CHUNKS])
    x_slot = lambda tile: lax.rem(tile + X_SLOTS, X_SLOTS)
    out_slot = lambda tile: lax.rem(tile + OUT_SLOTS, OUT_SLOTS)
    x_tile = lambda tile: xin_scr[x_slot(tile)].reshape(TILE_T, D_MODEL)
    fetch = lambda tile: _tile_block_copies(x_hbm, xin_scr, xin_sem, jnp.minimum(tile, n_total - 1), x_slot(tile),
                                            n_tiles, to_hbm=False)
    write_back = lambda tile: _tile_block_copies(o_hbm, out_scr, out_sem, tile, out_slot(tile), n_tiles, to_hbm=True)

    @pl.when(g >= 2 + OUT_SLOTS)
    def _():
        for copy in write_back(g - 2 - OUT_SLOTS):
            copy.wait()

    @pl.when(lax.rem(g + n_tiles - 1, n_tiles) == 0)
    def _():
        lxtail_scr[...] = lx0_ref[...]
        h_scr[...] = h0_ref[...]
        st_scr[...] = st0_ref[...]

    tiles_after_head = (4, 4, 3, 3, 2, 2, 1, 1)
    assert sum(tiles_after_head) == N_IN_TILES and len(tiles_after_head) == LRU_HEADS

    def finish(slot_finish, after_gates):
        res_scr[...] = _finish_tile(
            lambda: x_tile(g - 1), cos_ref, sin_ref, convw_ref, convb_ref, wrg_ref, wig_ref, brg_ref, big_ref,
            lam_ref, rng_ref, wout_ref, dmaskp_ref, kdec_ref, qdec_ref, even_ref, odd_ref, gst_ref,
            bdm_ref, slot_finish, xc_scr, xcb_scr, pre_scr, y_scr, lxtail_scr, h_scr, st_scr, after_gates)

    def final_norm(tile):
        out = _rmsnorm_rows(res_scr[...], fng_ref[...])
        out_scr[out_slot(tile)] = out.reshape(SLOT_SHAPE)

    def norm_and_write_back(tile):
        final_norm(tile)
        for copy in write_back(tile):
            copy.start()

    final_norm_after_head, pre_norm_after_head = 4, 6

    def step(u_project, u_next, slot_project, slot_finish, norm_previous=True):
        for copy in fetch(g + 1):
            copy.wait()
        for i, copy in enumerate(fetch(g + 2)):
            copy.start(priority=i % 2)

        def after_gates(h):
            first = sum(tiles_after_head[:h])
            for t in range(first, first + tiles_after_head[h]):
                _project_col_tile(t, u_project, win_ref, slot_project)
            if norm_previous and h == final_norm_after_head:
                final_norm(g - 2)
            if h == pre_norm_after_head:
                _norm_tile(x_tile(g + 1), g1_ref, u_next)

        finish(slot_finish, after_gates)
        if norm_previous:
            for i, copy in enumerate(write_back(g - 2)):
                copy.start(priority=i % 2)

    @pl.when(g == 0)
    def _():
        for copy in fetch(g):
            copy.start()
        for copy in win_copies + [wout_copy]:
            copy.start()
        for copy in fetch(g):
            copy.wait()
        for tile in (g + 1, g + 2):
            for copy in fetch(tile):
                copy.start()
        _norm_tile(x_tile(g), g1_ref, u_a)
        for t in range(N_IN_TILES):
            if t % (W_CHUNK_COLS // MXU_COLS) == 0:
                win_copies[t // (W_CHUNK_COLS // MXU_COLS)].wait()
            _project_col_tile(t, u_a, win_ref, slots[0])
        meta_rows = lambda ref, n: ref.at[pl.ds(0, n)]
        _meta_states(meta_ref, cos_meta_ref, sin_meta_ref, g1_ref,
                     win_ref.at[:, pl.ds(OFF_LX, LRU_WIDTH)], win_ref.at[:, pl.ds(OFF_K, RET_QK_WIDTH)],
                     win_ref.at[:, pl.ds(OFF_V, RET_WIDTH)], convw_ref, convb_ref, wrg_ref, wig_ref, brg_ref, big_ref,
                     lam_ref, kdec_time_ref, bdm_ref, lx0_ref, h0_ref, st0_ref,
                     meta_rows(slots[1][0], SUBLANES + N_META), meta_rows(xc_scr, N_META), meta_rows(xcb_scr, N_META),
                     meta_rows(pre_scr, N_META))
        for copy in fetch(g + 1):
            copy.wait()
        _norm_tile(x_tile(g + 1), g1_ref, u_b)

    @pl.when(g == 1)
    def _():
        wout_copy.wait()
        step(u_b, u_a, slots[1], slots[0], norm_previous=False)

    regular = jnp.logical_and(g >= 2, g < n_total)

    @pl.when(jnp.logical_and(regular, lax.rem(g, 2) == 0))
    def _():
        step(u_a, u_b, slots[0], slots[1])

    @pl.when(jnp.logical_and(regular, lax.rem(g, 2) == 1))
    def _():
        step(u_b, u_a, slots[1], slots[0])

    @pl.when(g == n_total)
    def _():
        for copy in fetch(g + 1):
            copy.wait()
        norm_and_write_back(g - 2)
        finish(slots[(n_total - 1) % 2], lambda h: None)
        for copy in write_back(g - 1 - OUT_SLOTS):
            copy.wait()
        norm_and_write_back(g - 1)
        for tile in (g - 2, g - 1):
            for copy in write_back(tile):
                copy.wait()


def _const_spec(shape):
    nd = len(shape)
    return pl.BlockSpec(shape, lambda g, _nd=nd: (0,) * _nd, pipeline_mode=pl.Buffered(1))


def kernel(x, meta_tokens, norm_gain, w_in, conv_w, conv_b, w_rg, b_rg, w_ig, b_ig,
           lru_lambda, ret_norm_gain, w_out, final_norm_gain):
    B, S, D = x.shape
    assert D == D_MODEL and S % TILE_T == 0 and TILE_T % CHUNK == 0
    assert norm_gain.shape[0] == 1, "single-layer block"
    assert meta_tokens.shape == (N_META, D_MODEL)

    assert w_in.shape == (1, D_MODEL, IN_WIDTH) and w_out.shape == (1, MIX_WIDTH, D_MODEL)
    assert w_rg.shape == w_ig.shape == (1, LRU_HEADS, LRU_BLOCK, LRU_BLOCK)

    g1 = norm_gain[0].reshape(1, D_MODEL)
    fng = final_norm_gain.reshape(1, D_MODEL)
    convw = conv_w[0]
    convb = conv_b[0].reshape(1, LRU_WIDTH)
    brg = b_rg[0].reshape(1, LRU_WIDTH)
    big = b_ig[0].reshape(1, LRU_WIDTH)
    lam = lru_lambda[0].reshape(1, LRU_WIDTH)
    rng = ret_norm_gain[0].reshape(1, RET_WIDTH)

    k_dec, dmask_blocked, k_dec_blocked, q_dec_blocked, even, odd, g_state, bd_mask = _retention_tables()
    cos_t, sin_t = _rotary_tables(N_META + S)
    cos_m, sin_m = jnp.asarray(cos_t[:N_META]), jnp.asarray(sin_t[:N_META])
    cos_blocked = jnp.asarray(_chunk_rows_blocked(cos_t[N_META:]))
    sin_blocked = jnp.asarray(_chunk_rows_blocked(sin_t[N_META:]))

    tt = TILE_T
    n_tiles = S // tt
    n_total = B * n_tiles
    assert n_total >= 3, "the four-stage tile pipeline needs at least three tiles"

    rot_spec = pl.BlockSpec((tt, LANES), lambda g: (jnp.maximum(g - 1, 0) % n_tiles, 0))
    in_specs = [
        pl.BlockSpec(memory_space=pl.ANY),
        rot_spec, rot_spec,
        _const_spec((1, D_MODEL)),
        pl.BlockSpec(memory_space=pl.ANY),
        _const_spec((CONV_WIDTH, LRU_WIDTH)), _const_spec((1, LRU_WIDTH)),
        _const_spec((None, LRU_HEADS, LRU_BLOCK, LRU_BLOCK)),
        _const_spec((None, LRU_HEADS, LRU_BLOCK, LRU_BLOCK)),
        _const_spec((1, LRU_WIDTH)), _const_spec((1, LRU_WIDTH)), _const_spec((1, LRU_WIDTH)),
        _const_spec((1, RET_WIDTH)),
        pl.BlockSpec(memory_space=pl.ANY),
        _const_spec((1, D_MODEL)),
        _const_spec((N_PAIRS, CHUNK, 2 * CHUNK)),
        _const_spec((CHUNK, RET_QK_WIDTH)), _const_spec((CHUNK, RET_QK_WIDTH)),
        _const_spec((1, RET_QK_WIDTH)), _const_spec((1, RET_QK_WIDTH)),
        _const_spec((N_PAIRS, LANES, PAIR_V)),
        _const_spec((LANES, PAIR_V)),
        _const_spec((N_META, D_MODEL)),
        _const_spec((N_META, LANES)), _const_spec((N_META, LANES)),
        _const_spec((CHUNK, RET_QK_WIDTH)),
    ]
    slot_scratch = [
        pltpu.VMEM((CONV_HIST + tt, LRU_WIDTH), f32),
        pltpu.VMEM((tt, LRU_WIDTH), f32),
        pltpu.VMEM((tt, RET_QK_WIDTH), f32),
        pltpu.VMEM((tt, RET_QK_WIDTH), f32),
        pltpu.VMEM((tt, RET_WIDTH), bf16),
        pltpu.VMEM((tt, RET_WIDTH), f32),
    ]
    scratch = [pltpu.VMEM((tt, D_MODEL), bf16)] * 2 + slot_scratch + slot_scratch + [
        pltpu.VMEM((tt, LRU_WIDTH), f32),
        pltpu.VMEM((tt, LRU_WIDTH), bf16),
        pltpu.VMEM((tt, 2 * LRU_WIDTH), f32),
        pltpu.VMEM((tt, MIX_WIDTH), bf16),
        pltpu.VMEM((CONV_HIST, LRU_WIDTH), f32),
        pltpu.VMEM((SUBLANES, LRU_WIDTH), f32),
        pltpu.VMEM((N_PAIRS, LANES, PAIR_V), f32),
        pltpu.VMEM((CONV_HIST, LRU_WIDTH), f32),
        pltpu.VMEM((SUBLANES, LRU_WIDTH), f32),
        pltpu.VMEM((N_PAIRS, LANES, PAIR_V), f32),
        pltpu.VMEM((tt, D_MODEL), f32),
        pltpu.VMEM((X_SLOTS,) + SLOT_SHAPE, f32),
        pltpu.VMEM((OUT_SLOTS,) + SLOT_SHAPE, f32),
        pltpu.SemaphoreType.DMA((X_SLOTS,)),
        pltpu.SemaphoreType.DMA((OUT_SLOTS,)),
        pltpu.VMEM((D_MODEL, IN_WIDTH), f32),
        pltpu.VMEM((MIX_WIDTH, D_MODEL), f32),
        pltpu.SemaphoreType.DMA((N_W_CHUNKS + 1,)),
    ]
    out = pl.pallas_call(
        functools.partial(_main_kernel, n_tiles, n_total),
        grid=(n_total + 1,),
        in_specs=in_specs,
        out_specs=pl.BlockSpec(memory_space=pl.ANY),
        out_shape=jax.ShapeDtypeStruct((B, S // CHUNK, CHUNK, D_MODEL), x.dtype),
        scratch_shapes=scratch,
        compiler_params=pltpu.CompilerParams(
            dimension_semantics=("arbitrary",),
            vmem_limit_bytes=VMEM_LIMIT_BYTES),
        name="hybrid_main",
    )(x.reshape(B, S // CHUNK, CHUNK, D_MODEL),
      cos_blocked, sin_blocked, g1, w_in, convw, convb, w_rg, w_ig, brg, big, lam, rng, w_out, fng,
      dmask_blocked, k_dec_blocked, q_dec_blocked, even, odd, g_state, bd_mask, meta_tokens, cos_m, sin_m, k_dec)
    return out.reshape(B, S, D_MODEL)
```

```python
import functools

import numpy as np
import jax
import jax.numpy as jnp
from jax import lax
from jax.experimental import pallas as pl
from jax.experimental.pallas import tpu as pltpu

f32 = jnp.float32
bf16 = jnp.bfloat16

D_MODEL = 1024
N_META = 16
LRU_WIDTH = 1024
LRU_HEADS = 8
LRU_BLOCK = 128
CONV_WIDTH = 4
LRU_C = 8.0
RET_HEADS = 8
RET_QK_DIM = 64
RET_V_DIM = 128
RET_QK_WIDTH = 512
RET_WIDTH = 1024
CHUNK = 128
ROPE_BASE = 10000.0
MIX_WIDTH = 2048
EPS = 1e-6
QK_SCALE = RET_QK_DIM ** -0.5

OFF_LX, OFF_GATE, OFF_Q, OFF_K, OFF_V, OFF_RG = 0, 1024, 2048, 2560, 3072, 4096
IN_WIDTH = 5120

LANES = 128
SUBLANES = 8
N_PAIRS = RET_HEADS // 2
PAIR_V = 2 * RET_V_DIM
TILE_T = 256
VMEM_LIMIT_BYTES = 60000 * 1024


def _lane_head():
    return np.arange(RET_QK_WIDTH) // RET_QK_DIM


BLOCK_STEPS = CHUNK // SUBLANES
CHUNK_ROW_TIME = np.arange(CHUNK).reshape(SUBLANES, BLOCK_STEPS).T.reshape(-1)


def _chunk_rows_blocked(table):
    n = table.shape[0]
    return table.reshape((n // CHUNK, CHUNK) + table.shape[1:])[:, CHUNK_ROW_TIME].reshape(table.shape)


def _retention_tables():
    log_g = np.log1p(-np.exp2(-5.0 - np.arange(RET_HEADS, dtype=np.float32))).astype(np.float32)
    idx = np.arange(CHUNK, dtype=np.float32)
    diff = idx[:, None] - idx[None, :]
    dmask = np.where(diff[None] >= 0.0, np.exp(np.maximum(diff, 0.0)[None] * log_g[:, None, None]), 0.0)
    dmask_pair = np.concatenate([dmask[0::2], dmask[1::2]], axis=-1)
    lg_lane = log_g[_lane_head()]
    k_dec = np.exp((CHUNK - 1.0 - idx)[:, None] * lg_lane[None, :])
    q_dec = np.exp((idx + 1.0)[:, None] * lg_lane[None, :]) * QK_SCALE
    even = (_lane_head() % 2 == 0)[None, :]
    g_chunk = np.exp(CHUNK * log_g)
    g_state = np.broadcast_to(np.repeat(g_chunk, RET_V_DIM).reshape(N_PAIRS, 1, PAIR_V), (N_PAIRS, LANES, PAIR_V))
    row_par = np.arange(LANES) // RET_QK_DIM
    col_par = np.arange(PAIR_V) // RET_V_DIM
    bd_mask = row_par[:, None] == col_par[None, :]
    key_order = np.concatenate([CHUNK_ROW_TIME, CHUNK + CHUNK_ROW_TIME])
    dmask_blocked = dmask_pair[:, CHUNK_ROW_TIME][:, :, key_order]
    as_f32 = lambda a: jnp.asarray(np.asarray(a, np.float32))
    return tuple(as_f32(t) for t in (k_dec, dmask_blocked, k_dec[CHUNK_ROW_TIME], q_dec[CHUNK_ROW_TIME],
                                     even, ~even, g_state, bd_mask))


def _rotary_tables(n_pos):
    half = RET_QK_DIM // 2
    inv = (np.float32(ROPE_BASE) ** (-np.arange(half, dtype=np.float32) / half)).astype(np.float32)
    ang = np.arange(n_pos).astype(np.float32)[:, None] * inv[None, :]
    cos, sin = np.cos(ang), np.sin(ang)
    cos_t = np.concatenate([cos, cos, cos, cos], axis=-1).astype(np.float32)
    sin_t = np.concatenate([-sin, sin, -sin, sin], axis=-1).astype(np.float32)
    return cos_t, sin_t


def _rmsnorm_rows(x, gain_row):
    ms = jnp.mean(x * x, axis=-1, keepdims=True)
    return x * lax.rsqrt(ms + EPS) * gain_row


def _silu(x):
    hx = 0.5 * x
    return hx * jnp.tanh(hx) + hx


def _half_conv_params(convw_ref, convb_ref):
    return [0.5 * convw_ref[k:k + 1, :] for k in range(CONV_WIDTH)], 0.5 * convb_ref[...]


def _conv(n_rows, lx_scr, convw_ref, convb_ref, xc_scr, xcb_scr):
    base = SUBLANES
    taps, bias = _half_conv_params(convw_ref, convb_ref)
    xc = bias + taps[3] * lx_scr[pl.ds(base, n_rows), :]
    xc = xc + taps[2] * lx_scr[pl.ds(base - 1, n_rows), :]
    xc = xc + taps[1] * lx_scr[pl.ds(base - 2, n_rows), :]
    xc = xc + taps[0] * lx_scr[pl.ds(base - 3, n_rows), :]
    xc_scr[...] = xc
    xcb_scr[...] = xc.astype(bf16)


CONV_HIST = (CONV_WIDTH - 1) * SUBLANES


def _conv_blocked(lx_scr, convw_ref, convb_ref, xc_scr, xcb_scr):
    taps, bias = _half_conv_params(convw_ref, convb_ref)
    last_sublane = lax.broadcasted_iota(jnp.int32, (SUBLANES, LRU_WIDTH), 0) == SUBLANES - 1
    for c in range(TILE_T // CHUNK):
        base = CONV_HIST + c * CHUNK
        rows = lambda group, n: lx_scr[base + group * SUBLANES:base + (group + n) * SUBLANES, :]
        wrapped = []
        for k in range(1, CONV_WIDTH):
            own, prev = rows(BLOCK_STEPS - k, 1), rows(-k, 1)
            wrapped.append(pltpu.roll(jnp.where(last_sublane, prev, own), 1, 0))
        xc = bias + taps[CONV_WIDTH - 1] * rows(0, BLOCK_STEPS)
        for k in range(1, CONV_WIDTH):
            shifted = jnp.concatenate(wrapped[:k][::-1] + [rows(0, BLOCK_STEPS - k)], axis=0)
            xc = xc + taps[CONV_WIDTH - 1 - k] * shifted
        xc_scr[c * CHUNK:(c + 1) * CHUNK, :] = xc
        xcb_scr[c * CHUNK:(c + 1) * CHUNK, :] = xc.astype(bf16)


def _gates_head(h, xcb_scr, wrg_ref, wig_ref, pre_scr):
    lanes = slice(h * LRU_BLOCK, (h + 1) * LRU_BLOCK)
    wg = jnp.concatenate([wrg_ref[h], wig_ref[h]], axis=-1).astype(bf16)
    pre = jnp.dot(xcb_scr[:, lanes], wg, preferred_element_type=f32)
    pre_scr[:, lanes] = pre[:, :LRU_BLOCK]
    pre_scr[:, LRU_WIDTH + h * LRU_BLOCK:LRU_WIDTH + (h + 1) * LRU_BLOCK] = pre[:, LRU_BLOCK:]


def _lru_maps(half_xc, half_pre_r, half_pre_i, half_brg, half_big, half_c_sp):
    nl = jnp.tanh(half_pre_r + half_brg) * half_c_sp + half_c_sp
    twice_i = jnp.tanh(half_pre_i + half_big) + 1.0
    a = jnp.exp(-nl)
    z = jnp.tanh(nl) * (1.0 + a * a)
    beta = jnp.where(z > 0.0, z * lax.rsqrt(z), 0.0)
    return a, beta * twice_i * half_xc


def _scan_sublanes(a, b):
    rowid = lax.broadcasted_iota(jnp.int32, a.shape, 0)
    for s in (1, 2, 4):
        keep = rowid >= s
        a_s = jnp.where(keep, pltpu.roll(a, s, 0), 1.0)
        b_s = jnp.where(keep, pltpu.roll(b, s, 0), 0.0)
        b = a * b_s + b
        a = a * a_s
    return a, b


def _lru_chunk_blocked(xc, pre_r, pre_i, brg, big, c_sp, carry):
    a, b = _lru_maps(xc, pre_r, pre_i, brg, big, c_sp)
    width = xc.shape[1]
    a3 = a.reshape(BLOCK_STEPS, SUBLANES, width)
    b3 = b.reshape(BLOCK_STEPS, SUBLANES, width)
    decay, local = [a3[0]], [b3[0]]
    for j in range(1, BLOCK_STEPS):
        local.append(a3[j] * local[-1] + b3[j])
        decay.append(a3[j] * decay[-1])
    a_blocks, b_blocks = _scan_sublanes(decay[-1], local[-1])
    after = a_blocks * carry + b_blocks
    rowid = lax.broadcasted_iota(jnp.int32, carry.shape, 0)
    before = jnp.where(rowid >= 1, pltpu.roll(after, 1, 0), carry)
    hs = [local[j] + decay[j] * before for j in range(BLOCK_STEPS)]
    new_carry = jnp.broadcast_to(after[SUBLANES - 1:SUBLANES, :], carry.shape)
    return jnp.concatenate(hs, axis=0), new_carry


def _lru_block(xc, pre_r, pre_i, brg, big, c_sp, carry):
    rows, width = xc.shape
    a, b = _lru_maps(xc, pre_r, pre_i, brg, big, c_sp)
    nv = rows // SUBLANES
    a3 = a.reshape(nv, SUBLANES, width)
    b3 = b.reshape(nv, SUBLANES, width)
    rowid = lax.broadcasted_iota(jnp.int32, (nv, SUBLANES, width), 1)
    for s in (1, 2, 4):
        keep = rowid >= s
        a_s = jnp.where(keep, pltpu.roll(a3, s, 1), 1.0)
        b_s = jnp.where(keep, pltpu.roll(b3, s, 1), 0.0)
        b3 = a3 * b_s + b3
        a3 = a3 * a_s
    hs = []
    for v in range(nv):
        h_v = a3[v] * carry + b3[v]
        carry = h_v[SUBLANES - 1:SUBLANES, :]
        hs.append(h_v)
    return jnp.concatenate(hs, axis=0), carry


def _rotary(t, cos, sin):
    half = RET_QK_DIM // 2
    in_first_half = (lax.broadcasted_iota(jnp.int32, t.shape, 1) // half) % 2 == 0
    partner = jnp.where(in_first_half, pltpu.roll(t, LANES - half, 1), pltpu.roll(t, half, 1))
    return t * cos + partner * sin


def _dot_t0(a, b):
    return lax.dot_general(a, b, (((0,), (0,)), ((), ())), preferred_element_type=f32)


def _dot_nt(a, b):
    return lax.dot_general(a, b, (((1,), (1,)), ((), ())), preferred_element_type=f32)


def _meta_states(meta_ref, cos_ref, sin_ref, g1_ref, wlx_ref, wk_ref, wv_ref, convw_ref, convb_ref,
                 wrg_ref, wig_ref, brg_ref, big_ref, lam_ref, kdec_ref, bdm_ref,
                 lx_out, h_out, st_out,
                 lx_scr, xc_scr, xcb_scr, pre_scr):
    u = _rmsnorm_rows(meta_ref[...], g1_ref[...]).astype(bf16)
    lx_scr[0:SUBLANES, :] = jnp.zeros((SUBLANES, LRU_WIDTH), f32)
    lx_scr[SUBLANES:SUBLANES + N_META, :] = jnp.dot(u, wlx_ref[...].astype(bf16), preferred_element_type=f32)
    k = jnp.dot(u, wk_ref[...].astype(bf16), preferred_element_type=f32)
    v = jnp.dot(u, wv_ref[...].astype(bf16), preferred_element_type=f32).astype(bf16)

    _conv(N_META, lx_scr, convw_ref, convb_ref, xc_scr, xcb_scr)
    for h in range(LRU_HEADS):
        _gates_head(h, xcb_scr, wrg_ref, wig_ref, pre_scr)
    for back in range(1, CONV_WIDTH):
        row = lx_scr[SUBLANES + N_META - back:SUBLANES + N_META - back + 1, :]
        group = CONV_WIDTH - 1 - back
        lx_out[group * SUBLANES:(group + 1) * SUBLANES, :] = jnp.broadcast_to(row, (SUBLANES, LRU_WIDTH))
    half_c_sp = 0.5 * LRU_C * jax.nn.softplus(-lam_ref[...])
    _, carry = _lru_block(xc_scr[...], pre_scr[:, :LRU_WIDTH], pre_scr[:, LRU_WIDTH:],
                          0.5 * brg_ref[...], 0.5 * big_ref[...], half_c_sp, jnp.zeros((1, LRU_WIDTH), f32))
    h_out[...] = jnp.broadcast_to(carry, (SUBLANES, LRU_WIDTH))

    cos, sin = cos_ref[...], sin_ref[...]
    for p in range(N_PAIRS):
        sl = slice(p * LANES, (p + 1) * LANES)
        k_rot = _rotary(k[:, sl], cos, sin)
        kd = (k_rot * kdec_ref[CHUNK - N_META:CHUNK, sl]).astype(bf16)
        st_out[p] = _dot_t0(kd, v[:, p * PAIR_V:(p + 1) * PAIR_V]) * bdm_ref[...]


def _norm_tile(x_tile, g1_ref, u_scr):
    u_scr[...] = _rmsnorm_rows(x_tile, g1_ref[...]).astype(bf16)


MXU_COLS = 256
N_IN_TILES = IN_WIDTH // MXU_COLS
W_CHUNK_COLS = 4 * MXU_COLS
N_W_CHUNKS = IN_WIDTH // W_CHUNK_COLS


def _project_col_tile(t, u_scr, win_ref, slot):
    lx_scr, gate_scr, q_scr, k_scr, v_scr, rg_scr = slot
    off = t * MXU_COLS
    res = jnp.dot(u_scr[...], win_ref[:, off:off + MXU_COLS].astype(bf16), preferred_element_type=f32)
    for dst, start, rows, dt in ((lx_scr, OFF_LX, slice(CONV_HIST, CONV_HIST + TILE_T), f32), (gate_scr, OFF_GATE, slice(None), f32),
                                 (q_scr, OFF_Q, slice(None), f32), (k_scr, OFF_K, slice(None), f32),
                                 (v_scr, OFF_V, slice(None), bf16), (rg_scr, OFF_RG, slice(None), f32)):
        width = dst.shape[1]
        if start <= off < start + width:
            dst[rows, off - start:off - start + MXU_COLS] = res.astype(dt)
            return
    raise AssertionError("column tile outside the projection")


def _finish_tile(load_x, cos_ref, sin_ref, convw_ref, convb_ref, wrg_ref, wig_ref, brg_ref, big_ref, lam_ref, rng_ref,
                 wout_ref, dmaskp_ref, kdec_ref, qdec_ref, even_ref, odd_ref, gst_ref, bdm_ref,
                 slot, xc_scr, xcb_scr, pre_scr, y_scr, lxtail_scr, h_scr, st_scr, after_gates):
    lx_scr, gate_scr, q_scr, k_scr, v_scr, rg_scr = slot
    tt = TILE_T

    def retention_unit(c, p):
        rows = slice(c * CHUNK, (c + 1) * CHUNK)
        cos, sin = cos_ref[rows, :], sin_ref[rows, :]
        sl = slice(p * LANES, (p + 1) * LANES)
        q_rot = _rotary(q_scr[rows, sl], cos, sin)
        k_rot = _rotary(k_scr[rows, sl], cos, sin)
        q_b = (q_rot * QK_SCALE).astype(bf16)
        q_d = (q_rot * qdec_ref[:, sl]).astype(bf16)
        k_d = (k_rot * kdec_ref[:, sl]).astype(bf16)
        k_b = k_rot.astype(bf16)
        first_head, second_head = even_ref[:, sl].astype(bf16), odd_ref[:, sl].astype(bf16)
        kk = jnp.concatenate([k_b * first_head, k_b * second_head], axis=0)
        kk_d = jnp.concatenate([k_d * first_head, k_d * second_head], axis=0)
        v_pair = v_scr[rows, p * PAIR_V:(p + 1) * PAIR_V]
        zeros = jnp.zeros((CHUNK, RET_V_DIM), bf16)
        v_bd = jnp.concatenate([jnp.concatenate([v_pair[:, :RET_V_DIM], zeros], axis=1),
                                jnp.concatenate([zeros, v_pair[:, RET_V_DIM:]], axis=1)], axis=0)
        st_pair = st_scr[p]
        s = _dot_nt(q_b, kk) * dmaskp_ref[p]
        o = jnp.dot(jnp.concatenate([s.astype(bf16), q_d], axis=1),
                    jnp.concatenate([v_bd, st_pair.astype(bf16)], axis=0), preferred_element_type=f32)
        for e in range(2):
            h = 2 * p + e
            hs = slice(h * RET_V_DIM, (h + 1) * RET_V_DIM)
            o_h = o[:, e * RET_V_DIM:(e + 1) * RET_V_DIM]
            mu = jnp.mean(o_h, axis=-1, keepdims=True)
            oc = o_h - mu
            var = jnp.mean(oc * oc, axis=-1, keepdims=True)
            on = oc * lax.rsqrt(var + EPS) * rng_ref[:, hs]
            y_scr[rows, LRU_WIDTH + h * RET_V_DIM:LRU_WIDTH + (h + 1) * RET_V_DIM] = (
                on * _silu(rg_scr[rows, hs])).astype(bf16)
        st_scr[p] = gst_ref[p] * st_pair + _dot_t0(kk_d, v_bd)

    lx_scr[0:CONV_HIST, :] = lxtail_scr[...]
    _conv_blocked(lx_scr, convw_ref, convb_ref, xc_scr, xcb_scr)
    lxtail_scr[...] = lx_scr[pl.ds(tt, CONV_HIST), :]
    for h in range(LRU_HEADS):
        lanes = slice(h * LRU_BLOCK, (h + 1) * LRU_BLOCK)
        ilanes = slice(LRU_WIDTH + h * LRU_BLOCK, LRU_WIDTH + (h + 1) * LRU_BLOCK)
        _gates_head(h, xcb_scr, wrg_ref, wig_ref, pre_scr)
        after_gates(h)
        full = (CHUNK, LRU_BLOCK)
        c_sp = jnp.broadcast_to(0.5 * LRU_C * jax.nn.softplus(-lam_ref[:, lanes]), full)
        brg, big = jnp.broadcast_to(0.5 * brg_ref[:, lanes], full), jnp.broadcast_to(0.5 * big_ref[:, lanes], full)
        carry = h_scr[:, lanes]
        for c in range(tt // CHUNK):
            rows = slice(c * CHUNK, (c + 1) * CHUNK)
            hb, carry = _lru_chunk_blocked(xc_scr[rows, lanes], pre_scr[rows, lanes], pre_scr[rows, ilanes],
                                           brg, big, c_sp, carry)
            y_scr[rows, lanes] = (hb * _silu(gate_scr[rows, lanes])).astype(bf16)
        h_scr[:, lanes] = carry

    for c in range(tt // CHUNK):
        for p in range(N_PAIRS):
            retention_unit(c, p)

    return load_x() + jnp.dot(y_scr[...], wout_ref[...].astype(bf16), preferred_element_type=f32)


X_SLOTS = 4
OUT_SLOTS = 2
SLOT_SHAPE = (TILE_T // CHUNK, BLOCK_STEPS, SUBLANES, D_MODEL)


def _tile_block_copies(hbm_ref, vmem_ref, sem_ref, tile, slot, n_tiles, to_hbm):
    chunks = TILE_T // CHUNK
    b = lax.div(tile, n_tiles)
    first_chunk = (tile - b * n_tiles) * chunks
    copies = []
    for s in range(SUBLANES):
        hbm = hbm_ref.at[b, pl.ds(first_chunk, chunks), pl.ds(s * BLOCK_STEPS, BLOCK_STEPS), :]
        vmem = vmem_ref.at[slot, :, :, s, :]
        src, dst = (vmem, hbm) if to_hbm else (hbm, vmem)
        copies.append(pltpu.make_async_copy(src, dst, sem_ref.at[slot]))
    return copies


def _main_kernel(n_tiles, n_total,
                 x_hbm, cos_ref, sin_ref, g1_ref, win_hbm, convw_ref, convb_ref, wrg_ref, wig_ref,
                 brg_ref, big_ref, lam_ref, rng_ref, wout_hbm, fng_ref, dmaskp_ref, kdec_ref, qdec_ref, even_ref, odd_ref,
                 gst_ref, bdm_ref, meta_ref, cos_meta_ref, sin_meta_ref, kdec_time_ref,
                 o_hbm,
                 u_a, u_b,
                 lx_a, gate_a, q_a, k_a, v_a, rg_a,
                 lx_b, gate_b, q_b, k_b, v_b, rg_b,
                 xc_scr, xcb_scr, pre_scr, y_scr, lxtail_scr, h_scr, st_scr,
                 lx0_ref, h0_ref, st0_ref, res_scr,
                 xin_scr, out_scr, xin_sem, out_sem,
                 win_ref, wout_ref, w_sem):
    g = pl.program_id(0)
    slots = ((lx_a, gate_a, q_a, k_a, v_a, rg_a), (lx_b, gate_b, q_b, k_b, v_b, rg_b))
    win_copies = [pltpu.make_async_copy(win_hbm.at[0, :, pl.ds(c * W_CHUNK_COLS, W_CHUNK_COLS)],
                                        win_ref.at[:, pl.ds(c * W_CHUNK_COLS, W_CHUNK_COLS)], w_sem.at[c])
                  for c in range(N_W_CHUNKS)]
    wout_copy = pltpu.make_async_copy(wout_hbm.at[0], wout_ref, w_sem.at[N_W_CHUNKS])
    x_slot = lambda tile: lax.rem(tile + X_SLOTS, X_SLOTS)
    out_slot = lambda tile: lax.rem(tile + OUT_SLOTS, OUT_SLOTS)
    x_tile = lambda tile: xin_scr[x_slot(tile)].reshape(TILE_T, D_MODEL)
    fetch = lambda tile: _tile_block_copies(x_hbm, xin_scr, xin_sem, jnp.minimum(tile, n_total - 1), x_slot(tile),
                                            n_tiles, to_hbm=False)
    write_back = lambda tile: _tile_block_copies(o_hbm, out_scr, out_sem, tile, out_slot(tile), n_tiles, to_hbm=True)

    @pl.when(g >= 2 + OUT_SLOTS)
    def _():
        for copy in write_back(g - 2 - OUT_SLOTS):
            copy.wait()

    @pl.when(lax.rem(g + n_tiles - 1, n_tiles) == 0)
    def _():
        lxtail_scr[...] = lx0_ref[...]
        h_scr[...] = h0_ref[...]
        st_scr[...] = st0_ref[...]

    tiles_after_head = (4, 4, 3, 3, 2, 2, 1, 1)
    assert sum(tiles_after_head) == N_IN_TILES and len(tiles_after_head) == LRU_HEADS

    def finish(slot_finish, after_gates):
        res_scr[...] = _finish_tile(
            lambda: x_tile(g - 1), cos_ref, sin_ref, convw_ref, convb_ref, wrg_ref, wig_ref, brg_ref, big_ref,
            lam_ref, rng_ref, wout_ref, dmaskp_ref, kdec_ref, qdec_ref, even_ref, odd_ref, gst_ref,
            bdm_ref, slot_finish, xc_scr, xcb_scr, pre_scr, y_scr, lxtail_scr, h_scr, st_scr, after_gates)

    def final_norm(tile):
        out = _rmsnorm_rows(res_scr[...], fng_ref[...])
        out_scr[out_slot(tile)] = out.reshape(SLOT_SHAPE)

    def norm_and_write_back(tile):
        final_norm(tile)
        for copy in write_back(tile):
            copy.start()

    final_norm_after_head, pre_norm_after_head = 4, 6

    def step(u_project, u_next, slot_project, slot_finish, norm_previous=True):
        for copy in fetch(g + 1):
            copy.wait()
        for i, copy in enumerate(fetch(g + 2)):
            copy.start(priority=i % 2)

        def after_gates(h):
            first = sum(tiles_after_head[:h])
            for t in range(first, first + tiles_after_head[h]):
                _project_col_tile(t, u_project, win_ref, slot_project)
            if norm_previous and h == final_norm_after_head:
                final_norm(g - 2)
            if h == pre_norm_after_head:
                _norm_tile(x_tile(g + 1), g1_ref, u_next)

        finish(slot_finish, after_gates)
        if norm_previous:
            for i, copy in enumerate(write_back(g - 2)):
                copy.start(priority=i % 2)

    @pl.when(g == 0)
    def _():
        for copy in fetch(g):
            copy.start()
        for copy in win_copies + [wout_copy]:
            copy.start()
        for copy in fetch(g):
            copy.wait()
        for tile in (g + 1, g + 2):
            for copy in fetch(tile):
                copy.start()
        _norm_tile(x_tile(g), g1_ref, u_a)
        for t in range(N_IN_TILES):
            if t % (W_CHUNK_COLS // MXU_COLS) == 0:
                win_copies[t // (W_CHUNK_COLS // MXU_COLS)].wait()
            _project_col_tile(t, u_a, win_ref, slots[0])
        meta_rows = lambda ref, n: ref.at[pl.ds(0, n)]
        _meta_states(meta_ref, cos_meta_ref, sin_meta_ref, g1_ref,
                     win_ref.at[:, pl.ds(OFF_LX, LRU_WIDTH)], win_ref.at[:, pl.ds(OFF_K, RET_QK_WIDTH)],
                     win_ref.at[:, pl.ds(OFF_V, RET_WIDTH)], convw_ref, convb_ref, wrg_ref, wig_ref, brg_ref, big_ref,
                     lam_ref, kdec_time_ref, bdm_ref, lx0_ref, h0_ref, st0_ref,
                     meta_rows(slots[1][0], SUBLANES + N_META), meta_rows(xc_scr, N_META), meta_rows(xcb_scr, N_META),
                     meta_rows(pre_scr, N_META))
        for copy in fetch(g + 1):
            copy.wait()
        _norm_tile(x_tile(g + 1), g1_ref, u_b)

    @pl.when(g == 1)
    def _():
        wout_copy.wait()
        step(u_b, u_a, slots[1], slots[0], norm_previous=False)

    regular = jnp.logical_and(g >= 2, g < n_total)

    @pl.when(jnp.logical_and(regular, lax.rem(g, 2) == 0))
    def _():
        step(u_a, u_b, slots[0], slots[1])

    @pl.when(jnp.logical_and(regular, lax.rem(g, 2) == 1))
    def _():
        step(u_b, u_a, slots[1], slots[0])

    @pl.when(g == n_total)
    def _():
        for copy in fetch(g + 1):
            copy.wait()
        norm_and_write_back(g - 2)
        finish(slots[(n_total - 1) % 2], lambda h: None)
        for copy in write_back(g - 1 - OUT_SLOTS):
            copy.wait()
        norm_and_write_back(g - 1)
        for tile in (g - 2, g - 1):
            for copy in write_back(tile):
                copy.wait()


def _const_spec(shape):
    nd = len(shape)
    return pl.BlockSpec(shape, lambda g, _nd=nd: (0,) * _nd, pipeline_mode=pl.Buffered(1))


def kernel(x, meta_tokens, norm_gain, w_in, conv_w, conv_b, w_rg, b_rg, w_ig, b_ig,
           lru_lambda, ret_norm_gain, w_out, final_norm_gain):
    B, S, D = x.shape
    assert D == D_MODEL and S % TILE_T == 0 and TILE_T % CHUNK == 0
    assert norm_gain.shape[0] == 1, "single-layer block"
    assert meta_tokens.shape == (N_META, D_MODEL)

    assert w_in.shape == (1, D_MODEL, IN_WIDTH) and w_out.shape == (1, MIX_WIDTH, D_MODEL)
    assert w_rg.shape == w_ig.shape == (1, LRU_HEADS, LRU_BLOCK, LRU_BLOCK)

    g1 = norm_gain[0].reshape(1, D_MODEL)
    fng = final_norm_gain.reshape(1, D_MODEL)
    convw = conv_w[0]
    convb = conv_b[0].reshape(1, LRU_WIDTH)
    brg = b_rg[0].reshape(1, LRU_WIDTH)
    big = b_ig[0].reshape(1, LRU_WIDTH)
    lam = lru_lambda[0].reshape(1, LRU_WIDTH)
    rng = ret_norm_gain[0].reshape(1, RET_WIDTH)

    k_dec, dmask_blocked, k_dec_blocked, q_dec_blocked, even, odd, g_state, bd_mask = _retention_tables()
    cos_t, sin_t = _rotary_tables(N_META + S)
    cos_m, sin_m = jnp.asarray(cos_t[:N_META]), jnp.asarray(sin_t[:N_META])
    cos_blocked = jnp.asarray(_chunk_rows_blocked(cos_t[N_META:]))
    sin_blocked = jnp.asarray(_chunk_rows_blocked(sin_t[N_META:]))

    tt = TILE_T
    n_tiles = S // tt
    n_total = B * n_tiles
    assert n_total >= 3, "the four-stage tile pipeline needs at least three tiles"

    rot_spec = pl.BlockSpec((tt, LANES), lambda g: (jnp.maximum(g - 1, 0) % n_tiles, 0))
    in_specs = [
        pl.BlockSpec(memory_space=pl.ANY),
        rot_spec, rot_spec,
        _const_spec((1, D_MODEL)),
        pl.BlockSpec(memory_space=pl.ANY),
        _const_spec((CONV_WIDTH, LRU_WIDTH)), _const_spec((1, LRU_WIDTH)),
        _const_spec((None, LRU_HEADS, LRU_BLOCK, LRU_BLOCK)),
        _const_spec((None, LRU_HEADS, LRU_BLOCK, LRU_BLOCK)),
        _const_spec((1, LRU_WIDTH)), _const_spec((1, LRU_WIDTH)), _const_spec((1, LRU_WIDTH)),
        _const_spec((1, RET_WIDTH)),
        pl.BlockSpec(memory_space=pl.ANY),
        _const_spec((1, D_MODEL)),
        _const_spec((N_PAIRS, CHUNK, 2 * CHUNK)),
        _const_spec((CHUNK, RET_QK_WIDTH)), _const_spec((CHUNK, RET_QK_WIDTH)),
        _const_spec((1, RET_QK_WIDTH)), _const_spec((1, RET_QK_WIDTH)),
        _const_spec((N_PAIRS, LANES, PAIR_V)),
        _const_spec((LANES, PAIR_V)),
        _const_spec((N_META, D_MODEL)),
        _const_spec((N_META, LANES)), _const_spec((N_META, LANES)),
        _const_spec((CHUNK, RET_QK_WIDTH)),
    ]
    slot_scratch = [
        pltpu.VMEM((CONV_HIST + tt, LRU_WIDTH), f32),
        pltpu.VMEM((tt, LRU_WIDTH), f32),
        pltpu.VMEM((tt, RET_QK_WIDTH), f32),
        pltpu.VMEM((tt, RET_QK_WIDTH), f32),
        pltpu.VMEM((tt, RET_WIDTH), bf16),
        pltpu.VMEM((tt, RET_WIDTH), f32),
    ]
    scratch = [pltpu.VMEM((tt, D_MODEL), bf16)] * 2 + slot_scratch + slot_scratch + [
        pltpu.VMEM((tt, LRU_WIDTH), f32),
        pltpu.VMEM((tt, LRU_WIDTH), bf16),
        pltpu.VMEM((tt, 2 * LRU_WIDTH), f32),
        pltpu.VMEM((tt, MIX_WIDTH), bf16),
        pltpu.VMEM((CONV_HIST, LRU_WIDTH), f32),
        pltpu.VMEM((SUBLANES, LRU_WIDTH), f32),
        pltpu.VMEM((N_PAIRS, LANES, PAIR_V), f32),
        pltpu.VMEM((CONV_HIST, LRU_WIDTH), f32),
        pltpu.VMEM((SUBLANES, LRU_WIDTH), f32),
        pltpu.VMEM((N_PAIRS, LANES, PAIR_V), f32),
        pltpu.VMEM((tt, D_MODEL), f32),
        pltpu.VMEM((X_SLOTS,) + SLOT_SHAPE, f32),
        pltpu.VMEM((OUT_SLOTS,) + SLOT_SHAPE, f32),
        pltpu.SemaphoreType.DMA((X_SLOTS,)),
        pltpu.SemaphoreType.DMA((OUT_SLOTS,)),
        pltpu.VMEM((D_MODEL, IN_WIDTH), f32),
        pltpu.VMEM((MIX_WIDTH, D_MODEL), f32),
        pltpu.SemaphoreType.DMA((N_W_CHUNKS + 1,)),
    ]
    out = pl.pallas_call(
        functools.partial(_main_kernel, n_tiles, n_total),
        grid=(n_total + 1,),
        in_specs=in_specs,
        out_specs=pl.BlockSpec(memory_space=pl.ANY),
        out_shape=jax.ShapeDtypeStruct((B, S // CHUNK, CHUNK, D_MODEL), x.dtype),
        scratch_shapes=scratch,
        compiler_params=pltpu.CompilerParams(
            dimension_semantics=("arbitrary",),
            vmem_limit_bytes=VMEM_LIMIT_BYTES),
        name="hybrid_main",
    )(x.reshape(B, S // CHUNK, CHUNK, D_MODEL),
      cos_blocked, sin_blocked, g1, w_in, convw, convb, w_rg, w_ig, brg, big, lam, rng, w_out, fng,
      dmask_blocked, k_dec_blocked, q_dec_blocked, even, odd, g_state, bd_mask, meta_tokens, cos_m, sin_m, k_dec)
    return out.reshape(B, S, D_MODEL)
```

```python
import functools

import numpy as np
import jax
import jax.numpy as jnp
from jax import lax
from jax.experimental import pallas as pl
from jax.experimental.pallas import tpu as pltpu

f32 = jnp.float32
bf16 = jnp.bfloat16

D_MODEL = 1024
N_META = 16
LRU_WIDTH = 1024
LRU_HEADS = 8
LRU_BLOCK = 128
CONV_WIDTH = 4
LRU_C = 8.0
RET_HEADS = 8
RET_QK_DIM = 64
RET_V_DIM = 128
RET_QK_WIDTH = 512
RET_WIDTH = 1024
CHUNK = 128
ROPE_BASE = 10000.0
MIX_WIDTH = 2048
EPS = 1e-6
QK_SCALE = RET_QK_DIM ** -0.5

OFF_LX, OFF_GATE, OFF_Q, OFF_K, OFF_V, OFF_RG = 0, 1024, 2048, 2560, 3072, 4096
IN_WIDTH = 5120

LANES = 128
SUBLANES = 8
N_PAIRS = RET_HEADS // 2
PAIR_V = 2 * RET_V_DIM
TILE_T = 256
VMEM_LIMIT_BYTES = 60000 * 1024


def _lane_head():
    return np.arange(RET_QK_WIDTH) // RET_QK_DIM


BLOCK_STEPS = CHUNK // SUBLANES
CHUNK_ROW_TIME = np.arange(CHUNK).reshape(SUBLANES, BLOCK_STEPS).T.reshape(-1)


def _chunk_rows_blocked(table):
    n = table.shape[0]
    return table.reshape((n // CHUNK, CHUNK) + table.shape[1:])[:, CHUNK_ROW_TIME].reshape(table.shape)


def _retention_tables():
    log_g = np.log1p(-np.exp2(-5.0 - np.arange(RET_HEADS, dtype=np.float32))).astype(np.float32)
    idx = np.arange(CHUNK, dtype=np.float32)
    diff = idx[:, None] - idx[None, :]
    dmask = np.where(diff[None] >= 0.0, np.exp(np.maximum(diff, 0.0)[None] * log_g[:, None, None]), 0.0)
    dmask_pair = np.concatenate([dmask[0::2], dmask[1::2]], axis=-1)
    lg_lane = log_g[_lane_head()]
    k_dec = np.exp((CHUNK - 1.0 - idx)[:, None] * lg_lane[None, :])
    q_dec = np.exp((idx + 1.0)[:, None] * lg_lane[None, :]) * QK_SCALE
    even = (_lane_head() % 2 == 0)[None, :]
    g_chunk = np.exp(CHUNK * log_g)
    g_state = np.broadcast_to(np.repeat(g_chunk, RET_V_DIM).reshape(N_PAIRS, 1, PAIR_V), (N_PAIRS, LANES, PAIR_V))
    row_par = np.arange(LANES) // RET_QK_DIM
    col_par = np.arange(PAIR_V) // RET_V_DIM
    bd_mask = row_par[:, None] == col_par[None, :]
    key_order = np.concatenate([CHUNK_ROW_TIME, CHUNK + CHUNK_ROW_TIME])
    dmask_blocked = dmask_pair[:, CHUNK_ROW_TIME][:, :, key_order] * QK_SCALE
    as_f32 = lambda a: jnp.asarray(np.asarray(a, np.float32))
    return tuple(as_f32(t) for t in (k_dec, dmask_blocked, k_dec[CHUNK_ROW_TIME], q_dec[CHUNK_ROW_TIME],
                                     even, ~even, g_state, bd_mask))


def _rotary_tables(n_pos):
    half = RET_QK_DIM // 2
    inv = (np.float32(ROPE_BASE) ** (-np.arange(half, dtype=np.float32) / half)).astype(np.float32)
    ang = np.arange(n_pos).astype(np.float32)[:, None] * inv[None, :]
    cos, sin = np.cos(ang), np.sin(ang)
    cos_t = np.concatenate([cos, cos, cos, cos], axis=-1).astype(np.float32)
    sin_t = np.concatenate([-sin, sin, -sin, sin], axis=-1).astype(np.float32)
    return cos_t, sin_t


def _rmsnorm_rows(x, gain_row):
    ms = jnp.mean(x * x, axis=-1, keepdims=True)
    return x * lax.rsqrt(ms + EPS) * gain_row


def _silu(x):
    hx = 0.5 * x
    return hx * jnp.tanh(hx) + hx


def _half_conv_params(convw_ref, convb_ref):
    return [0.5 * convw_ref[k:k + 1, :] for k in range(CONV_WIDTH)], 0.5 * convb_ref[...]


def _conv(n_rows, lx_scr, convw_ref, convb_ref, xc_scr, xcb_scr):
    base = SUBLANES
    taps, bias = _half_conv_params(convw_ref, convb_ref)
    xc = bias + taps[3] * lx_scr[pl.ds(base, n_rows), :]
    xc = xc + taps[2] * lx_scr[pl.ds(base - 1, n_rows), :]
    xc = xc + taps[1] * lx_scr[pl.ds(base - 2, n_rows), :]
    xc = xc + taps[0] * lx_scr[pl.ds(base - 3, n_rows), :]
    xc_scr[...] = xc
    xcb_scr[...] = xc.astype(bf16)


CONV_HIST = (CONV_WIDTH - 1) * SUBLANES


def _conv_blocked(lx_scr, convw_ref, convb_ref, xc_scr, xcb_scr):
    taps, bias = _half_conv_params(convw_ref, convb_ref)
    last_sublane = lax.broadcasted_iota(jnp.int32, (SUBLANES, LRU_WIDTH), 0) == SUBLANES - 1
    for c in range(TILE_T // CHUNK):
        base = CONV_HIST + c * CHUNK
        rows = lambda group, n: lx_scr[base + group * SUBLANES:base + (group + n) * SUBLANES, :]
        wrapped = []
        for k in range(1, CONV_WIDTH):
            own, prev = rows(BLOCK_STEPS - k, 1), rows(-k, 1)
            wrapped.append(pltpu.roll(jnp.where(last_sublane, prev, own), 1, 0))
        xc = bias + taps[CONV_WIDTH - 1] * rows(0, BLOCK_STEPS)
        for k in range(1, CONV_WIDTH):
            shifted = jnp.concatenate(wrapped[:k][::-1] + [rows(0, BLOCK_STEPS - k)], axis=0)
            xc = xc + taps[CONV_WIDTH - 1 - k] * shifted
        xc_scr[c * CHUNK:(c + 1) * CHUNK, :] = xc
        xcb_scr[c * CHUNK:(c + 1) * CHUNK, :] = xc.astype(bf16)


def _gates_head(h, xcb_scr, wrg_ref, wig_ref, pre_scr):
    lanes = slice(h * LRU_BLOCK, (h + 1) * LRU_BLOCK)
    wg = jnp.concatenate([wrg_ref[h], wig_ref[h]], axis=-1).astype(bf16)
    pre = jnp.dot(xcb_scr[:, lanes], wg, preferred_element_type=f32)
    pre_scr[:, lanes] = pre[:, :LRU_BLOCK]
    pre_scr[:, LRU_WIDTH + h * LRU_BLOCK:LRU_WIDTH + (h + 1) * LRU_BLOCK] = pre[:, LRU_BLOCK:]


def _lru_maps(half_xc, half_pre_r, half_pre_i, half_brg, half_big, half_c_sp):
    nl = jnp.tanh(half_pre_r + half_brg) * half_c_sp + half_c_sp
    twice_i = jnp.tanh(half_pre_i + half_big) + 1.0
    a = jnp.exp(-nl)
    z = jnp.tanh(nl) * (1.0 + a * a)
    beta = jnp.where(z > 0.0, z * lax.rsqrt(z), 0.0)
    return a, beta * twice_i * half_xc


def _scan_sublanes(a, b):
    rowid = lax.broadcasted_iota(jnp.int32, a.shape, 0)
    for s in (1, 2, 4):
        keep = rowid >= s
        a_s = jnp.where(keep, pltpu.roll(a, s, 0), 1.0)
        b_s = jnp.where(keep, pltpu.roll(b, s, 0), 0.0)
        b = a * b_s + b
        a = a * a_s
    return a, b


def _lru_chunk_blocked(xc, pre_r, pre_i, brg, big, c_sp, carry):
    a, b = _lru_maps(xc, pre_r, pre_i, brg, big, c_sp)
    width = xc.shape[1]
    a3 = a.reshape(BLOCK_STEPS, SUBLANES, width)
    b3 = b.reshape(BLOCK_STEPS, SUBLANES, width)
    decay, local = [a3[0]], [b3[0]]
    for j in range(1, BLOCK_STEPS):
        local.append(a3[j] * local[-1] + b3[j])
        decay.append(a3[j] * decay[-1])
    a_blocks, b_blocks = _scan_sublanes(decay[-1], local[-1])
    after = a_blocks * carry + b_blocks
    rowid = lax.broadcasted_iota(jnp.int32, carry.shape, 0)
    before = jnp.where(rowid >= 1, pltpu.roll(after, 1, 0), carry)
    hs = [local[j] + decay[j] * before for j in range(BLOCK_STEPS)]
    new_carry = jnp.broadcast_to(after[SUBLANES - 1:SUBLANES, :], carry.shape)
    return jnp.concatenate(hs, axis=0), new_carry


def _lru_block(xc, pre_r, pre_i, brg, big, c_sp, carry):
    rows, width = xc.shape
    a, b = _lru_maps(xc, pre_r, pre_i, brg, big, c_sp)
    nv = rows // SUBLANES
    a3 = a.reshape(nv, SUBLANES, width)
    b3 = b.reshape(nv, SUBLANES, width)
    rowid = lax.broadcasted_iota(jnp.int32, (nv, SUBLANES, width), 1)
    for s in (1, 2, 4):
        keep = rowid >= s
        a_s = jnp.where(keep, pltpu.roll(a3, s, 1), 1.0)
        b_s = jnp.where(keep, pltpu.roll(b3, s, 1), 0.0)
        b3 = a3 * b_s + b3
        a3 = a3 * a_s
    hs = []
    for v in range(nv):
        h_v = a3[v] * carry + b3[v]
        carry = h_v[SUBLANES - 1:SUBLANES, :]
        hs.append(h_v)
    return jnp.concatenate(hs, axis=0), carry


def _rotary(t, cos, sin):
    half = RET_QK_DIM // 2
    in_first_half = (lax.broadcasted_iota(jnp.int32, t.shape, 1) // half) % 2 == 0
    partner = jnp.where(in_first_half, pltpu.roll(t, LANES - half, 1), pltpu.roll(t, half, 1))
    return t * cos + partner * sin


def _dot_t0(a, b):
    return lax.dot_general(a, b, (((0,), (0,)), ((), ())), preferred_element_type=f32)


def _dot_nt(a, b):
    return lax.dot_general(a, b, (((1,), (1,)), ((), ())), preferred_element_type=f32)


def _meta_states(meta_ref, cos_ref, sin_ref, g1_ref, wlx_ref, wk_ref, wv_ref, convw_ref, convb_ref,
                 wrg_ref, wig_ref, brg_ref, big_ref, lam_ref, kdec_ref, bdm_ref,
                 lx_out, h_out, st_out,
                 lx_scr, xc_scr, xcb_scr, pre_scr):
    u = _rmsnorm_rows(meta_ref[...], g1_ref[...]).astype(bf16)
    lx_scr[0:SUBLANES, :] = jnp.zeros((SUBLANES, LRU_WIDTH), f32)
    lx_scr[SUBLANES:SUBLANES + N_META, :] = jnp.dot(u, wlx_ref[...].astype(bf16), preferred_element_type=f32)
    k = jnp.dot(u, wk_ref[...].astype(bf16), preferred_element_type=f32)
    v = jnp.dot(u, wv_ref[...].astype(bf16), preferred_element_type=f32).astype(bf16)

    _conv(N_META, lx_scr, convw_ref, convb_ref, xc_scr, xcb_scr)
    for h in range(LRU_HEADS):
        _gates_head(h, xcb_scr, wrg_ref, wig_ref, pre_scr)
    for back in range(1, CONV_WIDTH):
        row = lx_scr[SUBLANES + N_META - back:SUBLANES + N_META - back + 1, :]
        group = CONV_WIDTH - 1 - back
        lx_out[group * SUBLANES:(group + 1) * SUBLANES, :] = jnp.broadcast_to(row, (SUBLANES, LRU_WIDTH))
    half_c_sp = 0.5 * LRU_C * jax.nn.softplus(-lam_ref[...])
    _, carry = _lru_block(xc_scr[...], pre_scr[:, :LRU_WIDTH], pre_scr[:, LRU_WIDTH:],
                          0.5 * brg_ref[...], 0.5 * big_ref[...], half_c_sp, jnp.zeros((1, LRU_WIDTH), f32))
    h_out[...] = jnp.broadcast_to(carry, (SUBLANES, LRU_WIDTH))

    cos, sin = cos_ref[...], sin_ref[...]
    for p in range(N_PAIRS):
        sl = slice(p * LANES, (p + 1) * LANES)
        k_rot = _rotary(k[:, sl], cos, sin)
        kd = (k_rot * kdec_ref[CHUNK - N_META:CHUNK, sl]).astype(bf16)
        st_out[p] = _dot_t0(kd, v[:, p * PAIR_V:(p + 1) * PAIR_V]) * bdm_ref[...]


def _norm_tile(x_tile, g1_ref, u_scr):
    u_scr[...] = _rmsnorm_rows(x_tile, g1_ref[...]).astype(bf16)


MXU_COLS = 256
N_IN_TILES = IN_WIDTH // MXU_COLS
W_CHUNK_COLS = 4 * MXU_COLS
N_W_CHUNKS = IN_WIDTH // W_CHUNK_COLS


def _project_col_tile(t, u_scr, win_ref, slot):
    lx_scr, gate_scr, q_scr, k_scr, v_scr, rg_scr = slot
    off = t * MXU_COLS
    res = jnp.dot(u_scr[...], win_ref[:, off:off + MXU_COLS].astype(bf16), preferred_element_type=f32)
    for dst, start, rows, dt in ((lx_scr, OFF_LX, slice(CONV_HIST, CONV_HIST + TILE_T), f32), (gate_scr, OFF_GATE, slice(None), f32),
                                 (q_scr, OFF_Q, slice(None), f32), (k_scr, OFF_K, slice(None), f32),
                                 (v_scr, OFF_V, slice(None), bf16), (rg_scr, OFF_RG, slice(None), f32)):
        width = dst.shape[1]
        if start <= off < start + width:
            dst[rows, off - start:off - start + MXU_COLS] = res.astype(dt)
            return
    raise AssertionError("column tile outside the projection")


def _finish_tile(load_x, cos_ref, sin_ref, convw_ref, convb_ref, wrg_ref, wig_ref, brg_ref, big_ref, lam_ref, rng_ref,
                 wout_ref, dmaskp_ref, kdec_ref, qdec_ref, even_ref, odd_ref, gst_ref, bdm_ref,
                 slot, xc_scr, xcb_scr, pre_scr, y_scr, lxtail_scr, h_scr, st_scr, after_gates):
    lx_scr, gate_scr, q_scr, k_scr, v_scr, rg_scr = slot
    tt = TILE_T

    def retention_unit(c, p):
        rows = slice(c * CHUNK, (c + 1) * CHUNK)
        cos, sin = cos_ref[rows, :], sin_ref[rows, :]
        sl = slice(p * LANES, (p + 1) * LANES)
        q_rot = _rotary(q_scr[rows, sl], cos, sin)
        k_rot = _rotary(k_scr[rows, sl], cos, sin)
        q_b = q_rot.astype(bf16)
        q_d = (q_rot * qdec_ref[:, sl]).astype(bf16)
        k_d = (k_rot * kdec_ref[:, sl]).astype(bf16)
        k_b = k_rot.astype(bf16)
        first_head, second_head = even_ref[:, sl].astype(bf16), odd_ref[:, sl].astype(bf16)
        kk = jnp.concatenate([k_b * first_head, k_b * second_head], axis=0)
        kk_d = jnp.concatenate([k_d * first_head, k_d * second_head], axis=0)
        v_pair = v_scr[rows, p * PAIR_V:(p + 1) * PAIR_V]
        zeros = jnp.zeros((CHUNK, RET_V_DIM), bf16)
        v_bd = jnp.concatenate([jnp.concatenate([v_pair[:, :RET_V_DIM], zeros], axis=1),
                                jnp.concatenate([zeros, v_pair[:, RET_V_DIM:]], axis=1)], axis=0)
        st_pair = st_scr[p]
        s = _dot_nt(q_b, kk) * dmaskp_ref[p]
        o = jnp.dot(jnp.concatenate([s.astype(bf16), q_d], axis=1),
                    jnp.concatenate([v_bd, st_pair.astype(bf16)], axis=0), preferred_element_type=f32)
        for e in range(2):
            h = 2 * p + e
            hs = slice(h * RET_V_DIM, (h + 1) * RET_V_DIM)
            o_h = o[:, e * RET_V_DIM:(e + 1) * RET_V_DIM]
            mu = jnp.mean(o_h, axis=-1, keepdims=True)
            oc = o_h - mu
            var = jnp.mean(oc * oc, axis=-1, keepdims=True)
            on = oc * lax.rsqrt(var + EPS) * rng_ref[:, hs]
            y_scr[rows, LRU_WIDTH + h * RET_V_DIM:LRU_WIDTH + (h + 1) * RET_V_DIM] = (
                on * _silu(rg_scr[rows, hs])).astype(bf16)
        st_scr[p] = gst_ref[p] * st_pair + _dot_t0(kk_d, v_bd)

    lx_scr[0:CONV_HIST, :] = lxtail_scr[...]
    _conv_blocked(lx_scr, convw_ref, convb_ref, xc_scr, xcb_scr)
    lxtail_scr[...] = lx_scr[pl.ds(tt, CONV_HIST), :]
    for h in range(LRU_HEADS):
        lanes = slice(h * LRU_BLOCK, (h + 1) * LRU_BLOCK)
        ilanes = slice(LRU_WIDTH + h * LRU_BLOCK, LRU_WIDTH + (h + 1) * LRU_BLOCK)
        _gates_head(h, xcb_scr, wrg_ref, wig_ref, pre_scr)
        after_gates(h)
        full = (CHUNK, LRU_BLOCK)
        c_sp = jnp.broadcast_to(0.5 * LRU_C * jax.nn.softplus(-lam_ref[:, lanes]), full)
        brg, big = jnp.broadcast_to(0.5 * brg_ref[:, lanes], full), jnp.broadcast_to(0.5 * big_ref[:, lanes], full)
        carry = h_scr[:, lanes]
        for c in range(tt // CHUNK):
            rows = slice(c * CHUNK, (c + 1) * CHUNK)
            hb, carry = _lru_chunk_blocked(xc_scr[rows, lanes], pre_scr[rows, lanes], pre_scr[rows, ilanes],
                                           brg, big, c_sp, carry)
            y_scr[rows, lanes] = (hb * _silu(gate_scr[rows, lanes])).astype(bf16)
        h_scr[:, lanes] = carry

    for c in range(tt // CHUNK):
        for p in range(N_PAIRS):
            retention_unit(c, p)

    return load_x() + jnp.dot(y_scr[...], wout_ref[...].astype(bf16), preferred_element_type=f32)


X_SLOTS = 4
OUT_SLOTS = 2
SLOT_SHAPE = (TILE_T // CHUNK, BLOCK_STEPS, SUBLANES, D_MODEL)


def _tile_block_copies(hbm_ref, vmem_ref, sem_ref, tile, slot, n_tiles, to_hbm):
    chunks = TILE_T // CHUNK
    b = lax.div(tile, n_tiles)
    first_chunk = (tile - b * n_tiles) * chunks
    copies = []
    for s in range(SUBLANES):
        hbm = hbm_ref.at[b, pl.ds(first_chunk, chunks), pl.ds(s * BLOCK_STEPS, BLOCK_STEPS), :]
        vmem = vmem_ref.at[slot, :, :, s, :]
        src, dst = (vmem, hbm) if to_hbm else (hbm, vmem)
        copies.append(pltpu.make_async_copy(src, dst, sem_ref.at[slot]))
    return copies


def _main_kernel(n_tiles, n_total,
                 x_hbm, cos_ref, sin_ref, g1_ref, win_hbm, convw_ref, convb_ref, wrg_ref, wig_ref,
                 brg_ref, big_ref, lam_ref, rng_ref, wout_hbm, fng_ref, dmaskp_ref, kdec_ref, qdec_ref, even_ref, odd_ref,
                 gst_ref, bdm_ref, meta_ref, cos_meta_ref, sin_meta_ref, kdec_time_ref,
                 o_hbm,
                 u_a, u_b,
                 lx_a, gate_a, q_a, k_a, v_a, rg_a,
                 lx_b, gate_b, q_b, k_b, v_b, rg_b,
                 xc_scr, xcb_scr, pre_scr, y_scr, lxtail_scr, h_scr, st_scr,
                 lx0_ref, h0_ref, st0_ref, res_scr,
                 xin_scr, out_scr, xin_sem, out_sem,
                 win_ref, wout_ref, w_sem):
    g = pl.program_id(0)
    slots = ((lx_a, gate_a, q_a, k_a, v_a, rg_a), (lx_b, gate_b, q_b, k_b, v_b, rg_b))
    win_copies = [pltpu.make_async_copy(win_hbm.at[0, :, pl.ds(c * W_CHUNK_COLS, W_CHUNK_COLS)],
                                        win_ref.at[:, pl.ds(c * W_CHUNK_COLS, W_CHUNK_COLS)], w_sem.at[c])
                  for c in range(N_W_CHUNKS)]
    wout_copy = pltpu.make_async_copy(wout_hbm.at[0], wout_ref, w_sem.at[N_W_CHUNKS])
    x_slot = lambda tile: lax.rem(tile + X_SLOTS, X_SLOTS)
    out_slot = lambda tile: lax.rem(tile + OUT_SLOTS, OUT_SLOTS)
    x_tile = lambda tile: xin_scr[x_slot(tile)].reshape(TILE_T, D_MODEL)
    fetch = lambda tile: _tile_block_copies(x_hbm, xin_scr, xin_sem, jnp.minimum(tile, n_total - 1), x_slot(tile),
                                            n_tiles, to_hbm=False)
    write_back = lambda tile: _tile_block_copies(o_hbm, out_scr, out_sem, tile, out_slot(tile), n_tiles, to_hbm=True)

    @pl.when(g >= 2 + OUT_SLOTS)
    def _():
        for copy in write_back(g - 2 - OUT_SLOTS):
            copy.wait()

    @pl.when(lax.rem(g + n_tiles - 1, n_tiles) == 0)
    def _():
        lxtail_scr[...] = lx0_ref[...]
        h_scr[...] = h0_ref[...]
        st_scr[...] = st0_ref[...]

    tiles_after_head = (4, 4, 3, 3, 2, 2, 1, 1)
    assert sum(tiles_after_head) == N_IN_TILES and len(tiles_after_head) == LRU_HEADS

    def finish(slot_finish, after_gates):
        res_scr[...] = _finish_tile(
            lambda: x_tile(g - 1), cos_ref, sin_ref, convw_ref, convb_ref, wrg_ref, wig_ref, brg_ref, big_ref,
            lam_ref, rng_ref, wout_ref, dmaskp_ref, kdec_ref, qdec_ref, even_ref, odd_ref, gst_ref,
            bdm_ref, slot_finish, xc_scr, xcb_scr, pre_scr, y_scr, lxtail_scr, h_scr, st_scr, after_gates)

    def final_norm(tile):
        out = _rmsnorm_rows(res_scr[...], fng_ref[...])
        out_scr[out_slot(tile)] = out.reshape(SLOT_SHAPE)

    def norm_and_write_back(tile):
        final_norm(tile)
        for copy in write_back(tile):
            copy.start()

    final_norm_after_head, pre_norm_after_head = 4, 6

    def step(u_project, u_next, slot_project, slot_finish, norm_previous=True):
        for copy in fetch(g + 1):
            copy.wait()
        for i, copy in enumerate(fetch(g + 2)):
            copy.start(priority=i % 2)

        def after_gates(h):
            first = sum(tiles_after_head[:h])
            for t in range(first, first + tiles_after_head[h]):
                _project_col_tile(t, u_project, win_ref, slot_project)
            if norm_previous and h == final_norm_after_head:
                final_norm(g - 2)
            if h == pre_norm_after_head:
                _norm_tile(x_tile(g + 1), g1_ref, u_next)

        finish(slot_finish, after_gates)
        if norm_previous:
            for i, copy in enumerate(write_back(g - 2)):
                copy.start(priority=i % 2)

    @pl.when(g == 0)
    def _():
        for copy in fetch(g):
            copy.start()
        for copy in win_copies + [wout_copy]:
            copy.start()
        for copy in fetch(g):
            copy.wait()
        for tile in (g + 1, g + 2):
            for copy in fetch(tile):
                copy.start()
        _norm_tile(x_tile(g), g1_ref, u_a)
        for t in range(N_IN_TILES):
            if t % (W_CHUNK_COLS // MXU_COLS) == 0:
                win_copies[t // (W_CHUNK_COLS // MXU_COLS)].wait()
            _project_col_tile(t, u_a, win_ref, slots[0])
        meta_rows = lambda ref, n: ref.at[pl.ds(0, n)]
        _meta_states(meta_ref, cos_meta_ref, sin_meta_ref, g1_ref,
                     win_ref.at[:, pl.ds(OFF_LX, LRU_WIDTH)], win_ref.at[:, pl.ds(OFF_K, RET_QK_WIDTH)],
                     win_ref.at[:, pl.ds(OFF_V, RET_WIDTH)], convw_ref, convb_ref, wrg_ref, wig_ref, brg_ref, big_ref,
                     lam_ref, kdec_time_ref, bdm_ref, lx0_ref, h0_ref, st0_ref,
                     meta_rows(slots[1][0], SUBLANES + N_META), meta_rows(xc_scr, N_META), meta_rows(xcb_scr, N_META),
                     meta_rows(pre_scr, N_META))
        for copy in fetch(g + 1):
            copy.wait()
        _norm_tile(x_tile(g + 1), g1_ref, u_b)

    @pl.when(g == 1)
    def _():
        wout_copy.wait()
        step(u_b, u_a, slots[1], slots[0], norm_previous=False)

    regular = jnp.logical_and(g >= 2, g < n_total)

    @pl.when(jnp.logical_and(regular, lax.rem(g, 2) == 0))
    def _():
        step(u_a, u_b, slots[0], slots[1])

    @pl.when(jnp.logical_and(regular, lax.rem(g, 2) == 1))
    def _():
        step(u_b, u_a, slots[1], slots[0])

    @pl.when(g == n_total)
    def _():
        for copy in fetch(g + 1):
            copy.wait()
        norm_and_write_back(g - 2)
        finish(slots[(n_total - 1) % 2], lambda h: None)
        for copy in write_back(g - 1 - OUT_SLOTS):
            copy.wait()
        norm_and_write_back(g - 1)
        for tile in (g - 2, g - 1):
            for copy in write_back(tile):
                copy.wait()


def _const_spec(shape):
    nd = len(shape)
    return pl.BlockSpec(shape, lambda g, _nd=nd: (0,) * _nd, pipeline_mode=pl.Buffered(1))


def kernel(x, meta_tokens, norm_gain, w_in, conv_w, conv_b, w_rg, b_rg, w_ig, b_ig,
           lru_lambda, ret_norm_gain, w_out, final_norm_gain):
    B, S, D = x.shape
    assert D == D_MODEL and S % TILE_T == 0 and TILE_T % CHUNK == 0
    assert norm_gain.shape[0] == 1, "single-layer block"
    assert meta_tokens.shape == (N_META, D_MODEL)

    assert w_in.shape == (1, D_MODEL, IN_WIDTH) and w_out.shape == (1, MIX_WIDTH, D_MODEL)
    assert w_rg.shape == w_ig.shape == (1, LRU_HEADS, LRU_BLOCK, LRU_BLOCK)

    g1 = norm_gain[0].reshape(1, D_MODEL)
    fng = final_norm_gain.reshape(1, D_MODEL)
    convw = conv_w[0]
    convb = conv_b[0].reshape(1, LRU_WIDTH)
    brg = b_rg[0].reshape(1, LRU_WIDTH)
    big = b_ig[0].reshape(1, LRU_WIDTH)
    lam = lru_lambda[0].reshape(1, LRU_WIDTH)
    rng = ret_norm_gain[0].reshape(1, RET_WIDTH)

    k_dec, dmask_blocked, k_dec_blocked, q_dec_blocked, even, odd, g_state, bd_mask = _retention_tables()
    cos_t, sin_t = _rotary_tables(N_META + S)
    cos_m, sin_m = jnp.asarray(cos_t[:N_META]), jnp.asarray(sin_t[:N_META])
    cos_blocked = jnp.asarray(_chunk_rows_blocked(cos_t[N_META:]))
    sin_blocked = jnp.asarray(_chunk_rows_blocked(sin_t[N_META:]))

    tt = TILE_T
    n_tiles = S // tt
    n_total = B * n_tiles
    assert n_total >= 3, "the four-stage tile pipeline needs at least three tiles"

    rot_spec = pl.BlockSpec((tt, LANES), lambda g: (jnp.maximum(g - 1, 0) % n_tiles, 0))
    in_specs = [
        pl.BlockSpec(memory_space=pl.ANY),
        rot_spec, rot_spec,
        _const_spec((1, D_MODEL)),
        pl.BlockSpec(memory_space=pl.ANY),
        _const_spec((CONV_WIDTH, LRU_WIDTH)), _const_spec((1, LRU_WIDTH)),
        _const_spec((None, LRU_HEADS, LRU_BLOCK, LRU_BLOCK)),
        _const_spec((None, LRU_HEADS, LRU_BLOCK, LRU_BLOCK)),
        _const_spec((1, LRU_WIDTH)), _const_spec((1, LRU_WIDTH)), _const_spec((1, LRU_WIDTH)),
        _const_spec((1, RET_WIDTH)),
        pl.BlockSpec(memory_space=pl.ANY),
        _const_spec((1, D_MODEL)),
        _const_spec((N_PAIRS, CHUNK, 2 * CHUNK)),
        _const_spec((CHUNK, RET_QK_WIDTH)), _const_spec((CHUNK, RET_QK_WIDTH)),
        _const_spec((1, RET_QK_WIDTH)), _const_spec((1, RET_QK_WIDTH)),
        _const_spec((N_PAIRS, LANES, PAIR_V)),
        _const_spec((LANES, PAIR_V)),
        _const_spec((N_META, D_MODEL)),
        _const_spec((N_META, LANES)), _const_spec((N_META, LANES)),
        _const_spec((CHUNK, RET_QK_WIDTH)),
    ]
    slot_scratch = [
        pltpu.VMEM((CONV_HIST + tt, LRU_WIDTH), f32),
        pltpu.VMEM((tt, LRU_WIDTH), f32),
        pltpu.VMEM((tt, RET_QK_WIDTH), f32),
        pltpu.VMEM((tt, RET_QK_WIDTH), f32),
        pltpu.VMEM((tt, RET_WIDTH), bf16),
        pltpu.VMEM((tt, RET_WIDTH), f32),
    ]
    scratch = [pltpu.VMEM((tt, D_MODEL), bf16)] * 2 + slot_scratch + slot_scratch + [
        pltpu.VMEM((tt, LRU_WIDTH), f32),
        pltpu.VMEM((tt, LRU_WIDTH), bf16),
        pltpu.VMEM((tt, 2 * LRU_WIDTH), f32),
        pltpu.VMEM((tt, MIX_WIDTH), bf16),
        pltpu.VMEM((CONV_HIST, LRU_WIDTH), f32),
        pltpu.VMEM((SUBLANES, LRU_WIDTH), f32),
        pltpu.VMEM((N_PAIRS, LANES, PAIR_V), f32),
        pltpu.VMEM((CONV_HIST, LRU_WIDTH), f32),
        pltpu.VMEM((SUBLANES, LRU_WIDTH), f32),
        pltpu.VMEM((N_PAIRS, LANES, PAIR_V), f32),
        pltpu.VMEM((tt, D_MODEL), f32),
        pltpu.VMEM((X_SLOTS,) + SLOT_SHAPE, f32),
        pltpu.VMEM((OUT_SLOTS,) + SLOT_SHAPE, f32),
        pltpu.SemaphoreType.DMA((X_SLOTS,)),
        pltpu.SemaphoreType.DMA((OUT_SLOTS,)),
        pltpu.VMEM((D_MODEL, IN_WIDTH), f32),
        pltpu.VMEM((MIX_WIDTH, D_MODEL), f32),
        pltpu.SemaphoreType.DMA((N_W_CHUNKS + 1,)),
    ]
    out = pl.pallas_call(
        functools.partial(_main_kernel, n_tiles, n_total),
        grid=(n_total + 1,),
        in_specs=in_specs,
        out_specs=pl.BlockSpec(memory_space=pl.ANY),
        out_shape=jax.ShapeDtypeStruct((B, S // CHUNK, CHUNK, D_MODEL), x.dtype),
        scratch_shapes=scratch,
        compiler_params=pltpu.CompilerParams(
            dimension_semantics=("arbitrary",),
            vmem_limit_bytes=VMEM_LIMIT_BYTES),
        name="hybrid_main",
    )(x.reshape(B, S // CHUNK, CHUNK, D_MODEL),
      cos_blocked, sin_blocked, g1, w_in, convw, convb, w_rg, w_ig, brg, big, lam, rng, w_out, fng,
      dmask_blocked, k_dec_blocked, q_dec_blocked, even, odd, g_state, bd_mask, meta_tokens, cos_m, sin_m, k_dec)
    return out.reshape(B, S, D_MODEL)
```
